```python
import math
import jax
import jax.numpy as jnp
from jax import lax
import numpy as np

D_MODEL = 2048
BATCH = 4
SEQ = 2048
DEPTH = 1
DEC_BATCH = 128
DEC_SEQ = 4
PAST_LEN = 16384
PAGE_SIZE = 128

HG_HEADS = 16
HG_DK = 128
HG_DV = 128
HG_CHUNK = 32
GDN_QK_HEADS = 16
GDN_V_HEADS = 32
GDN_DK = 128
GDN_DV = 128
GDN_CHUNK = 64
CONV_W = 4
D_FF = 4 * D_MODEL
NORM_EPS = 1e-6

HG_W = HG_HEADS * HG_DK
HG_VW = HG_HEADS * HG_DV
GDN_QK_W = GDN_QK_HEADS * GDN_DK
GDN_V_W = GDN_V_HEADS * GDN_DV
CONV_CH = 2 * GDN_QK_W + GDN_V_W
GDN_SCALE = GDN_DK ** -0.5
IN_SPLITS = (HG_W, HG_W, HG_VW, HG_VW, GDN_QK_W, GDN_QK_W, GDN_V_W, GDN_V_W, GDN_V_HEADS, GDN_V_HEADS, D_MODEL, D_MODEL)
N_IN = 4 * HG_W + 2 * GDN_QK_W + 2 * GDN_V_W + 2 * GDN_V_HEADS + 2 * D_MODEL

kernel_name = "hybrid_hgrn2_gdn_decoder_step"


def rmsnorm(x, g):
    x32 = x.astype(jnp.float32)
    return x32 * lax.rsqrt(jnp.mean(x32 * x32, axis=-1, keepdims=True) + NORM_EPS) * g.astype(jnp.float32)


def l2norm(x):
    return x * lax.rsqrt(jnp.sum(x * x, axis=-1, keepdims=True) + NORM_EPS)


def to_chunks(x, C, n):
    B, T = x.shape[0], x.shape[1]
    x = jnp.pad(x, [(0, 0), (0, n * C - T)] + [(0, 0)] * (x.ndim - 2))
    x = x.reshape((B, n, C) + x.shape[2:])
    return jnp.moveaxis(x, (1, 3), (0, 2))


def from_chunks(x, T):
    n, B, H, C, D = x.shape
    x = jnp.moveaxis(x, (0, 2), (1, 3)).reshape(B, n * C, H, D)
    return x[:, :T]


def hgrn2_scan(q, f, inp, S0):
    T = q.shape[1]
    C = min(HG_CHUNK, T)
    n = -(-T // C)
    log_f = jnp.log(f)
    k = 1.0 - f
    qc, gc, kc, ic = [to_chunks(a, C, n) for a in (q, log_f, k, inp)]
    causal = jnp.tril(jnp.ones((C, C), dtype=bool))
    ref = C // 2

    def step(S, xs):
        q_, g_, k_, i_ = xs
        G = jnp.cumsum(g_, axis=2)
        G_ref = G[:, :, ref:ref + 1]
        G_last = G[:, :, C - 1:C]
        A = jnp.einsum('bhtd,bhsd->bhts', q_ * jnp.exp(G - G_ref), k_ * jnp.exp(G_ref - G))
        A = jnp.where(causal, A, 0.0)
        o = (jnp.einsum('bhts,bhsv->bhtv', A, i_)
             + jnp.einsum('bhtd,bhdv->bhtv', q_ * jnp.exp(G), S))
        S = (jnp.exp(G_last[:, :, 0])[..., None] * S
             + jnp.einsum('bhsd,bhsv->bhdv', k_ * jnp.exp(G_last - G), i_))
        return S, o

    S, o = lax.scan(step, S0, (qc, gc, kc, ic))
    return from_chunks(o, T), S


def gdn_scan(q, k, v, log_a, beta, S0):
    T = q.shape[1]
    C = min(GDN_CHUNK, T)
    n = -(-T // C)
    qc, kc, vc, gc, bc = [to_chunks(a, C, n) for a in (q, k, v, log_a, beta)]
    incl = jnp.tril(jnp.ones((C, C), dtype=bool))
    strict = jnp.tril(jnp.ones((C, C), dtype=bool), -1)
    eye = jnp.eye(C, dtype=jnp.float32)

    def step(S, xs):
        q_, k_, v_, g_, b_ = xs
        gam = jnp.cumsum(g_, axis=-1)
        L = jnp.exp(jnp.where(incl, gam[..., :, None] - gam[..., None, :], -jnp.inf))
        kb = k_ * b_[..., None]
        A = jnp.where(strict, jnp.einsum('bhtd,bhsd->bhts', kb, k_) * L, 0.0)
        rhs = jnp.concatenate([v_ * b_[..., None], kb * jnp.exp(gam)[..., None]], axis=-1)
        X = lax.linalg.triangular_solve(A + eye, rhs, left_side=True, lower=True, unit_diagonal=True)
        u, w = X[..., :GDN_DV], X[..., GDN_DV:]
        v_new = u - jnp.einsum('bhtd,bhdv->bhtv', w, S)
        attn = jnp.einsum('bhtd,bhsd->bhts', q_, k_) * L
        o = (jnp.einsum('bhtd,bhdv->bhtv', q_ * jnp.exp(gam)[..., None], S)
             + jnp.einsum('bhts,bhsv->bhtv', attn, v_new))
        g_last = gam[..., C - 1:C]
        S = (jnp.exp(g_last)[..., None] * S
             + jnp.einsum('bhsd,bhsv->bhdv', k_ * jnp.exp(g_last - gam)[..., None], v_new))
        return S, o

    S, o = lax.scan(step, S0, (qc, kc, vc, gc, bc))
    return from_chunks(o, T), S


def token_mixer(a, conv_buf, s_hg, s_gdn, lb, w_in, conv_w, A_log, dt_bias,
                hg_norm_g, gdn_norm_g, w_out_hg, w_out_gdn, w_o):
    B, T, _ = a.shape
    f32 = jnp.float32
    proj = jnp.matmul(a, w_in).astype(f32)
    offsets = [int(o) for o in np.cumsum(IN_SPLITS)[:-1]]
    hq, hf, hi, hg, gq, gk, gv, gz, gb, ga, gate_a, gate_b = jnp.split(proj, offsets, axis=-1)

    q_h = jax.nn.silu(hq).reshape(B, T, HG_HEADS, HG_DK)
    f_h = (lb + (1.0 - lb) * jax.nn.sigmoid(hf)).reshape(B, T, HG_HEADS, HG_DK)
    i_h = hi.reshape(B, T, HG_HEADS, HG_DV)
    o_hg, s_hg_new = hgrn2_scan(q_h, f_h, i_h, s_hg.astype(f32))
    o_hg = (rmsnorm(o_hg, hg_norm_g) * jax.nn.silu(hg.reshape(B, T, HG_HEADS, HG_DV))).reshape(B, T, HG_VW)

    u = jnp.concatenate([gq, gk, gv], axis=-1)
    ucat = jnp.concatenate([conv_buf.astype(f32), u], axis=1)
    cw = conv_w.astype(f32)
    conv = cw[0] * ucat[:, 0:T]
    for j in range(1, CONV_W):
        conv = conv + cw[j] * ucat[:, j:j + T]
    conv_buf_new = ucat[:, T:]
    conv = jax.nn.silu(conv)
    cq, ck, cv = jnp.split(conv, [GDN_QK_W, 2 * GDN_QK_W], axis=-1)
    rep = GDN_V_HEADS // GDN_QK_HEADS
    q_g = jnp.repeat(l2norm(cq.reshape(B, T, GDN_QK_HEADS, GDN_DK)), rep, axis=2) * GDN_SCALE
    k_g = jnp.repeat(l2norm(ck.reshape(B, T, GDN_QK_HEADS, GDN_DK)), rep, axis=2)
    v_g = cv.reshape(B, T, GDN_V_HEADS, GDN_DV)
    beta = jax.nn.sigmoid(gb)
    log_a = -jnp.exp(A_log.astype(f32)) * jax.nn.softplus(ga + dt_bias.astype(f32))
    o_gdn, s_gdn_new = gdn_scan(q_g, k_g, v_g, log_a, beta, s_gdn.astype(f32))
    o_gdn = (rmsnorm(o_gdn, gdn_norm_g) * jax.nn.silu(gz.reshape(B, T, GDN_V_HEADS, GDN_DV))).reshape(B, T, GDN_V_W)

    merged = (jax.nn.sigmoid(gate_a) * jnp.matmul(o_hg, w_out_hg)
              + jax.nn.sigmoid(gate_b) * jnp.matmul(o_gdn, w_out_gdn))
    return jnp.matmul(merged, w_o), conv_buf_new, s_hg_new, s_gdn_new


def run_trunk(x, c, conv_buf, st_hg, st_gdn, lb_logits, w_ada, b_ada, norm_g, w_in, conv_w,
              A_log, dt_bias, hg_norm_g, gdn_norm_g, w_out_hg, w_out_gdn, w_o, w_up, w_down):
    f32 = jnp.float32
    lbs = jnp.cumsum(jax.nn.softmax(lb_logits.astype(f32), axis=0), axis=0)
    h = x.astype(f32)
    cs = jax.nn.silu(c.astype(f32))
    new_conv, new_hg, new_gdn = [], [], []
    for l in range(DEPTH):
        mod = jnp.matmul(cs, w_ada[l]) + b_ada[l]
        sh1, sc1, g1, sh2, sc2, g2 = [m[:, None, :] for m in jnp.split(mod, 6, axis=-1)]
        a = rmsnorm(h, norm_g[l, 0]) * (1.0 + sc1) + sh1
        mix, cb, shg, sgd = token_mixer(a, conv_buf[l], st_hg[l], st_gdn[l], lbs[l], w_in[l], conv_w[l],
                                        A_log[l], dt_bias[l], hg_norm_g[l], gdn_norm_g[l],
                                        w_out_hg[l], w_out_gdn[l], w_o[l])
        h = h + g1 * rmsnorm(mix, norm_g[l, 1])
        a = rmsnorm(h, norm_g[l, 2]) * (1.0 + sc2) + sh2
        ff = jnp.matmul(jnp.square(jax.nn.relu(jnp.matmul(a, w_up[l]))), w_down[l])
        h = h + g2 * rmsnorm(ff, norm_g[l, 3])
        new_conv.append(cb)
        new_hg.append(shg)
        new_gdn.append(sgd)
    return h.astype(x.dtype), jnp.stack(new_conv), jnp.stack(new_hg), jnp.stack(new_gdn)


def setup_inputs(seed: int = 0) -> dict:
    key = jax.random.key(seed)
    ks = jax.random.split(key, 24)
    f32 = jnp.float32

    def nrm(k, shape, scale):
        return jax.random.normal(k, shape, f32) * scale

    dt = jnp.exp(jax.random.uniform(ks[14], (DEPTH, GDN_V_HEADS), f32, math.log(1e-3), math.log(1e-1)))
    return {
        "x_prompt": nrm(ks[0], (BATCH, SEQ, D_MODEL), 1.0),
        "x_sample": nrm(ks[1], (DEC_BATCH, DEC_SEQ, D_MODEL), 1.0),
        "state_hgrn": nrm(ks[2], (DEPTH, DEC_BATCH, HG_HEADS, HG_DK, HG_DV), 0.5),
        "state_gdn": nrm(ks[3], (DEPTH, DEC_BATCH, GDN_V_HEADS, GDN_DK, GDN_DV), 0.1),
        "cache_conv": nrm(ks[4], (DEPTH, DEC_BATCH, CONV_W - 1, CONV_CH), 1.0),
        "c_prompt": nrm(ks[5], (BATCH, D_MODEL), 1.0),
        "c_sample": nrm(ks[6], (DEC_BATCH, D_MODEL), 1.0),
        "lb_logits": nrm(ks[7], (DEPTH + 1, HG_W), 0.1),
        "w_ada": nrm(ks[8], (DEPTH, D_MODEL, 6 * D_MODEL), 0.5 * D_MODEL ** -0.5),
        "b_ada": nrm(ks[9], (DEPTH, 6 * D_MODEL), 0.02),
        "norm_g": 1.0 + nrm(ks[10], (DEPTH, 4, D_MODEL), 0.02),
        "w_in": nrm(ks[11], (DEPTH, D_MODEL, N_IN), D_MODEL ** -0.5),
        "conv_w": nrm(ks[12], (DEPTH, CONV_W, CONV_CH), CONV_W ** -0.5),
        "A_log": jnp.log(jax.random.uniform(ks[13], (DEPTH, GDN_V_HEADS), f32, 1.0, 16.0)),
        "dt_bias": dt + jnp.log(-jnp.expm1(-dt)),
        "hg_norm_g": 1.0 + nrm(ks[15], (DEPTH, HG_DV), 0.02),
        "gdn_norm_g": 1.0 + nrm(ks[16], (DEPTH, GDN_DV), 0.02),
        "w_out_hg": nrm(ks[17], (DEPTH, HG_VW, D_MODEL), HG_VW ** -0.5),
        "w_out_gdn": nrm(ks[18], (DEPTH, GDN_V_W, D_MODEL), GDN_V_W ** -0.5),
        "w_o": nrm(ks[19], (DEPTH, D_MODEL, D_MODEL), D_MODEL ** -0.5),
        "w_up": nrm(ks[20], (DEPTH, D_MODEL, D_FF), D_MODEL ** -0.5),
        "w_down": nrm(ks[21], (DEPTH, D_FF, D_MODEL), D_FF ** -0.5),
    }


def reference(x_prompt, x_sample, state_hgrn, state_gdn, cache_conv, c_prompt, c_sample,
              lb_logits, w_ada, b_ada, norm_g, w_in, conv_w, A_log, dt_bias, hg_norm_g, gdn_norm_g,
              w_out_hg, w_out_gdn, w_o, w_up, w_down):
    sdt = state_hgrn.dtype
    zero_conv = jnp.zeros((DEPTH, BATCH, CONV_W - 1, CONV_CH), cache_conv.dtype)
    zero_hg = jnp.zeros((DEPTH, BATCH, HG_HEADS, HG_DK, HG_DV), sdt)
    zero_gdn = jnp.zeros((DEPTH, BATCH, GDN_V_HEADS, GDN_DK, GDN_DV), state_gdn.dtype)
    y_prompt, conv_p, hg_p, gdn_p = run_trunk(
        x_prompt, c_prompt, zero_conv, zero_hg, zero_gdn, lb_logits, w_ada, b_ada, norm_g, w_in, conv_w,
        A_log, dt_bias, hg_norm_g, gdn_norm_g, w_out_hg, w_out_gdn, w_o, w_up, w_down)
    y_sample, conv_s, hg_s, gdn_s = run_trunk(
        x_sample, c_sample, cache_conv, state_hgrn, state_gdn, lb_logits, w_ada, b_ada, norm_g, w_in, conv_w,
        A_log, dt_bias, hg_norm_g, gdn_norm_g, w_out_hg, w_out_gdn, w_o, w_up, w_down)
    return (y_prompt, y_sample,
            hg_p.astype(sdt), gdn_p.astype(state_gdn.dtype), conv_p.astype(cache_conv.dtype),
            hg_s.astype(sdt), gdn_s.astype(state_gdn.dtype), conv_s.astype(cache_conv.dtype))
```

```python
import functools

import jax
import jax.numpy as jnp
from jax import lax
from jax.experimental import pallas as pl
from jax.experimental.pallas import tpu as pltpu

F32 = jnp.float32
BF16 = jnp.bfloat16
HIGHEST = lax.Precision.HIGHEST

HEAD_DIM = 128
HG_CHUNK = 32
GDN_CHUNK = 64
CONV_W = 4
NORM_EPS = 1e-6
SUBLANES = 8
VMEM_LIMIT = 48 * 1024 * 1024


def _cparams(*sem):
    return pltpu.CompilerParams(dimension_semantics=sem, vmem_limit_bytes=VMEM_LIMIT)


def _tile(n, pref, quantum):
    if n <= pref:
        return n
    t = (pref // quantum) * quantum
    while t > quantum and n % t:
        t -= quantum
    assert n % t == 0, (n, pref, quantum)
    return t


def _bdot(a, b):
    return jnp.dot(a.astype(BF16), b.astype(BF16), preferred_element_type=F32)


def _bdot_nt(a, b):
    return lax.dot_general(a.astype(BF16), b.astype(BF16), (((1,), (1,)), ((), ())),
                           preferred_element_type=F32)


def _fdot(a, b):
    return jnp.dot(a, b, precision=HIGHEST, preferred_element_type=F32)


def _split3(x):
    hi = x.astype(BF16)
    r = x - hi.astype(F32)
    mid = r.astype(BF16)
    lo = (r - mid.astype(F32)).astype(BF16)
    return hi, mid, lo


def _mask_dot(m_bf, x):
    hi, mid, lo = _split3(x)
    return (jnp.dot(m_bf, hi, preferred_element_type=F32)
            + jnp.dot(m_bf, mid, preferred_element_type=F32)
            + jnp.dot(m_bf, lo, preferred_element_type=F32))


def _sigmoid(x):
    return 1.0 / (1.0 + jnp.exp(-x))


def _silu(x):
    return x * _sigmoid(x)


def _rms(x):
    return x * lax.rsqrt(jnp.mean(x * x, axis=-1, keepdims=True) + NORM_EPS)


def _transpose_pad(x):
    r = x.shape[0]
    if r < HEAD_DIM:
        x = jnp.concatenate([x, jnp.zeros((HEAD_DIM - r, HEAD_DIM), x.dtype)], axis=0)
    return x.T


def _tri_masks(c):
    row = lax.broadcasted_iota(jnp.int32, (c, c), 0)
    col = lax.broadcasted_iota(jnp.int32, (c, c), 1)
    return col <= row, col < row


def _ada_kernel(c_ref, w_ref, b_ref, o_ref):
    cs = _silu(c_ref[...])
    w = w_ref[...]
    c_hi = cs.astype(BF16)
    c_lo = (cs - c_hi.astype(F32)).astype(BF16)
    w_hi = w.astype(BF16)
    w_lo = (w - w_hi.astype(F32)).astype(BF16)
    acc = jnp.dot(c_hi, w_hi, preferred_element_type=F32)
    acc += jnp.dot(c_lo, w_hi, preferred_element_type=F32)
    acc += jnp.dot(c_hi, w_lo, preferred_element_type=F32)
    o_ref[...] = acc + b_ref[...]


def _ada(c_all, w_ada, b_ada):
    m, d = c_all.shape
    n = w_ada.shape[-1]
    tn = _tile(n, 512, 128)
    return pl.pallas_call(
        _ada_kernel,
        grid=(n // tn,),
        in_specs=[pl.BlockSpec((m, d), lambda j: (0, 0)),
                  pl.BlockSpec((None, d, tn), lambda j: (0, 0, j)),
                  pl.BlockSpec((1, tn), lambda j: (0, j))],
        out_specs=pl.BlockSpec((m, tn), lambda j: (0, j)),
        out_shape=jax.ShapeDtypeStruct((m, n), F32),
        compiler_params=_cparams("arbitrary"),
        name="ada_mod",
    )(c_all, w_ada, b_ada)


def _mod_spec(mod, k, tm, rows_per_seq):
    d = mod.shape[-1]
    if mod.shape[1] == 1:
        tiles_per_seq = rows_per_seq // tm
        return pl.BlockSpec((None, 1, d), lambda i, *_: ((i // tiles_per_seq) * 6 + k, 0, 0))
    return pl.BlockSpec((None, tm, d), lambda i, *_: (k, i, 0))


def _prenorm_kernel(x_ref, g_ref, sc_ref, sh_ref, o_ref):
    xn = _rms(x_ref[...].astype(F32)) * g_ref[...]
    o_ref[...] = (xn * (1.0 + sc_ref[...]) + sh_ref[...]).astype(o_ref.dtype)


def _prenorm(x, g, mod, k_scale, k_shift, rows_per_seq, tm):
    m, d = x.shape
    return pl.pallas_call(
        _prenorm_kernel,
        grid=(m // tm,),
        in_specs=[pl.BlockSpec((tm, d), lambda i: (i, 0)),
                  pl.BlockSpec((1, d), lambda i: (0, 0)),
                  _mod_spec(mod, k_scale, tm, rows_per_seq),
                  _mod_spec(mod, k_shift, tm, rows_per_seq)],
        out_specs=pl.BlockSpec((tm, d), lambda i: (i, 0)),
        out_shape=jax.ShapeDtypeStruct((m, d), BF16),
        compiler_params=_cparams("parallel"),
        name="prenorm",
    )(x, g, mod, mod)


def _mm_kernel(a_ref, w_ref, o_ref):
    o_ref[...] = _bdot(a_ref[...], w_ref[...])


def _project(a, w3, n_cols, tm, tn):
    m, kdim = a.shape
    return pl.pallas_call(
        _mm_kernel,
        grid=(m // tm, n_cols // tn),
        in_specs=[pl.BlockSpec((tm, kdim), lambda i, j: (i, 0)),
                  pl.BlockSpec((None, kdim, tn), lambda i, j: (0, 0, j))],
        out_specs=pl.BlockSpec((tm, tn), lambda i, j: (i, j)),
        out_shape=jax.ShapeDtypeStruct((m, n_cols), F32),
        compiler_params=_cparams("parallel", "arbitrary"),
        name="in_proj",
    )(a, w3)


def _hgrn_elementwise(hq, hf, lb):
    q = _silu(hq)
    f = lb + (1.0 - lb) * _sigmoid(hf)
    return q, jnp.log(f), 1.0 - f


def _hgrn_chunk(q, g, k, inp, s, tril, tril_bf):
    c = q.shape[0]
    big_g = _mask_dot(tril_bf, g)
    g_ref = big_g[c // 2:c // 2 + 1]
    g_last = big_g[c - 1:c]
    qe = q * jnp.exp(big_g - g_ref)
    ke = k * jnp.exp(g_ref - big_g)
    a = jnp.where(tril, _bdot_nt(qe, ke), 0.0)
    o = _bdot(a, inp) + _bdot(q * jnp.exp(big_g), s)
    kl = k * jnp.exp(g_last - big_g)
    decay_rows = jnp.broadcast_to(jnp.exp(g_last), (SUBLANES, HEAD_DIM))
    t = _transpose_pad(jnp.concatenate([kl, decay_rows], axis=0))
    s_new = t[:, c:c + 1] * s + _bdot(t[:, :c], inp)
    return o, s_new


def _head_out(o, gate, norm_g):
    return _rms(o) * norm_g * _silu(gate)


def _hgrn_prompt_kernel(q_ref, f_ref, i_ref, g_ref, lb_ref, ng_ref, o_ref, st_ref, s_scr, *, hb):
    t = pl.program_id(2)
    c = HG_CHUNK

    @pl.when(t == 0)
    def _():
        s_scr[...] = jnp.zeros_like(s_scr)

    tril, _ = _tri_masks(c)
    tril_bf = jnp.where(tril, 1.0, 0.0).astype(BF16)
    lb = lb_ref[...]
    ng = ng_ref[...]

    def chunk(ci, carry):
        rows = pl.ds(pl.multiple_of(ci * c, c), c)
        q, g, k = _hgrn_elementwise(q_ref[rows, :], f_ref[rows, :], lb)
        inp = i_ref[rows, :]
        gate = g_ref[rows, :]
        for h in range(hb):
            sl = slice(h * HEAD_DIM, (h + 1) * HEAD_DIM)
            o, s_new = _hgrn_chunk(q[:, sl], g[:, sl], k[:, sl], inp[:, sl], s_scr[h], tril, tril_bf)
            s_scr[h] = s_new
            o_ref[rows, sl] = _head_out(o, gate[:, sl], ng).astype(o_ref.dtype)
        return carry

    lax.fori_loop(0, q_ref.shape[0] // c, chunk, 0)

    @pl.when(t == pl.num_programs(2) - 1)
    def _():
        st_ref[...] = s_scr[...].astype(st_ref.dtype)


def _hgrn_prompt(proj, lb, norm_g, batch, seq, heads, state_dtype):
    hb = _tile(heads, 8, 1)
    tt = _tile(seq, 256, HG_CHUNK)
    w = hb * HEAD_DIM
    nhb = heads // hb
    nt = seq // tt

    def col(seg):
        return pl.BlockSpec((tt, w), lambda b, h, t: (b * nt + t, seg * nhb + h))

    return pl.pallas_call(
        functools.partial(_hgrn_prompt_kernel, hb=hb),
        grid=(batch, nhb, nt),
        in_specs=[col(0), col(1), col(2), col(3),
                  pl.BlockSpec((1, w), lambda b, h, t: (0, h)),
                  pl.BlockSpec((1, HEAD_DIM), lambda b, h, t: (0, 0))],
        out_specs=[pl.BlockSpec((tt, w), lambda b, h, t: (b * nt + t, h)),
                   pl.BlockSpec((None, hb, HEAD_DIM, HEAD_DIM), lambda b, h, t: (b, h, 0, 0))],
        out_shape=[jax.ShapeDtypeStruct((batch * seq, heads * HEAD_DIM), BF16),
                   jax.ShapeDtypeStruct((batch, heads, HEAD_DIM, HEAD_DIM), state_dtype)],
        scratch_shapes=[pltpu.VMEM((hb, HEAD_DIM, HEAD_DIM), F32)],
        compiler_params=_cparams("parallel", "parallel", "arbitrary"),
        name="hgrn_prompt",
    )(proj, proj, proj, proj, lb, norm_g)


def _hgrn_sample_kernel(q_ref, f_ref, i_ref, g_ref, lb_ref, ng_ref, s0_ref, o_ref, st_ref, *, hb, seq):
    c = SUBLANES
    per_group = c // seq
    tril, _ = _tri_masks(c)
    tril_bf = jnp.where(tril, 1.0, 0.0).astype(BF16)
    lb = lb_ref[...]
    ng = ng_ref[...]
    row = lax.broadcasted_iota(jnp.int32, (c, 1), 0)

    def group(gi, carry):
        rows = pl.ds(pl.multiple_of(gi * c, c), c)
        q, g, k = _hgrn_elementwise(q_ref[rows, :], f_ref[rows, :], lb)
        inp = i_ref[rows, :]
        gate = g_ref[rows, :]
        for h in range(hb):
            sl = slice(h * HEAD_DIM, (h + 1) * HEAD_DIM)
            o_all = jnp.zeros((c, HEAD_DIM), F32)
            for p in range(per_group):
                valid = (row >= p * seq) & (row < (p + 1) * seq)
                qm = jnp.where(valid, q[:, sl], 0.0)
                gm = jnp.where(valid, g[:, sl], 0.0)
                km = jnp.where(valid, k[:, sl], 0.0)
                b = gi * per_group + p
                o, s_new = _hgrn_chunk(qm, gm, km, inp[:, sl], s0_ref[b, h].astype(F32), tril, tril_bf)
                st_ref[b, h] = s_new.astype(st_ref.dtype)
                o_all = jnp.where(valid, o, o_all)
            o_ref[rows, sl] = _head_out(o_all, gate[:, sl], ng).astype(o_ref.dtype)
        return carry

    lax.fori_loop(0, q_ref.shape[0] // c, group, 0)


def _hgrn_sample(proj, lb, norm_g, s0, batch, seq, heads):
    assert SUBLANES % seq == 0
    hb = _tile(heads, 4, 1)
    nb = _tile(batch, 8, SUBLANES // seq)
    w = hb * HEAD_DIM
    nhb = heads // hb
    r = nb * seq

    def col(seg):
        return pl.BlockSpec((r, w), lambda b, h: (b, seg * nhb + h))

    st_spec = pl.BlockSpec((nb, hb, HEAD_DIM, HEAD_DIM), lambda b, h: (b, h, 0, 0))
    return pl.pallas_call(
        functools.partial(_hgrn_sample_kernel, hb=hb, seq=seq),
        grid=(batch // nb, nhb),
        in_specs=[col(0), col(1), col(2), col(3),
                  pl.BlockSpec((1, w), lambda b, h: (0, h)),
                  pl.BlockSpec((1, HEAD_DIM), lambda b, h: (0, 0)),
                  st_spec],
        out_specs=[pl.BlockSpec((r, w), lambda b, h: (b, h)), st_spec],
        out_shape=[jax.ShapeDtypeStruct((batch * seq, heads * HEAD_DIM), BF16),
                   jax.ShapeDtypeStruct(s0.shape, s0.dtype)],
        compiler_params=_cparams("parallel", "parallel"),
        name="hgrn_sample",
    )(proj, proj, proj, proj, lb, norm_g, s0)


def _gdn_gates_kernel(x_ref, alog_ref, dtb_ref, beta_ref, gam_ref, *, nv, chunk):
    x = x_ref[...]
    r = x.shape[0]
    beta_ref[...] = _sigmoid(x[:, :nv])
    z = x[:, nv:] + dtb_ref[...]
    softplus = jnp.maximum(z, 0.0) + jnp.log(1.0 + jnp.exp(-jnp.abs(z)))
    g = -jnp.exp(alog_ref[...]) * softplus
    row = lax.broadcasted_iota(jnp.int32, (r, r), 0)
    col = lax.broadcasted_iota(jnp.int32, (r, r), 1)
    m = (col <= row) & ((row // chunk) == (col // chunk))
    gam_ref[...] = _mask_dot(jnp.where(m, 1.0, 0.0).astype(BF16), g)


def _gdn_gates(x, a_log, dt_bias, chunk):
    m, two_nv = x.shape
    nv = two_nv // 2
    tr = _tile(m, 256, max(chunk, SUBLANES))
    out = jax.ShapeDtypeStruct((m, nv), F32)
    return pl.pallas_call(
        functools.partial(_gdn_gates_kernel, nv=nv, chunk=chunk),
        grid=(m // tr,),
        in_specs=[pl.BlockSpec((tr, two_nv), lambda i: (i, 0)),
                  pl.BlockSpec((1, nv), lambda i: (0, 0)),
                  pl.BlockSpec((1, nv), lambda i: (0, 0))],
        out_specs=[pl.BlockSpec((tr, nv), lambda i: (i, 0))] * 2,
        out_shape=[out, out],
        compiler_params=_cparams("parallel"),
        name="gdn_gates",
    )(x, a_log, dt_bias)


def _conv_finish(y, kind_q, kind_k, o_ref, scale):
    y = _silu(y)
    w = y.shape[1]

    def l2(mult):
        for h in range(w // HEAD_DIM):
            sl = slice(h * HEAD_DIM, (h + 1) * HEAD_DIM)
            yh = y[:, sl]
            o_ref[:, sl] = yh * (lax.rsqrt(jnp.sum(yh * yh, axis=-1, keepdims=True) + NORM_EPS) * mult)

    @pl.when(kind_q)
    def _():
        l2(scale)

    @pl.when(kind_k)
    def _():
        l2(1.0)

    @pl.when(jnp.logical_not(kind_q | kind_k))
    def _():
        o_ref[...] = y


def _conv_prompt_kernel(cur_ref, prev_ref, buf_ref, cw_ref, o_ref, *, tiles_per_seq, nq_tiles, scale):
    i = pl.program_id(0)
    j = pl.program_id(1)
    first = (i % tiles_per_seq) == 0
    cur = cur_ref[...]
    tt = cur.shape[0]
    hist = jnp.where(first, buf_ref[...], prev_ref[...])
    x = jnp.concatenate([hist, cur], axis=0)
    cw = cw_ref[...]
    y = cw[CONV_W - 1:CONV_W] * cur
    for s in range(1, CONV_W):
        shifted = pltpu.roll(x, s, axis=0)[SUBLANES:SUBLANES + tt]
        y = y + cw[CONV_W - 1 - s:CONV_W - s] * shifted
    _conv_finish(y, j < nq_tiles, (j >= nq_tiles) & (j < 2 * nq_tiles), o_ref, scale)


def _conv_prompt(proj, col0, conv_buf8, conv_w, batch, seq, qk_w, ch):
    wc = _tile(qk_w, 512, HEAD_DIM)
    tt = _tile(seq, 256, SUBLANES)
    nt = seq // tt
    c0 = col0 // wc
    assert col0 % wc == 0
    return pl.pallas_call(
        functools.partial(_conv_prompt_kernel, tiles_per_seq=nt, nq_tiles=qk_w // wc,
                          scale=float(HEAD_DIM) ** -0.5),
        grid=(batch * nt, ch // wc),
        in_specs=[pl.BlockSpec((tt, wc), lambda i, j: (i, c0 + j)),
                  pl.BlockSpec((SUBLANES, wc),
                               lambda i, j: (jnp.maximum(i * (tt // SUBLANES) - 1, 0), c0 + j)),
                  pl.BlockSpec((SUBLANES, wc), lambda i, j: (i // nt, j)),
                  pl.BlockSpec((None, CONV_W, wc), lambda i, j: (0, 0, j))],
        out_specs=pl.BlockSpec((tt, wc), lambda i, j: (i, j)),
        out_shape=jax.ShapeDtypeStruct((batch * seq, ch), F32),
        compiler_params=_cparams("parallel", "parallel"),
        name="gdn_conv_prompt",
    )(proj, proj, conv_buf8, conv_w)


def _conv_sample_kernel(u_ref, buf_ref, cw_ref, o_ref, *, seq, nq_tiles, scale):
    j = pl.program_id(1)
    u = u_ref[...]
    buf = buf_ref[...]
    r = u.shape[0]
    tok = lax.broadcasted_iota(jnp.int32, (r, 1), 0) % seq
    cw = cw_ref[...]
    y = cw[CONV_W - 1:CONV_W] * u
    for s in range(1, CONV_W):
        from_u = pltpu.roll(u, s, axis=0)
        up = CONV_W - 1 - s
        from_buf = pltpu.roll(buf, r - up, axis=0) if up else buf
        y = y + cw[CONV_W - 1 - s:CONV_W - s] * jnp.where(tok >= s, from_u, from_buf)
    _conv_finish(y, j < nq_tiles, (j >= nq_tiles) & (j < 2 * nq_tiles), o_ref, scale)


def _conv_sample(proj, col0, buf_rows, conv_w, seq, qk_w, ch):
    m = proj.shape[0]
    wc = _tile(qk_w, 512, HEAD_DIM)
    tr = _tile(m, 256, SUBLANES)
    c0 = col0 // wc
    assert col0 % wc == 0 and tr % seq == 0 and seq >= CONV_W - 1
    return pl.pallas_call(
        functools.partial(_conv_sample_kernel, seq=seq, nq_tiles=qk_w // wc,
                          scale=float(HEAD_DIM) ** -0.5),
        grid=(m // tr, ch // wc),
        in_specs=[pl.BlockSpec((tr, wc), lambda i, j: (i, c0 + j)),
                  pl.BlockSpec((tr, wc), lambda i, j: (i, j)),
                  pl.BlockSpec((None, CONV_W, wc), lambda i, j: (0, 0, j))],
        out_specs=pl.BlockSpec((tr, wc), lambda i, j: (i, j)),
        out_shape=jax.ShapeDtypeStruct((m, ch), F32),
        compiler_params=_cparams("parallel", "parallel"),
        name="gdn_conv_sample",
    )(proj, buf_rows, conv_w)


def _gdn_pair_products(q, k):
    return _bdot_nt(k, k), _bdot_nt(q, k)


def _gdn_chunk(q, k, v, kk, qk, beta, gcol, grow, s, incl, strict):
    c = q.shape[0]
    decay = jnp.where(incl, jnp.exp(gcol - grow), 0.0)
    a = jnp.where(strict, beta * kk * decay, 0.0)
    egc = jnp.exp(gcol)
    x = jnp.concatenate([v * beta, k * (beta * egc)], axis=1)
    p = -a
    steps = c.bit_length() - 1
    for lvl in range(steps):
        x = x + _fdot(p, x)
        if lvl + 1 < steps:
            p = _fdot(p, p)
    u = x[:, :HEAD_DIM]
    w = x[:, HEAD_DIM:]
    v_new = u - _bdot(w, s)
    o = _bdot(q * egc, s) + _bdot(qk * decay, v_new)
    g_last = gcol[c - 1:c]
    kd = k * jnp.exp(g_last - gcol)
    s_new = jnp.exp(g_last) * s + _bdot(_transpose_pad(kd)[:, :c], v_new)
    return o, s_new


def _gdn_prompt_kernel(q_ref, k_ref, v_ref, z_ref, bc_ref, gc_ref, gr_ref, ng_ref,
                       o_ref, st_ref, s_scr, *, hb, rep):
    t = pl.program_id(2)
    c = GDN_CHUNK

    @pl.when(t == 0)
    def _():
        s_scr[...] = jnp.zeros_like(s_scr)

    incl, strict = _tri_masks(c)
    ng = ng_ref[...]

    def chunk(ci, carry):
        start = pl.multiple_of(ci * c, c)
        rows = pl.ds(start, c)
        bc = bc_ref[rows, :]
        gc = gc_ref[rows, :]
        gr = gr_ref[ci]
        for pr in range(hb // rep):
            psl = slice(pr * HEAD_DIM, (pr + 1) * HEAD_DIM)
            q = q_ref[rows, psl]
            k = k_ref[rows, psl]
            kk, qk = _gdn_pair_products(q, k)
            for e in range(rep):
                h = pr * rep + e
                sl = slice(h * HEAD_DIM, (h + 1) * HEAD_DIM)
                o, s_new = _gdn_chunk(q, k, v_ref[rows, sl], kk, qk, bc[:, h:h + 1], gc[:, h:h + 1],
                                      gr[h:h + 1, :], s_scr[h], incl, strict)
                s_scr[h] = s_new
                o_ref[rows, sl] = _head_out(o, z_ref[rows, sl], ng).astype(o_ref.dtype)
        return carry

    lax.fori_loop(0, q_ref.shape[0] // c, chunk, 0)

    @pl.when(t == pl.num_programs(2) - 1)
    def _():
        st_ref[...] = s_scr[...].astype(st_ref.dtype)


def _gdn_layouts(beta, gam, nv, hb, c):
    m = beta.shape[0]
    nhb = nv // hb
    as_cols = lambda a: a.reshape(m, nhb, hb).transpose(1, 0, 2)
    return as_cols(beta), as_cols(gam), gam.reshape(m // c, c, nhb, hb).transpose(2, 0, 3, 1)


def _gdn_prompt(qkv, proj, z_col0, beta, gam, norm_g, batch, seq, qk_heads, v_heads, state_dtype):
    rep = v_heads // qk_heads
    hb = _tile(v_heads, 4, rep)
    tt = _tile(seq, 256, 128)
    nt = seq // tt
    nhb = v_heads // hb
    wq = (hb // rep) * HEAD_DIM
    wv = hb * HEAD_DIM
    qk_w = qk_heads * HEAD_DIM
    bc, gc, gr = _gdn_layouts(beta, gam, v_heads, hb, GDN_CHUNK)
    k0 = qk_w // wq
    v0 = 2 * qk_w // wv
    z0 = z_col0 // wv
    assert z_col0 % wv == 0
    return pl.pallas_call(
        functools.partial(_gdn_prompt_kernel, hb=hb, rep=rep),
        grid=(batch, nhb, nt),
        in_specs=[pl.BlockSpec((tt, wq), lambda b, h, t: (b * nt + t, h)),
                  pl.BlockSpec((tt, wq), lambda b, h, t: (b * nt + t, k0 + h)),
                  pl.BlockSpec((tt, wv), lambda b, h, t: (b * nt + t, v0 + h)),
                  pl.BlockSpec((tt, wv), lambda b, h, t: (b * nt + t, z0 + h)),
                  pl.BlockSpec((None, tt, hb), lambda b, h, t: (h, b * nt + t, 0)),
                  pl.BlockSpec((None, tt, hb), lambda b, h, t: (h, b * nt + t, 0)),
                  pl.BlockSpec((None, tt // GDN_CHUNK, hb, GDN_CHUNK), lambda b, h, t: (h, b * nt + t, 0, 0)),
                  pl.BlockSpec((1, HEAD_DIM), lambda b, h, t: (0, 0))],
        out_specs=[pl.BlockSpec((tt, wv), lambda b, h, t: (b * nt + t, h)),
                   pl.BlockSpec((None, hb, HEAD_DIM, HEAD_DIM), lambda b, h, t: (b, h, 0, 0))],
        out_shape=[jax.ShapeDtypeStruct((batch * seq, v_heads * HEAD_DIM), BF16),
                   jax.ShapeDtypeStruct((batch, v_heads, HEAD_DIM, HEAD_DIM), state_dtype)],
        scratch_shapes=[pltpu.VMEM((hb, HEAD_DIM, HEAD_DIM), F32)],
        compiler_params=_cparams("parallel", "parallel", "arbitrary"),
        name="gdn_prompt",
    )(qkv, qkv, qkv, proj, bc, gc, gr, norm_g)


def _gdn_sample_kernel(q_ref, k_ref, v_ref, z_ref, bc_ref, gc_ref, gr_ref, ng_ref, s0_ref,
                       o_ref, st_ref, *, hb, rep, seq):
    c = SUBLANES
    per_group = c // seq
    incl, strict = _tri_masks(c)
    ng = ng_ref[...]
    row = lax.broadcasted_iota(jnp.int32, (c, 1), 0)
    lane = lax.broadcasted_iota(jnp.int32, (1, c), 1)

    def group(gi, carry):
        start = pl.multiple_of(gi * c, c)
        rows = pl.ds(start, c)
        bc = bc_ref[rows, :]
        gc = gc_ref[rows, :]
        gr = gr_ref[gi]
        for pr in range(hb // rep):
            psl = slice(pr * HEAD_DIM, (pr + 1) * HEAD_DIM)
            q_all = q_ref[rows, psl]
            k_all = k_ref[rows, psl]
            o_all = [jnp.zeros((c, HEAD_DIM), F32) for _ in range(rep)]
            for p in range(per_group):
                lo, hi = p * seq, (p + 1) * seq
                valid = (row >= lo) & (row < hi)
                valid_l = (lane >= lo) & (lane < hi)
                q = jnp.where(valid, q_all, 0.0)
                k = jnp.where(valid, k_all, 0.0)
                kk, qk = _gdn_pair_products(q, k)
                b = gi * per_group + p
                for e in range(rep):
                    h = pr * rep + e
                    sl = slice(h * HEAD_DIM, (h + 1) * HEAD_DIM)
                    beta = jnp.where(valid, bc[:, h:h + 1], 0.0)
                    gcol = jnp.where(row < lo, 0.0, jnp.where(valid, gc[:, h:h + 1], gc[hi - 1:hi, h:h + 1]))
                    grow = jnp.where(lane < lo, 0.0, jnp.where(valid_l, gr[h:h + 1, :], gr[h:h + 1, hi - 1:hi]))
                    o, s_new = _gdn_chunk(q, k, v_ref[rows, sl], kk, qk, beta, gcol, grow,
                                          s0_ref[b, h].astype(F32), incl, strict)
                    st_ref[b, h] = s_new.astype(st_ref.dtype)
                    o_all[e] = jnp.where(valid, o, o_all[e])
            for e in range(rep):
                h = pr * rep + e
                sl = slice(h * HEAD_DIM, (h + 1) * HEAD_DIM)
                o_ref[rows, sl] = _head_out(o_all[e], z_ref[rows, sl], ng).astype(o_ref.dtype)
        return carry

    lax.fori_loop(0, q_ref.shape[0] // c, group, 0)


def _gdn_sample(qkv, proj, z_col0, beta, gam, norm_g, s0, batch, seq, qk_heads, v_heads):
    rep = v_heads // qk_heads
    hb = _tile(v_heads, 4, rep)
    nb = _tile(batch, 8, SUBLANES // seq)
    r = nb * seq
    nhb = v_heads // hb
    wq = (hb // rep) * HEAD_DIM
    wv = hb * HEAD_DIM
    qk_w = qk_heads * HEAD_DIM
    bc, gc, gr = _gdn_layouts(beta, gam, v_heads, hb, SUBLANES)
    k0 = qk_w // wq
    v0 = 2 * qk_w // wv
    z0 = z_col0 // wv
    assert z_col0 % wv == 0
    st_spec = pl.BlockSpec((nb, hb, HEAD_DIM, HEAD_DIM), lambda b, h: (b, h, 0, 0))
    return pl.pallas_call(
        functools.partial(_gdn_sample_kernel, hb=hb, rep=rep, seq=seq),
        grid=(batch // nb, nhb),
        in_specs=[pl.BlockSpec((r, wq), lambda b, h: (b, h)),
                  pl.BlockSpec((r, wq), lambda b, h: (b, k0 + h)),
                  pl.BlockSpec((r, wv), lambda b, h: (b, v0 + h)),
                  pl.BlockSpec((r, wv), lambda b, h: (b, z0 + h)),
                  pl.BlockSpec((None, r, hb), lambda b, h: (h, b, 0)),
                  pl.BlockSpec((None, r, hb), lambda b, h: (h, b, 0)),
                  pl.BlockSpec((None, r // SUBLANES, hb, SUBLANES), lambda b, h: (h, b, 0, 0)),
                  pl.BlockSpec((1, HEAD_DIM), lambda b, h: (0, 0)),
                  st_spec],
        out_specs=[pl.BlockSpec((r, wv), lambda b, h: (b, h)), st_spec],
        out_shape=[jax.ShapeDtypeStruct((batch * seq, v_heads * HEAD_DIM), BF16),
                   jax.ShapeDtypeStruct(s0.shape, s0.dtype)],
        compiler_params=_cparams("parallel", "parallel"),
        name="gdn_sample",
    )(qkv, qkv, qkv, proj, bc, gc, gr, norm_g, s0)


def _merge_kernel(oh_ref, og_ref, wa_ref, wb_ref, ga_ref, gb_ref, o_ref):
    ya = _bdot(oh_ref[...], wa_ref[...])
    yb = _bdot(og_ref[...], wb_ref[...])
    o_ref[...] = (_sigmoid(ga_ref[...]) * ya + _sigmoid(gb_ref[...]) * yb).astype(o_ref.dtype)


def _merge(o_hg, o_gdn, w_a, w_b, gates, tm):
    m, ka = o_hg.shape
    kb = o_gdn.shape[1]
    d = w_a.shape[-1]
    tn = _tile(d, 256, 128)
    nj = d // tn
    return pl.pallas_call(
        _merge_kernel,
        grid=(m // tm, nj),
        in_specs=[pl.BlockSpec((tm, ka), lambda i, j: (i, 0)),
                  pl.BlockSpec((tm, kb), lambda i, j: (i, 0)),
                  pl.BlockSpec((None, ka, tn), lambda i, j: (0, 0, j)),
                  pl.BlockSpec((None, kb, tn), lambda i, j: (0, 0, j)),
                  pl.BlockSpec((tm, tn), lambda i, j: (i, j)),
                  pl.BlockSpec((tm, tn), lambda i, j: (i, nj + j))],
        out_specs=pl.BlockSpec((tm, tn), lambda i, j: (i, j)),
        out_shape=jax.ShapeDtypeStruct((m, d), BF16),
        compiler_params=_cparams("parallel", "arbitrary"),
        name="merge",
    )(o_hg, o_gdn, w_a, w_b, gates, gates)


def _outproj_kernel(m_ref, w_ref, x_ref, g1_ref, sc_ref, sh_ref, ng_ref, h_ref, a_ref, acc_ref):
    kk = pl.program_id(1)

    @pl.when(kk == 0)
    def _():
        acc_ref[...] = jnp.zeros_like(acc_ref)

    acc_ref[...] += _bdot(m_ref[...], w_ref[...])

    @pl.when(kk == pl.num_programs(1) - 1)
    def _():
        ng = ng_ref[...]
        h = x_ref[...].astype(F32) + g1_ref[...] * (_rms(acc_ref[...]) * ng[0:1])
        h_ref[...] = h
        a_ref[...] = ((_rms(h) * ng[1:2]) * (1.0 + sc_ref[...]) + sh_ref[...]).astype(a_ref.dtype)


def _outproj(merged, w_o, x, mod, norm_g2, rows_per_seq, tm):
    m, d = x.shape
    kdim = merged.shape[1]
    tk = _tile(kdim, 512, 128)
    row = lambda i, k: (i, 0)
    return pl.pallas_call(
        _outproj_kernel,
        grid=(m // tm, kdim // tk),
        in_specs=[pl.BlockSpec((tm, tk), lambda i, k: (i, k)),
                  pl.BlockSpec((None, tk, d), lambda i, k: (0, k, 0)),
                  pl.BlockSpec((tm, d), row),
                  _mod_spec(mod, 2, tm, rows_per_seq),
                  _mod_spec(mod, 4, tm, rows_per_seq),
                  _mod_spec(mod, 3, tm, rows_per_seq),
                  pl.BlockSpec((2, d), lambda i, k: (0, 0))],
        out_specs=[pl.BlockSpec((tm, d), row), pl.BlockSpec((tm, d), row)],
        out_shape=[jax.ShapeDtypeStruct((m, d), F32), jax.ShapeDtypeStruct((m, d), BF16)],
        scratch_shapes=[pltpu.VMEM((tm, d), F32)],
        compiler_params=_cparams("parallel", "arbitrary"),
        name="out_proj",
    )(merged, w_o, x, mod, mod, mod, norm_g2)


def _ffn_kernel(a_ref, wu_ref, wd_ref, h_ref, g2_ref, ng_ref, y_ref, acc_ref):
    f = pl.program_id(1)

    @pl.when(f == 0)
    def _():
        acc_ref[...] = jnp.zeros_like(acc_ref)

    hid = jnp.square(jnp.maximum(_bdot(a_ref[...], wu_ref[...]), 0.0))
    acc_ref[...] += _bdot(hid, wd_ref[...])

    @pl.when(f == pl.num_programs(1) - 1)
    def _():
        y_ref[...] = (h_ref[...] + g2_ref[...] * (_rms(acc_ref[...]) * ng_ref[...])).astype(y_ref.dtype)


def _ffn(a, w_up, w_down, h, mod, norm_g, rows_per_seq, tm, out_dtype):
    m, d = h.shape
    dff = w_up.shape[-1]
    tf = _tile(dff, 512, 128)
    row = lambda i, f: (i, 0)
    return pl.pallas_call(
        _ffn_kernel,
        grid=(m // tm, dff // tf),
        in_specs=[pl.BlockSpec((tm, d), row),
                  pl.BlockSpec((None, d, tf), lambda i, f: (0, 0, f)),
                  pl.BlockSpec((None, tf, d), lambda i, f: (0, f, 0)),
                  pl.BlockSpec((tm, d), row),
                  _mod_spec(mod, 5, tm, rows_per_seq),
                  pl.BlockSpec((1, d), lambda i, f: (0, 0))],
        out_specs=pl.BlockSpec((tm, d), row),
        out_shape=jax.ShapeDtypeStruct((m, d), out_dtype),
        scratch_shapes=[pltpu.VMEM((tm, d), F32)],
        compiler_params=_cparams("parallel", "arbitrary"),
        name="ffn",
    )(a, w_up, w_down, h, mod, norm_g)


def _run_group(x3, mod, rows_per_seq_for_mod, conv_buf, s_hg, s_gdn, lb, p, prompt):
    batch, seq, d = x3.shape
    m = batch * seq
    x = x3.reshape(m, d)
    hg_heads, gv_heads = p["hg_heads"], p["gv_heads"]
    gq_heads = p["gq_heads"]
    hg_w = hg_heads * HEAD_DIM
    gq_w = gq_heads * HEAD_DIM
    gv_w = gv_heads * HEAD_DIM
    ch = 2 * gq_w + gv_w
    main = 4 * hg_w + ch + gv_w
    conv0 = 4 * hg_w
    z0 = conv0 + ch
    tm = _tile(m, 512, SUBLANES)
    if prompt:
        tm = _tile(seq, tm, SUBLANES)

    a1 = _prenorm(x, p["norm_g"][0:1], mod, 1, 0, rows_per_seq_for_mod, tm)
    proj = _project(a1, p["w_in"], main, tm, _tile(main, 512, 128))
    tail = _project(a1, p["w_tail"], p["w_tail"].shape[-1], tm, p["w_tail"].shape[-1])
    gates = _project(a1, p["w_gates"], 2 * d, tm, _tile(2 * d, 512, 128))

    if prompt:
        o_hg, s_hg_new = _hgrn_prompt(proj, lb, p["hg_norm_g"], batch, seq, hg_heads, p["hg_dtype"])
        beta, gam = _gdn_gates(tail, p["a_log"], p["dt_bias"], GDN_CHUNK)
        buf8 = jnp.zeros((batch * SUBLANES, ch), F32)
        qkv = _conv_prompt(proj, conv0, buf8, p["conv_w"], batch, seq, gq_w, ch)
        o_gdn, s_gdn_new = _gdn_prompt(qkv, proj, z0, beta, gam, p["gdn_norm_g"], batch, seq,
                                       gq_heads, gv_heads, p["gdn_dtype"])
    else:
        o_hg, s_hg_new = _hgrn_sample(proj, lb, p["hg_norm_g"], s_hg, batch, seq, hg_heads)
        beta, gam = _gdn_gates(tail, p["a_log"], p["dt_bias"], seq)
        buf_rows = jnp.pad(conv_buf.astype(F32), ((0, 0), (0, seq - (CONV_W - 1)), (0, 0))).reshape(m, ch)
        qkv = _conv_sample(proj, conv0, buf_rows, p["conv_w"], seq, gq_w, ch)
        o_gdn, s_gdn_new = _gdn_sample(qkv, proj, z0, beta, gam, p["gdn_norm_g"], s_gdn, batch, seq,
                                       gq_heads, gv_heads)

    u = proj[:, conv0:conv0 + ch].reshape(batch, seq, ch)
    conv_new = u[:, seq - (CONV_W - 1):]

    merged = _merge(o_hg, o_gdn, p["w_out_hg"], p["w_out_gdn"], gates, tm)
    h1, a2 = _outproj(merged, p["w_o"], x, mod, p["norm_g"][1:3], rows_per_seq_for_mod, tm)
    y = _ffn(a2, p["w_up"], p["w_down"], h1, mod, p["norm_g"][3:4], rows_per_seq_for_mod, tm, x3.dtype)
    return y.reshape(batch, seq, d), conv_new, s_hg_new, s_gdn_new


def kernel(x_prompt, x_sample, state_hgrn, state_gdn, cache_conv, c_prompt, c_sample, lb_logits, w_ada,
           b_ada, norm_g, w_in, conv_w, A_log, dt_bias, hg_norm_g, gdn_norm_g, w_out_hg, w_out_gdn, w_o,
           w_up, w_down):
    depth = w_in.shape[0]
    assert depth == 1, "single-layer trunk"
    bp, tp, d = x_prompt.shape
    bs, ts, _ = x_sample.shape
    hg_heads = state_hgrn.shape[2]
    gv_heads = state_gdn.shape[2]
    ch = cache_conv.shape[-1]
    gq_heads = (ch // HEAD_DIM - gv_heads) // 2
    assert tp >= CONV_W - 1 and ts >= CONV_W - 1
    assert tp % GDN_CHUNK == 0 and tp % HG_CHUNK == 0

    main = 4 * hg_heads * HEAD_DIM + ch + gv_heads * HEAD_DIM
    p = dict(
        hg_heads=hg_heads, gv_heads=gv_heads, gq_heads=gq_heads,
        hg_dtype=state_hgrn.dtype, gdn_dtype=state_gdn.dtype,
        norm_g=norm_g[0].astype(F32), w_in=w_in,
        w_tail=w_in[:, :, main:main + 2 * gv_heads], w_gates=w_in[:, :, main + 2 * gv_heads:],
        conv_w=conv_w.astype(F32), a_log=A_log.astype(F32), dt_bias=dt_bias.astype(F32),
        hg_norm_g=hg_norm_g.astype(F32), gdn_norm_g=gdn_norm_g.astype(F32),
        w_out_hg=w_out_hg, w_out_gdn=w_out_gdn, w_o=w_o, w_up=w_up, w_down=w_down,
    )

    lbs = jnp.cumsum(jax.nn.softmax(lb_logits.astype(F32), axis=0), axis=0)
    lb = lbs[0:1]

    n_c = bp + bs
    pad = (-n_c) % SUBLANES
    c_all = jnp.concatenate([c_prompt.astype(F32), c_sample.astype(F32), jnp.zeros((pad, d), F32)], axis=0)
    mod_all = _ada(c_all, w_ada, b_ada.astype(F32))
    mod_p = mod_all[:bp].reshape(bp * 6, 1, d)
    mod_s = jnp.repeat(mod_all[bp:bp + bs].reshape(bs, 6, d), ts, axis=0).transpose(1, 0, 2)

    y_p, conv_p, hg_p, gdn_p = _run_group(x_prompt, mod_p, tp, None, None, None, lb, p, prompt=True)
    y_s, conv_s, hg_s, gdn_s = _run_group(x_sample, mod_s, ts, cache_conv[0], state_hgrn[0], state_gdn[0],
                                          lb, p, prompt=False)
    return (y_p, y_s,
            hg_p[None], gdn_p[None], conv_p.astype(cache_conv.dtype)[None],
            hg_s[None], gdn_s[None], conv_s.astype(cache_conv.dtype)[None])
```

```python
import functools

import jax
import jax.numpy as jnp
from jax import lax
from jax.experimental import pallas as pl
from jax.experimental.pallas import tpu as pltpu

F32 = jnp.float32
BF16 = jnp.bfloat16
HIGHEST = lax.Precision.HIGHEST

HEAD_DIM = 128
HG_CHUNK = 32
GDN_CHUNK = 64
CONV_W = 4
NORM_EPS = 1e-6
SUBLANES = 8
VMEM_LIMIT = 48 * 1024 * 1024


def _cparams(*sem):
    return pltpu.CompilerParams(dimension_semantics=sem, vmem_limit_bytes=VMEM_LIMIT)


def _tile(n, pref, quantum):
    if n <= pref:
        return n
    t = (pref // quantum) * quantum
    while t > quantum and n % t:
        t -= quantum
    assert n % t == 0, (n, pref, quantum)
    return t


def _bdot(a, b):
    return jnp.dot(a.astype(BF16), b.astype(BF16), preferred_element_type=F32)


def _bdot_nt(a, b):
    return lax.dot_general(a.astype(BF16), b.astype(BF16), (((1,), (1,)), ((), ())),
                           preferred_element_type=F32)


def _fdot(a, b):
    return jnp.dot(a, b, precision=HIGHEST, preferred_element_type=F32)


def _split3(x):
    hi = x.astype(BF16)
    r = x - hi.astype(F32)
    mid = r.astype(BF16)
    lo = (r - mid.astype(F32)).astype(BF16)
    return hi, mid, lo


def _mask_dot(m_bf, x):
    hi, mid, lo = _split3(x)
    return (jnp.dot(m_bf, hi, preferred_element_type=F32)
            + jnp.dot(m_bf, mid, preferred_element_type=F32)
            + jnp.dot(m_bf, lo, preferred_element_type=F32))


def _sigmoid(x):
    return 1.0 / (1.0 + jnp.exp(-x))


def _silu(x):
    return x * _sigmoid(x)


def _rms(x):
    return x * lax.rsqrt(jnp.mean(x * x, axis=-1, keepdims=True) + NORM_EPS)


def _transpose_pad(x):
    r = x.shape[0]
    if r < HEAD_DIM:
        x = jnp.concatenate([x, jnp.zeros((HEAD_DIM - r, HEAD_DIM), x.dtype)], axis=0)
    return x.T


def _tri_masks(c):
    row = lax.broadcasted_iota(jnp.int32, (c, c), 0)
    col = lax.broadcasted_iota(jnp.int32, (c, c), 1)
    return col <= row, col < row


def _ada_kernel(c_ref, w_ref, b_ref, o_ref):
    cs = _silu(c_ref[...])
    w = w_ref[...]
    c_hi = cs.astype(BF16)
    c_lo = (cs - c_hi.astype(F32)).astype(BF16)
    w_hi = w.astype(BF16)
    w_lo = (w - w_hi.astype(F32)).astype(BF16)
    acc = jnp.dot(c_hi, w_hi, preferred_element_type=F32)
    acc += jnp.dot(c_lo, w_hi, preferred_element_type=F32)
    acc += jnp.dot(c_hi, w_lo, preferred_element_type=F32)
    o_ref[...] = acc + b_ref[...]


def _ada(c_all, w_ada, b_ada):
    m, d = c_all.shape
    n = w_ada.shape[-1]
    tn = _tile(n, 512, 128)
    return pl.pallas_call(
        _ada_kernel,
        grid=(n // tn,),
        in_specs=[pl.BlockSpec((m, d), lambda j: (0, 0)),
                  pl.BlockSpec((None, d, tn), lambda j: (0, 0, j)),
                  pl.BlockSpec((1, tn), lambda j: (0, j))],
        out_specs=pl.BlockSpec((m, tn), lambda j: (0, j)),
        out_shape=jax.ShapeDtypeStruct((m, n), F32),
        compiler_params=_cparams("arbitrary"),
        name="ada_mod",
    )(c_all, w_ada, b_ada)


def _mod_spec(mod, k, tm, rows_per_seq):
    d = mod.shape[-1]
    if mod.shape[1] == 1:
        tiles_per_seq = rows_per_seq // tm
        return pl.BlockSpec((None, 1, d), lambda i, *_: ((i // tiles_per_seq) * 6 + k, 0, 0))
    return pl.BlockSpec((None, tm, d), lambda i, *_: (k, i, 0))


def _prenorm_kernel(x_ref, g_ref, sc_ref, sh_ref, o_ref):
    xn = _rms(x_ref[...].astype(F32)) * g_ref[...]
    o_ref[...] = (xn * (1.0 + sc_ref[...]) + sh_ref[...]).astype(o_ref.dtype)


def _prenorm(x, g, mod, k_scale, k_shift, rows_per_seq, tm):
    m, d = x.shape
    return pl.pallas_call(
        _prenorm_kernel,
        grid=(m // tm,),
        in_specs=[pl.BlockSpec((tm, d), lambda i: (i, 0)),
                  pl.BlockSpec((1, d), lambda i: (0, 0)),
                  _mod_spec(mod, k_scale, tm, rows_per_seq),
                  _mod_spec(mod, k_shift, tm, rows_per_seq)],
        out_specs=pl.BlockSpec((tm, d), lambda i: (i, 0)),
        out_shape=jax.ShapeDtypeStruct((m, d), BF16),
        compiler_params=_cparams("parallel"),
        name="prenorm",
    )(x, g, mod, mod)


def _mm_kernel(a_ref, w_ref, o_ref):
    o_ref[...] = _bdot(a_ref[...], w_ref[...])


def _project(a, w3, n_cols, tm, tn):
    m, kdim = a.shape
    return pl.pallas_call(
        _mm_kernel,
        grid=(m // tm, n_cols // tn),
        in_specs=[pl.BlockSpec((tm, kdim), lambda i, j: (i, 0)),
                  pl.BlockSpec((None, kdim, tn), lambda i, j: (0, 0, j))],
        out_specs=pl.BlockSpec((tm, tn), lambda i, j: (i, j)),
        out_shape=jax.ShapeDtypeStruct((m, n_cols), F32),
        compiler_params=_cparams("parallel", "arbitrary"),
        name="in_proj",
    )(a, w3)


def _hgrn_elementwise(hq, hf, lb):
    q = _silu(hq)
    f = lb + (1.0 - lb) * _sigmoid(hf)
    return q, jnp.log(f), 1.0 - f


def _hgrn_chunk_heads(q, g, k, inp, states, tril, tril_bf):
    c = q.shape[0]
    hb = len(states)
    heads = [slice(h * HEAD_DIM, (h + 1) * HEAD_DIM) for h in range(hb)]
    big_g = _mask_dot(tril_bf, g)
    g_ref = big_g[c // 2:c // 2 + 1]
    g_last = big_g[c - 1:c]
    qe = q * jnp.exp(big_g - g_ref)
    ke = k * jnp.exp(g_ref - big_g)
    qs = q * jnp.exp(big_g)
    kl = k * jnp.exp(g_last - big_g)
    d_hi, d_mid, d_lo = [p.astype(F32) for p in _split3(jnp.exp(g_last))]
    extra = jnp.concatenate([d_hi, d_mid, d_lo, jnp.zeros((SUBLANES - 3, q.shape[1]), F32)], axis=0)
    a = [jnp.where(tril, _bdot_nt(qe[:, sl], ke[:, sl]), 0.0) for sl in heads]
    o = [_bdot(a[h], inp[:, sl]) + _bdot(qs[:, sl], states[h]) for h, sl in enumerate(heads)]
    t = [_transpose_pad(jnp.concatenate([kl[:, sl], extra[:, sl]], axis=0))[:, :c + SUBLANES] for sl in heads]
    piece_rows = lax.broadcasted_iota(jnp.int32, (SUBLANES, HEAD_DIM), 0) < 3
    ones_rows = jnp.where(piece_rows, 1.0, 0.0)
    zeros_c = jnp.zeros((c, HEAD_DIM), F32)
    new_states = []
    for h, sl in enumerate(heads):
        rhs = jnp.concatenate([jnp.concatenate([inp[:, sl], zeros_c], axis=1),
                               jnp.concatenate([jnp.zeros((SUBLANES, HEAD_DIM), F32), ones_rows], axis=1)], axis=0)
        ud = _bdot(t[h], rhs)
        new_states.append(ud[:, HEAD_DIM:] * states[h] + ud[:, :HEAD_DIM])
    return list(zip(o, new_states))


def _head_out(o, gate, norm_g):
    return _rms(o) * norm_g * _silu(gate)


def _hgrn_prompt_kernel(q_ref, f_ref, i_ref, g_ref, lb_ref, ng_ref, o_ref, st_ref, s_scr, *, hb):
    t = pl.program_id(2)
    c = HG_CHUNK

    @pl.when(t == 0)
    def _():
        s_scr[...] = jnp.zeros_like(s_scr)

    tril, _ = _tri_masks(c)
    tril_bf = jnp.where(tril, 1.0, 0.0).astype(BF16)
    lb = lb_ref[...]
    ng = ng_ref[...]

    def chunk(ci, carry):
        rows = pl.ds(pl.multiple_of(ci * c, c), c)
        q, g, k = _hgrn_elementwise(q_ref[rows, :], f_ref[rows, :], lb)
        inp = i_ref[rows, :]
        gate = g_ref[rows, :]
        res = _hgrn_chunk_heads(q, g, k, inp, [s_scr[h] for h in range(hb)], tril, tril_bf)
        for h, (o, s_new) in enumerate(res):
            sl = slice(h * HEAD_DIM, (h + 1) * HEAD_DIM)
            s_scr[h] = s_new
            o_ref[rows, sl] = _head_out(o, gate[:, sl], ng).astype(o_ref.dtype)
        return carry

    lax.fori_loop(0, q_ref.shape[0] // c, chunk, 0)

    @pl.when(t == pl.num_programs(2) - 1)
    def _():
        st_ref[...] = s_scr[...].astype(st_ref.dtype)


def _hgrn_prompt(proj, lb, norm_g, batch, seq, heads, state_dtype):
    hb = _tile(heads, 8, 1)
    tt = _tile(seq, 256, HG_CHUNK)
    w = hb * HEAD_DIM
    nhb = heads // hb
    nt = seq // tt

    def col(seg):
        return pl.BlockSpec((tt, w), lambda b, h, t: (b * nt + t, seg * nhb + h))

    return pl.pallas_call(
        functools.partial(_hgrn_prompt_kernel, hb=hb),
        grid=(batch, nhb, nt),
        in_specs=[col(0), col(1), col(2), col(3),
                  pl.BlockSpec((1, w), lambda b, h, t: (0, h)),
                  pl.BlockSpec((1, HEAD_DIM), lambda b, h, t: (0, 0))],
        out_specs=[pl.BlockSpec((tt, w), lambda b, h, t: (b * nt + t, h)),
                   pl.BlockSpec((None, hb, HEAD_DIM, HEAD_DIM), lambda b, h, t: (b, h, 0, 0))],
        out_shape=[jax.ShapeDtypeStruct((batch * seq, heads * HEAD_DIM), BF16),
                   jax.ShapeDtypeStruct((batch, heads, HEAD_DIM, HEAD_DIM), state_dtype)],
        scratch_shapes=[pltpu.VMEM((hb, HEAD_DIM, HEAD_DIM), F32)],
        compiler_params=_cparams("parallel", "parallel", "arbitrary"),
        name="hgrn_prompt",
    )(proj, proj, proj, proj, lb, norm_g)


def _hgrn_sample_kernel(q_ref, f_ref, i_ref, g_ref, lb_ref, ng_ref, s0_ref, o_ref, st_ref, *, hb, seq):
    c = SUBLANES
    per_group = c // seq
    tril, _ = _tri_masks(c)
    tril_bf = jnp.where(tril, 1.0, 0.0).astype(BF16)
    lb = lb_ref[...]
    ng = ng_ref[...]
    row = lax.broadcasted_iota(jnp.int32, (c, 1), 0)

    def group(gi, carry):
        rows = pl.ds(pl.multiple_of(gi * c, c), c)
        q, g, k = _hgrn_elementwise(q_ref[rows, :], f_ref[rows, :], lb)
        inp = i_ref[rows, :]
        gate = g_ref[rows, :]
        valid = [(row >= p * seq) & (row < (p + 1) * seq) for p in range(per_group)]
        masked = lambda x: jnp.concatenate([jnp.where(v, x, 0.0) for v in valid], axis=1)
        states = [s0_ref[gi * per_group + p, h].astype(F32) for p in range(per_group) for h in range(hb)]
        res = _hgrn_chunk_heads(masked(q), masked(g), masked(k), jnp.concatenate([inp] * per_group, axis=1),
                                states, tril, tril_bf)
        for h in range(hb):
            sl = slice(h * HEAD_DIM, (h + 1) * HEAD_DIM)
            o_all = jnp.zeros((c, HEAD_DIM), F32)
            for p in range(per_group):
                o, s_new = res[p * hb + h]
                st_ref[gi * per_group + p, h] = s_new.astype(st_ref.dtype)
                o_all = jnp.where(valid[p], o, o_all)
            o_ref[rows, sl] = _head_out(o_all, gate[:, sl], ng).astype(o_ref.dtype)
        return carry

    lax.fori_loop(0, q_ref.shape[0] // c, group, 0)


def _hgrn_sample(proj, lb, norm_g, s0, batch, seq, heads):
    assert SUBLANES % seq == 0
    hb = _tile(heads, 4, 1)
    nb = _tile(batch, 8, SUBLANES // seq)
    w = hb * HEAD_DIM
    nhb = heads // hb
    r = nb * seq

    def col(seg):
        return pl.BlockSpec((r, w), lambda b, h: (b, seg * nhb + h))

    st_spec = pl.BlockSpec((nb, hb, HEAD_DIM, HEAD_DIM), lambda b, h: (b, h, 0, 0))
    return pl.pallas_call(
        functools.partial(_hgrn_sample_kernel, hb=hb, seq=seq),
        grid=(batch // nb, nhb),
        in_specs=[col(0), col(1), col(2), col(3),
                  pl.BlockSpec((1, w), lambda b, h: (0, h)),
                  pl.BlockSpec((1, HEAD_DIM), lambda b, h: (0, 0)),
                  st_spec],
        out_specs=[pl.BlockSpec((r, w), lambda b, h: (b, h)), st_spec],
        out_shape=[jax.ShapeDtypeStruct((batch * seq, heads * HEAD_DIM), BF16),
                   jax.ShapeDtypeStruct(s0.shape, s0.dtype)],
        compiler_params=_cparams("parallel", "parallel"),
        name="hgrn_sample",
    )(proj, proj, proj, proj, lb, norm_g, s0)


def _gdn_gates_kernel(x_ref, alog_ref, dtb_ref, beta_ref, gam_ref, gtot_ref, *, nv, chunk):
    x = x_ref[...]
    r = x.shape[0]
    beta_ref[...] = _sigmoid(x[:, :nv])
    z = x[:, nv:] + dtb_ref[...]
    softplus = jnp.maximum(z, 0.0) + jnp.log(1.0 + jnp.exp(-jnp.abs(z)))
    g = -jnp.exp(alog_ref[...]) * softplus
    row = lax.broadcasted_iota(jnp.int32, (r, r), 0)
    col = lax.broadcasted_iota(jnp.int32, (r, r), 1)
    same = (row // chunk) == (col // chunk)
    gam_ref[...] = _mask_dot(jnp.where(same & (col <= row), 1.0, 0.0).astype(BF16), g)
    gtot_ref[...] = _mask_dot(jnp.where(same, 1.0, 0.0).astype(BF16), g)


def _gdn_gates(x, a_log, dt_bias, chunk):
    m, two_nv = x.shape
    nv = two_nv // 2
    tr = _tile(m, 256, max(chunk, SUBLANES))
    out = jax.ShapeDtypeStruct((m, nv), F32)
    return pl.pallas_call(
        functools.partial(_gdn_gates_kernel, nv=nv, chunk=chunk),
        grid=(m // tr,),
        in_specs=[pl.BlockSpec((tr, two_nv), lambda i: (i, 0)),
                  pl.BlockSpec((1, nv), lambda i: (0, 0)),
                  pl.BlockSpec((1, nv), lambda i: (0, 0))],
        out_specs=[pl.BlockSpec((tr, nv), lambda i: (i, 0))] * 3,
        out_shape=[out, out, out],
        compiler_params=_cparams("parallel"),
        name="gdn_gates",
    )(x, a_log, dt_bias)


def _conv_finish(y, kind_q, kind_k, o_ref, scale):
    y = _silu(y)
    w = y.shape[1]

    def l2(mult):
        for h in range(w // HEAD_DIM):
            sl = slice(h * HEAD_DIM, (h + 1) * HEAD_DIM)
            yh = y[:, sl]
            o_ref[:, sl] = yh * (lax.rsqrt(jnp.sum(yh * yh, axis=-1, keepdims=True) + NORM_EPS) * mult)

    @pl.when(kind_q)
    def _():
        l2(scale)

    @pl.when(kind_k)
    def _():
        l2(1.0)

    @pl.when(jnp.logical_not(kind_q | kind_k))
    def _():
        o_ref[...] = y


def _conv_prompt_kernel(cur_ref, prev_ref, buf_ref, cw_ref, o_ref, *, tiles_per_seq, nq_tiles, scale):
    i = pl.program_id(0)
    j = pl.program_id(1)
    first = (i % tiles_per_seq) == 0
    cur = cur_ref[...]
    tt = cur.shape[0]
    hist = jnp.where(first, buf_ref[...], prev_ref[...])
    x = jnp.concatenate([hist, cur], axis=0)
    cw = cw_ref[...]
    y = cw[CONV_W - 1:CONV_W] * cur
    for s in range(1, CONV_W):
        shifted = pltpu.roll(x, s, axis=0)[SUBLANES:SUBLANES + tt]
        y = y + cw[CONV_W - 1 - s:CONV_W - s] * shifted
    _conv_finish(y, j < nq_tiles, (j >= nq_tiles) & (j < 2 * nq_tiles), o_ref, scale)


def _conv_prompt(proj, col0, conv_buf8, conv_w, batch, seq, qk_w, ch):
    wc = _tile(qk_w, 512, HEAD_DIM)
    tt = _tile(seq, 256, SUBLANES)
    nt = seq // tt
    c0 = col0 // wc
    assert col0 % wc == 0
    return pl.pallas_call(
        functools.partial(_conv_prompt_kernel, tiles_per_seq=nt, nq_tiles=qk_w // wc,
                          scale=float(HEAD_DIM) ** -0.5),
        grid=(batch * nt, ch // wc),
        in_specs=[pl.BlockSpec((tt, wc), lambda i, j: (i, c0 + j)),
                  pl.BlockSpec((SUBLANES, wc),
                               lambda i, j: (jnp.maximum(i * (tt // SUBLANES) - 1, 0), c0 + j)),
                  pl.BlockSpec((SUBLANES, wc), lambda i, j: (i // nt, j)),
                  pl.BlockSpec((None, CONV_W, wc), lambda i, j: (0, 0, j))],
        out_specs=pl.BlockSpec((tt, wc), lambda i, j: (i, j)),
        out_shape=jax.ShapeDtypeStruct((batch * seq, ch), F32),
        compiler_params=_cparams("parallel", "parallel"),
        name="gdn_conv_prompt",
    )(proj, proj, conv_buf8, conv_w)


def _conv_sample_kernel(u_ref, buf_ref, cw_ref, o_ref, *, seq, nq_tiles, scale):
    j = pl.program_id(1)
    u = u_ref[...]
    buf = buf_ref[...]
    r = u.shape[0]
    tok = lax.broadcasted_iota(jnp.int32, (r, 1), 0) % seq
    cw = cw_ref[...]
    y = cw[CONV_W - 1:CONV_W] * u
    for s in range(1, CONV_W):
        from_u = pltpu.roll(u, s, axis=0)
        up = CONV_W - 1 - s
        from_buf = pltpu.roll(buf, r - up, axis=0) if up else buf
        y = y + cw[CONV_W - 1 - s:CONV_W - s] * jnp.where(tok >= s, from_u, from_buf)
    _conv_finish(y, j < nq_tiles, (j >= nq_tiles) & (j < 2 * nq_tiles), o_ref, scale)


def _conv_sample(proj, col0, buf_rows, conv_w, seq, qk_w, ch):
    m = proj.shape[0]
    wc = _tile(qk_w, 512, HEAD_DIM)
    tr = _tile(m, 256, SUBLANES)
    c0 = col0 // wc
    assert col0 % wc == 0 and tr % seq == 0 and seq >= CONV_W - 1
    return pl.pallas_call(
        functools.partial(_conv_sample_kernel, seq=seq, nq_tiles=qk_w // wc,
                          scale=float(HEAD_DIM) ** -0.5),
        grid=(m // tr, ch // wc),
        in_specs=[pl.BlockSpec((tr, wc), lambda i, j: (i, c0 + j)),
                  pl.BlockSpec((tr, wc), lambda i, j: (i, j)),
                  pl.BlockSpec((None, CONV_W, wc), lambda i, j: (0, 0, j))],
        out_specs=pl.BlockSpec((tr, wc), lambda i, j: (i, j)),
        out_shape=jax.ShapeDtypeStruct((m, ch), F32),
        compiler_params=_cparams("parallel", "parallel"),
        name="gdn_conv_sample",
    )(proj, buf_rows, conv_w)


GDN_TILE = 64
HALF_LANES = 64


def _mm_tile(a, b):
    return _bdot(a[:, :GDN_TILE], b)


def _lane_bcast_all(x, sel):
    return _mask_dot_right(x, sel)


def _mask_dot_right(x, m_bf):
    hi, mid, lo = _split3(x)
    return (jnp.dot(hi, m_bf, preferred_element_type=F32)
            + jnp.dot(mid, m_bf, preferred_element_type=F32)
            + jnp.dot(lo, m_bf, preferred_element_type=F32))


def _head_select(hb):
    head = lax.broadcasted_iota(jnp.int32, (hb, hb * HEAD_DIM), 0)
    lane_head = lax.broadcasted_iota(jnp.int32, (hb, hb * HEAD_DIM), 1) // HEAD_DIM
    return jnp.where(head == lane_head, 1.0, 0.0).astype(BF16)


def _unit_lower_inverses(a_list, rows, cols, block):
    eye = jnp.where(rows == cols, 1.0, 0.0)
    pairs = (rows >> 1) == (cols >> 1)
    d = [eye - jnp.where(pairs, a, 0.0) for a in a_list]
    b = 2
    while b < block:
        sh = b.bit_length()
        coupling = ((rows >> sh) == (cols >> sh)) & ((rows >> (sh - 1)) != (cols >> (sh - 1)))
        m1 = [_mm_tile(jnp.where(coupling, a, 0.0), di) for a, di in zip(a_list, d)]
        m2 = [_mm_tile(di, mi) for di, mi in zip(d, m1)]
        d = [di - mi for di, mi in zip(d, m2)]
        b *= 2
    return d


def _gdn_pair_products(q, k):
    k_pad = jnp.concatenate([k, jnp.zeros_like(k)], axis=0)
    kq = _bdot_nt(jnp.concatenate([k, q], axis=0), k_pad)
    return kq[:GDN_TILE], kq[GDN_TILE:]


def _gdn_wy(qs, ks, vs, bc, gc, gr, rep, rows, cols, incl, strict, block):
    hb = len(vs)
    prods = [_gdn_pair_products(q, k) for q, k in zip(qs, ks)]
    decays, a_list = [], []
    for h in range(hb):
        sl = slice(h * HEAD_DIM, (h + 1) * HEAD_DIM)
        decay = jnp.where(incl, jnp.exp(gc[:, sl] - gr[h:h + 1, :]), 0.0)
        decays.append(decay)
        a_list.append(jnp.where(strict, bc[:, sl] * prods[h // rep][0] * decay, 0.0))
    t_inv = _unit_lower_inverses(a_list, rows, cols, block)
    xs = []
    for h in range(hb):
        sl = slice(h * HEAD_DIM, (h + 1) * HEAD_DIM)
        beta = bc[:, sl]
        rhs = jnp.concatenate([vs[h] * beta, ks[h // rep] * (beta * jnp.exp(gc[:, sl]))], axis=1)
        xs.append(_mm_tile(t_inv[h], rhs))
    return [(x[:, :HEAD_DIM], x[:, HEAD_DIM:], prods[h // rep][1] * decays[h]) for h, x in enumerate(xs)]


def _tile_iotas():
    shape = (GDN_TILE, HEAD_DIM)
    return lax.broadcasted_iota(jnp.int32, shape, 0), lax.broadcasted_iota(jnp.int32, shape, 1)


def _gdn_prompt_kernel(q_ref, k_ref, v_ref, z_ref, bc_ref, gc_ref, gt_ref, gr_ref, ng_ref,
                       o_ref, st_ref, s_scr, u_scr, w_scr, a_scr, *, hb, rep):
    t = pl.program_id(2)
    c = GDN_CHUNK
    n_chunks = q_ref.shape[0] // c

    @pl.when(t == 0)
    def _():
        s_scr[...] = jnp.zeros_like(s_scr)

    rows_i, cols_i = _tile_iotas()
    incl = cols_i <= rows_i
    strict = cols_i < rows_i
    ng = ng_ref[...]
    sel = _head_select(hb)

    def solve(ci, carry):
        rows = pl.ds(pl.multiple_of(ci * c, c), c)
        bc = _lane_bcast_all(bc_ref[rows, :], sel)
        gc = _lane_bcast_all(gc_ref[rows, :], sel)
        head = lambda ref, i: ref[rows, i * HEAD_DIM:(i + 1) * HEAD_DIM]
        wy = _gdn_wy([head(q_ref, i) for i in range(hb // rep)], [head(k_ref, i) for i in range(hb // rep)],
                     [head(v_ref, h) for h in range(hb)], bc, gc, gr_ref[ci], rep,
                     rows_i, cols_i, incl, strict, c)
        for h, (u, w, attn) in enumerate(wy):
            sl = slice(h * HEAD_DIM, (h + 1) * HEAD_DIM)
            u_scr[rows, sl] = u
            w_scr[rows, sl] = w.astype(w_scr.dtype)
            a_scr[rows, sl] = attn.astype(a_scr.dtype)
        return carry

    lax.fori_loop(0, n_chunks, solve, 0)

    def recur(ci, carry):
        rows = pl.ds(pl.multiple_of(ci * c, c), c)
        gc = _lane_bcast_all(gc_ref[rows, :], sel)
        gt = _lane_bcast_all(gt_ref[rows, :], sel)
        for pr in range(hb // rep):
            psl = slice(pr * HEAD_DIM, (pr + 1) * HEAD_DIM)
            k_t = _transpose_pad(k_ref[rows, psl])[:, :c]
            s_pair = s_scr[pr]
            qs = _bdot(q_ref[rows, psl], s_pair)
            v_scaled, decays = [], []
            for e in range(rep):
                h = pr * rep + e
                sl = slice(h * HEAD_DIM, (h + 1) * HEAD_DIM)
                esl = slice(e * HEAD_DIM, (e + 1) * HEAD_DIM)
                gcol = gc[:, sl]
                gtot = gt[:, sl]
                v_new = u_scr[rows, sl] - _bdot(w_scr[rows, sl], s_pair[:, esl])
                o = qs[:, esl] * jnp.exp(gcol) + _bdot(a_scr[rows, sl][:, :c], v_new)
                o_ref[rows, sl] = _head_out(o, z_ref[rows, sl], ng).astype(o_ref.dtype)
                v_scaled.append(v_new * jnp.exp(gtot - gcol))
                decays.append(jnp.exp(gtot[0:1]))
            s_scr[pr] = (jnp.concatenate(decays, axis=1) * s_pair
                         + _bdot(k_t, jnp.concatenate(v_scaled, axis=1)))
        return carry

    lax.fori_loop(0, n_chunks, recur, 0)

    @pl.when(t == pl.num_programs(2) - 1)
    def _():
        for h in range(hb):
            st_ref[h] = s_scr[h // rep][:, (h % rep) * HEAD_DIM:(h % rep + 1) * HEAD_DIM].astype(st_ref.dtype)


def _gdn_layouts(beta, gam, gtot, nv, hb):
    m = beta.shape[0]
    nhb = nv // hb
    as_cols = lambda a: a.reshape(m, nhb, hb).transpose(1, 0, 2)
    g_rows = gam.reshape(m // GDN_TILE, GDN_TILE, nhb, hb).transpose(2, 0, 3, 1)
    g_rows = jnp.pad(g_rows, ((0, 0), (0, 0), (0, 0), (0, HEAD_DIM - GDN_TILE)))
    return as_cols(beta), as_cols(gam), as_cols(gtot), g_rows


def _gdn_prompt(qkv, proj, z_col0, gate_arrays, norm_g, batch, seq, qk_heads, v_heads, state_dtype):
    rep = v_heads // qk_heads
    hb = _tile(v_heads, 8, rep)
    tt = _tile(seq, 256, GDN_CHUNK)
    nt = seq // tt
    nhb = v_heads // hb
    wq = (hb // rep) * HEAD_DIM
    wv = hb * HEAD_DIM
    qk_w = qk_heads * HEAD_DIM
    bc, gc, gt, gr = _gdn_layouts(*gate_arrays, v_heads, hb)
    k0 = qk_w // wq
    v0 = 2 * qk_w // wv
    z0 = z_col0 // wv
    assert z_col0 % wv == 0 and GDN_CHUNK == GDN_TILE
    col_spec = pl.BlockSpec((None, tt, hb), lambda b, h, t: (h, b * nt + t, 0))
    return pl.pallas_call(
        functools.partial(_gdn_prompt_kernel, hb=hb, rep=rep),
        grid=(batch, nhb, nt),
        in_specs=[pl.BlockSpec((tt, wq), lambda b, h, t: (b * nt + t, h)),
                  pl.BlockSpec((tt, wq), lambda b, h, t: (b * nt + t, k0 + h)),
                  pl.BlockSpec((tt, wv), lambda b, h, t: (b * nt + t, v0 + h)),
                  pl.BlockSpec((tt, wv), lambda b, h, t: (b * nt + t, z0 + h)),
                  col_spec, col_spec, col_spec,
                  pl.BlockSpec((None, tt // GDN_TILE, hb, HEAD_DIM), lambda b, h, t: (h, b * nt + t, 0, 0)),
                  pl.BlockSpec((1, HEAD_DIM), lambda b, h, t: (0, 0))],
        out_specs=[pl.BlockSpec((tt, wv), lambda b, h, t: (b * nt + t, h)),
                   pl.BlockSpec((None, hb, HEAD_DIM, HEAD_DIM), lambda b, h, t: (b, h, 0, 0))],
        out_shape=[jax.ShapeDtypeStruct((batch * seq, v_heads * HEAD_DIM), BF16),
                   jax.ShapeDtypeStruct((batch, v_heads, HEAD_DIM, HEAD_DIM), state_dtype)],
        scratch_shapes=[pltpu.VMEM((hb // rep, HEAD_DIM, rep * HEAD_DIM), F32),
                        pltpu.VMEM((tt, wv), F32),
                        pltpu.VMEM((tt, wv), BF16),
                        pltpu.VMEM((tt, wv), BF16)],
        compiler_params=_cparams("parallel", "parallel", "arbitrary"),
        name="gdn_prompt",
    )(qkv, qkv, qkv, proj, bc, gc, gt, gr, norm_g)


def _gdn_sample_kernel(q_ref, k_ref, v_ref, z_ref, bc_ref, gc_ref, gt_ref, gr_ref, ng_ref, s0_ref,
                       o_ref, st_ref, *, hb, rep, seq):
    c = GDN_TILE
    nseq = c // seq
    per_group = SUBLANES // seq
    lsh = seq.bit_length() - 1
    rows_i, cols_i = _tile_iotas()
    same = (rows_i >> lsh) == (cols_i >> lsh)
    incl = (cols_i <= rows_i) & same
    strict = (cols_i < rows_i) & same
    row8 = lax.broadcasted_iota(jnp.int32, (SUBLANES, 1), 0)
    lane_seq = lax.broadcasted_iota(jnp.int32, (1, HEAD_DIM), 1) >> lsh
    ng = ng_ref[...]
    sel = _head_select(hb)
    bc = _lane_bcast_all(bc_ref[...], sel)
    gc = _lane_bcast_all(gc_ref[...], sel)
    gt = _lane_bcast_all(gt_ref[...], sel)
    gr = gr_ref[0]
    head = lambda ref, i: ref[:, i * HEAD_DIM:(i + 1) * HEAD_DIM]
    qs = [head(q_ref, i) for i in range(hb // rep)]
    ks = [head(k_ref, i) for i in range(hb // rep)]
    wy = _gdn_wy(qs, ks, [head(v_ref, h) for h in range(hb)], bc, gc, gr, rep,
                 rows_i, cols_i, incl, strict, seq)
    k_t_seq = []
    for k in ks:
        k_t = _transpose_pad(k)
        k_t_seq.append(jnp.concatenate([jnp.where(lane_seq == b, k_t, 0.0)[:, :c] for b in range(nseq)],
                                       axis=0).astype(BF16))
    qe = [qs[h // rep] * jnp.exp(head(gc, h)) for h in range(hb)]
    ws_parts = [[] for _ in range(hb)]
    qs_parts = [[] for _ in range(hb)]
    for g in range(c // SUBLANES):
        gsl = slice(g * SUBLANES, (g + 1) * SUBLANES)
        for h in range(hb):
            acc = jnp.zeros((2 * SUBLANES, HEAD_DIM), F32)
            for p in range(per_group):
                valid = (row8 >= p * seq) & (row8 < (p + 1) * seq)
                lhs = jnp.concatenate([jnp.where(valid, wy[h][1][gsl], 0.0),
                                       jnp.where(valid, qe[h][gsl], 0.0)], axis=0)
                acc = acc + _bdot(lhs, s0_ref[g * per_group + p, h])
            ws_parts[h].append(acc[:SUBLANES])
            qs_parts[h].append(acc[SUBLANES:])
    v_new = [wy[h][0] - jnp.concatenate(ws_parts[h], axis=0) for h in range(hb)]
    outs = [jnp.concatenate(qs_parts[h], axis=0) + _bdot(wy[h][2][:, :c], v_new[h]) for h in range(hb)]
    upds = [jnp.dot(k_t_seq[h // rep], (v_new[h] * jnp.exp(head(gt, h) - head(gc, h))).astype(BF16),
                    preferred_element_type=F32) for h in range(hb)]
    for h in range(hb):
        o_ref[:, h * HEAD_DIM:(h + 1) * HEAD_DIM] = _head_out(outs[h], head(z_ref, h), ng).astype(o_ref.dtype)
        decay = jnp.exp(head(gt, h))
        for b in range(nseq):
            st_ref[b, h] = (decay[b * seq:b * seq + 1] * s0_ref[b, h].astype(F32)
                            + upds[h][b * HEAD_DIM:(b + 1) * HEAD_DIM]).astype(st_ref.dtype)


def _gdn_sample(qkv, proj, z_col0, gate_arrays, norm_g, s0, batch, seq, qk_heads, v_heads):
    rep = v_heads // qk_heads
    hb = _tile(v_heads, 4, rep)
    assert GDN_TILE % seq == 0 and SUBLANES % seq == 0 and seq & (seq - 1) == 0
    nb = GDN_TILE // seq
    assert batch % nb == 0
    r = GDN_TILE
    nhb = v_heads // hb
    wq = (hb // rep) * HEAD_DIM
    wv = hb * HEAD_DIM
    qk_w = qk_heads * HEAD_DIM
    bc, gc, gt, gr = _gdn_layouts(*gate_arrays, v_heads, hb)
    k0 = qk_w // wq
    v0 = 2 * qk_w // wv
    z0 = z_col0 // wv
    assert z_col0 % wv == 0
    st_spec = pl.BlockSpec((nb, hb, HEAD_DIM, HEAD_DIM), lambda b, h: (b, h, 0, 0))
    col_spec = pl.BlockSpec((None, r, hb), lambda b, h: (h, b, 0))
    return pl.pallas_call(
        functools.partial(_gdn_sample_kernel, hb=hb, rep=rep, seq=seq),
        grid=(batch // nb, nhb),
        in_specs=[pl.BlockSpec((r, wq), lambda b, h: (b, h)),
                  pl.BlockSpec((r, wq), lambda b, h: (b, k0 + h)),
                  pl.BlockSpec((r, wv), lambda b, h: (b, v0 + h)),
                  pl.BlockSpec((r, wv), lambda b, h: (b, z0 + h)),
                  col_spec, col_spec, col_spec,
                  pl.BlockSpec((None, 1, hb, HEAD_DIM), lambda b, h: (h, b, 0, 0)),
                  pl.BlockSpec((1, HEAD_DIM), lambda b, h: (0, 0)),
                  st_spec],
        out_specs=[pl.BlockSpec((r, wv), lambda b, h: (b, h)), st_spec],
        out_shape=[jax.ShapeDtypeStruct((batch * seq, v_heads * HEAD_DIM), BF16),
                   jax.ShapeDtypeStruct(s0.shape, s0.dtype)],
        compiler_params=_cparams("parallel", "parallel"),
        name="gdn_sample",
    )(qkv, qkv, qkv, proj, bc, gc, gt, gr, norm_g, s0)


def _merge_kernel(oh_ref, og_ref, wa_ref, wb_ref, ga_ref, gb_ref, o_ref):
    ya = _bdot(oh_ref[...], wa_ref[...])
    yb = _bdot(og_ref[...], wb_ref[...])
    o_ref[...] = (_sigmoid(ga_ref[...]) * ya + _sigmoid(gb_ref[...]) * yb).astype(o_ref.dtype)


def _merge(o_hg, o_gdn, w_a, w_b, gates, tm):
    m, ka = o_hg.shape
    kb = o_gdn.shape[1]
    d = w_a.shape[-1]
    tn = _tile(d, 256, 128)
    nj = d // tn
    return pl.pallas_call(
        _merge_kernel,
        grid=(m // tm, nj),
        in_specs=[pl.BlockSpec((tm, ka), lambda i, j: (i, 0)),
                  pl.BlockSpec((tm, kb), lambda i, j: (i, 0)),
                  pl.BlockSpec((None, ka, tn), lambda i, j: (0, 0, j)),
                  pl.BlockSpec((None, kb, tn), lambda i, j: (0, 0, j)),
                  pl.BlockSpec((tm, tn), lambda i, j: (i, j)),
                  pl.BlockSpec((tm, tn), lambda i, j: (i, nj + j))],
        out_specs=pl.BlockSpec((tm, tn), lambda i, j: (i, j)),
        out_shape=jax.ShapeDtypeStruct((m, d), BF16),
        compiler_params=_cparams("parallel", "arbitrary"),
        name="merge",
    )(o_hg, o_gdn, w_a, w_b, gates, gates)


def _outproj_kernel(m_ref, w_ref, x_ref, g1_ref, sc_ref, sh_ref, ng_ref, h_ref, a_ref, acc_ref):
    kk = pl.program_id(1)

    @pl.when(kk == 0)
    def _():
        acc_ref[...] = jnp.zeros_like(acc_ref)

    acc_ref[...] += _bdot(m_ref[...], w_ref[...])

    @pl.when(kk == pl.num_programs(1) - 1)
    def _():
        ng = ng_ref[...]
        h = x_ref[...].astype(F32) + g1_ref[...] * (_rms(acc_ref[...]) * ng[0:1])
        h_ref[...] = h
        a_ref[...] = ((_rms(h) * ng[1:2]) * (1.0 + sc_ref[...]) + sh_ref[...]).astype(a_ref.dtype)


def _outproj(merged, w_o, x, mod, norm_g2, rows_per_seq, tm):
    m, d = x.shape
    kdim = merged.shape[1]
    tk = _tile(kdim, 512, 128)
    row = lambda i, k: (i, 0)
    return pl.pallas_call(
        _outproj_kernel,
        grid=(m // tm, kdim // tk),
        in_specs=[pl.BlockSpec((tm, tk), lambda i, k: (i, k)),
                  pl.BlockSpec((None, tk, d), lambda i, k: (0, k, 0)),
                  pl.BlockSpec((tm, d), row),
                  _mod_spec(mod, 2, tm, rows_per_seq),
                  _mod_spec(mod, 4, tm, rows_per_seq),
                  _mod_spec(mod, 3, tm, rows_per_seq),
                  pl.BlockSpec((2, d), lambda i, k: (0, 0))],
        out_specs=[pl.BlockSpec((tm, d), row), pl.BlockSpec((tm, d), row)],
        out_shape=[jax.ShapeDtypeStruct((m, d), F32), jax.ShapeDtypeStruct((m, d), BF16)],
        scratch_shapes=[pltpu.VMEM((tm, d), F32)],
        compiler_params=_cparams("parallel", "arbitrary"),
        name="out_proj",
    )(merged, w_o, x, mod, mod, mod, norm_g2)


def _ffn_kernel(a_ref, wu_ref, wd_ref, h_ref, g2_ref, ng_ref, y_ref, acc_ref):
    f = pl.program_id(1)

    @pl.when(f == 0)
    def _():
        acc_ref[...] = jnp.zeros_like(acc_ref)

    hid = jnp.square(jnp.maximum(_bdot(a_ref[...], wu_ref[...]), 0.0))
    acc_ref[...] += _bdot(hid, wd_ref[...])

    @pl.when(f == pl.num_programs(1) - 1)
    def _():
        y_ref[...] = (h_ref[...] + g2_ref[...] * (_rms(acc_ref[...]) * ng_ref[...])).astype(y_ref.dtype)


def _ffn(a, w_up, w_down, h, mod, norm_g, rows_per_seq, tm, out_dtype):
    m, d = h.shape
    dff = w_up.shape[-1]
    tf = _tile(dff, 512, 128)
    row = lambda i, f: (i, 0)
    return pl.pallas_call(
        _ffn_kernel,
        grid=(m // tm, dff // tf),
        in_specs=[pl.BlockSpec((tm, d), row),
                  pl.BlockSpec((None, d, tf), lambda i, f: (0, 0, f)),
                  pl.BlockSpec((None, tf, d), lambda i, f: (0, f, 0)),
                  pl.BlockSpec((tm, d), row),
                  _mod_spec(mod, 5, tm, rows_per_seq),
                  pl.BlockSpec((1, d), lambda i, f: (0, 0))],
        out_specs=pl.BlockSpec((tm, d), row),
        out_shape=jax.ShapeDtypeStruct((m, d), out_dtype),
        scratch_shapes=[pltpu.VMEM((tm, d), F32)],
        compiler_params=_cparams("parallel", "arbitrary"),
        name="ffn",
    )(a, w_up, w_down, h, mod, norm_g)


def _run_group(x3, mod, rows_per_seq_for_mod, conv_buf, s_hg, s_gdn, lb, p, prompt):
    batch, seq, d = x3.shape
    m = batch * seq
    x = x3.reshape(m, d)
    hg_heads, gv_heads = p["hg_heads"], p["gv_heads"]
    gq_heads = p["gq_heads"]
    hg_w = hg_heads * HEAD_DIM
    gq_w = gq_heads * HEAD_DIM
    gv_w = gv_heads * HEAD_DIM
    ch = 2 * gq_w + gv_w
    main = 4 * hg_w + ch + gv_w
    conv0 = 4 * hg_w
    z0 = conv0 + ch
    tm = _tile(m, 512, SUBLANES)
    if prompt:
        tm = _tile(seq, tm, SUBLANES)

    a1 = _prenorm(x, p["norm_g"][0:1], mod, 1, 0, rows_per_seq_for_mod, tm)
    proj = _project(a1, p["w_in"], main, tm, _tile(main, 512, 128))
    tail = _project(a1, p["w_tail"], p["w_tail"].shape[-1], tm, p["w_tail"].shape[-1])
    gates = _project(a1, p["w_gates"], 2 * d, tm, _tile(2 * d, 512, 128))

    if prompt:
        o_hg, s_hg_new = _hgrn_prompt(proj, lb, p["hg_norm_g"], batch, seq, hg_heads, p["hg_dtype"])
        gate_arrays = _gdn_gates(tail, p["a_log"], p["dt_bias"], GDN_CHUNK)
        buf8 = jnp.zeros((batch * SUBLANES, ch), F32)
        qkv = _conv_prompt(proj, conv0, buf8, p["conv_w"], batch, seq, gq_w, ch)
        o_gdn, s_gdn_new = _gdn_prompt(qkv, proj, z0, gate_arrays, p["gdn_norm_g"], batch, seq,
                                       gq_heads, gv_heads, p["gdn_dtype"])
    else:
        o_hg, s_hg_new = _hgrn_sample(proj, lb, p["hg_norm_g"], s_hg, batch, seq, hg_heads)
        gate_arrays = _gdn_gates(tail, p["a_log"], p["dt_bias"], seq)
        buf_rows = jnp.pad(conv_buf.astype(F32), ((0, 0), (0, seq - (CONV_W - 1)), (0, 0))).reshape(m, ch)
        qkv = _conv_sample(proj, conv0, buf_rows, p["conv_w"], seq, gq_w, ch)
        o_gdn, s_gdn_new = _gdn_sample(qkv, proj, z0, gate_arrays, p["gdn_norm_g"], s_gdn, batch, seq,
                                       gq_heads, gv_heads)

    u = proj[:, conv0:conv0 + ch].reshape(batch, seq, ch)
    conv_new = u[:, seq - (CONV_W - 1):]

    merged = _merge(o_hg, o_gdn, p["w_out_hg"], p["w_out_gdn"], gates, tm)
    h1, a2 = _outproj(merged, p["w_o"], x, mod, p["norm_g"][1:3], rows_per_seq_for_mod, tm)
    y = _ffn(a2, p["w_up"], p["w_down"], h1, mod, p["norm_g"][3:4], rows_per_seq_for_mod, tm, x3.dtype)
    return y.reshape(batch, seq, d), conv_new, s_hg_new, s_gdn_new


def kernel(x_prompt, x_sample, state_hgrn, state_gdn, cache_conv, c_prompt, c_sample, lb_logits, w_ada,
           b_ada, norm_g, w_in, conv_w, A_log, dt_bias, hg_norm_g, gdn_norm_g, w_out_hg, w_out_gdn, w_o,
           w_up, w_down):
    depth = w_in.shape[0]
    assert depth == 1, "single-layer trunk"
    bp, tp, d = x_prompt.shape
    bs, ts, _ = x_sample.shape
    hg_heads = state_hgrn.shape[2]
    gv_heads = state_gdn.shape[2]
    ch = cache_conv.shape[-1]
    gq_heads = (ch // HEAD_DIM - gv_heads) // 2
    assert tp >= CONV_W - 1 and ts >= CONV_W - 1
    assert tp % GDN_CHUNK == 0 and tp % HG_CHUNK == 0

    main = 4 * hg_heads * HEAD_DIM + ch + gv_heads * HEAD_DIM
    p = dict(
        hg_heads=hg_heads, gv_heads=gv_heads, gq_heads=gq_heads,
        hg_dtype=state_hgrn.dtype, gdn_dtype=state_gdn.dtype,
        norm_g=norm_g[0].astype(F32), w_in=w_in,
        w_tail=w_in[:, :, main:main + 2 * gv_heads], w_gates=w_in[:, :, main + 2 * gv_heads:],
        conv_w=conv_w.astype(F32), a_log=A_log.astype(F32), dt_bias=dt_bias.astype(F32),
        hg_norm_g=hg_norm_g.astype(F32), gdn_norm_g=gdn_norm_g.astype(F32),
        w_out_hg=w_out_hg, w_out_gdn=w_out_gdn, w_o=w_o, w_up=w_up, w_down=w_down,
    )

    lbs = jnp.cumsum(jax.nn.softmax(lb_logits.astype(F32), axis=0), axis=0)
    lb = lbs[0:1]

    n_c = bp + bs
    pad = (-n_c) % SUBLANES
    c_all = jnp.concatenate([c_prompt.astype(F32), c_sample.astype(F32), jnp.zeros((pad, d), F32)], axis=0)
    mod_all = _ada(c_all, w_ada, b_ada.astype(F32))
    mod_p = mod_all[:bp].reshape(bp * 6, 1, d)
    mod_s = jnp.repeat(mod_all[bp:bp + bs].reshape(bs, 6, d), ts, axis=0).transpose(1, 0, 2)

    y_p, conv_p, hg_p, gdn_p = _run_group(x_prompt, mod_p, tp, None, None, None, lb, p, prompt=True)
    y_s, conv_s, hg_s, gdn_s = _run_group(x_sample, mod_s, ts, cache_conv[0], state_hgrn[0], state_gdn[0],
                                          lb, p, prompt=False)
    return (y_p, y_s,
            hg_p[None], gdn_p[None], conv_p.astype(cache_conv.dtype)[None],
            hg_s[None], gdn_s[None], conv_s.astype(cache_conv.dtype)[None])
```

```python
import functools

import jax
import jax.numpy as jnp
from jax import lax
from jax.experimental import pallas as pl
from jax.experimental.pallas import tpu as pltpu

F32 = jnp.float32
BF16 = jnp.bfloat16
HIGHEST = lax.Precision.HIGHEST

HEAD_DIM = 128
HG_CHUNK = 32
GDN_CHUNK = 64
CONV_W = 4
NORM_EPS = 1e-6
SUBLANES = 8
VMEM_LIMIT = 48 * 1024 * 1024


def _cparams(*sem):
    return pltpu.CompilerParams(dimension_semantics=sem, vmem_limit_bytes=VMEM_LIMIT)


def _tile(n, pref, quantum):
    if n <= pref:
        return n
    t = (pref // quantum) * quantum
    while t > quantum and n % t:
        t -= quantum
    assert n % t == 0, (n, pref, quantum)
    return t


def _bdot(a, b):
    return jnp.dot(a.astype(BF16), b.astype(BF16), preferred_element_type=F32)


def _bdot_nt(a, b):
    return lax.dot_general(a.astype(BF16), b.astype(BF16), (((1,), (1,)), ((), ())),
                           preferred_element_type=F32)


def _fdot(a, b):
    return jnp.dot(a, b, precision=HIGHEST, preferred_element_type=F32)


def _split3(x):
    hi = x.astype(BF16)
    r = x - hi.astype(F32)
    mid = r.astype(BF16)
    lo = (r - mid.astype(F32)).astype(BF16)
    return hi, mid, lo


def _mask_dot(m_bf, x):
    hi, mid, lo = _split3(x)
    return (jnp.dot(m_bf, hi, preferred_element_type=F32)
            + jnp.dot(m_bf, mid, preferred_element_type=F32)
            + jnp.dot(m_bf, lo, preferred_element_type=F32))


def _sigmoid(x):
    return 1.0 / (1.0 + jnp.exp(-x))


def _silu(x):
    return x * _sigmoid(x)


def _rms(x):
    return x * lax.rsqrt(jnp.mean(x * x, axis=-1, keepdims=True) + NORM_EPS)


def _transpose_pad(x):
    r = x.shape[0]
    if r < HEAD_DIM:
        x = jnp.concatenate([x, jnp.zeros((HEAD_DIM - r, HEAD_DIM), x.dtype)], axis=0)
    return x.T


def _tri_masks(c):
    row = lax.broadcasted_iota(jnp.int32, (c, c), 0)
    col = lax.broadcasted_iota(jnp.int32, (c, c), 1)
    return col <= row, col < row


def _ada_kernel(c_ref, w_ref, b_ref, o_ref):
    cs = _silu(c_ref[...])
    w = w_ref[...]
    c_hi = cs.astype(BF16)
    c_lo = (cs - c_hi.astype(F32)).astype(BF16)
    w_hi = w.astype(BF16)
    w_lo = (w - w_hi.astype(F32)).astype(BF16)
    acc = jnp.dot(c_hi, w_hi, preferred_element_type=F32)
    acc += jnp.dot(c_lo, w_hi, preferred_element_type=F32)
    acc += jnp.dot(c_hi, w_lo, preferred_element_type=F32)
    o_ref[...] = acc + b_ref[...]


def _ada(c_all, w_ada, b_ada):
    m, d = c_all.shape
    n = w_ada.shape[-1]
    tn = _tile(n, 512, 128)
    return pl.pallas_call(
        _ada_kernel,
        grid=(n // tn,),
        in_specs=[pl.BlockSpec((m, d), lambda j: (0, 0)),
                  pl.BlockSpec((None, d, tn), lambda j: (0, 0, j)),
                  pl.BlockSpec((1, tn), lambda j: (0, j))],
        out_specs=pl.BlockSpec((m, tn), lambda j: (0, j)),
        out_shape=jax.ShapeDtypeStruct((m, n), F32),
        compiler_params=_cparams("arbitrary"),
        name="ada_mod",
    )(c_all, w_ada, b_ada)


def _mod_spec(mod, k, tm, rows_per_seq):
    d = mod.shape[-1]
    if mod.shape[1] == 1:
        tiles_per_seq = rows_per_seq // tm
        return pl.BlockSpec((None, 1, d), lambda i, *_: ((i // tiles_per_seq) * 6 + k, 0, 0))
    return pl.BlockSpec((None, tm, d), lambda i, *_: (k, i, 0))


def _prenorm_kernel(x_ref, g_ref, sc_ref, sh_ref, o_ref):
    xn = _rms(x_ref[...].astype(F32)) * g_ref[...]
    o_ref[...] = (xn * (1.0 + sc_ref[...]) + sh_ref[...]).astype(o_ref.dtype)


def _prenorm(x, g, mod, k_scale, k_shift, rows_per_seq, tm):
    m, d = x.shape
    return pl.pallas_call(
        _prenorm_kernel,
        grid=(m // tm,),
        in_specs=[pl.BlockSpec((tm, d), lambda i: (i, 0)),
                  pl.BlockSpec((1, d), lambda i: (0, 0)),
                  _mod_spec(mod, k_scale, tm, rows_per_seq),
                  _mod_spec(mod, k_shift, tm, rows_per_seq)],
        out_specs=pl.BlockSpec((tm, d), lambda i: (i, 0)),
        out_shape=jax.ShapeDtypeStruct((m, d), BF16),
        compiler_params=_cparams("parallel"),
        name="prenorm",
    )(x, g, mod, mod)


def _mm_kernel(a_ref, w_ref, o_ref):
    o_ref[...] = _bdot(a_ref[...], w_ref[...])


def _project(a, w3, n_cols, tm, tn):
    m, kdim = a.shape
    return pl.pallas_call(
        _mm_kernel,
        grid=(m // tm, n_cols // tn),
        in_specs=[pl.BlockSpec((tm, kdim), lambda i, j: (i, 0)),
                  pl.BlockSpec((None, kdim, tn), lambda i, j: (0, 0, j))],
        out_specs=pl.BlockSpec((tm, tn), lambda i, j: (i, j)),
        out_shape=jax.ShapeDtypeStruct((m, n_cols), F32),
        compiler_params=_cparams("parallel", "arbitrary"),
        name="in_proj",
    )(a, w3)


REALIGN_ROWS = 64


def _realign_kernel(a_ref, b_ref, o_ref, *, shift):
    tn = o_ref.shape[1]

    def body(r, carry):
        rows = pl.ds(pl.multiple_of(r * REALIGN_ROWS, REALIGN_ROWS), REALIGN_ROWS)
        both = jnp.concatenate([a_ref[rows, :], b_ref[rows, :]], axis=1)
        o_ref[rows, :] = both[:, shift:shift + tn]
        return carry

    lax.fori_loop(0, o_ref.shape[0] // REALIGN_ROWS, body, 0)


def _realign_cols(w3, off, width):
    kdim = w3.shape[1]
    tn = _tile(width, 256, 128)
    b0, shift = divmod(off, tn)
    assert kdim % REALIGN_ROWS == 0 and off + width <= w3.shape[2]
    last = pl.cdiv(w3.shape[2], tn) - 1
    return pl.pallas_call(
        functools.partial(_realign_kernel, shift=shift),
        grid=(width // tn,),
        in_specs=[pl.BlockSpec((None, kdim, tn), lambda j: (0, 0, b0 + j)),
                  pl.BlockSpec((None, kdim, tn), lambda j: (0, 0, jnp.minimum(b0 + j + 1, last)))],
        out_specs=pl.BlockSpec((None, kdim, tn), lambda j: (0, 0, j)),
        out_shape=jax.ShapeDtypeStruct((1, kdim, width), w3.dtype),
        compiler_params=_cparams("parallel"),
        name="realign_cols",
    )(w3, w3)


def _hgrn_elementwise(hq, hf, lb):
    q = _silu(hq)
    f = lb + (1.0 - lb) * _sigmoid(hf)
    return q, jnp.log(f), 1.0 - f


def _hgrn_chunk_heads(q, g, k, inp, states, tril, tril_bf):
    c = q.shape[0]
    hb = len(states)
    heads = [slice(h * HEAD_DIM, (h + 1) * HEAD_DIM) for h in range(hb)]
    big_g = _mask_dot(tril_bf, g)
    g_ref = big_g[c // 2:c // 2 + 1]
    g_last = big_g[c - 1:c]
    qe = q * jnp.exp(big_g - g_ref)
    ke = k * jnp.exp(g_ref - big_g)
    qs = q * jnp.exp(big_g)
    kl = k * jnp.exp(g_last - big_g)
    d_hi, d_mid, d_lo = [p.astype(F32) for p in _split3(jnp.exp(g_last))]
    extra = jnp.concatenate([d_hi, d_mid, d_lo, jnp.zeros((SUBLANES - 3, q.shape[1]), F32)], axis=0)
    a = [jnp.where(tril, _bdot_nt(qe[:, sl], ke[:, sl]), 0.0) for sl in heads]
    o = [_bdot(a[h], inp[:, sl]) + _bdot(qs[:, sl], states[h]) for h, sl in enumerate(heads)]
    t = [_transpose_pad(jnp.concatenate([kl[:, sl], extra[:, sl]], axis=0))[:, :c + SUBLANES] for sl in heads]
    piece_rows = lax.broadcasted_iota(jnp.int32, (SUBLANES, HEAD_DIM), 0) < 3
    ones_rows = jnp.where(piece_rows, 1.0, 0.0)
    zeros_c = jnp.zeros((c, HEAD_DIM), F32)
    new_states = []
    for h, sl in enumerate(heads):
        rhs = jnp.concatenate([jnp.concatenate([inp[:, sl], zeros_c], axis=1),
                               jnp.concatenate([jnp.zeros((SUBLANES, HEAD_DIM), F32), ones_rows], axis=1)], axis=0)
        ud = _bdot(t[h], rhs)
        new_states.append(ud[:, HEAD_DIM:] * states[h] + ud[:, :HEAD_DIM])
    return list(zip(o, new_states))


def _head_out(o, gate, norm_g):
    mean_sq = _lane_sum_bcast(o * o) * (1.0 / HEAD_DIM)
    return o * lax.rsqrt(mean_sq + NORM_EPS) * norm_g * _silu(gate)


def _hgrn_prompt_kernel(q_ref, f_ref, i_ref, g_ref, lb_ref, ng_ref, o_ref, st_ref, s_scr, *, hb):
    t = pl.program_id(2)
    c = HG_CHUNK

    @pl.when(t == 0)
    def _():
        s_scr[...] = jnp.zeros_like(s_scr)

    tril, _ = _tri_masks(c)
    tril_bf = jnp.where(tril, 1.0, 0.0).astype(BF16)
    lb = lb_ref[...]
    ng = ng_ref[...]

    def chunk(ci, carry):
        rows = pl.ds(pl.multiple_of(ci * c, c), c)
        q, g, k = _hgrn_elementwise(q_ref[rows, :], f_ref[rows, :], lb)
        inp = i_ref[rows, :]
        gate = g_ref[rows, :]
        res = _hgrn_chunk_heads(q, g, k, inp, [s_scr[h] for h in range(hb)], tril, tril_bf)
        for h, (o, s_new) in enumerate(res):
            sl = slice(h * HEAD_DIM, (h + 1) * HEAD_DIM)
            s_scr[h] = s_new
            o_ref[rows, sl] = _head_out(o, gate[:, sl], ng).astype(o_ref.dtype)
        return carry

    lax.fori_loop(0, q_ref.shape[0] // c, chunk, 0)

    @pl.when(t == pl.num_programs(2) - 1)
    def _():
        st_ref[...] = s_scr[...].astype(st_ref.dtype)


def _hgrn_prompt(proj, lb, norm_g, batch, seq, heads, state_dtype):
    hb = _tile(heads, 8, 1)
    tt = _tile(seq, 256, HG_CHUNK)
    w = hb * HEAD_DIM
    nhb = heads // hb
    nt = seq // tt

    def col(seg):
        return pl.BlockSpec((tt, w), lambda b, h, t: (b * nt + t, seg * nhb + h))

    return pl.pallas_call(
        functools.partial(_hgrn_prompt_kernel, hb=hb),
        grid=(batch, nhb, nt),
        in_specs=[col(0), col(1), col(2), col(3),
                  pl.BlockSpec((1, w), lambda b, h, t: (0, h)),
                  pl.BlockSpec((1, HEAD_DIM), lambda b, h, t: (0, 0))],
        out_specs=[pl.BlockSpec((tt, w), lambda b, h, t: (b * nt + t, h)),
                   pl.BlockSpec((None, hb, HEAD_DIM, HEAD_DIM), lambda b, h, t: (b, h, 0, 0))],
        out_shape=[jax.ShapeDtypeStruct((batch * seq, heads * HEAD_DIM), BF16),
                   jax.ShapeDtypeStruct((batch, heads, HEAD_DIM, HEAD_DIM), state_dtype)],
        scratch_shapes=[pltpu.VMEM((hb, HEAD_DIM, HEAD_DIM), F32)],
        compiler_params=_cparams("parallel", "parallel", "arbitrary"),
        name="hgrn_prompt",
    )(proj, proj, proj, proj, lb, norm_g)


def _hgrn_sample_kernel(q_ref, f_ref, i_ref, g_ref, lb_ref, ng_ref, s0_ref, o_ref, st_ref, *, hb, seq):
    c = SUBLANES
    per_group = c // seq
    tril, _ = _tri_masks(c)
    tril_bf = jnp.where(tril, 1.0, 0.0).astype(BF16)
    lb = lb_ref[...]
    ng = ng_ref[...]
    row = lax.broadcasted_iota(jnp.int32, (c, 1), 0)

    def group(gi, carry):
        rows = pl.ds(pl.multiple_of(gi * c, c), c)
        q, g, k = _hgrn_elementwise(q_ref[rows, :], f_ref[rows, :], lb)
        inp = i_ref[rows, :]
        gate = g_ref[rows, :]
        valid = [(row >= p * seq) & (row < (p + 1) * seq) for p in range(per_group)]
        masked = lambda x: jnp.concatenate([jnp.where(v, x, 0.0) for v in valid], axis=1)
        states = [s0_ref[gi * per_group + p, h].astype(F32) for p in range(per_group) for h in range(hb)]
        res = _hgrn_chunk_heads(masked(q), masked(g), masked(k), jnp.concatenate([inp] * per_group, axis=1),
                                states, tril, tril_bf)
        for h in range(hb):
            sl = slice(h * HEAD_DIM, (h + 1) * HEAD_DIM)
            o_all = jnp.zeros((c, HEAD_DIM), F32)
            for p in range(per_group):
                o, s_new = res[p * hb + h]
                st_ref[gi * per_group + p, h] = s_new.astype(st_ref.dtype)
                o_all = jnp.where(valid[p], o, o_all)
            o_ref[rows, sl] = _head_out(o_all, gate[:, sl], ng).astype(o_ref.dtype)
        return carry

    lax.fori_loop(0, q_ref.shape[0] // c, group, 0)


def _hgrn_sample(proj, lb, norm_g, s0, batch, seq, heads):
    assert SUBLANES % seq == 0
    hb = _tile(heads, 4, 1)
    nb = _tile(batch, 8, SUBLANES // seq)
    w = hb * HEAD_DIM
    nhb = heads // hb
    r = nb * seq

    def col(seg):
        return pl.BlockSpec((r, w), lambda b, h: (b, seg * nhb + h))

    st_spec = pl.BlockSpec((nb, hb, HEAD_DIM, HEAD_DIM), lambda b, h: (b, h, 0, 0))
    return pl.pallas_call(
        functools.partial(_hgrn_sample_kernel, hb=hb, seq=seq),
        grid=(batch // nb, nhb),
        in_specs=[col(0), col(1), col(2), col(3),
                  pl.BlockSpec((1, w), lambda b, h: (0, h)),
                  pl.BlockSpec((1, HEAD_DIM), lambda b, h: (0, 0)),
                  st_spec],
        out_specs=[pl.BlockSpec((r, w), lambda b, h: (b, h)), st_spec],
        out_shape=[jax.ShapeDtypeStruct((batch * seq, heads * HEAD_DIM), BF16),
                   jax.ShapeDtypeStruct(s0.shape, s0.dtype)],
        compiler_params=_cparams("parallel", "parallel"),
        name="hgrn_sample",
    )(proj, proj, proj, proj, lb, norm_g, s0)


def _gdn_gates_kernel(x_ref, alog_ref, dtb_ref, beta_ref, gam_ref, gtot_ref, *, nv, chunk):
    x = x_ref[...]
    r = x.shape[0]
    beta_ref[...] = _sigmoid(x[:, :nv])
    z = x[:, nv:] + dtb_ref[...]
    softplus = jnp.maximum(z, 0.0) + jnp.log(1.0 + jnp.exp(-jnp.abs(z)))
    g = -jnp.exp(alog_ref[...]) * softplus
    row = lax.broadcasted_iota(jnp.int32, (r, r), 0)
    col = lax.broadcasted_iota(jnp.int32, (r, r), 1)
    same = (row // chunk) == (col // chunk)
    gam_ref[...] = _mask_dot(jnp.where(same & (col <= row), 1.0, 0.0).astype(BF16), g)
    gtot_ref[...] = _mask_dot(jnp.where(same, 1.0, 0.0).astype(BF16), g)


def _gdn_gates(x, a_log, dt_bias, chunk):
    m, two_nv = x.shape
    nv = two_nv // 2
    tr = _tile(m, 256, max(chunk, SUBLANES))
    out = jax.ShapeDtypeStruct((m, nv), F32)
    return pl.pallas_call(
        functools.partial(_gdn_gates_kernel, nv=nv, chunk=chunk),
        grid=(m // tr,),
        in_specs=[pl.BlockSpec((tr, two_nv), lambda i: (i, 0)),
                  pl.BlockSpec((1, nv), lambda i: (0, 0)),
                  pl.BlockSpec((1, nv), lambda i: (0, 0))],
        out_specs=[pl.BlockSpec((tr, nv), lambda i: (i, 0))] * 3,
        out_shape=[out, out, out],
        compiler_params=_cparams("parallel"),
        name="gdn_gates",
    )(x, a_log, dt_bias)


def _lane_sum_bcast(x):
    hi = x.astype(BF16)
    lo = (x - hi.astype(F32)).astype(BF16)
    return jnp.dot(jnp.concatenate([hi, lo], axis=1), jnp.ones((2 * HEAD_DIM, HEAD_DIM), BF16),
                   preferred_element_type=F32)


def _l2_heads(y, mult):
    parts = []
    for h in range(y.shape[1] // HEAD_DIM):
        yh = y[:, h * HEAD_DIM:(h + 1) * HEAD_DIM]
        parts.append(yh * (lax.rsqrt(_lane_sum_bcast(yh * yh) + NORM_EPS) * mult))
    return jnp.concatenate(parts, axis=1)


def _conv_finish(y, kind_q, kind_k, o_ref, scale):
    y = _silu(y)

    @pl.when(kind_q)
    def _():
        o_ref[...] = _l2_heads(y, scale)

    @pl.when(kind_k)
    def _():
        o_ref[...] = _l2_heads(y, 1.0)

    @pl.when(jnp.logical_not(kind_q | kind_k))
    def _():
        o_ref[...] = y


CONV_ROWS = 32


def _conv_prompt_kernel(cur_ref, prev_ref, buf_ref, cw_ref, o_ref, x_scr, *, tiles_per_seq, nq_tiles, scale):
    i = pl.program_id(0)
    j = pl.program_id(1)
    first = (i % tiles_per_seq) == 0
    tt = cur_ref.shape[0]
    x_scr[0:SUBLANES, :] = jnp.where(first, buf_ref[...], prev_ref[...])
    x_scr[SUBLANES:SUBLANES + tt, :] = cur_ref[...]
    cw = cw_ref[...]

    def run(finish):
        for r0 in range(0, tt, CONV_ROWS):
            y = None
            for s in range(CONV_W):
                lo = SUBLANES + r0 - s
                term = cw[CONV_W - 1 - s:CONV_W - s] * x_scr[lo:lo + CONV_ROWS, :]
                y = term if y is None else y + term
            o_ref[r0:r0 + CONV_ROWS, :] = finish(_silu(y))

    kind_q = j < nq_tiles
    kind_k = (j >= nq_tiles) & (j < 2 * nq_tiles)
    pl.when(kind_q)(lambda: run(lambda y: _l2_heads(y, scale)))
    pl.when(kind_k)(lambda: run(lambda y: _l2_heads(y, 1.0)))
    pl.when(jnp.logical_not(kind_q | kind_k))(lambda: run(lambda y: y))


def _conv_prompt(proj, col0, conv_buf8, conv_w, batch, seq, qk_w, ch):
    wc = _tile(qk_w, 512, HEAD_DIM)
    tt = _tile(seq, 256, SUBLANES)
    nt = seq // tt
    c0 = col0 // wc
    assert col0 % wc == 0
    return pl.pallas_call(
        functools.partial(_conv_prompt_kernel, tiles_per_seq=nt, nq_tiles=qk_w // wc,
                          scale=float(HEAD_DIM) ** -0.5),
        grid=(batch * nt, ch // wc),
        in_specs=[pl.BlockSpec((tt, wc), lambda i, j: (i, c0 + j)),
                  pl.BlockSpec((SUBLANES, wc),
                               lambda i, j: (jnp.maximum(i * (tt // SUBLANES) - 1, 0), c0 + j)),
                  pl.BlockSpec((SUBLANES, wc), lambda i, j: (i // nt, j)),
                  pl.BlockSpec((None, CONV_W, wc), lambda i, j: (0, 0, j))],
        out_specs=pl.BlockSpec((tt, wc), lambda i, j: (i, j)),
        out_shape=jax.ShapeDtypeStruct((batch * seq, ch), F32),
        scratch_shapes=[pltpu.VMEM((tt + SUBLANES, wc), F32)],
        compiler_params=_cparams("parallel", "parallel"),
        name="gdn_conv_prompt",
    )(proj, proj, conv_buf8, conv_w)


def _conv_sample_kernel(u_ref, buf_ref, cw_ref, o_ref, *, seq, nq_tiles, scale):
    j = pl.program_id(1)
    u = u_ref[...]
    buf = buf_ref[...]
    r = u.shape[0]
    tok = lax.broadcasted_iota(jnp.int32, (r, 1), 0) % seq
    cw = cw_ref[...]
    y = cw[CONV_W - 1:CONV_W] * u
    for s in range(1, CONV_W):
        from_u = pltpu.roll(u, s, axis=0)
        up = CONV_W - 1 - s
        from_buf = pltpu.roll(buf, r - up, axis=0) if up else buf
        y = y + cw[CONV_W - 1 - s:CONV_W - s] * jnp.where(tok >= s, from_u, from_buf)
    _conv_finish(y, j < nq_tiles, (j >= nq_tiles) & (j < 2 * nq_tiles), o_ref, scale)


def _conv_sample(proj, col0, buf_rows, conv_w, seq, qk_w, ch):
    m = proj.shape[0]
    wc = _tile(qk_w, 512, HEAD_DIM)
    tr = _tile(m, 256, SUBLANES)
    c0 = col0 // wc
    assert col0 % wc == 0 and tr % seq == 0 and seq >= CONV_W - 1
    return pl.pallas_call(
        functools.partial(_conv_sample_kernel, seq=seq, nq_tiles=qk_w // wc,
                          scale=float(HEAD_DIM) ** -0.5),
        grid=(m // tr, ch // wc),
        in_specs=[pl.BlockSpec((tr, wc), lambda i, j: (i, c0 + j)),
                  pl.BlockSpec((tr, wc), lambda i, j: (i, j)),
                  pl.BlockSpec((None, CONV_W, wc), lambda i, j: (0, 0, j))],
        out_specs=pl.BlockSpec((tr, wc), lambda i, j: (i, j)),
        out_shape=jax.ShapeDtypeStruct((m, ch), F32),
        compiler_params=_cparams("parallel", "parallel"),
        name="gdn_conv_sample",
    )(proj, buf_rows, conv_w)


GDN_TILE = 64
HALF_LANES = 64


def _mm_tile(a, b):
    return _bdot(a[:, :GDN_TILE], b)


def _lane_bcast_all(x, sel):
    return _mask_dot_right(x, sel)


def _mask_dot_right(x, m_bf):
    hi, mid, lo = _split3(x)
    return (jnp.dot(hi, m_bf, preferred_element_type=F32)
            + jnp.dot(mid, m_bf, preferred_element_type=F32)
            + jnp.dot(lo, m_bf, preferred_element_type=F32))


def _head_select(hb):
    head = lax.broadcasted_iota(jnp.int32, (hb, hb * HEAD_DIM), 0)
    lane_head = lax.broadcasted_iota(jnp.int32, (hb, hb * HEAD_DIM), 1) // HEAD_DIM
    return jnp.where(head == lane_head, 1.0, 0.0).astype(BF16)


def _unit_lower_inverses(a_list, rows, cols, block):
    eye = jnp.where(rows == cols, 1.0, 0.0)
    pairs = (rows >> 1) == (cols >> 1)
    d = [eye - jnp.where(pairs, a, 0.0) for a in a_list]
    b = 2
    while b < block:
        sh = b.bit_length()
        coupling = ((rows >> sh) == (cols >> sh)) & ((rows >> (sh - 1)) != (cols >> (sh - 1)))
        m1 = [_mm_tile(jnp.where(coupling, a, 0.0), di) for a, di in zip(a_list, d)]
        m2 = [_mm_tile(di, mi) for di, mi in zip(d, m1)]
        d = [di - mi for di, mi in zip(d, m2)]
        b *= 2
    return d


def _gdn_pair_products(q, k):
    k_pad = jnp.concatenate([k, jnp.zeros_like(k)], axis=0)
    kq = _bdot_nt(jnp.concatenate([k, q], axis=0), k_pad)
    return kq[:GDN_TILE], kq[GDN_TILE:]


def _gdn_wy(qs, ks, vs, bc, gc, gr, rep, rows, cols, incl, strict, block):
    hb = len(vs)
    prods = [_gdn_pair_products(q, k) for q, k in zip(qs, ks)]
    decays, a_list = [], []
    for h in range(hb):
        sl = slice(h * HEAD_DIM, (h + 1) * HEAD_DIM)
        decay = jnp.where(incl, jnp.exp(gc[:, sl] - gr[h:h + 1, :]), 0.0)
        decays.append(decay)
        a_list.append(jnp.where(strict, bc[:, sl] * prods[h // rep][0] * decay, 0.0))
    t_inv = _unit_lower_inverses(a_list, rows, cols, block)
    xs = []
    for h in range(hb):
        sl = slice(h * HEAD_DIM, (h + 1) * HEAD_DIM)
        beta = bc[:, sl]
        rhs = jnp.concatenate([vs[h] * beta, ks[h // rep] * (beta * jnp.exp(gc[:, sl]))], axis=1)
        xs.append(_mm_tile(t_inv[h], rhs))
    return [(x[:, :HEAD_DIM], x[:, HEAD_DIM:], prods[h // rep][1] * decays[h]) for h, x in enumerate(xs)]


def _tile_iotas():
    shape = (GDN_TILE, HEAD_DIM)
    return lax.broadcasted_iota(jnp.int32, shape, 0), lax.broadcasted_iota(jnp.int32, shape, 1)


def _gdn_prompt_kernel(q_ref, k_ref, v_ref, z_ref, bc_ref, gc_ref, gt_ref, gr_ref, ng_ref,
                       o_ref, st_ref, s_scr, u_scr, w_scr, a_scr, *, hb, rep):
    t = pl.program_id(2)
    c = GDN_CHUNK
    n_chunks = q_ref.shape[0] // c

    @pl.when(t == 0)
    def _():
        s_scr[...] = jnp.zeros_like(s_scr)

    rows_i, cols_i = _tile_iotas()
    incl = cols_i <= rows_i
    strict = cols_i < rows_i
    ng = ng_ref[...]
    sel = _head_select(hb)

    def solve(ci, carry):
        rows = pl.ds(pl.multiple_of(ci * c, c), c)
        bc = _lane_bcast_all(bc_ref[rows, :], sel)
        gc = _lane_bcast_all(gc_ref[rows, :], sel)
        head = lambda ref, i: ref[rows, i * HEAD_DIM:(i + 1) * HEAD_DIM]
        wy = _gdn_wy([head(q_ref, i) for i in range(hb // rep)], [head(k_ref, i) for i in range(hb // rep)],
                     [head(v_ref, h) for h in range(hb)], bc, gc, gr_ref[ci], rep,
                     rows_i, cols_i, incl, strict, c)
        for h, (u, w, attn) in enumerate(wy):
            sl = slice(h * HEAD_DIM, (h + 1) * HEAD_DIM)
            u_scr[rows, sl] = u
            w_scr[rows, sl] = w.astype(w_scr.dtype)
            a_scr[rows, sl] = attn.astype(a_scr.dtype)
        return carry

    lax.fori_loop(0, n_chunks, solve, 0)

    def recur(ci, carry):
        rows = pl.ds(pl.multiple_of(ci * c, c), c)
        gc = _lane_bcast_all(gc_ref[rows, :], sel)
        gt = _lane_bcast_all(gt_ref[rows, :], sel)
        for pr in range(hb // rep):
            psl = slice(pr * HEAD_DIM, (pr + 1) * HEAD_DIM)
            k_t = _transpose_pad(k_ref[rows, psl])[:, :c]
            s_pair = s_scr[pr]
            qs = _bdot(q_ref[rows, psl], s_pair)
            v_scaled, decays = [], []
            for e in range(rep):
                h = pr * rep + e
                sl = slice(h * HEAD_DIM, (h + 1) * HEAD_DIM)
                esl = slice(e * HEAD_DIM, (e + 1) * HEAD_DIM)
                gcol = gc[:, sl]
                gtot = gt[:, sl]
                v_new = u_scr[rows, sl] - _bdot(w_scr[rows, sl], s_pair[:, esl])
                o = qs[:, esl] * jnp.exp(gcol) + _bdot(a_scr[rows, sl][:, :c], v_new)
                o_ref[rows, sl] = _head_out(o, z_ref[rows, sl], ng).astype(o_ref.dtype)
                v_scaled.append(v_new * jnp.exp(gtot - gcol))
                decays.append(jnp.exp(gtot[0:1]))
            s_scr[pr] = (jnp.concatenate(decays, axis=1) * s_pair
                         + _bdot(k_t, jnp.concatenate(v_scaled, axis=1)))
        return carry

    lax.fori_loop(0, n_chunks, recur, 0)

    @pl.when(t == pl.num_programs(2) - 1)
    def _():
        for h in range(hb):
            st_ref[h] = s_scr[h // rep][:, (h % rep) * HEAD_DIM:(h % rep + 1) * HEAD_DIM].astype(st_ref.dtype)


def _gdn_layouts(beta, gam, gtot, nv, hb):
    m = beta.shape[0]
    nhb = nv // hb
    as_cols = lambda a: a.reshape(m, nhb, hb).transpose(1, 0, 2)
    g_rows = gam.reshape(m // GDN_TILE, GDN_TILE, nhb, hb).transpose(2, 0, 3, 1)
    g_rows = jnp.pad(g_rows, ((0, 0), (0, 0), (0, 0), (0, HEAD_DIM - GDN_TILE)))
    return as_cols(beta), as_cols(gam), as_cols(gtot), g_rows


def _gdn_prompt(qkv, proj, z_col0, gate_arrays, norm_g, batch, seq, qk_heads, v_heads, state_dtype):
    rep = v_heads // qk_heads
    hb = _tile(v_heads, 8, rep)
    tt = _tile(seq, 256, GDN_CHUNK)
    nt = seq // tt
    nhb = v_heads // hb
    wq = (hb // rep) * HEAD_DIM
    wv = hb * HEAD_DIM
    qk_w = qk_heads * HEAD_DIM
    bc, gc, gt, gr = _gdn_layouts(*gate_arrays, v_heads, hb)
    k0 = qk_w // wq
    v0 = 2 * qk_w // wv
    z0 = z_col0 // wv
    assert z_col0 % wv == 0 and GDN_CHUNK == GDN_TILE
    col_spec = pl.BlockSpec((None, tt, hb), lambda b, h, t: (h, b * nt + t, 0))
    return pl.pallas_call(
        functools.partial(_gdn_prompt_kernel, hb=hb, rep=rep),
        grid=(batch, nhb, nt),
        in_specs=[pl.BlockSpec((tt, wq), lambda b, h, t: (b * nt + t, h)),
                  pl.BlockSpec((tt, wq), lambda b, h, t: (b * nt + t, k0 + h)),
                  pl.BlockSpec((tt, wv), lambda b, h, t: (b * nt + t, v0 + h)),
                  pl.BlockSpec((tt, wv), lambda b, h, t: (b * nt + t, z0 + h)),
                  col_spec, col_spec, col_spec,
                  pl.BlockSpec((None, tt // GDN_TILE, hb, HEAD_DIM), lambda b, h, t: (h, b * nt + t, 0, 0)),
                  pl.BlockSpec((1, HEAD_DIM), lambda b, h, t: (0, 0))],
        out_specs=[pl.BlockSpec((tt, wv), lambda b, h, t: (b * nt + t, h)),
                   pl.BlockSpec((None, hb, HEAD_DIM, HEAD_DIM), lambda b, h, t: (b, h, 0, 0))],
        out_shape=[jax.ShapeDtypeStruct((batch * seq, v_heads * HEAD_DIM), BF16),
                   jax.ShapeDtypeStruct((batch, v_heads, HEAD_DIM, HEAD_DIM), state_dtype)],
        scratch_shapes=[pltpu.VMEM((hb // rep, HEAD_DIM, rep * HEAD_DIM), F32),
                        pltpu.VMEM((tt, wv), F32),
                        pltpu.VMEM((tt, wv), BF16),
                        pltpu.VMEM((tt, wv), BF16)],
        compiler_params=_cparams("parallel", "parallel", "arbitrary"),
        name="gdn_prompt",
    )(qkv, qkv, qkv, proj, bc, gc, gt, gr, norm_g)


def _gdn_sample_kernel(q_ref, k_ref, v_ref, z_ref, bc_ref, gc_ref, gt_ref, gr_ref, ng_ref, s0_ref,
                       o_ref, st_ref, *, hb, rep, seq):
    c = GDN_TILE
    nseq = c // seq
    per_group = SUBLANES // seq
    lsh = seq.bit_length() - 1
    rows_i, cols_i = _tile_iotas()
    same = (rows_i >> lsh) == (cols_i >> lsh)
    incl = (cols_i <= rows_i) & same
    strict = (cols_i < rows_i) & same
    row8 = lax.broadcasted_iota(jnp.int32, (SUBLANES, 1), 0)
    lane_seq = lax.broadcasted_iota(jnp.int32, (1, HEAD_DIM), 1) >> lsh
    ng = ng_ref[...]
    sel = _head_select(hb)
    bc = _lane_bcast_all(bc_ref[...], sel)
    gc = _lane_bcast_all(gc_ref[...], sel)
    gt = _lane_bcast_all(gt_ref[...], sel)
    gr = gr_ref[0]
    head = lambda ref, i: ref[:, i * HEAD_DIM:(i + 1) * HEAD_DIM]
    qs = [head(q_ref, i) for i in range(hb // rep)]
    ks = [head(k_ref, i) for i in range(hb // rep)]
    wy = _gdn_wy(qs, ks, [head(v_ref, h) for h in range(hb)], bc, gc, gr, rep,
                 rows_i, cols_i, incl, strict, seq)
    k_t_seq = []
    for k in ks:
        k_t = _transpose_pad(k)
        k_t_seq.append(jnp.concatenate([jnp.where(lane_seq == b, k_t, 0.0)[:, :c] for b in range(nseq)],
                                       axis=0).astype(BF16))
    qe = [qs[h // rep] * jnp.exp(head(gc, h)) for h in range(hb)]
    ws_parts = [[] for _ in range(hb)]
    qs_parts = [[] for _ in range(hb)]
    for g in range(c // SUBLANES):
        gsl = slice(g * SUBLANES, (g + 1) * SUBLANES)
        for h in range(hb):
            acc = jnp.zeros((2 * SUBLANES, HEAD_DIM), F32)
            for p in range(per_group):
                valid = (row8 >= p * seq) & (row8 < (p + 1) * seq)
                lhs = jnp.concatenate([jnp.where(valid, wy[h][1][gsl], 0.0),
                                       jnp.where(valid, qe[h][gsl], 0.0)], axis=0)
                acc = acc + _bdot(lhs, s0_ref[g * per_group + p, h])
            ws_parts[h].append(acc[:SUBLANES])
            qs_parts[h].append(acc[SUBLANES:])
    v_new = [wy[h][0] - jnp.concatenate(ws_parts[h], axis=0) for h in range(hb)]
    outs = [jnp.concatenate(qs_parts[h], axis=0) + _bdot(wy[h][2][:, :c], v_new[h]) for h in range(hb)]
    upds = [jnp.dot(k_t_seq[h // rep], (v_new[h] * jnp.exp(head(gt, h) - head(gc, h))).astype(BF16),
                    preferred_element_type=F32) for h in range(hb)]
    for h in range(hb):
        o_ref[:, h * HEAD_DIM:(h + 1) * HEAD_DIM] = _head_out(outs[h], head(z_ref, h), ng).astype(o_ref.dtype)
        decay = jnp.exp(head(gt, h))
        for b in range(nseq):
            st_ref[b, h] = (decay[b * seq:b * seq + 1] * s0_ref[b, h].astype(F32)
                            + upds[h][b * HEAD_DIM:(b + 1) * HEAD_DIM]).astype(st_ref.dtype)


def _gdn_sample(qkv, proj, z_col0, gate_arrays, norm_g, s0, batch, seq, qk_heads, v_heads):
    rep = v_heads // qk_heads
    hb = _tile(v_heads, 4, rep)
    assert GDN_TILE % seq == 0 and SUBLANES % seq == 0 and seq & (seq - 1) == 0
    nb = GDN_TILE // seq
    assert batch % nb == 0
    r = GDN_TILE
    nhb = v_heads // hb
    wq = (hb // rep) * HEAD_DIM
    wv = hb * HEAD_DIM
    qk_w = qk_heads * HEAD_DIM
    bc, gc, gt, gr = _gdn_layouts(*gate_arrays, v_heads, hb)
    k0 = qk_w // wq
    v0 = 2 * qk_w // wv
    z0 = z_col0 // wv
    assert z_col0 % wv == 0
    st_spec = pl.BlockSpec((nb, hb, HEAD_DIM, HEAD_DIM), lambda b, h: (b, h, 0, 0))
    col_spec = pl.BlockSpec((None, r, hb), lambda b, h: (h, b, 0))
    return pl.pallas_call(
        functools.partial(_gdn_sample_kernel, hb=hb, rep=rep, seq=seq),
        grid=(batch // nb, nhb),
        in_specs=[pl.BlockSpec((r, wq), lambda b, h: (b, h)),
                  pl.BlockSpec((r, wq), lambda b, h: (b, k0 + h)),
                  pl.BlockSpec((r, wv), lambda b, h: (b, v0 + h)),
                  pl.BlockSpec((r, wv), lambda b, h: (b, z0 + h)),
                  col_spec, col_spec, col_spec,
                  pl.BlockSpec((None, 1, hb, HEAD_DIM), lambda b, h: (h, b, 0, 0)),
                  pl.BlockSpec((1, HEAD_DIM), lambda b, h: (0, 0)),
                  st_spec],
        out_specs=[pl.BlockSpec((r, wv), lambda b, h: (b, h)), st_spec],
        out_shape=[jax.ShapeDtypeStruct((batch * seq, v_heads * HEAD_DIM), BF16),
                   jax.ShapeDtypeStruct(s0.shape, s0.dtype)],
        compiler_params=_cparams("parallel", "parallel"),
        name="gdn_sample",
    )(qkv, qkv, qkv, proj, bc, gc, gt, gr, norm_g, s0)


def _merge_kernel(oh_ref, og_ref, wa_ref, wb_ref, ga_ref, gb_ref, o_ref):
    ya = _bdot(oh_ref[...], wa_ref[...])
    yb = _bdot(og_ref[...], wb_ref[...])
    o_ref[...] = (_sigmoid(ga_ref[...]) * ya + _sigmoid(gb_ref[...]) * yb).astype(o_ref.dtype)


def _merge(o_hg, o_gdn, w_a, w_b, gates, tm):
    m, ka = o_hg.shape
    kb = o_gdn.shape[1]
    d = w_a.shape[-1]
    tn = _tile(d, 256, 128)
    nj = d // tn
    return pl.pallas_call(
        _merge_kernel,
        grid=(m // tm, nj),
        in_specs=[pl.BlockSpec((tm, ka), lambda i, j: (i, 0), pipeline_mode=pl.Buffered(1)),
                  pl.BlockSpec((tm, kb), lambda i, j: (i, 0), pipeline_mode=pl.Buffered(1)),
                  pl.BlockSpec((None, ka, tn), lambda i, j: (0, 0, j)),
                  pl.BlockSpec((None, kb, tn), lambda i, j: (0, 0, j)),
                  pl.BlockSpec((tm, tn), lambda i, j: (i, j)),
                  pl.BlockSpec((tm, tn), lambda i, j: (i, nj + j))],
        out_specs=pl.BlockSpec((tm, tn), lambda i, j: (i, j)),
        out_shape=jax.ShapeDtypeStruct((m, d), BF16),
        compiler_params=_cparams("parallel", "arbitrary"),
        name="merge",
    )(o_hg, o_gdn, w_a, w_b, gates, gates)


def _outproj_kernel(m_ref, w_ref, x_ref, g1_ref, sc_ref, sh_ref, ng_ref, h_ref, a_ref, acc_ref):
    kk = pl.program_id(1)

    @pl.when(kk == 0)
    def _():
        acc_ref[...] = jnp.zeros_like(acc_ref)

    acc_ref[...] += _bdot(m_ref[...], w_ref[...])

    @pl.when(kk == pl.num_programs(1) - 1)
    def _():
        ng = ng_ref[...]
        h = x_ref[...].astype(F32) + g1_ref[...] * (_rms(acc_ref[...]) * ng[0:1])
        h_ref[...] = h
        a_ref[...] = ((_rms(h) * ng[1:2]) * (1.0 + sc_ref[...]) + sh_ref[...]).astype(a_ref.dtype)


def _outproj(merged, w_o, x, mod, norm_g2, rows_per_seq, tm):
    m, d = x.shape
    kdim = merged.shape[1]
    tk = _tile(kdim, 512, 128)
    row = lambda i, k: (i, 0)
    return pl.pallas_call(
        _outproj_kernel,
        grid=(m // tm, kdim // tk),
        in_specs=[pl.BlockSpec((tm, tk), lambda i, k: (i, k)),
                  pl.BlockSpec((None, tk, d), lambda i, k: (0, k, 0)),
                  pl.BlockSpec((tm, d), row),
                  _mod_spec(mod, 2, tm, rows_per_seq),
                  _mod_spec(mod, 4, tm, rows_per_seq),
                  _mod_spec(mod, 3, tm, rows_per_seq),
                  pl.BlockSpec((2, d), lambda i, k: (0, 0))],
        out_specs=[pl.BlockSpec((tm, d), row), pl.BlockSpec((tm, d), row)],
        out_shape=[jax.ShapeDtypeStruct((m, d), F32), jax.ShapeDtypeStruct((m, d), BF16)],
        scratch_shapes=[pltpu.VMEM((tm, d), F32)],
        compiler_params=_cparams("parallel", "arbitrary"),
        name="out_proj",
    )(merged, w_o, x, mod, mod, mod, norm_g2)


def _ffn_up_kernel(a_ref, w_ref, o_ref):
    o_ref[...] = jnp.square(jnp.maximum(_bdot(a_ref[...], w_ref[...]), 0.0)).astype(o_ref.dtype)


def _ffn_up(a, w_up, tm):
    m, d = a.shape
    dff = w_up.shape[-1]
    tf = _tile(dff, 512, 128)
    return pl.pallas_call(
        _ffn_up_kernel,
        grid=(m // tm, dff // tf),
        in_specs=[pl.BlockSpec((tm, d), lambda i, f: (i, 0)),
                  pl.BlockSpec((None, d, tf), lambda i, f: (0, 0, f))],
        out_specs=pl.BlockSpec((tm, tf), lambda i, f: (i, f)),
        out_shape=jax.ShapeDtypeStruct((m, dff), BF16),
        compiler_params=_cparams("parallel", "arbitrary"),
        name="ffn_up",
    )(a, w_up)


def _ffn_down_kernel(hid_ref, w_ref, h_ref, g2_ref, ng_ref, y_ref):
    j = pl.program_id(1)
    tn = w_ref.shape[1]
    y_ref[:, pl.ds(pl.multiple_of(j * tn, tn), tn)] = _bdot(hid_ref[...], w_ref[...])

    @pl.when(j == pl.num_programs(1) - 1)
    def _():
        y_ref[...] = h_ref[...] + g2_ref[...] * (_rms(y_ref[...]) * ng_ref[...])


def _ffn_down(hid, w_down, h, mod, norm_g, rows_per_seq, tm, out_dtype):
    m, d = h.shape
    dff = hid.shape[1]
    tn = _tile(d, 256, 128)
    row = lambda i, j: (i, 0)
    assert out_dtype == F32, "the resident output block holds the f32 pre-norm result"
    return pl.pallas_call(
        _ffn_down_kernel,
        grid=(m // tm, d // tn),
        in_specs=[pl.BlockSpec((tm, dff), row, pipeline_mode=pl.Buffered(1)),
                  pl.BlockSpec((None, dff, tn), lambda i, j: (0, 0, j)),
                  pl.BlockSpec((tm, d), row, pipeline_mode=pl.Buffered(1)),
                  _mod_spec(mod, 5, tm, rows_per_seq),
                  pl.BlockSpec((1, d), lambda i, j: (0, 0))],
        out_specs=pl.BlockSpec((tm, d), row),
        out_shape=jax.ShapeDtypeStruct((m, d), out_dtype),
        compiler_params=_cparams("parallel", "arbitrary"),
        name="ffn_down",
    )(hid, w_down, h, mod, norm_g)


def _run_group(x3, mod, rows_per_seq_for_mod, conv_buf, s_hg, s_gdn, lb, p, prompt):
    batch, seq, d = x3.shape
    m = batch * seq
    x = x3.reshape(m, d)
    hg_heads, gv_heads = p["hg_heads"], p["gv_heads"]
    gq_heads = p["gq_heads"]
    hg_w = hg_heads * HEAD_DIM
    gq_w = gq_heads * HEAD_DIM
    gv_w = gv_heads * HEAD_DIM
    ch = 2 * gq_w + gv_w
    main = 4 * hg_w + ch + gv_w
    conv0 = 4 * hg_w
    z0 = conv0 + ch
    rows_unit = seq if prompt else m
    tm = _tile(rows_unit, 512, SUBLANES)
    tm_proj = _tile(rows_unit, 2048, SUBLANES)
    tm_wide = _tile(rows_unit, 1024, SUBLANES)

    a1 = _prenorm(x, p["norm_g"][0:1], mod, 1, 0, rows_per_seq_for_mod, tm)
    proj = _project(a1, p["w_in"], main, tm_proj, _tile(main, 512, 128))
    tail = _project(a1, p["w_tail"], p["w_tail"].shape[-1], tm_proj, p["w_tail"].shape[-1])
    gates = _project(a1, p["w_gates"], 2 * d, tm_proj, _tile(2 * d, 512, 128))

    if prompt:
        o_hg, s_hg_new = _hgrn_prompt(proj, lb, p["hg_norm_g"], batch, seq, hg_heads, p["hg_dtype"])
        gate_arrays = _gdn_gates(tail, p["a_log"], p["dt_bias"], GDN_CHUNK)
        buf8 = jnp.zeros((batch * SUBLANES, ch), F32)
        qkv = _conv_prompt(proj, conv0, buf8, p["conv_w"], batch, seq, gq_w, ch)
        o_gdn, s_gdn_new = _gdn_prompt(qkv, proj, z0, gate_arrays, p["gdn_norm_g"], batch, seq,
                                       gq_heads, gv_heads, p["gdn_dtype"])
    else:
        o_hg, s_hg_new = _hgrn_sample(proj, lb, p["hg_norm_g"], s_hg, batch, seq, hg_heads)
        gate_arrays = _gdn_gates(tail, p["a_log"], p["dt_bias"], seq)
        buf_rows = jnp.pad(conv_buf.astype(F32), ((0, 0), (0, seq - (CONV_W - 1)), (0, 0))).reshape(m, ch)
        qkv = _conv_sample(proj, conv0, buf_rows, p["conv_w"], seq, gq_w, ch)
        o_gdn, s_gdn_new = _gdn_sample(qkv, proj, z0, gate_arrays, p["gdn_norm_g"], s_gdn, batch, seq,
                                       gq_heads, gv_heads)

    conv_new = proj.reshape(batch, seq, main)[:, seq - (CONV_W - 1):, conv0:conv0 + ch]

    merged = _merge(o_hg, o_gdn, p["w_out_hg"], p["w_out_gdn"], gates, tm_wide)
    h1, a2 = _outproj(merged, p["w_o"], x, mod, p["norm_g"][1:3], rows_per_seq_for_mod, tm)
    hid = _ffn_up(a2, p["w_up"], tm_proj)
    y = _ffn_down(hid, p["w_down"], h1, mod, p["norm_g"][3:4], rows_per_seq_for_mod, tm, x3.dtype)
    return y.reshape(batch, seq, d), conv_new, s_hg_new, s_gdn_new


def kernel(x_prompt, x_sample, state_hgrn, state_gdn, cache_conv, c_prompt, c_sample, lb_logits, w_ada,
           b_ada, norm_g, w_in, conv_w, A_log, dt_bias, hg_norm_g, gdn_norm_g, w_out_hg, w_out_gdn, w_o,
           w_up, w_down):
    depth = w_in.shape[0]
    assert depth == 1, "single-layer trunk"
    bp, tp, d = x_prompt.shape
    bs, ts, _ = x_sample.shape
    hg_heads = state_hgrn.shape[2]
    gv_heads = state_gdn.shape[2]
    ch = cache_conv.shape[-1]
    gq_heads = (ch // HEAD_DIM - gv_heads) // 2
    assert tp >= CONV_W - 1 and ts >= CONV_W - 1
    assert tp % GDN_CHUNK == 0 and tp % HG_CHUNK == 0

    main = 4 * hg_heads * HEAD_DIM + ch + gv_heads * HEAD_DIM
    p = dict(
        hg_heads=hg_heads, gv_heads=gv_heads, gq_heads=gq_heads,
        hg_dtype=state_hgrn.dtype, gdn_dtype=state_gdn.dtype,
        norm_g=norm_g[0].astype(F32), w_in=w_in,
        w_tail=w_in[:, :, main:main + 2 * gv_heads],
        w_gates=_realign_cols(w_in, main + 2 * gv_heads, 2 * d),
        conv_w=conv_w.astype(F32), a_log=A_log.astype(F32), dt_bias=dt_bias.astype(F32),
        hg_norm_g=hg_norm_g.astype(F32), gdn_norm_g=gdn_norm_g.astype(F32),
        w_out_hg=w_out_hg, w_out_gdn=w_out_gdn, w_o=w_o, w_up=w_up, w_down=w_down,
    )

    lbs = jnp.cumsum(jax.nn.softmax(lb_logits.astype(F32), axis=0), axis=0)
    lb = lbs[0:1]

    n_c = bp + bs
    pad = (-n_c) % SUBLANES
    c_all = jnp.concatenate([c_prompt.astype(F32), c_sample.astype(F32), jnp.zeros((pad, d), F32)], axis=0)
    mod_all = _ada(c_all, w_ada, b_ada.astype(F32))
    mod_p = mod_all[:bp].reshape(bp * 6, 1, d)
    mod_s = jnp.repeat(mod_all[bp:bp + bs].reshape(bs, 6, d), ts, axis=0).transpose(1, 0, 2)

    y_p, conv_p, hg_p, gdn_p = _run_group(x_prompt, mod_p, tp, None, None, None, lb, p, prompt=True)
    y_s, conv_s, hg_s, gdn_s = _run_group(x_sample, mod_s, ts, cache_conv[0], state_hgrn[0], state_gdn[0],
                                          lb, p, prompt=False)
    return (y_p, y_s,
            hg_p[None], gdn_p[None], conv_p.astype(cache_conv.dtype)[None],
            hg_s[None], gdn_s[None], conv_s.astype(cache_conv.dtype)[None])
```

```python
import functools

import jax
import jax.numpy as jnp
from jax import lax
from jax.experimental import pallas as pl
from jax.experimental.pallas import tpu as pltpu

F32 = jnp.float32
BF16 = jnp.bfloat16
HIGHEST = lax.Precision.HIGHEST

HEAD_DIM = 128
HG_CHUNK = 32
GDN_CHUNK = 64
CONV_W = 4
NORM_EPS = 1e-6
SUBLANES = 8
VMEM_LIMIT = 48 * 1024 * 1024
VMEM_LIMIT_FFN_DOWN = 54 * 1024 * 1024


def _cparams(*sem, vmem_limit=VMEM_LIMIT):
    return pltpu.CompilerParams(dimension_semantics=sem, vmem_limit_bytes=vmem_limit)


def _tile(n, pref, quantum):
    if n <= pref:
        return n
    t = (pref // quantum) * quantum
    while t > quantum and n % t:
        t -= quantum
    assert n % t == 0, (n, pref, quantum)
    return t


def _bdot(a, b):
    return jnp.dot(a.astype(BF16), b.astype(BF16), preferred_element_type=F32)


def _bdot_nt(a, b):
    return lax.dot_general(a.astype(BF16), b.astype(BF16), (((1,), (1,)), ((), ())),
                           preferred_element_type=F32)


def _fdot(a, b):
    return jnp.dot(a, b, precision=HIGHEST, preferred_element_type=F32)


def _split3(x):
    hi = x.astype(BF16)
    r = x - hi.astype(F32)
    mid = r.astype(BF16)
    lo = (r - mid.astype(F32)).astype(BF16)
    return hi, mid, lo


def _mask_dot(m_bf, x):
    hi, mid, lo = _split3(x)
    return (jnp.dot(m_bf, hi, preferred_element_type=F32)
            + jnp.dot(m_bf, mid, preferred_element_type=F32)
            + jnp.dot(m_bf, lo, preferred_element_type=F32))


def _sigmoid(x):
    return 1.0 / (1.0 + jnp.exp(-x))


def _silu(x):
    return x * _sigmoid(x)


def _rms(x):
    return x * lax.rsqrt(jnp.mean(x * x, axis=-1, keepdims=True) + NORM_EPS)


def _transpose_pad(x):
    r = x.shape[0]
    if r < HEAD_DIM:
        x = jnp.concatenate([x, jnp.zeros((HEAD_DIM - r, HEAD_DIM), x.dtype)], axis=0)
    return x.T


def _tri_masks(c):
    row = lax.broadcasted_iota(jnp.int32, (c, c), 0)
    col = lax.broadcasted_iota(jnp.int32, (c, c), 1)
    return col <= row, col < row


def _ada_kernel(c_ref, w_ref, b_ref, o_ref):
    cs = _silu(c_ref[...])
    w = w_ref[...]
    c_hi = cs.astype(BF16)
    c_lo = (cs - c_hi.astype(F32)).astype(BF16)
    w_hi = w.astype(BF16)
    w_lo = (w - w_hi.astype(F32)).astype(BF16)
    acc = jnp.dot(c_hi, w_hi, preferred_element_type=F32)
    acc += jnp.dot(c_lo, w_hi, preferred_element_type=F32)
    acc += jnp.dot(c_hi, w_lo, preferred_element_type=F32)
    o_ref[...] = acc + b_ref[...]


def _ada(c_all, w_ada, b_ada):
    m, d = c_all.shape
    n = w_ada.shape[-1]
    tn = _tile(n, 512, 128)
    return pl.pallas_call(
        _ada_kernel,
        grid=(n // tn,),
        in_specs=[pl.BlockSpec((m, d), lambda j: (0, 0)),
                  pl.BlockSpec((None, d, tn), lambda j: (0, 0, j)),
                  pl.BlockSpec((1, tn), lambda j: (0, j))],
        out_specs=pl.BlockSpec((m, tn), lambda j: (0, j)),
        out_shape=jax.ShapeDtypeStruct((m, n), F32),
        compiler_params=_cparams("arbitrary"),
        name="ada_mod",
    )(c_all, w_ada, b_ada)


def _mod_spec(mod, k, tm, rows_per_seq):
    d = mod.shape[-1]
    if mod.shape[1] == 1:
        tiles_per_seq = rows_per_seq // tm
        return pl.BlockSpec((None, 1, d), lambda i, *_: ((i // tiles_per_seq) * 6 + k, 0, 0))
    return pl.BlockSpec((None, tm, d), lambda i, *_: (k, i, 0))


def _prenorm_kernel(x_ref, g_ref, sc_ref, sh_ref, o_ref):
    xn = _rms(x_ref[...].astype(F32)) * g_ref[...]
    o_ref[...] = (xn * (1.0 + sc_ref[...]) + sh_ref[...]).astype(o_ref.dtype)


def _prenorm(x, g, mod, k_scale, k_shift, rows_per_seq, tm):
    m, d = x.shape
    return pl.pallas_call(
        _prenorm_kernel,
        grid=(m // tm,),
        in_specs=[pl.BlockSpec((tm, d), lambda i: (i, 0)),
                  pl.BlockSpec((1, d), lambda i: (0, 0)),
                  _mod_spec(mod, k_scale, tm, rows_per_seq),
                  _mod_spec(mod, k_shift, tm, rows_per_seq)],
        out_specs=pl.BlockSpec((tm, d), lambda i: (i, 0)),
        out_shape=jax.ShapeDtypeStruct((m, d), BF16),
        compiler_params=_cparams("parallel"),
        name="prenorm",
    )(x, g, mod, mod)


def _mm_kernel(a_ref, w_ref, o_ref):
    o_ref[...] = _bdot(a_ref[...], w_ref[...])


def _project(a, w, n_cols, tm, tn):
    m, kdim = a.shape
    return pl.pallas_call(
        _mm_kernel,
        grid=(m // tm, n_cols // tn),
        in_specs=[pl.BlockSpec((tm, kdim), lambda i, j: (i, 0)),
                  pl.BlockSpec((kdim, tn), lambda i, j: (0, j))],
        out_specs=pl.BlockSpec((tm, tn), lambda i, j: (i, j)),
        out_shape=jax.ShapeDtypeStruct((m, n_cols), F32),
        compiler_params=_cparams("parallel", "arbitrary"),
        name="in_proj",
    )(a, w)


REALIGN_ROWS = 64


def _realign_kernel(a_ref, b_ref, o_ref, *, shift):
    tn = o_ref.shape[1]

    def body(r, carry):
        rows = pl.ds(pl.multiple_of(r * REALIGN_ROWS, REALIGN_ROWS), REALIGN_ROWS)
        both = jnp.concatenate([a_ref[rows, :], b_ref[rows, :]], axis=1)
        o_ref[rows, :] = both[:, shift:shift + tn]
        return carry

    lax.fori_loop(0, o_ref.shape[0] // REALIGN_ROWS, body, 0)


def _realign_cols(w, off, width):
    kdim = w.shape[0]
    tn = _tile(width, 256, 128)
    b0, shift = divmod(off, tn)
    assert kdim % REALIGN_ROWS == 0 and off + width <= w.shape[1]
    last = pl.cdiv(w.shape[1], tn) - 1
    return pl.pallas_call(
        functools.partial(_realign_kernel, shift=shift),
        grid=(width // tn,),
        in_specs=[pl.BlockSpec((kdim, tn), lambda j: (0, b0 + j)),
                  pl.BlockSpec((kdim, tn), lambda j: (0, jnp.minimum(b0 + j + 1, last)))],
        out_specs=pl.BlockSpec((kdim, tn), lambda j: (0, j)),
        out_shape=jax.ShapeDtypeStruct((kdim, width), w.dtype),
        compiler_params=_cparams("parallel"),
        name="realign_cols",
    )(w, w)


def _hgrn_elementwise(hq, hf, lb):
    q = _silu(hq)
    f = lb + (1.0 - lb) * _sigmoid(hf)
    return q, jnp.log(f), 1.0 - f


def _hgrn_chunk_heads(q, g, k, inp, states, tril, tril_bf):
    c = q.shape[0]
    hb = len(states)
    heads = [slice(h * HEAD_DIM, (h + 1) * HEAD_DIM) for h in range(hb)]
    big_g = _mask_dot(tril_bf, g)
    g_ref = big_g[c // 2:c // 2 + 1]
    g_last = big_g[c - 1:c]
    qe = q * jnp.exp(big_g - g_ref)
    ke = k * jnp.exp(g_ref - big_g)
    qs = q * jnp.exp(big_g)
    kl = k * jnp.exp(g_last - big_g)
    d_hi, d_mid, d_lo = [p.astype(F32) for p in _split3(jnp.exp(g_last))]
    extra = jnp.concatenate([d_hi, d_mid, d_lo, jnp.zeros((SUBLANES - 3, q.shape[1]), F32)], axis=0)
    a = [jnp.where(tril, _bdot_nt(qe[:, sl], ke[:, sl]), 0.0) for sl in heads]
    o = [_bdot(a[h], inp[:, sl]) + _bdot(qs[:, sl], states[h]) for h, sl in enumerate(heads)]
    t = [_transpose_pad(jnp.concatenate([kl[:, sl], extra[:, sl]], axis=0))[:, :c + SUBLANES] for sl in heads]
    piece_rows = lax.broadcasted_iota(jnp.int32, (SUBLANES, HEAD_DIM), 0) < 3
    ones_rows = jnp.where(piece_rows, 1.0, 0.0)
    zeros_c = jnp.zeros((c, HEAD_DIM), F32)
    new_states = []
    for h, sl in enumerate(heads):
        rhs = jnp.concatenate([jnp.concatenate([inp[:, sl], zeros_c], axis=1),
                               jnp.concatenate([jnp.zeros((SUBLANES, HEAD_DIM), F32), ones_rows], axis=1)], axis=0)
        ud = _bdot(t[h], rhs)
        new_states.append(ud[:, HEAD_DIM:] * states[h] + ud[:, :HEAD_DIM])
    return list(zip(o, new_states))


def _head_out(o, gate, norm_g):
    mean_sq = _lane_sum_bcast(o * o) * (1.0 / HEAD_DIM)
    return o * lax.rsqrt(mean_sq + NORM_EPS) * norm_g * _silu(gate)


def _hgrn_prompt_kernel(q_ref, f_ref, i_ref, g_ref, lb_ref, ng_ref, o_ref, st_ref, s_scr, *, hb):
    t = pl.program_id(2)
    c = HG_CHUNK

    @pl.when(t == 0)
    def _():
        s_scr[...] = jnp.zeros_like(s_scr)

    tril, _ = _tri_masks(c)
    tril_bf = jnp.where(tril, 1.0, 0.0).astype(BF16)
    lb = lb_ref[...]
    ng = ng_ref[...]

    def chunk(ci, carry):
        rows = pl.ds(pl.multiple_of(ci * c, c), c)
        q, g, k = _hgrn_elementwise(q_ref[rows, :], f_ref[rows, :], lb)
        inp = i_ref[rows, :]
        gate = g_ref[rows, :]
        res = _hgrn_chunk_heads(q, g, k, inp, [s_scr[h] for h in range(hb)], tril, tril_bf)
        for h, (o, s_new) in enumerate(res):
            sl = slice(h * HEAD_DIM, (h + 1) * HEAD_DIM)
            s_scr[h] = s_new
            o_ref[rows, sl] = _head_out(o, gate[:, sl], ng).astype(o_ref.dtype)
        return carry

    lax.fori_loop(0, q_ref.shape[0] // c, chunk, 0)

    @pl.when(t == pl.num_programs(2) - 1)
    def _():
        st_ref[...] = s_scr[...].astype(st_ref.dtype)


def _hgrn_prompt(proj, lb, norm_g, batch, seq, heads, state_dtype):
    hb = _tile(heads, 8, 1)
    tt = _tile(seq, 256, HG_CHUNK)
    w = hb * HEAD_DIM
    nhb = heads // hb
    nt = seq // tt

    def col(seg):
        return pl.BlockSpec((tt, w), lambda b, h, t: (b * nt + t, seg * nhb + h))

    return pl.pallas_call(
        functools.partial(_hgrn_prompt_kernel, hb=hb),
        grid=(batch, nhb, nt),
        in_specs=[col(0), col(1), col(2), col(3),
                  pl.BlockSpec((1, w), lambda b, h, t: (0, h)),
                  pl.BlockSpec((1, HEAD_DIM), lambda b, h, t: (0, 0))],
        out_specs=[pl.BlockSpec((tt, w), lambda b, h, t: (b * nt + t, h)),
                   pl.BlockSpec((None, hb, HEAD_DIM, HEAD_DIM), lambda b, h, t: (b, h, 0, 0))],
        out_shape=[jax.ShapeDtypeStruct((batch * seq, heads * HEAD_DIM), BF16),
                   jax.ShapeDtypeStruct((batch, heads, HEAD_DIM, HEAD_DIM), state_dtype)],
        scratch_shapes=[pltpu.VMEM((hb, HEAD_DIM, HEAD_DIM), F32)],
        compiler_params=_cparams("parallel", "parallel", "arbitrary"),
        name="hgrn_prompt",
    )(proj, proj, proj, proj, lb, norm_g)


def _hgrn_sample_kernel(q_ref, f_ref, i_ref, g_ref, lb_ref, ng_ref, s0_ref, o_ref, st_ref, *, hb, seq):
    c = SUBLANES
    per_group = c // seq
    tril, _ = _tri_masks(c)
    tril_bf = jnp.where(tril, 1.0, 0.0).astype(BF16)
    lb = lb_ref[...]
    ng = ng_ref[...]
    row = lax.broadcasted_iota(jnp.int32, (c, 1), 0)

    def group(gi, carry):
        rows = pl.ds(pl.multiple_of(gi * c, c), c)
        q, g, k = _hgrn_elementwise(q_ref[rows, :], f_ref[rows, :], lb)
        inp = i_ref[rows, :]
        gate = g_ref[rows, :]
        valid = [(row >= p * seq) & (row < (p + 1) * seq) for p in range(per_group)]
        masked = lambda x: jnp.concatenate([jnp.where(v, x, 0.0) for v in valid], axis=1)
        states = [s0_ref[gi * per_group + p, h].astype(F32) for p in range(per_group) for h in range(hb)]
        res = _hgrn_chunk_heads(masked(q), masked(g), masked(k), jnp.concatenate([inp] * per_group, axis=1),
                                states, tril, tril_bf)
        for h in range(hb):
            sl = slice(h * HEAD_DIM, (h + 1) * HEAD_DIM)
            o_all = jnp.zeros((c, HEAD_DIM), F32)
            for p in range(per_group):
                o, s_new = res[p * hb + h]
                st_ref[gi * per_group + p, h] = s_new.astype(st_ref.dtype)
                o_all = jnp.where(valid[p], o, o_all)
            o_ref[rows, sl] = _head_out(o_all, gate[:, sl], ng).astype(o_ref.dtype)
        return carry

    lax.fori_loop(0, q_ref.shape[0] // c, group, 0)


def _hgrn_sample(proj, lb, norm_g, s0, batch, seq, heads):
    assert SUBLANES % seq == 0
    hb = _tile(heads, 4, 1)
    nb = _tile(batch, 8, SUBLANES // seq)
    w = hb * HEAD_DIM
    nhb = heads // hb
    r = nb * seq

    def col(seg):
        return pl.BlockSpec((r, w), lambda b, h: (b, seg * nhb + h))

    st_spec = pl.BlockSpec((nb, hb, HEAD_DIM, HEAD_DIM), lambda b, h: (b, h, 0, 0))
    return pl.pallas_call(
        functools.partial(_hgrn_sample_kernel, hb=hb, seq=seq),
        grid=(batch // nb, nhb),
        in_specs=[col(0), col(1), col(2), col(3),
                  pl.BlockSpec((1, w), lambda b, h: (0, h)),
                  pl.BlockSpec((1, HEAD_DIM), lambda b, h: (0, 0)),
                  st_spec],
        out_specs=[pl.BlockSpec((r, w), lambda b, h: (b, h)), st_spec],
        out_shape=[jax.ShapeDtypeStruct((batch * seq, heads * HEAD_DIM), BF16),
                   jax.ShapeDtypeStruct(s0.shape, s0.dtype)],
        compiler_params=_cparams("parallel", "parallel"),
        name="hgrn_sample",
    )(proj, proj, proj, proj, lb, norm_g, s0)


def _gdn_gates_kernel(x_ref, alog_ref, dtb_ref, beta_ref, gam_ref, gtot_ref, *, nv, chunk):
    x = x_ref[...]
    r = x.shape[0]
    beta_ref[...] = _sigmoid(x[:, :nv])
    z = x[:, nv:] + dtb_ref[...]
    softplus = jnp.maximum(z, 0.0) + jnp.log(1.0 + jnp.exp(-jnp.abs(z)))
    g = -jnp.exp(alog_ref[...]) * softplus
    row = lax.broadcasted_iota(jnp.int32, (r, r), 0)
    col = lax.broadcasted_iota(jnp.int32, (r, r), 1)
    same = (row // chunk) == (col // chunk)
    gam_ref[...] = _mask_dot(jnp.where(same & (col <= row), 1.0, 0.0).astype(BF16), g)
    gtot_ref[...] = _mask_dot(jnp.where(same, 1.0, 0.0).astype(BF16), g)


def _gdn_gates(x, a_log, dt_bias, chunk):
    m, two_nv = x.shape
    nv = two_nv // 2
    tr = _tile(m, 256, max(chunk, SUBLANES))
    out = jax.ShapeDtypeStruct((m, nv), F32)
    return pl.pallas_call(
        functools.partial(_gdn_gates_kernel, nv=nv, chunk=chunk),
        grid=(m // tr,),
        in_specs=[pl.BlockSpec((tr, two_nv), lambda i: (i, 0)),
                  pl.BlockSpec((1, nv), lambda i: (0, 0)),
                  pl.BlockSpec((1, nv), lambda i: (0, 0))],
        out_specs=[pl.BlockSpec((tr, nv), lambda i: (i, 0))] * 3,
        out_shape=[out, out, out],
        compiler_params=_cparams("parallel"),
        name="gdn_gates",
    )(x, a_log, dt_bias)


def _lane_sum_bcast(x):
    hi = x.astype(BF16)
    lo = (x - hi.astype(F32)).astype(BF16)
    return jnp.dot(jnp.concatenate([hi, lo], axis=1), jnp.ones((2 * HEAD_DIM, HEAD_DIM), BF16),
                   preferred_element_type=F32)


def _l2_heads(y, mult):
    parts = []
    for h in range(y.shape[1] // HEAD_DIM):
        yh = y[:, h * HEAD_DIM:(h + 1) * HEAD_DIM]
        parts.append(yh * (lax.rsqrt(_lane_sum_bcast(yh * yh) + NORM_EPS) * mult))
    return jnp.concatenate(parts, axis=1)


def _conv_finish(y, kind_q, kind_k, o_ref, scale):
    y = _silu(y)

    @pl.when(kind_q)
    def _():
        o_ref[...] = _l2_heads(y, scale)

    @pl.when(kind_k)
    def _():
        o_ref[...] = _l2_heads(y, 1.0)

    @pl.when(jnp.logical_not(kind_q | kind_k))
    def _():
        o_ref[...] = y


CONV_ROWS = 32


def _conv_prompt_kernel(cur_ref, prev_ref, buf_ref, cw_ref, o_ref, x_scr, *, tiles_per_seq, nq_tiles, scale):
    i = pl.program_id(0)
    j = pl.program_id(1)
    first = (i % tiles_per_seq) == 0
    tt = cur_ref.shape[0]
    x_scr[0:SUBLANES, :] = jnp.where(first, buf_ref[...], prev_ref[...])
    x_scr[SUBLANES:SUBLANES + tt, :] = cur_ref[...]
    cw = cw_ref[...]

    def run(finish):
        for r0 in range(0, tt, CONV_ROWS):
            y = None
            for s in range(CONV_W):
                lo = SUBLANES + r0 - s
                term = cw[CONV_W - 1 - s:CONV_W - s] * x_scr[lo:lo + CONV_ROWS, :]
                y = term if y is None else y + term
            o_ref[r0:r0 + CONV_ROWS, :] = finish(_silu(y))

    kind_q = j < nq_tiles
    kind_k = (j >= nq_tiles) & (j < 2 * nq_tiles)
    pl.when(kind_q)(lambda: run(lambda y: _l2_heads(y, scale)))
    pl.when(kind_k)(lambda: run(lambda y: _l2_heads(y, 1.0)))
    pl.when(jnp.logical_not(kind_q | kind_k))(lambda: run(lambda y: y))


def _conv_prompt(proj, col0, conv_buf8, conv_w, batch, seq, qk_w, ch):
    wc = _tile(qk_w, 512, HEAD_DIM)
    tt = _tile(seq, 256, SUBLANES)
    nt = seq // tt
    c0 = col0 // wc
    assert col0 % wc == 0
    return pl.pallas_call(
        functools.partial(_conv_prompt_kernel, tiles_per_seq=nt, nq_tiles=qk_w // wc,
                          scale=float(HEAD_DIM) ** -0.5),
        grid=(batch * nt, ch // wc),
        in_specs=[pl.BlockSpec((tt, wc), lambda i, j: (i, c0 + j)),
                  pl.BlockSpec((SUBLANES, wc),
                               lambda i, j: (jnp.maximum(i * (tt // SUBLANES) - 1, 0), c0 + j)),
                  pl.BlockSpec((SUBLANES, wc), lambda i, j: (i // nt, j)),
                  pl.BlockSpec((None, CONV_W, wc), lambda i, j: (0, 0, j))],
        out_specs=pl.BlockSpec((tt, wc), lambda i, j: (i, j)),
        out_shape=jax.ShapeDtypeStruct((batch * seq, ch), F32),
        scratch_shapes=[pltpu.VMEM((tt + SUBLANES, wc), F32)],
        compiler_params=_cparams("parallel", "parallel"),
        name="gdn_conv_prompt",
    )(proj, proj, conv_buf8, conv_w)


def _conv_sample_kernel(u_ref, buf_ref, cw_ref, o_ref, *, seq, nq_tiles, scale):
    j = pl.program_id(1)
    u = u_ref[...]
    buf = buf_ref[...]
    r = u.shape[0]
    tok = lax.broadcasted_iota(jnp.int32, (r, 1), 0) % seq
    cw = cw_ref[...]
    y = cw[CONV_W - 1:CONV_W] * u
    for s in range(1, CONV_W):
        from_u = pltpu.roll(u, s, axis=0)
        up = CONV_W - 1 - s
        from_buf = pltpu.roll(buf, r - up, axis=0) if up else buf
        y = y + cw[CONV_W - 1 - s:CONV_W - s] * jnp.where(tok >= s, from_u, from_buf)
    _conv_finish(y, j < nq_tiles, (j >= nq_tiles) & (j < 2 * nq_tiles), o_ref, scale)


def _conv_sample(proj, col0, buf_rows, conv_w, seq, qk_w, ch):
    m = proj.shape[0]
    wc = _tile(qk_w, 512, HEAD_DIM)
    tr = _tile(m, 256, SUBLANES)
    c0 = col0 // wc
    assert col0 % wc == 0 and tr % seq == 0 and seq >= CONV_W - 1
    return pl.pallas_call(
        functools.partial(_conv_sample_kernel, seq=seq, nq_tiles=qk_w // wc,
                          scale=float(HEAD_DIM) ** -0.5),
        grid=(m // tr, ch // wc),
        in_specs=[pl.BlockSpec((tr, wc), lambda i, j: (i, c0 + j)),
                  pl.BlockSpec((tr, wc), lambda i, j: (i, j)),
                  pl.BlockSpec((None, CONV_W, wc), lambda i, j: (0, 0, j))],
        out_specs=pl.BlockSpec((tr, wc), lambda i, j: (i, j)),
        out_shape=jax.ShapeDtypeStruct((m, ch), F32),
        compiler_params=_cparams("parallel", "parallel"),
        name="gdn_conv_sample",
    )(proj, buf_rows, conv_w)


GDN_TILE = 64
HALF_LANES = 64


def _mm_tile(a, b):
    return _bdot(a[:, :GDN_TILE], b)


def _lane_bcast_all(x, sel):
    return _mask_dot_right(x, sel)


def _mask_dot_right(x, m_bf):
    hi, mid, lo = _split3(x)
    return (jnp.dot(hi, m_bf, preferred_element_type=F32)
            + jnp.dot(mid, m_bf, preferred_element_type=F32)
            + jnp.dot(lo, m_bf, preferred_element_type=F32))


def _head_select(hb):
    head = lax.broadcasted_iota(jnp.int32, (hb, hb * HEAD_DIM), 0)
    lane_head = lax.broadcasted_iota(jnp.int32, (hb, hb * HEAD_DIM), 1) // HEAD_DIM
    return jnp.where(head == lane_head, 1.0, 0.0).astype(BF16)


def _unit_lower_inverses(a_list, rows, cols, block):
    eye = jnp.where(rows == cols, 1.0, 0.0)
    pairs = (rows >> 1) == (cols >> 1)
    d = [eye - jnp.where(pairs, a, 0.0) for a in a_list]
    b = 2
    while b < block:
        sh = b.bit_length()
        coupling = ((rows >> sh) == (cols >> sh)) & ((rows >> (sh - 1)) != (cols >> (sh - 1)))
        m1 = [_mm_tile(jnp.where(coupling, a, 0.0), di) for a, di in zip(a_list, d)]
        m2 = [_mm_tile(di, mi) for di, mi in zip(d, m1)]
        d = [di - mi for di, mi in zip(d, m2)]
        b *= 2
    return d


def _gdn_pair_products(q, k):
    k_pad = jnp.concatenate([k, jnp.zeros_like(k)], axis=0)
    kq = _bdot_nt(jnp.concatenate([k, q], axis=0), k_pad)
    return kq[:GDN_TILE], kq[GDN_TILE:]


def _gdn_wy(tiles, rep, rows, cols, incl, strict, block):
    heads = [slice(h * HEAD_DIM, (h + 1) * HEAD_DIM) for h in range(len(tiles[0][2]))]
    prods = [[_gdn_pair_products(q, k) for q, k in zip(qs, ks)] for qs, ks, *_ in tiles]
    decays, a_list = [], []
    for (qs, ks, vs, bc, gc, gr), prod in zip(tiles, prods):
        for h, sl in enumerate(heads):
            decay = jnp.where(incl, jnp.exp(gc[:, sl] - gr[h:h + 1, :]), 0.0)
            decays.append(decay)
            a_list.append(jnp.where(strict, bc[:, sl] * prod[h // rep][0] * decay, 0.0))
    t_inv = _unit_lower_inverses(a_list, rows, cols, block)
    xs = []
    for t, (qs, ks, vs, bc, gc, gr) in enumerate(tiles):
        for h, sl in enumerate(heads):
            beta = bc[:, sl]
            rhs = jnp.concatenate([vs[h] * beta, ks[h // rep] * (beta * jnp.exp(gc[:, sl]))], axis=1)
            xs.append(_mm_tile(t_inv[t * len(heads) + h], rhs))
    out = []
    for t in range(len(tiles)):
        out.append([(xs[t * len(heads) + h][:, :HEAD_DIM], xs[t * len(heads) + h][:, HEAD_DIM:],
                     prods[t][h // rep][1] * decays[t * len(heads) + h]) for h in range(len(heads))])
    return out


def _tile_iotas():
    shape = (GDN_TILE, HEAD_DIM)
    return lax.broadcasted_iota(jnp.int32, shape, 0), lax.broadcasted_iota(jnp.int32, shape, 1)


def _gdn_prompt_kernel(q_ref, k_ref, v_ref, z_ref, bc_ref, gc_ref, gt_ref, gr_ref, ng_ref,
                       o_ref, st_ref, s_scr, u_scr, w_scr, a_scr, *, hb, rep):
    t = pl.program_id(2)
    c = GDN_CHUNK
    n_chunks = q_ref.shape[0] // c

    @pl.when(t == 0)
    def _():
        s_scr[...] = jnp.zeros_like(s_scr)

    rows_i, cols_i = _tile_iotas()
    incl = cols_i <= rows_i
    strict = cols_i < rows_i
    ng = ng_ref[...]
    sel = _head_select(hb)

    def chunk_operands(ci):
        rows = pl.ds(ci * c, c)
        head = lambda ref, i: ref[rows, i * HEAD_DIM:(i + 1) * HEAD_DIM]
        return ([head(q_ref, i) for i in range(hb // rep)], [head(k_ref, i) for i in range(hb // rep)],
                [head(v_ref, h) for h in range(hb)], _lane_bcast_all(bc_ref[rows, :], sel),
                _lane_bcast_all(gc_ref[rows, :], sel), gr_ref[ci])

    wy = _gdn_wy([chunk_operands(ci) for ci in range(n_chunks)], rep, rows_i, cols_i, incl, strict, c)
    for ci in range(n_chunks):
        rows = pl.ds(ci * c, c)
        for h, (u, w, attn) in enumerate(wy[ci]):
            sl = slice(h * HEAD_DIM, (h + 1) * HEAD_DIM)
            u_scr[rows, sl] = u
            w_scr[rows, sl] = w.astype(w_scr.dtype)
            a_scr[rows, sl] = attn.astype(a_scr.dtype)

    def recur(ci, carry):
        rows = pl.ds(pl.multiple_of(ci * c, c), c)
        gc = _lane_bcast_all(gc_ref[rows, :], sel)
        gt = _lane_bcast_all(gt_ref[rows, :], sel)
        heads = [slice(h * HEAD_DIM, (h + 1) * HEAD_DIM) for h in range(hb)]
        pairs = range(hb // rep)
        s_pairs = [s_scr[pr] for pr in pairs]
        qs = [_bdot(q_ref[rows, heads[pr]], s_pairs[pr]) for pr in pairs]
        v_new = [u_scr[rows, sl] - _bdot(w_scr[rows, sl], s_pairs[h // rep][:, heads[h % rep]])
                 for h, sl in enumerate(heads)]
        outs = [qs[h // rep][:, heads[h % rep]] * jnp.exp(gc[:, sl]) + _bdot(a_scr[rows, sl][:, :c], v_new[h])
                for h, sl in enumerate(heads)]
        for pr in pairs:
            k_t = _transpose_pad(k_ref[rows, heads[pr]])[:, :c]
            mine = range(pr * rep, (pr + 1) * rep)
            v_scaled = jnp.concatenate([v_new[h] * jnp.exp(gt[:, heads[h]] - gc[:, heads[h]]) for h in mine], axis=1)
            decays = jnp.concatenate([jnp.exp(gt[0:1, heads[h]]) for h in mine], axis=1)
            s_scr[pr] = decays * s_pairs[pr] + _bdot(k_t, v_scaled)
        for h, sl in enumerate(heads):
            o_ref[rows, sl] = _head_out(outs[h], z_ref[rows, sl], ng).astype(o_ref.dtype)
        return carry

    lax.fori_loop(0, n_chunks, recur, 0)

    @pl.when(t == pl.num_programs(2) - 1)
    def _():
        for h in range(hb):
            st_ref[h] = s_scr[h // rep][:, (h % rep) * HEAD_DIM:(h % rep + 1) * HEAD_DIM].astype(st_ref.dtype)


def _gdn_layouts(beta, gam, gtot, nv, hb):
    m = beta.shape[0]
    nhb = nv // hb
    as_cols = lambda a: a.reshape(m, nhb, hb).transpose(1, 0, 2)
    g_rows = gam.reshape(m // GDN_TILE, GDN_TILE, nhb, hb).transpose(2, 0, 3, 1)
    g_rows = jnp.pad(g_rows, ((0, 0), (0, 0), (0, 0), (0, HEAD_DIM - GDN_TILE)))
    return as_cols(beta), as_cols(gam), as_cols(gtot), g_rows


def _gdn_prompt(qkv, proj, z_col0, gate_arrays, norm_g, batch, seq, qk_heads, v_heads, state_dtype):
    rep = v_heads // qk_heads
    hb = _tile(v_heads, 8, rep)
    tt = _tile(seq, 256, GDN_CHUNK)
    nt = seq // tt
    nhb = v_heads // hb
    wq = (hb // rep) * HEAD_DIM
    wv = hb * HEAD_DIM
    qk_w = qk_heads * HEAD_DIM
    bc, gc, gt, gr = _gdn_layouts(*gate_arrays, v_heads, hb)
    k0 = qk_w // wq
    v0 = 2 * qk_w // wv
    z0 = z_col0 // wv
    assert z_col0 % wv == 0 and GDN_CHUNK == GDN_TILE
    col_spec = pl.BlockSpec((None, tt, hb), lambda b, h, t: (h, b * nt + t, 0))
    return pl.pallas_call(
        functools.partial(_gdn_prompt_kernel, hb=hb, rep=rep),
        grid=(batch, nhb, nt),
        in_specs=[pl.BlockSpec((tt, wq), lambda b, h, t: (b * nt + t, h)),
                  pl.BlockSpec((tt, wq), lambda b, h, t: (b * nt + t, k0 + h)),
                  pl.BlockSpec((tt, wv), lambda b, h, t: (b * nt + t, v0 + h)),
                  pl.BlockSpec((tt, wv), lambda b, h, t: (b * nt + t, z0 + h)),
                  col_spec, col_spec, col_spec,
                  pl.BlockSpec((None, tt // GDN_TILE, hb, HEAD_DIM), lambda b, h, t: (h, b * nt + t, 0, 0)),
                  pl.BlockSpec((1, HEAD_DIM), lambda b, h, t: (0, 0))],
        out_specs=[pl.BlockSpec((tt, wv), lambda b, h, t: (b * nt + t, h)),
                   pl.BlockSpec((None, hb, HEAD_DIM, HEAD_DIM), lambda b, h, t: (b, h, 0, 0))],
        out_shape=[jax.ShapeDtypeStruct((batch * seq, v_heads * HEAD_DIM), BF16),
                   jax.ShapeDtypeStruct((batch, v_heads, HEAD_DIM, HEAD_DIM), state_dtype)],
        scratch_shapes=[pltpu.VMEM((hb // rep, HEAD_DIM, rep * HEAD_DIM), F32),
                        pltpu.VMEM((tt, wv), F32),
                        pltpu.VMEM((tt, wv), BF16),
                        pltpu.VMEM((tt, wv), BF16)],
        compiler_params=_cparams("parallel", "parallel", "arbitrary"),
        name="gdn_prompt",
    )(qkv, qkv, qkv, proj, bc, gc, gt, gr, norm_g)


def _gdn_sample_kernel(q_ref, k_ref, v_ref, z_ref, bc_ref, gc_ref, gt_ref, gr_ref, ng_ref, s0_ref,
                       o_ref, st_ref, *, hb, rep, seq):
    c = GDN_TILE
    nseq = c // seq
    per_group = SUBLANES // seq
    lsh = seq.bit_length() - 1
    rows_i, cols_i = _tile_iotas()
    same = (rows_i >> lsh) == (cols_i >> lsh)
    incl = (cols_i <= rows_i) & same
    strict = (cols_i < rows_i) & same
    row8 = lax.broadcasted_iota(jnp.int32, (SUBLANES, 1), 0)
    lane_seq = lax.broadcasted_iota(jnp.int32, (1, HEAD_DIM), 1) >> lsh
    ng = ng_ref[...]
    sel = _head_select(hb)
    bc = _lane_bcast_all(bc_ref[...], sel)
    gc = _lane_bcast_all(gc_ref[...], sel)
    gt = _lane_bcast_all(gt_ref[...], sel)
    gr = gr_ref[0]
    head = lambda ref, i: ref[:, i * HEAD_DIM:(i + 1) * HEAD_DIM]
    qs = [head(q_ref, i) for i in range(hb // rep)]
    ks = [head(k_ref, i) for i in range(hb // rep)]
    wy = _gdn_wy([(qs, ks, [head(v_ref, h) for h in range(hb)], bc, gc, gr)], rep,
                 rows_i, cols_i, incl, strict, seq)[0]
    k_t_seq = []
    for k in ks:
        k_t = _transpose_pad(k)
        k_t_seq.append(jnp.concatenate([jnp.where(lane_seq == b, k_t, 0.0)[:, :c] for b in range(nseq)],
                                       axis=0).astype(BF16))
    qe = [qs[h // rep] * jnp.exp(head(gc, h)) for h in range(hb)]
    ws_parts = [[] for _ in range(hb)]
    qs_parts = [[] for _ in range(hb)]
    for g in range(c // SUBLANES):
        gsl = slice(g * SUBLANES, (g + 1) * SUBLANES)
        for h in range(hb):
            acc = jnp.zeros((2 * SUBLANES, HEAD_DIM), F32)
            for p in range(per_group):
                valid = (row8 >= p * seq) & (row8 < (p + 1) * seq)
                lhs = jnp.concatenate([jnp.where(valid, wy[h][1][gsl], 0.0),
                                       jnp.where(valid, qe[h][gsl], 0.0)], axis=0)
                acc = acc + _bdot(lhs, s0_ref[g * per_group + p, h])
            ws_parts[h].append(acc[:SUBLANES])
            qs_parts[h].append(acc[SUBLANES:])
    v_new = [wy[h][0] - jnp.concatenate(ws_parts[h], axis=0) for h in range(hb)]
    outs = [jnp.concatenate(qs_parts[h], axis=0) + _bdot(wy[h][2][:, :c], v_new[h]) for h in range(hb)]
    upds = [jnp.dot(k_t_seq[h // rep], (v_new[h] * jnp.exp(head(gt, h) - head(gc, h))).astype(BF16),
                    preferred_element_type=F32) for h in range(hb)]
    for h in range(hb):
        o_ref[:, h * HEAD_DIM:(h + 1) * HEAD_DIM] = _head_out(outs[h], head(z_ref, h), ng).astype(o_ref.dtype)
        decay = jnp.exp(head(gt, h))
        for b in range(nseq):
            st_ref[b, h] = (decay[b * seq:b * seq + 1] * s0_ref[b, h].astype(F32)
                            + upds[h][b * HEAD_DIM:(b + 1) * HEAD_DIM]).astype(st_ref.dtype)


def _gdn_sample(qkv, proj, z_col0, gate_arrays, norm_g, s0, batch, seq, qk_heads, v_heads):
    rep = v_heads // qk_heads
    hb = _tile(v_heads, 4, rep)
    assert GDN_TILE % seq == 0 and SUBLANES % seq == 0 and seq & (seq - 1) == 0
    nb = GDN_TILE // seq
    assert batch % nb == 0
    r = GDN_TILE
    nhb = v_heads // hb
    wq = (hb // rep) * HEAD_DIM
    wv = hb * HEAD_DIM
    qk_w = qk_heads * HEAD_DIM
    bc, gc, gt, gr = _gdn_layouts(*gate_arrays, v_heads, hb)
    k0 = qk_w // wq
    v0 = 2 * qk_w // wv
    z0 = z_col0 // wv
    assert z_col0 % wv == 0
    st_spec = pl.BlockSpec((nb, hb, HEAD_DIM, HEAD_DIM), lambda b, h: (b, h, 0, 0))
    col_spec = pl.BlockSpec((None, r, hb), lambda b, h: (h, b, 0))
    return pl.pallas_call(
        functools.partial(_gdn_sample_kernel, hb=hb, rep=rep, seq=seq),
        grid=(batch // nb, nhb),
        in_specs=[pl.BlockSpec((r, wq), lambda b, h: (b, h)),
                  pl.BlockSpec((r, wq), lambda b, h: (b, k0 + h)),
                  pl.BlockSpec((r, wv), lambda b, h: (b, v0 + h)),
                  pl.BlockSpec((r, wv), lambda b, h: (b, z0 + h)),
                  col_spec, col_spec, col_spec,
                  pl.BlockSpec((None, 1, hb, HEAD_DIM), lambda b, h: (h, b, 0, 0)),
                  pl.BlockSpec((1, HEAD_DIM), lambda b, h: (0, 0)),
                  st_spec],
        out_specs=[pl.BlockSpec((r, wv), lambda b, h: (b, h)), st_spec],
        out_shape=[jax.ShapeDtypeStruct((batch * seq, v_heads * HEAD_DIM), BF16),
                   jax.ShapeDtypeStruct(s0.shape, s0.dtype)],
        compiler_params=_cparams("parallel", "parallel"),
        name="gdn_sample",
    )(qkv, qkv, qkv, proj, bc, gc, gt, gr, norm_g, s0)


def _merge_kernel(oh_ref, og_ref, wa_ref, wb_ref, ga_ref, gb_ref, o_ref):
    ya = _bdot(oh_ref[...], wa_ref[...])
    yb = _bdot(og_ref[...], wb_ref[...])
    o_ref[...] = (_sigmoid(ga_ref[...]) * ya + _sigmoid(gb_ref[...]) * yb).astype(o_ref.dtype)


def _merge(o_hg, o_gdn, w_a, w_b, gates, tm):
    m, ka = o_hg.shape
    kb = o_gdn.shape[1]
    d = w_a.shape[-1]
    tn = _tile(d, 256, 128)
    nj = d // tn
    return pl.pallas_call(
        _merge_kernel,
        grid=(m // tm, nj),
        in_specs=[pl.BlockSpec((tm, ka), lambda i, j: (i, 0), pipeline_mode=pl.Buffered(1)),
                  pl.BlockSpec((tm, kb), lambda i, j: (i, 0), pipeline_mode=pl.Buffered(1)),
                  pl.BlockSpec((None, ka, tn), lambda i, j: (0, 0, j)),
                  pl.BlockSpec((None, kb, tn), lambda i, j: (0, 0, j)),
                  pl.BlockSpec((tm, tn), lambda i, j: (i, j)),
                  pl.BlockSpec((tm, tn), lambda i, j: (i, nj + j))],
        out_specs=pl.BlockSpec((tm, tn), lambda i, j: (i, j)),
        out_shape=jax.ShapeDtypeStruct((m, d), BF16),
        compiler_params=_cparams("parallel", "arbitrary"),
        name="merge",
    )(o_hg, o_gdn, w_a, w_b, gates, gates)


def _outproj_kernel(m_ref, w_ref, x_ref, g1_ref, sc_ref, sh_ref, ng_ref, h_ref, a_ref, acc_ref):
    kk = pl.program_id(1)

    @pl.when(kk == 0)
    def _():
        acc_ref[...] = jnp.zeros_like(acc_ref)

    acc_ref[...] += _bdot(m_ref[...], w_ref[...])

    @pl.when(kk == pl.num_programs(1) - 1)
    def _():
        ng = ng_ref[...]
        h = x_ref[...].astype(F32) + g1_ref[...] * (_rms(acc_ref[...]) * ng[0:1])
        h_ref[...] = h
        a_ref[...] = ((_rms(h) * ng[1:2]) * (1.0 + sc_ref[...]) + sh_ref[...]).astype(a_ref.dtype)


def _outproj(merged, w_o, x, mod, norm_g2, rows_per_seq, tm):
    m, d = x.shape
    kdim = merged.shape[1]
    tk = _tile(kdim, 512, 128)
    row = lambda i, k: (i, 0)
    return pl.pallas_call(
        _outproj_kernel,
        grid=(m // tm, kdim // tk),
        in_specs=[pl.BlockSpec((tm, tk), lambda i, k: (i, k)),
                  pl.BlockSpec((None, tk, d), lambda i, k: (0, k, 0)),
                  pl.BlockSpec((tm, d), row),
                  _mod_spec(mod, 2, tm, rows_per_seq),
                  _mod_spec(mod, 4, tm, rows_per_seq),
                  _mod_spec(mod, 3, tm, rows_per_seq),
                  pl.BlockSpec((2, d), lambda i, k: (0, 0))],
        out_specs=[pl.BlockSpec((tm, d), row), pl.BlockSpec((tm, d), row)],
        out_shape=[jax.ShapeDtypeStruct((m, d), F32), jax.ShapeDtypeStruct((m, d), BF16)],
        scratch_shapes=[pltpu.VMEM((tm, d), F32)],
        compiler_params=_cparams("parallel", "arbitrary"),
        name="out_proj",
    )(merged, w_o, x, mod, mod, mod, norm_g2)


def _cast_kernel(x_ref, o_ref):
    o_ref[...] = x_ref[...].astype(o_ref.dtype)


def _to_bf16(w):
    kdim, n = w.shape
    tk = _tile(kdim, 512, 16)
    return pl.pallas_call(
        _cast_kernel,
        grid=(kdim // tk,),
        in_specs=[pl.BlockSpec((tk, n), lambda i: (i, 0))],
        out_specs=pl.BlockSpec((tk, n), lambda i: (i, 0)),
        out_shape=jax.ShapeDtypeStruct((kdim, n), BF16),
        compiler_params=_cparams("parallel"),
        name="weight_to_bf16",
    )(w)


def _ffn_up_kernel(a_ref, w_ref, o_ref):
    o_ref[...] = jnp.square(jnp.maximum(_bdot(a_ref[...], w_ref[...]), 0.0)).astype(o_ref.dtype)


def _ffn_up(a, w_up, tm):
    m, d = a.shape
    dff = w_up.shape[-1]
    tf = _tile(dff, 512, 128)
    return pl.pallas_call(
        _ffn_up_kernel,
        grid=(m // tm, dff // tf),
        in_specs=[pl.BlockSpec((tm, d), lambda i, f: (i, 0)),
                  pl.BlockSpec((None, d, tf), lambda i, f: (0, 0, f))],
        out_specs=pl.BlockSpec((tm, tf), lambda i, f: (i, f)),
        out_shape=jax.ShapeDtypeStruct((m, dff), BF16),
        compiler_params=_cparams("parallel", "arbitrary"),
        name="ffn_up",
    )(a, w_up)


EPILOGUE_ROWS = 128


def _ffn_down_kernel(hid_ref, w_ref, h_ref, g2_ref, ng_ref, y_ref):
    j = pl.program_id(1)
    tn = w_ref.shape[1]
    y_ref[:, pl.ds(pl.multiple_of(j * tn, tn), tn)] = _bdot(hid_ref[...], w_ref[...])

    @pl.when(j == pl.num_programs(1) - 1)
    def _():
        step = min(EPILOGUE_ROWS, y_ref.shape[0])

        def body(r, carry):
            rows = pl.ds(pl.multiple_of(r * step, step), step)
            g2 = g2_ref[rows, :] if g2_ref.shape[0] > 1 else g2_ref[...]
            y_ref[rows, :] = h_ref[rows, :] + g2 * (_rms(y_ref[rows, :]) * ng_ref[...])
            return carry

        lax.fori_loop(0, y_ref.shape[0] // step, body, 0)


def _ffn_down(hid, w_down, h, mod, norm_g, rows_per_seq, tm, out_dtype):
    m, d = h.shape
    dff = hid.shape[1]
    tn = _tile(d, 256, 128)
    row = lambda i, j: (i, 0)
    assert out_dtype == F32, "the resident output block holds the f32 pre-norm result"
    return pl.pallas_call(
        _ffn_down_kernel,
        grid=(m // tm, d // tn),
        in_specs=[pl.BlockSpec((tm, dff), row, pipeline_mode=pl.Buffered(1)),
                  pl.BlockSpec((dff, tn), lambda i, j: (0, j)),
                  pl.BlockSpec((tm, d), row, pipeline_mode=pl.Buffered(1)),
                  _mod_spec(mod, 5, tm, rows_per_seq),
                  pl.BlockSpec((1, d), lambda i, j: (0, 0))],
        out_specs=pl.BlockSpec((tm, d), row),
        out_shape=jax.ShapeDtypeStruct((m, d), out_dtype),
        compiler_params=_cparams("parallel", "arbitrary", vmem_limit=VMEM_LIMIT_FFN_DOWN),
        name="ffn_down",
    )(hid, w_down, h, mod, norm_g)


def _run_group(x3, mod, rows_per_seq_for_mod, conv_buf, s_hg, s_gdn, lb, p, prompt):
    batch, seq, d = x3.shape
    m = batch * seq
    x = x3.reshape(m, d)
    hg_heads, gv_heads = p["hg_heads"], p["gv_heads"]
    gq_heads = p["gq_heads"]
    hg_w = hg_heads * HEAD_DIM
    gq_w = gq_heads * HEAD_DIM
    gv_w = gv_heads * HEAD_DIM
    ch = 2 * gq_w + gv_w
    main = 4 * hg_w + ch + gv_w
    conv0 = 4 * hg_w
    z0 = conv0 + ch
    rows_unit = seq if prompt else m
    tm = _tile(rows_unit, 512, SUBLANES)
    tm_proj = _tile(rows_unit, 2048, SUBLANES)
    tm_wide = _tile(rows_unit, 1024, SUBLANES)

    a1 = _prenorm(x, p["norm_g"][0:1], mod, 1, 0, rows_per_seq_for_mod, tm)
    proj = _project(a1, p["w_in"], main, tm_proj, _tile(main, 512, 128))
    tail = _project(a1, p["w_tail"], p["w_tail"].shape[-1], tm_proj, p["w_tail"].shape[-1])
    gates = _project(a1, p["w_gates"], 2 * d, tm_proj, _tile(2 * d, 512, 128))

    if prompt:
        o_hg, s_hg_new = _hgrn_prompt(proj, lb, p["hg_norm_g"], batch, seq, hg_heads, p["hg_dtype"])
        gate_arrays = _gdn_gates(tail, p["a_log"], p["dt_bias"], GDN_CHUNK)
        buf8 = jnp.zeros((batch * SUBLANES, ch), F32)
        qkv = _conv_prompt(proj, conv0, buf8, p["conv_w"], batch, seq, gq_w, ch)
        o_gdn, s_gdn_new = _gdn_prompt(qkv, proj, z0, gate_arrays, p["gdn_norm_g"], batch, seq,
                                       gq_heads, gv_heads, p["gdn_dtype"])
    else:
        o_hg, s_hg_new = _hgrn_sample(proj, lb, p["hg_norm_g"], s_hg, batch, seq, hg_heads)
        gate_arrays = _gdn_gates(tail, p["a_log"], p["dt_bias"], seq)
        buf_rows = jnp.pad(conv_buf.astype(F32), ((0, 0), (0, seq - (CONV_W - 1)), (0, 0))).reshape(m, ch)
        qkv = _conv_sample(proj, conv0, buf_rows, p["conv_w"], seq, gq_w, ch)
        o_gdn, s_gdn_new = _gdn_sample(qkv, proj, z0, gate_arrays, p["gdn_norm_g"], s_gdn, batch, seq,
                                       gq_heads, gv_heads)

    if seq % SUBLANES == 0:
        conv_new = proj.reshape(batch, seq, main)[:, seq - (CONV_W - 1):, conv0:conv0 + ch]
    else:
        conv_new = proj[:, conv0:conv0 + ch].reshape(batch, seq, ch)[:, seq - (CONV_W - 1):]

    merged = _merge(o_hg, o_gdn, p["w_out_hg"], p["w_out_gdn"], gates, tm_wide)
    h1, a2 = _outproj(merged, p["w_o"], x, mod, p["norm_g"][1:3], rows_per_seq_for_mod, tm)
    hid = _ffn_up(a2, p["w_up"], tm_proj)
    y = _ffn_down(hid, p["w_down_bf16"], h1, mod, p["norm_g"][3:4], rows_per_seq_for_mod, tm_wide, x3.dtype)
    return y.reshape(batch, seq, d), conv_new, s_hg_new, s_gdn_new


def kernel(x_prompt, x_sample, state_hgrn, state_gdn, cache_conv, c_prompt, c_sample, lb_logits, w_ada,
           b_ada, norm_g, w_in, conv_w, A_log, dt_bias, hg_norm_g, gdn_norm_g, w_out_hg, w_out_gdn, w_o,
           w_up, w_down):
    depth = w_in.shape[0]
    assert depth == 1, "single-layer trunk"
    bp, tp, d = x_prompt.shape
    bs, ts, _ = x_sample.shape
    hg_heads = state_hgrn.shape[2]
    gv_heads = state_gdn.shape[2]
    ch = cache_conv.shape[-1]
    gq_heads = (ch // HEAD_DIM - gv_heads) // 2
    assert tp >= CONV_W - 1 and ts >= CONV_W - 1
    assert tp % GDN_CHUNK == 0 and tp % HG_CHUNK == 0

    main = 4 * hg_heads * HEAD_DIM + ch + gv_heads * HEAD_DIM
    p = dict(
        hg_heads=hg_heads, gv_heads=gv_heads, gq_heads=gq_heads,
        hg_dtype=state_hgrn.dtype, gdn_dtype=state_gdn.dtype,
        norm_g=norm_g[0].astype(F32), w_in=w_in[0],
        w_tail=w_in[0, :, main:main + 2 * gv_heads],
        w_gates=_realign_cols(w_in[0], main + 2 * gv_heads, 2 * d),
        conv_w=conv_w.astype(F32), a_log=A_log.astype(F32), dt_bias=dt_bias.astype(F32),
        hg_norm_g=hg_norm_g.astype(F32), gdn_norm_g=gdn_norm_g.astype(F32),
        w_out_hg=w_out_hg, w_out_gdn=w_out_gdn, w_o=w_o, w_up=w_up, w_down_bf16=_to_bf16(w_down[0]),
    )

    lbs = jnp.cumsum(jax.nn.softmax(lb_logits.astype(F32), axis=0), axis=0)
    lb = lbs[0:1]

    n_c = bp + bs
    pad = (-n_c) % SUBLANES
    c_all = jnp.concatenate([c_prompt.astype(F32), c_sample.astype(F32), jnp.zeros((pad, d), F32)], axis=0)
    mod_all = _ada(c_all, w_ada, b_ada.astype(F32))
    mod_p = mod_all[:bp].reshape(bp * 6, 1, d)
    mod_s = jnp.repeat(mod_all[bp:bp + bs].reshape(bs, 6, d), ts, axis=0).transpose(1, 0, 2)

    y_p, conv_p, hg_p, gdn_p = _run_group(x_prompt, mod_p, tp, None, None, None, lb, p, prompt=True)
    y_s, conv_s, hg_s, gdn_s = _run_group(x_sample, mod_s, ts, cache_conv[0], state_hgrn[0], state_gdn[0],
                                          lb, p, prompt=False)
    return (y_p, y_s,
            hg_p[None], gdn_p[None], conv_p.astype(cache_conv.dtype)[None],
            hg_s[None], gdn_s[None], conv_s.astype(cache_conv.dtype)[None])
```

```python
import functools

import jax
import jax.numpy as jnp
from jax import lax
from jax.experimental import pallas as pl
from jax.experimental.pallas import tpu as pltpu

F32 = jnp.float32
BF16 = jnp.bfloat16
HIGHEST = lax.Precision.HIGHEST

HEAD_DIM = 128
HG_CHUNK = 32
GDN_CHUNK = 64
CONV_W = 4
NORM_EPS = 1e-6
SUBLANES = 8
VMEM_LIMIT = 48 * 1024 * 1024
VMEM_LIMIT_FFN_DOWN = 54 * 1024 * 1024


def _cparams(*sem, vmem_limit=VMEM_LIMIT):
    return pltpu.CompilerParams(dimension_semantics=sem, vmem_limit_bytes=vmem_limit)


def _tile(n, pref, quantum):
    if n <= pref:
        return n
    t = (pref // quantum) * quantum
    while t > quantum and n % t:
        t -= quantum
    assert n % t == 0, (n, pref, quantum)
    return t


def _bdot(a, b):
    return jnp.dot(a.astype(BF16), b.astype(BF16), preferred_element_type=F32)


def _bdot_nt(a, b):
    return lax.dot_general(a.astype(BF16), b.astype(BF16), (((1,), (1,)), ((), ())),
                           preferred_element_type=F32)


def _fdot(a, b):
    return jnp.dot(a, b, precision=HIGHEST, preferred_element_type=F32)


def _split3(x):
    hi = x.astype(BF16)
    r = x - hi.astype(F32)
    mid = r.astype(BF16)
    lo = (r - mid.astype(F32)).astype(BF16)
    return hi, mid, lo


def _mask_dot(m_bf, x):
    hi, mid, lo = _split3(x)
    return (jnp.dot(m_bf, hi, preferred_element_type=F32)
            + jnp.dot(m_bf, mid, preferred_element_type=F32)
            + jnp.dot(m_bf, lo, preferred_element_type=F32))


def _sigmoid(x):
    return 1.0 / (1.0 + jnp.exp(-x))


def _silu(x):
    return x * _sigmoid(x)


def _rms(x):
    return x * lax.rsqrt(jnp.mean(x * x, axis=-1, keepdims=True) + NORM_EPS)


def _transpose_pad(x):
    r = x.shape[0]
    if r < HEAD_DIM:
        x = jnp.concatenate([x, jnp.zeros((HEAD_DIM - r, HEAD_DIM), x.dtype)], axis=0)
    return x.T


def _tri_masks(c):
    row = lax.broadcasted_iota(jnp.int32, (c, c), 0)
    col = lax.broadcasted_iota(jnp.int32, (c, c), 1)
    return col <= row, col < row


def _ada_kernel(c_ref, w_ref, b_ref, o_ref):
    cs = _silu(c_ref[...])
    w = w_ref[...]
    c_hi = cs.astype(BF16)
    c_lo = (cs - c_hi.astype(F32)).astype(BF16)
    w_hi = w.astype(BF16)
    w_lo = (w - w_hi.astype(F32)).astype(BF16)
    acc = jnp.dot(c_hi, w_hi, preferred_element_type=F32)
    acc += jnp.dot(c_lo, w_hi, preferred_element_type=F32)
    acc += jnp.dot(c_hi, w_lo, preferred_element_type=F32)
    o_ref[...] = acc + b_ref[...]


def _ada(c_all, w_ada, b_ada):
    m, d = c_all.shape
    n = w_ada.shape[-1]
    tn = _tile(n, 512, 128)
    return pl.pallas_call(
        _ada_kernel,
        grid=(n // tn,),
        in_specs=[pl.BlockSpec((m, d), lambda j: (0, 0)),
                  pl.BlockSpec((None, d, tn), lambda j: (0, 0, j)),
                  pl.BlockSpec((1, tn), lambda j: (0, j))],
        out_specs=pl.BlockSpec((m, tn), lambda j: (0, j)),
        out_shape=jax.ShapeDtypeStruct((m, n), F32),
        compiler_params=_cparams("arbitrary"),
        name="ada_mod",
    )(c_all, w_ada, b_ada)


def _mod_spec(mod, k, tm, rows_per_seq):
    d = mod.shape[-1]
    if mod.shape[1] == 1:
        tiles_per_seq = rows_per_seq // tm
        return pl.BlockSpec((None, 1, d), lambda i, *_: ((i // tiles_per_seq) * 6 + k, 0, 0))
    return pl.BlockSpec((None, tm, d), lambda i, *_: (k, i, 0))


def _prenorm_kernel(x_ref, g_ref, sc_ref, sh_ref, o_ref):
    xn = _rms(x_ref[...].astype(F32)) * g_ref[...]
    o_ref[...] = (xn * (1.0 + sc_ref[...]) + sh_ref[...]).astype(o_ref.dtype)


def _prenorm(x, g, mod, k_scale, k_shift, rows_per_seq, tm):
    m, d = x.shape
    return pl.pallas_call(
        _prenorm_kernel,
        grid=(m // tm,),
        in_specs=[pl.BlockSpec((tm, d), lambda i: (i, 0)),
                  pl.BlockSpec((1, d), lambda i: (0, 0)),
                  _mod_spec(mod, k_scale, tm, rows_per_seq),
                  _mod_spec(mod, k_shift, tm, rows_per_seq)],
        out_specs=pl.BlockSpec((tm, d), lambda i: (i, 0)),
        out_shape=jax.ShapeDtypeStruct((m, d), BF16),
        compiler_params=_cparams("parallel"),
        name="prenorm",
    )(x, g, mod, mod)


def _mm_nt_kernel(a_ref, w_ref, o_ref):
    o_ref[...] = _bdot_nt(a_ref[...], w_ref[...])


def _project(a, w_t, n_cols, tm, tn):
    m, kdim = a.shape
    return pl.pallas_call(
        _mm_nt_kernel,
        grid=(m // tm, n_cols // tn),
        in_specs=[pl.BlockSpec((tm, kdim), lambda i, j: (i, 0)),
                  pl.BlockSpec((tn, kdim), lambda i, j: (j, 0))],
        out_specs=pl.BlockSpec((tm, tn), lambda i, j: (i, j)),
        out_shape=jax.ShapeDtypeStruct((m, n_cols), F32),
        compiler_params=_cparams("parallel", "arbitrary"),
        name="in_proj",
    )(a, w_t)


def _hgrn_elementwise(hq, hf, lb):
    q = _silu(hq)
    f = lb + (1.0 - lb) * _sigmoid(hf)
    return q, jnp.log(f), 1.0 - f


def _hgrn_chunk_heads(q, g, k, inp, states, tril, tril_bf):
    c = q.shape[0]
    hb = len(states)
    heads = [slice(h * HEAD_DIM, (h + 1) * HEAD_DIM) for h in range(hb)]
    big_g = _mask_dot(tril_bf, g)
    g_ref = big_g[c // 2:c // 2 + 1]
    g_last = big_g[c - 1:c]
    qe = q * jnp.exp(big_g - g_ref)
    ke = k * jnp.exp(g_ref - big_g)
    qs = q * jnp.exp(big_g)
    kl = k * jnp.exp(g_last - big_g)
    d_hi, d_mid, d_lo = [p.astype(F32) for p in _split3(jnp.exp(g_last))]
    extra = jnp.concatenate([d_hi, d_mid, d_lo, jnp.zeros((SUBLANES - 3, q.shape[1]), F32)], axis=0)
    a = [jnp.where(tril, _bdot_nt(qe[:, sl], ke[:, sl]), 0.0) for sl in heads]
    o = [_bdot(a[h], inp[:, sl]) + _bdot(qs[:, sl], states[h]) for h, sl in enumerate(heads)]
    t = [_transpose_pad(jnp.concatenate([kl[:, sl], extra[:, sl]], axis=0))[:, :c + SUBLANES] for sl in heads]
    piece_rows = lax.broadcasted_iota(jnp.int32, (SUBLANES, HEAD_DIM), 0) < 3
    ones_rows = jnp.where(piece_rows, 1.0, 0.0)
    zeros_c = jnp.zeros((c, HEAD_DIM), F32)
    new_states = []
    for h, sl in enumerate(heads):
        rhs = jnp.concatenate([jnp.concatenate([inp[:, sl], zeros_c], axis=1),
                               jnp.concatenate([jnp.zeros((SUBLANES, HEAD_DIM), F32), ones_rows], axis=1)], axis=0)
        ud = _bdot(t[h], rhs)
        new_states.append(ud[:, HEAD_DIM:] * states[h] + ud[:, :HEAD_DIM])
    return list(zip(o, new_states))


def _head_out(o, gate, norm_g):
    mean_sq = _lane_sum_bcast(o * o) * (1.0 / HEAD_DIM)
    return o * lax.rsqrt(mean_sq + NORM_EPS) * norm_g * _silu(gate)


def _hgrn_prompt_kernel(q_ref, f_ref, i_ref, g_ref, lb_ref, ng_ref, o_ref, st_ref, s_scr, *, hb):
    t = pl.program_id(2)
    c = HG_CHUNK

    @pl.when(t == 0)
    def _():
        s_scr[...] = jnp.zeros_like(s_scr)

    tril, _ = _tri_masks(c)
    tril_bf = jnp.where(tril, 1.0, 0.0).astype(BF16)
    lb = lb_ref[...]
    ng = ng_ref[...]

    def chunk(ci, carry):
        rows = pl.ds(pl.multiple_of(ci * c, c), c)
        q, g, k = _hgrn_elementwise(q_ref[rows, :], f_ref[rows, :], lb)
        inp = i_ref[rows, :]
        gate = g_ref[rows, :]
        res = _hgrn_chunk_heads(q, g, k, inp, [s_scr[h] for h in range(hb)], tril, tril_bf)
        for h, (o, s_new) in enumerate(res):
            sl = slice(h * HEAD_DIM, (h + 1) * HEAD_DIM)
            s_scr[h] = s_new
            o_ref[rows, sl] = _head_out(o, gate[:, sl], ng).astype(o_ref.dtype)
        return carry

    lax.fori_loop(0, q_ref.shape[0] // c, chunk, 0)

    @pl.when(t == pl.num_programs(2) - 1)
    def _():
        st_ref[...] = s_scr[...].astype(st_ref.dtype)


def _hgrn_prompt(proj, lb, norm_g, batch, seq, heads, state_dtype):
    hb = _tile(heads, 8, 1)
    tt = _tile(seq, 256, HG_CHUNK)
    w = hb * HEAD_DIM
    nhb = heads // hb
    nt = seq // tt

    def col(seg):
        return pl.BlockSpec((tt, w), lambda b, h, t: (b * nt + t, seg * nhb + h))

    return pl.pallas_call(
        functools.partial(_hgrn_prompt_kernel, hb=hb),
        grid=(batch, nhb, nt),
        in_specs=[col(0), col(1), col(2), col(3),
                  pl.BlockSpec((1, w), lambda b, h, t: (0, h)),
                  pl.BlockSpec((1, HEAD_DIM), lambda b, h, t: (0, 0))],
        out_specs=[pl.BlockSpec((tt, w), lambda b, h, t: (b * nt + t, h)),
                   pl.BlockSpec((None, hb, HEAD_DIM, HEAD_DIM), lambda b, h, t: (b, h, 0, 0))],
        out_shape=[jax.ShapeDtypeStruct((batch * seq, heads * HEAD_DIM), BF16),
                   jax.ShapeDtypeStruct((batch, heads, HEAD_DIM, HEAD_DIM), state_dtype)],
        scratch_shapes=[pltpu.VMEM((hb, HEAD_DIM, HEAD_DIM), F32)],
        compiler_params=_cparams("parallel", "parallel", "arbitrary"),
        name="hgrn_prompt",
    )(proj, proj, proj, proj, lb, norm_g)


def _hgrn_sample_kernel(q_ref, f_ref, i_ref, g_ref, lb_ref, ng_ref, s0_ref, o_ref, st_ref, *, hb, seq):
    c = SUBLANES
    per_group = c // seq
    tril, _ = _tri_masks(c)
    tril_bf = jnp.where(tril, 1.0, 0.0).astype(BF16)
    lb = lb_ref[...]
    ng = ng_ref[...]
    row = lax.broadcasted_iota(jnp.int32, (c, 1), 0)

    def group(gi, carry):
        rows = pl.ds(pl.multiple_of(gi * c, c), c)
        q, g, k = _hgrn_elementwise(q_ref[rows, :], f_ref[rows, :], lb)
        inp = i_ref[rows, :]
        gate = g_ref[rows, :]
        valid = [(row >= p * seq) & (row < (p + 1) * seq) for p in range(per_group)]
        masked = lambda x: jnp.concatenate([jnp.where(v, x, 0.0) for v in valid], axis=1)
        states = [s0_ref[gi * per_group + p, h].astype(F32) for p in range(per_group) for h in range(hb)]
        res = _hgrn_chunk_heads(masked(q), masked(g), masked(k), jnp.concatenate([inp] * per_group, axis=1),
                                states, tril, tril_bf)
        for h in range(hb):
            sl = slice(h * HEAD_DIM, (h + 1) * HEAD_DIM)
            o_all = jnp.zeros((c, HEAD_DIM), F32)
            for p in range(per_group):
                o, s_new = res[p * hb + h]
                st_ref[gi * per_group + p, h] = s_new.astype(st_ref.dtype)
                o_all = jnp.where(valid[p], o, o_all)
            o_ref[rows, sl] = _head_out(o_all, gate[:, sl], ng).astype(o_ref.dtype)
        return carry

    lax.fori_loop(0, q_ref.shape[0] // c, group, 0)


def _hgrn_sample(proj, lb, norm_g, s0, batch, seq, heads):
    assert SUBLANES % seq == 0
    hb = _tile(heads, 4, 1)
    nb = _tile(batch, 8, SUBLANES // seq)
    w = hb * HEAD_DIM
    nhb = heads // hb
    r = nb * seq

    def col(seg):
        return pl.BlockSpec((r, w), lambda b, h: (b, seg * nhb + h))

    st_spec = pl.BlockSpec((nb, hb, HEAD_DIM, HEAD_DIM), lambda b, h: (b, h, 0, 0))
    return pl.pallas_call(
        functools.partial(_hgrn_sample_kernel, hb=hb, seq=seq),
        grid=(batch // nb, nhb),
        in_specs=[col(0), col(1), col(2), col(3),
                  pl.BlockSpec((1, w), lambda b, h: (0, h)),
                  pl.BlockSpec((1, HEAD_DIM), lambda b, h: (0, 0)),
                  st_spec],
        out_specs=[pl.BlockSpec((r, w), lambda b, h: (b, h)), st_spec],
        out_shape=[jax.ShapeDtypeStruct((batch * seq, heads * HEAD_DIM), BF16),
                   jax.ShapeDtypeStruct(s0.shape, s0.dtype)],
        compiler_params=_cparams("parallel", "parallel"),
        name="hgrn_sample",
    )(proj, proj, proj, proj, lb, norm_g, s0)


def _gdn_gates_kernel(x_ref, alog_ref, dtb_ref, beta_ref, gam_ref, gtot_ref, *, nv, chunk):
    x = x_ref[...]
    r = x.shape[0]
    beta_ref[...] = _sigmoid(x[:, :nv])
    z = x[:, nv:] + dtb_ref[...]
    softplus = jnp.maximum(z, 0.0) + jnp.log(1.0 + jnp.exp(-jnp.abs(z)))
    g = -jnp.exp(alog_ref[...]) * softplus
    row = lax.broadcasted_iota(jnp.int32, (r, r), 0)
    col = lax.broadcasted_iota(jnp.int32, (r, r), 1)
    same = (row // chunk) == (col // chunk)
    gam_ref[...] = _mask_dot(jnp.where(same & (col <= row), 1.0, 0.0).astype(BF16), g)
    gtot_ref[...] = _mask_dot(jnp.where(same, 1.0, 0.0).astype(BF16), g)


def _gdn_gates(x, a_log, dt_bias, chunk):
    m, two_nv = x.shape
    nv = two_nv // 2
    tr = _tile(m, 256, max(chunk, SUBLANES))
    out = jax.ShapeDtypeStruct((m, nv), F32)
    return pl.pallas_call(
        functools.partial(_gdn_gates_kernel, nv=nv, chunk=chunk),
        grid=(m // tr,),
        in_specs=[pl.BlockSpec((tr, two_nv), lambda i: (i, 0)),
                  pl.BlockSpec((1, nv), lambda i: (0, 0)),
                  pl.BlockSpec((1, nv), lambda i: (0, 0))],
        out_specs=[pl.BlockSpec((tr, nv), lambda i: (i, 0))] * 3,
        out_shape=[out, out, out],
        compiler_params=_cparams("parallel"),
        name="gdn_gates",
    )(x, a_log, dt_bias)


def _lane_sum_bcast(x):
    hi = x.astype(BF16)
    lo = (x - hi.astype(F32)).astype(BF16)
    return jnp.dot(jnp.concatenate([hi, lo], axis=1), jnp.ones((2 * HEAD_DIM, HEAD_DIM), BF16),
                   preferred_element_type=F32)


def _l2_heads(y, mult):
    parts = []
    for h in range(y.shape[1] // HEAD_DIM):
        yh = y[:, h * HEAD_DIM:(h + 1) * HEAD_DIM]
        parts.append(yh * (lax.rsqrt(_lane_sum_bcast(yh * yh) + NORM_EPS) * mult))
    return jnp.concatenate(parts, axis=1)


def _conv_finish(y, kind_q, kind_k, o_ref, scale):
    y = _silu(y)

    @pl.when(kind_q)
    def _():
        o_ref[...] = _l2_heads(y, scale)

    @pl.when(kind_k)
    def _():
        o_ref[...] = _l2_heads(y, 1.0)

    @pl.when(jnp.logical_not(kind_q | kind_k))
    def _():
        o_ref[...] = y


CONV_ROWS = 32


def _conv_prompt_kernel(cur_ref, prev_ref, buf_ref, cw_ref, o_ref, x_scr, *, tiles_per_seq, nq_tiles, scale):
    i = pl.program_id(0)
    j = pl.program_id(1)
    first = (i % tiles_per_seq) == 0
    tt = cur_ref.shape[0]
    x_scr[0:SUBLANES, :] = jnp.where(first, buf_ref[...], prev_ref[...])
    x_scr[SUBLANES:SUBLANES + tt, :] = cur_ref[...]
    cw = cw_ref[...]

    def run(finish):
        for r0 in range(0, tt, CONV_ROWS):
            y = None
            for s in range(CONV_W):
                lo = SUBLANES + r0 - s
                term = cw[CONV_W - 1 - s:CONV_W - s] * x_scr[lo:lo + CONV_ROWS, :]
                y = term if y is None else y + term
            o_ref[r0:r0 + CONV_ROWS, :] = finish(_silu(y))

    kind_q = j < nq_tiles
    kind_k = (j >= nq_tiles) & (j < 2 * nq_tiles)
    pl.when(kind_q)(lambda: run(lambda y: _l2_heads(y, scale)))
    pl.when(kind_k)(lambda: run(lambda y: _l2_heads(y, 1.0)))
    pl.when(jnp.logical_not(kind_q | kind_k))(lambda: run(lambda y: y))


def _conv_prompt(proj, col0, conv_buf8, conv_w, batch, seq, qk_w, ch):
    wc = _tile(qk_w, 512, HEAD_DIM)
    tt = _tile(seq, 256, SUBLANES)
    nt = seq // tt
    c0 = col0 // wc
    assert col0 % wc == 0
    return pl.pallas_call(
        functools.partial(_conv_prompt_kernel, tiles_per_seq=nt, nq_tiles=qk_w // wc,
                          scale=float(HEAD_DIM) ** -0.5),
        grid=(batch * nt, ch // wc),
        in_specs=[pl.BlockSpec((tt, wc), lambda i, j: (i, c0 + j)),
                  pl.BlockSpec((SUBLANES, wc),
                               lambda i, j: (jnp.maximum(i * (tt // SUBLANES) - 1, 0), c0 + j)),
                  pl.BlockSpec((SUBLANES, wc), lambda i, j: (i // nt, j)),
                  pl.BlockSpec((None, CONV_W, wc), lambda i, j: (0, 0, j))],
        out_specs=pl.BlockSpec((tt, wc), lambda i, j: (i, j)),
        out_shape=jax.ShapeDtypeStruct((batch * seq, ch), F32),
        scratch_shapes=[pltpu.VMEM((tt + SUBLANES, wc), F32)],
        compiler_params=_cparams("parallel", "parallel"),
        name="gdn_conv_prompt",
    )(proj, proj, conv_buf8, conv_w)


def _conv_sample_kernel(u_ref, buf_ref, cw_ref, o_ref, *, seq, nq_tiles, scale):
    j = pl.program_id(1)
    u = u_ref[...]
    buf = buf_ref[...]
    r = u.shape[0]
    tok = lax.broadcasted_iota(jnp.int32, (r, 1), 0) % seq
    cw = cw_ref[...]
    y = cw[CONV_W - 1:CONV_W] * u
    for s in range(1, CONV_W):
        from_u = pltpu.roll(u, s, axis=0)
        up = CONV_W - 1 - s
        from_buf = pltpu.roll(buf, r - up, axis=0) if up else buf
        y = y + cw[CONV_W - 1 - s:CONV_W - s] * jnp.where(tok >= s, from_u, from_buf)
    _conv_finish(y, j < nq_tiles, (j >= nq_tiles) & (j < 2 * nq_tiles), o_ref, scale)


def _conv_sample(proj, col0, buf_rows, conv_w, seq, qk_w, ch):
    m = proj.shape[0]
    wc = _tile(qk_w, 512, HEAD_DIM)
    tr = _tile(m, 256, SUBLANES)
    c0 = col0 // wc
    assert col0 % wc == 0 and tr % seq == 0 and seq >= CONV_W - 1
    return pl.pallas_call(
        functools.partial(_conv_sample_kernel, seq=seq, nq_tiles=qk_w // wc,
                          scale=float(HEAD_DIM) ** -0.5),
        grid=(m // tr, ch // wc),
        in_specs=[pl.BlockSpec((tr, wc), lambda i, j: (i, c0 + j)),
                  pl.BlockSpec((tr, wc), lambda i, j: (i, j)),
                  pl.BlockSpec((None, CONV_W, wc), lambda i, j: (0, 0, j))],
        out_specs=pl.BlockSpec((tr, wc), lambda i, j: (i, j)),
        out_shape=jax.ShapeDtypeStruct((m, ch), F32),
        compiler_params=_cparams("parallel", "parallel"),
        name="gdn_conv_sample",
    )(proj, buf_rows, conv_w)


GDN_TILE = 64
HALF_LANES = 64


def _mm_tile(a, b):
    return _bdot(a[:, :GDN_TILE], b)


def _lane_bcast_all(x, sel):
    return _mask_dot_right(x, sel)


def _mask_dot_right(x, m_bf):
    hi, mid, lo = _split3(x)
    return (jnp.dot(hi, m_bf, preferred_element_type=F32)
            + jnp.dot(mid, m_bf, preferred_element_type=F32)
            + jnp.dot(lo, m_bf, preferred_element_type=F32))


def _head_select(hb):
    head = lax.broadcasted_iota(jnp.int32, (hb, hb * HEAD_DIM), 0)
    lane_head = lax.broadcasted_iota(jnp.int32, (hb, hb * HEAD_DIM), 1) // HEAD_DIM
    return jnp.where(head == lane_head, 1.0, 0.0).astype(BF16)


def _unit_lower_inverses(a_list, rows, cols, block):
    eye = jnp.where(rows == cols, 1.0, 0.0)
    pairs = (rows >> 1) == (cols >> 1)
    d = [eye - jnp.where(pairs, a, 0.0) for a in a_list]
    b = 2
    while b < block:
        sh = b.bit_length()
        coupling = ((rows >> sh) == (cols >> sh)) & ((rows >> (sh - 1)) != (cols >> (sh - 1)))
        m1 = [_mm_tile(jnp.where(coupling, a, 0.0), di) for a, di in zip(a_list, d)]
        m2 = [_mm_tile(di, mi) for di, mi in zip(d, m1)]
        d = [di - mi for di, mi in zip(d, m2)]
        b *= 2
    return d


def _gdn_pair_products(q, k):
    k_pad = jnp.concatenate([k, jnp.zeros_like(k)], axis=0)
    kq = _bdot_nt(jnp.concatenate([k, q], axis=0), k_pad)
    return kq[:GDN_TILE], kq[GDN_TILE:]


def _gdn_wy(tiles, rep, rows, cols, incl, strict, block):
    heads = [slice(h * HEAD_DIM, (h + 1) * HEAD_DIM) for h in range(len(tiles[0][2]))]
    prods = [[_gdn_pair_products(q, k) for q, k in zip(qs, ks)] for qs, ks, *_ in tiles]
    decays, a_list = [], []
    for (qs, ks, vs, bc, gc, gr), prod in zip(tiles, prods):
        for h, sl in enumerate(heads):
            decay = jnp.where(incl, jnp.exp(gc[:, sl] - gr[h:h + 1, :]), 0.0)
            decays.append(decay)
            a_list.append(jnp.where(strict, bc[:, sl] * prod[h // rep][0] * decay, 0.0))
    t_inv = _unit_lower_inverses(a_list, rows, cols, block)
    xs = []
    for t, (qs, ks, vs, bc, gc, gr) in enumerate(tiles):
        for h, sl in enumerate(heads):
            beta = bc[:, sl]
            rhs = jnp.concatenate([vs[h] * beta, ks[h // rep] * (beta * jnp.exp(gc[:, sl]))], axis=1)
            xs.append(_mm_tile(t_inv[t * len(heads) + h], rhs))
    out = []
    for t in range(len(tiles)):
        out.append([(xs[t * len(heads) + h][:, :HEAD_DIM], xs[t * len(heads) + h][:, HEAD_DIM:],
                     prods[t][h // rep][1] * decays[t * len(heads) + h]) for h in range(len(heads))])
    return out


def _tile_iotas():
    shape = (GDN_TILE, HEAD_DIM)
    return lax.broadcasted_iota(jnp.int32, shape, 0), lax.broadcasted_iota(jnp.int32, shape, 1)


def _causal_conv(x_scr, cw, r0, n, cols):
    y = None
    for s in range(CONV_W):
        lo = SUBLANES + r0 - s
        term = cw[CONV_W - 1 - s:CONV_W - s, cols] * x_scr[lo:lo + n, cols]
        y = term if y is None else y + term
    return y


def _gdn_prompt_kernel(uq_ref, uk_ref, uv_ref, pq_ref, pk_ref, pv_ref, cwq_ref, cwk_ref, cwv_ref,
                       z_ref, bc_ref, gc_ref, gt_ref, gr_ref, ng_ref,
                       o_ref, st_ref, s_scr, u_scr, w_scr, a_scr, xq_scr, xk_scr, xv_scr, q_scr, k_scr,
                       *, hb, rep, scale):
    t = pl.program_id(2)
    c = GDN_CHUNK
    tt = uq_ref.shape[0]
    n_chunks = tt // c

    @pl.when(t == 0)
    def _():
        s_scr[...] = jnp.zeros_like(s_scr)

    keep = jnp.where(t == 0, 0.0, 1.0)
    for x_scr, prev_ref, cur_ref in ((xq_scr, pq_ref, uq_ref), (xk_scr, pk_ref, uk_ref), (xv_scr, pv_ref, uv_ref)):
        x_scr[0:SUBLANES, :] = prev_ref[...] * keep
        x_scr[SUBLANES:SUBLANES + tt, :] = cur_ref[...]

    rows_i, cols_i = _tile_iotas()
    incl = cols_i <= rows_i
    strict = cols_i < rows_i
    ng = ng_ref[...]
    sel = _head_select(hb)
    cwq, cwk, cwv = cwq_ref[...], cwk_ref[...], cwv_ref[...]
    heads = [slice(h * HEAD_DIM, (h + 1) * HEAD_DIM) for h in range(hb)]

    def chunk_operands(ci):
        rows = pl.ds(ci * c, c)
        qs = [_l2_heads(_silu(_causal_conv(xq_scr, cwq, ci * c, c, heads[i])), scale) for i in range(hb // rep)]
        ks = [_l2_heads(_silu(_causal_conv(xk_scr, cwk, ci * c, c, heads[i])), 1.0) for i in range(hb // rep)]
        vs = [_silu(_causal_conv(xv_scr, cwv, ci * c, c, heads[h])) for h in range(hb)]
        for i in range(hb // rep):
            q_scr[rows, heads[i]] = qs[i]
            k_scr[rows, heads[i]] = ks[i]
        return (qs, ks, vs, _lane_bcast_all(bc_ref[rows, :], sel),
                _lane_bcast_all(gc_ref[rows, :], sel), gr_ref[ci])

    wy = _gdn_wy([chunk_operands(ci) for ci in range(n_chunks)], rep, rows_i, cols_i, incl, strict, c)
    for ci in range(n_chunks):
        rows = pl.ds(ci * c, c)
        for h, (u, w, attn) in enumerate(wy[ci]):
            sl = slice(h * HEAD_DIM, (h + 1) * HEAD_DIM)
            u_scr[rows, sl] = u
            w_scr[rows, sl] = w.astype(w_scr.dtype)
            a_scr[rows, sl] = attn.astype(a_scr.dtype)

    def recur(ci, carry):
        rows = pl.ds(pl.multiple_of(ci * c, c), c)
        gc = _lane_bcast_all(gc_ref[rows, :], sel)
        gt = _lane_bcast_all(gt_ref[rows, :], sel)
        heads = [slice(h * HEAD_DIM, (h + 1) * HEAD_DIM) for h in range(hb)]
        pairs = range(hb // rep)
        s_pairs = [s_scr[pr] for pr in pairs]
        qs = [_bdot(q_scr[rows, heads[pr]], s_pairs[pr]) for pr in pairs]
        v_new = [u_scr[rows, sl] - _bdot(w_scr[rows, sl], s_pairs[h // rep][:, heads[h % rep]])
                 for h, sl in enumerate(heads)]
        outs = [qs[h // rep][:, heads[h % rep]] * jnp.exp(gc[:, sl]) + _bdot(a_scr[rows, sl][:, :c], v_new[h])
                for h, sl in enumerate(heads)]
        for pr in pairs:
            k_t = _transpose_pad(k_scr[rows, heads[pr]])[:, :c]
            mine = range(pr * rep, (pr + 1) * rep)
            v_scaled = jnp.concatenate([v_new[h] * jnp.exp(gt[:, heads[h]] - gc[:, heads[h]]) for h in mine], axis=1)
            decays = jnp.concatenate([jnp.exp(gt[0:1, heads[h]]) for h in mine], axis=1)
            s_scr[pr] = decays * s_pairs[pr] + _bdot(k_t, v_scaled)
        for h, sl in enumerate(heads):
            o_ref[rows, sl] = _head_out(outs[h], z_ref[rows, sl], ng).astype(o_ref.dtype)
        return carry

    lax.fori_loop(0, n_chunks, recur, 0)

    @pl.when(t == pl.num_programs(2) - 1)
    def _():
        for h in range(hb):
            st_ref[h] = s_scr[h // rep][:, (h % rep) * HEAD_DIM:(h % rep + 1) * HEAD_DIM].astype(st_ref.dtype)


def _gdn_layouts(beta, gam, gtot, nv, hb):
    m = beta.shape[0]
    nhb = nv // hb
    as_cols = lambda a: a.reshape(m, nhb, hb).transpose(1, 0, 2)
    g_rows = gam.reshape(m // GDN_TILE, GDN_TILE, nhb, hb).transpose(2, 0, 3, 1)
    g_rows = jnp.pad(g_rows, ((0, 0), (0, 0), (0, 0), (0, HEAD_DIM - GDN_TILE)))
    return as_cols(beta), as_cols(gam), as_cols(gtot), g_rows


def _gdn_prompt(proj, conv_col0, z_col0, conv_w, gate_arrays, norm_g, batch, seq, qk_heads, v_heads,
                state_dtype):
    rep = v_heads // qk_heads
    hb = _tile(v_heads, 8, rep)
    tt = _tile(seq, 256, GDN_CHUNK)
    nt = seq // tt
    nhb = v_heads // hb
    wq = (hb // rep) * HEAD_DIM
    wv = hb * HEAD_DIM
    qk_w = qk_heads * HEAD_DIM
    bc, gc, gt, gr = _gdn_layouts(*gate_arrays, v_heads, hb)
    assert conv_col0 % wv == 0 and z_col0 % wv == 0 and GDN_CHUNK == GDN_TILE and tt % SUBLANES == 0
    q0 = conv_col0 // wq
    k0 = (conv_col0 + qk_w) // wq
    v0 = (conv_col0 + 2 * qk_w) // wv
    z0 = z_col0 // wv
    kc0 = qk_w // wq
    vc0 = 2 * qk_w // wv
    col_spec = pl.BlockSpec((None, tt, hb), lambda b, h, t: (h, b * nt + t, 0))
    sub = tt // SUBLANES

    def cur(width, c0):
        return pl.BlockSpec((tt, width), lambda b, h, t: (b * nt + t, c0 + h))

    def prev(width, c0):
        return pl.BlockSpec((SUBLANES, width), lambda b, h, t: (jnp.maximum((b * nt + t) * sub - 1, 0), c0 + h))

    def taps(width, c0):
        return pl.BlockSpec((None, CONV_W, width), lambda b, h, t: (0, 0, c0 + h))

    return pl.pallas_call(
        functools.partial(_gdn_prompt_kernel, hb=hb, rep=rep, scale=float(HEAD_DIM) ** -0.5),
        grid=(batch, nhb, nt),
        in_specs=[cur(wq, q0), cur(wq, k0), cur(wv, v0),
                  prev(wq, q0), prev(wq, k0), prev(wv, v0),
                  taps(wq, 0), taps(wq, kc0), taps(wv, vc0),
                  cur(wv, z0),
                  col_spec, col_spec, col_spec,
                  pl.BlockSpec((None, tt // GDN_TILE, hb, HEAD_DIM), lambda b, h, t: (h, b * nt + t, 0, 0)),
                  pl.BlockSpec((1, HEAD_DIM), lambda b, h, t: (0, 0))],
        out_specs=[pl.BlockSpec((tt, wv), lambda b, h, t: (b * nt + t, h)),
                   pl.BlockSpec((None, hb, HEAD_DIM, HEAD_DIM), lambda b, h, t: (b, h, 0, 0))],
        out_shape=[jax.ShapeDtypeStruct((batch * seq, v_heads * HEAD_DIM), BF16),
                   jax.ShapeDtypeStruct((batch, v_heads, HEAD_DIM, HEAD_DIM), state_dtype)],
        scratch_shapes=[pltpu.VMEM((hb // rep, HEAD_DIM, rep * HEAD_DIM), F32),
                        pltpu.VMEM((tt, wv), F32),
                        pltpu.VMEM((tt, wv), BF16),
                        pltpu.VMEM((tt, wv), BF16),
                        pltpu.VMEM((tt + SUBLANES, wq), F32),
                        pltpu.VMEM((tt + SUBLANES, wq), F32),
                        pltpu.VMEM((tt + SUBLANES, wv), F32),
                        pltpu.VMEM((tt, wq), F32),
                        pltpu.VMEM((tt, wq), F32)],
        compiler_params=_cparams("parallel", "parallel", "arbitrary"),
        name="gdn_prompt",
    )(proj, proj, proj, proj, proj, proj, conv_w, conv_w, conv_w, proj, bc, gc, gt, gr, norm_g)


def _gdn_sample_kernel(q_ref, k_ref, v_ref, z_ref, bc_ref, gc_ref, gt_ref, gr_ref, ng_ref, s0_ref,
                       o_ref, st_ref, *, hb, rep, seq):
    c = GDN_TILE
    nseq = c // seq
    per_group = SUBLANES // seq
    lsh = seq.bit_length() - 1
    rows_i, cols_i = _tile_iotas()
    same = (rows_i >> lsh) == (cols_i >> lsh)
    incl = (cols_i <= rows_i) & same
    strict = (cols_i < rows_i) & same
    row8 = lax.broadcasted_iota(jnp.int32, (SUBLANES, 1), 0)
    lane_seq = lax.broadcasted_iota(jnp.int32, (1, HEAD_DIM), 1) >> lsh
    ng = ng_ref[...]
    sel = _head_select(hb)
    bc = _lane_bcast_all(bc_ref[...], sel)
    gc = _lane_bcast_all(gc_ref[...], sel)
    gt = _lane_bcast_all(gt_ref[...], sel)
    gr = gr_ref[0]
    head = lambda ref, i: ref[:, i * HEAD_DIM:(i + 1) * HEAD_DIM]
    qs = [head(q_ref, i) for i in range(hb // rep)]
    ks = [head(k_ref, i) for i in range(hb // rep)]
    wy = _gdn_wy([(qs, ks, [head(v_ref, h) for h in range(hb)], bc, gc, gr)], rep,
                 rows_i, cols_i, incl, strict, seq)[0]
    k_t_seq = []
    for k in ks:
        k_t = _transpose_pad(k)
        k_t_seq.append(jnp.concatenate([jnp.where(lane_seq == b, k_t, 0.0)[:, :c] for b in range(nseq)],
                                       axis=0).astype(BF16))
    qe = [qs[h // rep] * jnp.exp(head(gc, h)) for h in range(hb)]
    ws_parts = [[] for _ in range(hb)]
    qs_parts = [[] for _ in range(hb)]
    for g in range(c // SUBLANES):
        gsl = slice(g * SUBLANES, (g + 1) * SUBLANES)
        for h in range(hb):
            acc = jnp.zeros((2 * SUBLANES, HEAD_DIM), F32)
            for p in range(per_group):
                valid = (row8 >= p * seq) & (row8 < (p + 1) * seq)
                lhs = jnp.concatenate([jnp.where(valid, wy[h][1][gsl], 0.0),
                                       jnp.where(valid, qe[h][gsl], 0.0)], axis=0)
                acc = acc + _bdot(lhs, s0_ref[g * per_group + p, h])
            ws_parts[h].append(acc[:SUBLANES])
            qs_parts[h].append(acc[SUBLANES:])
    v_new = [wy[h][0] - jnp.concatenate(ws_parts[h], axis=0) for h in range(hb)]
    outs = [jnp.concatenate(qs_parts[h], axis=0) + _bdot(wy[h][2][:, :c], v_new[h]) for h in range(hb)]
    upds = [jnp.dot(k_t_seq[h // rep], (v_new[h] * jnp.exp(head(gt, h) - head(gc, h))).astype(BF16),
                    preferred_element_type=F32) for h in range(hb)]
    for h in range(hb):
        o_ref[:, h * HEAD_DIM:(h + 1) * HEAD_DIM] = _head_out(outs[h], head(z_ref, h), ng).astype(o_ref.dtype)
        decay = jnp.exp(head(gt, h))
        for b in range(nseq):
            st_ref[b, h] = (decay[b * seq:b * seq + 1] * s0_ref[b, h].astype(F32)
                            + upds[h][b * HEAD_DIM:(b + 1) * HEAD_DIM]).astype(st_ref.dtype)


def _gdn_sample(qkv, proj, z_col0, gate_arrays, norm_g, s0, batch, seq, qk_heads, v_heads):
    rep = v_heads // qk_heads
    hb = _tile(v_heads, 4, rep)
    assert GDN_TILE % seq == 0 and SUBLANES % seq == 0 and seq & (seq - 1) == 0
    nb = GDN_TILE // seq
    assert batch % nb == 0
    r = GDN_TILE
    nhb = v_heads // hb
    wq = (hb // rep) * HEAD_DIM
    wv = hb * HEAD_DIM
    qk_w = qk_heads * HEAD_DIM
    bc, gc, gt, gr = _gdn_layouts(*gate_arrays, v_heads, hb)
    k0 = qk_w // wq
    v0 = 2 * qk_w // wv
    z0 = z_col0 // wv
    assert z_col0 % wv == 0
    st_spec = pl.BlockSpec((nb, hb, HEAD_DIM, HEAD_DIM), lambda b, h: (b, h, 0, 0))
    col_spec = pl.BlockSpec((None, r, hb), lambda b, h: (h, b, 0))
    return pl.pallas_call(
        functools.partial(_gdn_sample_kernel, hb=hb, rep=rep, seq=seq),
        grid=(batch // nb, nhb),
        in_specs=[pl.BlockSpec((r, wq), lambda b, h: (b, h)),
                  pl.BlockSpec((r, wq), lambda b, h: (b, k0 + h)),
                  pl.BlockSpec((r, wv), lambda b, h: (b, v0 + h)),
                  pl.BlockSpec((r, wv), lambda b, h: (b, z0 + h)),
                  col_spec, col_spec, col_spec,
                  pl.BlockSpec((None, 1, hb, HEAD_DIM), lambda b, h: (h, b, 0, 0)),
                  pl.BlockSpec((1, HEAD_DIM), lambda b, h: (0, 0)),
                  st_spec],
        out_specs=[pl.BlockSpec((r, wv), lambda b, h: (b, h)), st_spec],
        out_shape=[jax.ShapeDtypeStruct((batch * seq, v_heads * HEAD_DIM), BF16),
                   jax.ShapeDtypeStruct(s0.shape, s0.dtype)],
        compiler_params=_cparams("parallel", "parallel"),
        name="gdn_sample",
    )(qkv, qkv, qkv, proj, bc, gc, gt, gr, norm_g, s0)


def _merge_kernel(oh_ref, og_ref, wa_ref, wb_ref, ga_ref, gb_ref, o_ref):
    ya = _bdot(oh_ref[...], wa_ref[...])
    yb = _bdot(og_ref[...], wb_ref[...])
    o_ref[...] = (_sigmoid(ga_ref[...]) * ya + _sigmoid(gb_ref[...]) * yb).astype(o_ref.dtype)


def _merge(o_hg, o_gdn, w_a, w_b, gates, tm):
    m, ka = o_hg.shape
    kb = o_gdn.shape[1]
    d = w_a.shape[-1]
    tn = _tile(d, 256, 128)
    nj = d // tn
    return pl.pallas_call(
        _merge_kernel,
        grid=(m // tm, nj),
        in_specs=[pl.BlockSpec((tm, ka), lambda i, j: (i, 0), pipeline_mode=pl.Buffered(1)),
                  pl.BlockSpec((tm, kb), lambda i, j: (i, 0), pipeline_mode=pl.Buffered(1)),
                  pl.BlockSpec((None, ka, tn), lambda i, j: (0, 0, j)),
                  pl.BlockSpec((None, kb, tn), lambda i, j: (0, 0, j)),
                  pl.BlockSpec((tm, tn), lambda i, j: (i, j)),
                  pl.BlockSpec((tm, tn), lambda i, j: (i, nj + j))],
        out_specs=pl.BlockSpec((tm, tn), lambda i, j: (i, j)),
        out_shape=jax.ShapeDtypeStruct((m, d), BF16),
        compiler_params=_cparams("parallel", "arbitrary"),
        name="merge",
    )(o_hg, o_gdn, w_a, w_b, gates, gates)


def _outproj_kernel(m_ref, w_ref, x_ref, g1_ref, sc_ref, sh_ref, ng_ref, h_ref, a_ref, acc_ref):
    kk = pl.program_id(1)

    @pl.when(kk == 0)
    def _():
        acc_ref[...] = jnp.zeros_like(acc_ref)

    acc_ref[...] += _bdot(m_ref[...], w_ref[...])

    @pl.when(kk == pl.num_programs(1) - 1)
    def _():
        ng = ng_ref[...]
        h = x_ref[...].astype(F32) + g1_ref[...] * (_rms(acc_ref[...]) * ng[0:1])
        h_ref[...] = h
        a_ref[...] = ((_rms(h) * ng[1:2]) * (1.0 + sc_ref[...]) + sh_ref[...]).astype(a_ref.dtype)


def _outproj(merged, w_o, x, mod, norm_g2, rows_per_seq, tm):
    m, d = x.shape
    kdim = merged.shape[1]
    tk = _tile(kdim, 512, 128)
    row = lambda i, k: (i, 0)
    return pl.pallas_call(
        _outproj_kernel,
        grid=(m // tm, kdim // tk),
        in_specs=[pl.BlockSpec((tm, tk), lambda i, k: (i, k)),
                  pl.BlockSpec((None, tk, d), lambda i, k: (0, k, 0)),
                  pl.BlockSpec((tm, d), row),
                  _mod_spec(mod, 2, tm, rows_per_seq),
                  _mod_spec(mod, 4, tm, rows_per_seq),
                  _mod_spec(mod, 3, tm, rows_per_seq),
                  pl.BlockSpec((2, d), lambda i, k: (0, 0))],
        out_specs=[pl.BlockSpec((tm, d), row), pl.BlockSpec((tm, d), row)],
        out_shape=[jax.ShapeDtypeStruct((m, d), F32), jax.ShapeDtypeStruct((m, d), BF16)],
        scratch_shapes=[pltpu.VMEM((tm, d), F32)],
        compiler_params=_cparams("parallel", "arbitrary"),
        name="out_proj",
    )(merged, w_o, x, mod, mod, mod, norm_g2)


def _cast_kernel(x_ref, o_ref):
    o_ref[...] = x_ref[...].astype(o_ref.dtype)


def _to_bf16(w):
    kdim, n = w.shape
    tk = _tile(kdim, 512, 16)
    return pl.pallas_call(
        _cast_kernel,
        grid=(kdim // tk,),
        in_specs=[pl.BlockSpec((tk, n), lambda i: (i, 0))],
        out_specs=pl.BlockSpec((tk, n), lambda i: (i, 0)),
        out_shape=jax.ShapeDtypeStruct((kdim, n), BF16),
        compiler_params=_cparams("parallel"),
        name="weight_to_bf16",
    )(w)


def _ffn_up_kernel(a_ref, w_ref, o_ref):
    o_ref[...] = jnp.square(jnp.maximum(_bdot(a_ref[...], w_ref[...]), 0.0)).astype(o_ref.dtype)


def _ffn_up(a, w_up, tm):
    m, d = a.shape
    dff = w_up.shape[-1]
    tf = _tile(dff, 512, 128)
    return pl.pallas_call(
        _ffn_up_kernel,
        grid=(m // tm, dff // tf),
        in_specs=[pl.BlockSpec((tm, d), lambda i, f: (i, 0)),
                  pl.BlockSpec((None, d, tf), lambda i, f: (0, 0, f))],
        out_specs=pl.BlockSpec((tm, tf), lambda i, f: (i, f)),
        out_shape=jax.ShapeDtypeStruct((m, dff), BF16),
        compiler_params=_cparams("parallel", "arbitrary"),
        name="ffn_up",
    )(a, w_up)


EPILOGUE_ROWS = 128


def _ffn_down_kernel(hid_ref, w_ref, h_ref, g2_ref, ng_ref, y_ref):
    j = pl.program_id(1)
    tn = w_ref.shape[1]
    y_ref[:, pl.ds(pl.multiple_of(j * tn, tn), tn)] = _bdot(hid_ref[...], w_ref[...])

    @pl.when(j == pl.num_programs(1) - 1)
    def _():
        step = min(EPILOGUE_ROWS, y_ref.shape[0])

        def body(r, carry):
            rows = pl.ds(pl.multiple_of(r * step, step), step)
            g2 = g2_ref[rows, :] if g2_ref.shape[0] > 1 else g2_ref[...]
            y_ref[rows, :] = h_ref[rows, :] + g2 * (_rms(y_ref[rows, :]) * ng_ref[...])
            return carry

        lax.fori_loop(0, y_ref.shape[0] // step, body, 0)


def _ffn_down(hid, w_down, h, mod, norm_g, rows_per_seq, tm, out_dtype):
    m, d = h.shape
    dff = hid.shape[1]
    tn = _tile(d, 256, 128)
    row = lambda i, j: (i, 0)
    assert out_dtype == F32, "the resident output block holds the f32 pre-norm result"
    return pl.pallas_call(
        _ffn_down_kernel,
        grid=(m // tm, d // tn),
        in_specs=[pl.BlockSpec((tm, dff), row, pipeline_mode=pl.Buffered(1)),
                  pl.BlockSpec((dff, tn), lambda i, j: (0, j)),
                  pl.BlockSpec((tm, d), row, pipeline_mode=pl.Buffered(1)),
                  _mod_spec(mod, 5, tm, rows_per_seq),
                  pl.BlockSpec((1, d), lambda i, j: (0, 0))],
        out_specs=pl.BlockSpec((tm, d), row),
        out_shape=jax.ShapeDtypeStruct((m, d), out_dtype),
        compiler_params=_cparams("parallel", "arbitrary", vmem_limit=VMEM_LIMIT_FFN_DOWN),
        name="ffn_down",
    )(hid, w_down, h, mod, norm_g)


def _run_group(x3, mod, rows_per_seq_for_mod, conv_buf, s_hg, s_gdn, lb, p, prompt):
    batch, seq, d = x3.shape
    m = batch * seq
    x = x3.reshape(m, d)
    hg_heads, gv_heads = p["hg_heads"], p["gv_heads"]
    gq_heads = p["gq_heads"]
    hg_w = hg_heads * HEAD_DIM
    gq_w = gq_heads * HEAD_DIM
    gv_w = gv_heads * HEAD_DIM
    ch = 2 * gq_w + gv_w
    main = 4 * hg_w + ch + gv_w
    conv0 = 4 * hg_w
    z0 = conv0 + ch
    rows_unit = seq if prompt else m
    tm = _tile(rows_unit, 512, SUBLANES)
    tm_proj = _tile(rows_unit, 2048, SUBLANES)
    tm_wide = _tile(rows_unit, 1024, SUBLANES)

    a1 = _prenorm(x, p["norm_g"][0:1], mod, 1, 0, rows_per_seq_for_mod, tm)
    proj = _project(a1, p["w_in"], main, tm_proj, _tile(main, 512, 128))
    tail = _project(a1, p["w_tail"], p["w_tail"].shape[0], tm_proj, p["w_tail"].shape[0])
    gates = _project(a1, p["w_gates"], 2 * d, tm_proj, _tile(2 * d, 512, 128))

    if prompt:
        o_hg, s_hg_new = _hgrn_prompt(proj, lb, p["hg_norm_g"], batch, seq, hg_heads, p["hg_dtype"])
        gate_arrays = _gdn_gates(tail, p["a_log"], p["dt_bias"], GDN_CHUNK)
        o_gdn, s_gdn_new = _gdn_prompt(proj, conv0, z0, p["conv_w"], gate_arrays, p["gdn_norm_g"], batch, seq,
                                       gq_heads, gv_heads, p["gdn_dtype"])
    else:
        o_hg, s_hg_new = _hgrn_sample(proj, lb, p["hg_norm_g"], s_hg, batch, seq, hg_heads)
        gate_arrays = _gdn_gates(tail, p["a_log"], p["dt_bias"], seq)
        buf_rows = jnp.pad(conv_buf.astype(F32), ((0, 0), (0, seq - (CONV_W - 1)), (0, 0))).reshape(m, ch)
        qkv = _conv_sample(proj, conv0, buf_rows, p["conv_w"], seq, gq_w, ch)
        o_gdn, s_gdn_new = _gdn_sample(qkv, proj, z0, gate_arrays, p["gdn_norm_g"], s_gdn, batch, seq,
                                       gq_heads, gv_heads)

    if seq % SUBLANES == 0:
        conv_new = proj.reshape(batch, seq, main)[:, seq - (CONV_W - 1):, conv0:conv0 + ch]
    else:
        conv_new = proj[:, conv0:conv0 + ch].reshape(batch, seq, ch)[:, seq - (CONV_W - 1):]

    merged = _merge(o_hg, o_gdn, p["w_out_hg"], p["w_out_gdn"], gates, tm_wide)
    h1, a2 = _outproj(merged, p["w_o"], x, mod, p["norm_g"][1:3], rows_per_seq_for_mod, tm)
    hid = _ffn_up(a2, p["w_up"], tm_proj)
    y = _ffn_down(hid, p["w_down_bf16"], h1, mod, p["norm_g"][3:4], rows_per_seq_for_mod, tm_wide, x3.dtype)
    return y.reshape(batch, seq, d), conv_new, s_hg_new, s_gdn_new


def kernel(x_prompt, x_sample, state_hgrn, state_gdn, cache_conv, c_prompt, c_sample, lb_logits, w_ada,
           b_ada, norm_g, w_in, conv_w, A_log, dt_bias, hg_norm_g, gdn_norm_g, w_out_hg, w_out_gdn, w_o,
           w_up, w_down):
    depth = w_in.shape[0]
    assert depth == 1, "single-layer trunk"
    bp, tp, d = x_prompt.shape
    bs, ts, _ = x_sample.shape
    hg_heads = state_hgrn.shape[2]
    gv_heads = state_gdn.shape[2]
    ch = cache_conv.shape[-1]
    gq_heads = (ch // HEAD_DIM - gv_heads) // 2
    assert tp >= CONV_W - 1 and ts >= CONV_W - 1
    assert tp % GDN_CHUNK == 0 and tp % HG_CHUNK == 0

    main = 4 * hg_heads * HEAD_DIM + ch + gv_heads * HEAD_DIM
    w_in_t = jnp.swapaxes(w_in[0], 0, 1)
    p = dict(
        hg_heads=hg_heads, gv_heads=gv_heads, gq_heads=gq_heads,
        hg_dtype=state_hgrn.dtype, gdn_dtype=state_gdn.dtype,
        norm_g=norm_g[0].astype(F32), w_in=w_in_t,
        w_tail=w_in_t[main:main + 2 * gv_heads],
        w_gates=w_in_t[main + 2 * gv_heads:],
        conv_w=conv_w.astype(F32), a_log=A_log.astype(F32), dt_bias=dt_bias.astype(F32),
        hg_norm_g=hg_norm_g.astype(F32), gdn_norm_g=gdn_norm_g.astype(F32),
        w_out_hg=w_out_hg, w_out_gdn=w_out_gdn, w_o=w_o, w_up=w_up, w_down_bf16=_to_bf16(w_down[0]),
    )

    lbs = jnp.cumsum(jax.nn.softmax(lb_logits.astype(F32), axis=0), axis=0)
    lb = lbs[0:1]

    n_c = bp + bs
    pad = (-n_c) % SUBLANES
    c_all = jnp.concatenate([c_prompt.astype(F32), c_sample.astype(F32), jnp.zeros((pad, d), F32)], axis=0)
    mod_all = _ada(c_all, w_ada, b_ada.astype(F32))
    mod_p = mod_all[:bp].reshape(bp * 6, 1, d)
    mod_s = jnp.repeat(mod_all[bp:bp + bs].reshape(bs, 6, d), ts, axis=0).transpose(1, 0, 2)

    y_p, conv_p, hg_p, gdn_p = _run_group(x_prompt, mod_p, tp, None, None, None, lb, p, prompt=True)
    y_s, conv_s, hg_s, gdn_s = _run_group(x_sample, mod_s, ts, cache_conv[0], state_hgrn[0], state_gdn[0],
                                          lb, p, prompt=False)
    return (y_p, y_s,
            hg_p[None], gdn_p[None], conv_p.astype(cache_conv.dtype)[None],
            hg_s[None], gdn_s[None], conv_s.astype(cache_conv.dtype)[None])
```

```python
import functools

import jax
import jax.numpy as jnp
from jax import lax
from jax.experimental import pallas as pl
from jax.experimental.pallas import tpu as pltpu

F32 = jnp.float32
BF16 = jnp.bfloat16
HIGHEST = lax.Precision.HIGHEST

HEAD_DIM = 128
HG_CHUNK = 32
GDN_CHUNK = 64
CONV_W = 4
NORM_EPS = 1e-6
SUBLANES = 8
VMEM_LIMIT = 48 * 1024 * 1024
VMEM_LIMIT_FFN_DOWN = 54 * 1024 * 1024


def _cparams(*sem, vmem_limit=VMEM_LIMIT):
    return pltpu.CompilerParams(dimension_semantics=sem, vmem_limit_bytes=vmem_limit)


def _tile(n, pref, quantum):
    if n <= pref:
        return n
    t = (pref // quantum) * quantum
    while t > quantum and n % t:
        t -= quantum
    assert n % t == 0, (n, pref, quantum)
    return t


def _bdot(a, b):
    return jnp.dot(a.astype(BF16), b.astype(BF16), preferred_element_type=F32)


def _bdot_nt(a, b):
    return lax.dot_general(a.astype(BF16), b.astype(BF16), (((1,), (1,)), ((), ())),
                           preferred_element_type=F32)


def _fdot(a, b):
    return jnp.dot(a, b, precision=HIGHEST, preferred_element_type=F32)


def _split3(x):
    hi = x.astype(BF16)
    r = x - hi.astype(F32)
    mid = r.astype(BF16)
    lo = (r - mid.astype(F32)).astype(BF16)
    return hi, mid, lo


def _mask_dot(m_bf, x):
    hi, mid, lo = _split3(x)
    return (jnp.dot(m_bf, hi, preferred_element_type=F32)
            + jnp.dot(m_bf, mid, preferred_element_type=F32)
            + jnp.dot(m_bf, lo, preferred_element_type=F32))


def _sigmoid(x):
    return 1.0 / (1.0 + jnp.exp(-x))


def _silu(x):
    return x * _sigmoid(x)


def _rms(x):
    return x * lax.rsqrt(jnp.mean(x * x, axis=-1, keepdims=True) + NORM_EPS)


def _transpose_pad(x):
    r = x.shape[0]
    if r < HEAD_DIM:
        x = jnp.concatenate([x, jnp.zeros((HEAD_DIM - r, HEAD_DIM), x.dtype)], axis=0)
    return x.T


def _tri_masks(c):
    row = lax.broadcasted_iota(jnp.int32, (c, c), 0)
    col = lax.broadcasted_iota(jnp.int32, (c, c), 1)
    return col <= row, col < row


def _ada_kernel(c_ref, w_ref, b_ref, o_ref):
    cs = _silu(c_ref[...])
    w = w_ref[...]
    c_hi = cs.astype(BF16)
    c_lo = (cs - c_hi.astype(F32)).astype(BF16)
    w_hi = w.astype(BF16)
    w_lo = (w - w_hi.astype(F32)).astype(BF16)
    acc = jnp.dot(c_hi, w_hi, preferred_element_type=F32)
    acc += jnp.dot(c_lo, w_hi, preferred_element_type=F32)
    acc += jnp.dot(c_hi, w_lo, preferred_element_type=F32)
    o_ref[...] = acc + b_ref[...]


def _ada(c_all, w_ada, b_ada):
    m, d = c_all.shape
    n = w_ada.shape[-1]
    tn = _tile(n, 512, 128)
    return pl.pallas_call(
        _ada_kernel,
        grid=(n // tn,),
        in_specs=[pl.BlockSpec((m, d), lambda j: (0, 0)),
                  pl.BlockSpec((None, d, tn), lambda j: (0, 0, j)),
                  pl.BlockSpec((1, tn), lambda j: (0, j))],
        out_specs=pl.BlockSpec((m, tn), lambda j: (0, j)),
        out_shape=jax.ShapeDtypeStruct((m, n), F32),
        compiler_params=_cparams("arbitrary"),
        name="ada_mod",
    )(c_all, w_ada, b_ada)


def _mod_spec(mod, k, tm, rows_per_seq):
    d = mod.shape[-1]
    if mod.shape[1] == 1:
        tiles_per_seq = rows_per_seq // tm
        return pl.BlockSpec((None, 1, d), lambda i, *_: ((i // tiles_per_seq) * 6 + k, 0, 0))
    return pl.BlockSpec((None, tm, d), lambda i, *_: (k, i, 0))


def _prenorm_kernel(x_ref, g_ref, sc_ref, sh_ref, o_ref):
    xn = _rms(x_ref[...].astype(F32)) * g_ref[...]
    o_ref[...] = (xn * (1.0 + sc_ref[...]) + sh_ref[...]).astype(o_ref.dtype)


def _prenorm(x, g, mod, k_scale, k_shift, rows_per_seq, tm):
    m, d = x.shape
    return pl.pallas_call(
        _prenorm_kernel,
        grid=(m // tm,),
        in_specs=[pl.BlockSpec((tm, d), lambda i: (i, 0)),
                  pl.BlockSpec((1, d), lambda i: (0, 0)),
                  _mod_spec(mod, k_scale, tm, rows_per_seq),
                  _mod_spec(mod, k_shift, tm, rows_per_seq)],
        out_specs=pl.BlockSpec((tm, d), lambda i: (i, 0)),
        out_shape=jax.ShapeDtypeStruct((m, d), BF16),
        compiler_params=_cparams("parallel"),
        name="prenorm",
    )(x, g, mod, mod)


def _mm_nt_kernel(a_ref, w_ref, o_ref):
    o_ref[...] = _bdot_nt(a_ref[...], w_ref[...])


def _project(a, w_t, n_cols, tm, tn):
    m, kdim = a.shape
    return pl.pallas_call(
        _mm_nt_kernel,
        grid=(m // tm, n_cols // tn),
        in_specs=[pl.BlockSpec((tm, kdim), lambda i, j: (i, 0)),
                  pl.BlockSpec((tn, kdim), lambda i, j: (j, 0))],
        out_specs=pl.BlockSpec((tm, tn), lambda i, j: (i, j)),
        out_shape=jax.ShapeDtypeStruct((m, n_cols), F32),
        compiler_params=_cparams("parallel", "arbitrary"),
        name="in_proj",
    )(a, w_t)


def _hgrn_elementwise(hq, hf, lb):
    q = _silu(hq)
    f = lb + (1.0 - lb) * _sigmoid(hf)
    return q, jnp.log(f), 1.0 - f


def _hgrn_chunk_free(q, g, k, inp, n, tril, tril_bf):
    c = q.shape[0]
    heads = [slice(h * HEAD_DIM, (h + 1) * HEAD_DIM) for h in range(n)]
    big_g = _mask_dot(tril_bf, g)
    g_ref = big_g[c // 2:c // 2 + 1]
    g_last = big_g[c - 1:c]
    qe = q * jnp.exp(big_g - g_ref)
    ke = k * jnp.exp(g_ref - big_g)
    qs = q * jnp.exp(big_g)
    kl = k * jnp.exp(g_last - big_g)
    d_hi, d_mid, d_lo = [p.astype(F32) for p in _split3(jnp.exp(g_last))]
    extra = jnp.concatenate([d_hi, d_mid, d_lo, jnp.zeros((SUBLANES - 3, q.shape[1]), F32)], axis=0)
    a = [jnp.where(tril, _bdot_nt(qe[:, sl], ke[:, sl]), 0.0) for sl in heads]
    o_intra = [_bdot(a[h], inp[:, sl]) for h, sl in enumerate(heads)]
    t = [_transpose_pad(jnp.concatenate([kl[:, sl], extra[:, sl]], axis=0))[:, :c + SUBLANES] for sl in heads]
    piece_rows = lax.broadcasted_iota(jnp.int32, (SUBLANES, HEAD_DIM), 0) < 3
    ones_rows = jnp.where(piece_rows, 1.0, 0.0)
    zeros_c = jnp.zeros((c, HEAD_DIM), F32)
    uds = []
    for h, sl in enumerate(heads):
        rhs = jnp.concatenate([jnp.concatenate([inp[:, sl], zeros_c], axis=1),
                               jnp.concatenate([jnp.zeros((SUBLANES, HEAD_DIM), F32), ones_rows], axis=1)], axis=0)
        uds.append(_bdot(t[h], rhs))
    return o_intra, qs, uds


def _hgrn_chunk_heads(q, g, k, inp, states, tril, tril_bf):
    o_intra, qs, uds = _hgrn_chunk_free(q, g, k, inp, len(states), tril, tril_bf)
    out = []
    for h, s in enumerate(states):
        sl = slice(h * HEAD_DIM, (h + 1) * HEAD_DIM)
        out.append((o_intra[h] + _bdot(qs[:, sl], s), uds[h][:, HEAD_DIM:] * s + uds[h][:, :HEAD_DIM]))
    return out


def _head_out(o, gate, norm_g):
    mean_sq = _lane_sum_bcast(o * o) * (1.0 / HEAD_DIM)
    return o * lax.rsqrt(mean_sq + NORM_EPS) * norm_g * _silu(gate)


HGRN_CHUNKS_PER_PASS = 4


def _hgrn_prompt_kernel(q_ref, f_ref, i_ref, g_ref, lb_ref, ng_ref, o_ref, st_ref,
                        s_scr, oi_scr, qs_scr, ud_scr, *, hb):
    t = pl.program_id(2)
    c = HG_CHUNK
    n_chunks = q_ref.shape[0] // c
    per = HGRN_CHUNKS_PER_PASS if n_chunks % HGRN_CHUNKS_PER_PASS == 0 else 1
    heads = [slice(h * HEAD_DIM, (h + 1) * HEAD_DIM) for h in range(hb)]

    @pl.when(t == 0)
    def _():
        s_scr[...] = jnp.zeros_like(s_scr)

    tril, _ = _tri_masks(c)
    tril_bf = jnp.where(tril, 1.0, 0.0).astype(BF16)
    lb = lb_ref[...]
    ng = ng_ref[...]

    def free_part(it, carry):
        rows = [pl.ds(pl.multiple_of((it * per + j) * c, c), c) for j in range(per)]
        parts = [_hgrn_elementwise(q_ref[r, :], f_ref[r, :], lb) for r in rows]
        cat = lambda xs: jnp.concatenate(xs, axis=1)
        o_intra, qs, uds = _hgrn_chunk_free(cat([p[0] for p in parts]), cat([p[1] for p in parts]),
                                            cat([p[2] for p in parts]), cat([i_ref[r, :] for r in rows]),
                                            per * hb, tril, tril_bf)
        for j, r in enumerate(rows):
            qs_scr[r, :] = qs[:, j * hb * HEAD_DIM:(j + 1) * hb * HEAD_DIM].astype(qs_scr.dtype)
            for h, sl in enumerate(heads):
                oi_scr[r, sl] = o_intra[j * hb + h]
                ud_scr[(it * per + j) * hb + h] = uds[j * hb + h]
        return carry

    lax.fori_loop(0, n_chunks // per, free_part, 0)

    states = [s_scr[h] for h in range(hb)]
    outs = []
    for ci in range(n_chunks):
        rows = pl.ds(ci * c, c)
        outs.append([oi_scr[rows, sl] + _bdot(qs_scr[rows, sl], states[h]) for h, sl in enumerate(heads)])
        for h in range(hb):
            ud = ud_scr[ci * hb + h]
            states[h] = ud[:, HEAD_DIM:] * states[h] + ud[:, :HEAD_DIM]
    for h in range(hb):
        s_scr[h] = states[h]
    for ci in range(n_chunks):
        rows = pl.ds(ci * c, c)
        for h, sl in enumerate(heads):
            o_ref[rows, sl] = _head_out(outs[ci][h], g_ref[rows, sl], ng).astype(o_ref.dtype)

    @pl.when(t == pl.num_programs(2) - 1)
    def _():
        st_ref[...] = s_scr[...].astype(st_ref.dtype)


def _hgrn_prompt(proj, lb, norm_g, batch, seq, heads, state_dtype):
    hb = _tile(heads, 8, 1)
    tt = _tile(seq, 256, HG_CHUNK)
    w = hb * HEAD_DIM
    nhb = heads // hb
    nt = seq // tt

    def col(seg):
        return pl.BlockSpec((tt, w), lambda b, h, t: (b * nt + t, seg * nhb + h))

    return pl.pallas_call(
        functools.partial(_hgrn_prompt_kernel, hb=hb),
        grid=(batch, nhb, nt),
        in_specs=[col(0), col(1), col(2), col(3),
                  pl.BlockSpec((1, w), lambda b, h, t: (0, h)),
                  pl.BlockSpec((1, HEAD_DIM), lambda b, h, t: (0, 0))],
        out_specs=[pl.BlockSpec((tt, w), lambda b, h, t: (b * nt + t, h)),
                   pl.BlockSpec((None, hb, HEAD_DIM, HEAD_DIM), lambda b, h, t: (b, h, 0, 0))],
        out_shape=[jax.ShapeDtypeStruct((batch * seq, heads * HEAD_DIM), BF16),
                   jax.ShapeDtypeStruct((batch, heads, HEAD_DIM, HEAD_DIM), state_dtype)],
        scratch_shapes=[pltpu.VMEM((hb, HEAD_DIM, HEAD_DIM), F32),
                        pltpu.VMEM((tt, w), F32),
                        pltpu.VMEM((tt, w), BF16),
                        pltpu.VMEM(((tt // HG_CHUNK) * hb, HEAD_DIM, 2 * HEAD_DIM), F32)],
        compiler_params=_cparams("parallel", "parallel", "arbitrary"),
        name="hgrn_prompt",
    )(proj, proj, proj, proj, lb, norm_g)


def _hgrn_sample_kernel(q_ref, f_ref, i_ref, g_ref, lb_ref, ng_ref, s0_ref, o_ref, st_ref, *, hb, seq):
    c = SUBLANES
    per_group = c // seq
    tril, _ = _tri_masks(c)
    tril_bf = jnp.where(tril, 1.0, 0.0).astype(BF16)
    lb = lb_ref[...]
    ng = ng_ref[...]
    row = lax.broadcasted_iota(jnp.int32, (c, 1), 0)

    n_groups = q_ref.shape[0] // c
    valid = [(row >= p * seq) & (row < (p + 1) * seq) for p in range(per_group)]
    masked = lambda x: [jnp.where(v, x, 0.0) for v in valid]
    qs, gs, ks, inps, states = [], [], [], [], []
    for gi in range(n_groups):
        rows = pl.ds(gi * c, c)
        q, g, k = _hgrn_elementwise(q_ref[rows, :], f_ref[rows, :], lb)
        qs += masked(q)
        gs += masked(g)
        ks += masked(k)
        inps += [i_ref[rows, :]] * per_group
        states += [s0_ref[gi * per_group + p, h].astype(F32) for p in range(per_group) for h in range(hb)]
    cat = lambda xs: jnp.concatenate(xs, axis=1)
    res = _hgrn_chunk_heads(cat(qs), cat(gs), cat(ks), cat(inps), states, tril, tril_bf)
    for gi in range(n_groups):
        rows = pl.ds(gi * c, c)
        for h in range(hb):
            sl = slice(h * HEAD_DIM, (h + 1) * HEAD_DIM)
            o_all = jnp.zeros((c, HEAD_DIM), F32)
            for p in range(per_group):
                o, s_new = res[(gi * per_group + p) * hb + h]
                st_ref[gi * per_group + p, h] = s_new.astype(st_ref.dtype)
                o_all = jnp.where(valid[p], o, o_all)
            o_ref[rows, sl] = _head_out(o_all, g_ref[rows, sl], ng).astype(o_ref.dtype)


def _hgrn_sample(proj, lb, norm_g, s0, batch, seq, heads):
    assert SUBLANES % seq == 0
    hb = _tile(heads, 4, 1)
    nb = _tile(batch, 8, SUBLANES // seq)
    w = hb * HEAD_DIM
    nhb = heads // hb
    r = nb * seq

    def col(seg):
        return pl.BlockSpec((r, w), lambda b, h: (b, seg * nhb + h))

    st_spec = pl.BlockSpec((nb, hb, HEAD_DIM, HEAD_DIM), lambda b, h: (b, h, 0, 0))
    return pl.pallas_call(
        functools.partial(_hgrn_sample_kernel, hb=hb, seq=seq),
        grid=(batch // nb, nhb),
        in_specs=[col(0), col(1), col(2), col(3),
                  pl.BlockSpec((1, w), lambda b, h: (0, h)),
                  pl.BlockSpec((1, HEAD_DIM), lambda b, h: (0, 0)),
                  st_spec],
        out_specs=[pl.BlockSpec((r, w), lambda b, h: (b, h)), st_spec],
        out_shape=[jax.ShapeDtypeStruct((batch * seq, heads * HEAD_DIM), BF16),
                   jax.ShapeDtypeStruct(s0.shape, s0.dtype)],
        compiler_params=_cparams("parallel", "parallel"),
        name="hgrn_sample",
    )(proj, proj, proj, proj, lb, norm_g, s0)


def _gdn_gates_kernel(x_ref, alog_ref, dtb_ref, beta_ref, gam_ref, gtot_ref, *, nv, chunk):
    x = x_ref[...]
    r = x.shape[0]
    beta_ref[...] = _sigmoid(x[:, :nv])
    z = x[:, nv:] + dtb_ref[...]
    softplus = jnp.maximum(z, 0.0) + jnp.log(1.0 + jnp.exp(-jnp.abs(z)))
    g = -jnp.exp(alog_ref[...]) * softplus
    row = lax.broadcasted_iota(jnp.int32, (r, r), 0)
    col = lax.broadcasted_iota(jnp.int32, (r, r), 1)
    same = (row // chunk) == (col // chunk)
    gam_ref[...] = _mask_dot(jnp.where(same & (col <= row), 1.0, 0.0).astype(BF16), g)
    gtot_ref[...] = _mask_dot(jnp.where(same, 1.0, 0.0).astype(BF16), g)


def _gdn_gates(x, a_log, dt_bias, chunk):
    m, two_nv = x.shape
    nv = two_nv // 2
    tr = _tile(m, 256, max(chunk, SUBLANES))
    out = jax.ShapeDtypeStruct((m, nv), F32)
    return pl.pallas_call(
        functools.partial(_gdn_gates_kernel, nv=nv, chunk=chunk),
        grid=(m // tr,),
        in_specs=[pl.BlockSpec((tr, two_nv), lambda i: (i, 0)),
                  pl.BlockSpec((1, nv), lambda i: (0, 0)),
                  pl.BlockSpec((1, nv), lambda i: (0, 0))],
        out_specs=[pl.BlockSpec((tr, nv), lambda i: (i, 0))] * 3,
        out_shape=[out, out, out],
        compiler_params=_cparams("parallel"),
        name="gdn_gates",
    )(x, a_log, dt_bias)


def _lane_sum_bcast(x):
    hi = x.astype(BF16)
    lo = (x - hi.astype(F32)).astype(BF16)
    return jnp.dot(jnp.concatenate([hi, lo], axis=1), jnp.ones((2 * HEAD_DIM, HEAD_DIM), BF16),
                   preferred_element_type=F32)


def _l2_heads(y, mult):
    parts = []
    for h in range(y.shape[1] // HEAD_DIM):
        yh = y[:, h * HEAD_DIM:(h + 1) * HEAD_DIM]
        parts.append(yh * (lax.rsqrt(_lane_sum_bcast(yh * yh) + NORM_EPS) * mult))
    return jnp.concatenate(parts, axis=1)


def _conv_finish(y, kind_q, kind_k, o_ref, scale):
    y = _silu(y)

    @pl.when(kind_q)
    def _():
        o_ref[...] = _l2_heads(y, scale)

    @pl.when(kind_k)
    def _():
        o_ref[...] = _l2_heads(y, 1.0)

    @pl.when(jnp.logical_not(kind_q | kind_k))
    def _():
        o_ref[...] = y


CONV_ROWS = 32


def _conv_prompt_kernel(cur_ref, prev_ref, buf_ref, cw_ref, o_ref, x_scr, *, tiles_per_seq, nq_tiles, scale):
    i = pl.program_id(0)
    j = pl.program_id(1)
    first = (i % tiles_per_seq) == 0
    tt = cur_ref.shape[0]
    x_scr[0:SUBLANES, :] = jnp.where(first, buf_ref[...], prev_ref[...])
    x_scr[SUBLANES:SUBLANES + tt, :] = cur_ref[...]
    cw = cw_ref[...]

    def run(finish):
        for r0 in range(0, tt, CONV_ROWS):
            y = None
            for s in range(CONV_W):
                lo = SUBLANES + r0 - s
                term = cw[CONV_W - 1 - s:CONV_W - s] * x_scr[lo:lo + CONV_ROWS, :]
                y = term if y is None else y + term
            o_ref[r0:r0 + CONV_ROWS, :] = finish(_silu(y))

    kind_q = j < nq_tiles
    kind_k = (j >= nq_tiles) & (j < 2 * nq_tiles)
    pl.when(kind_q)(lambda: run(lambda y: _l2_heads(y, scale)))
    pl.when(kind_k)(lambda: run(lambda y: _l2_heads(y, 1.0)))
    pl.when(jnp.logical_not(kind_q | kind_k))(lambda: run(lambda y: y))


def _conv_prompt(proj, col0, conv_buf8, conv_w, batch, seq, qk_w, ch):
    wc = _tile(qk_w, 512, HEAD_DIM)
    tt = _tile(seq, 256, SUBLANES)
    nt = seq // tt
    c0 = col0 // wc
    assert col0 % wc == 0
    return pl.pallas_call(
        functools.partial(_conv_prompt_kernel, tiles_per_seq=nt, nq_tiles=qk_w // wc,
                          scale=float(HEAD_DIM) ** -0.5),
        grid=(batch * nt, ch // wc),
        in_specs=[pl.BlockSpec((tt, wc), lambda i, j: (i, c0 + j)),
                  pl.BlockSpec((SUBLANES, wc),
                               lambda i, j: (jnp.maximum(i * (tt // SUBLANES) - 1, 0), c0 + j)),
                  pl.BlockSpec((SUBLANES, wc), lambda i, j: (i // nt, j)),
                  pl.BlockSpec((None, CONV_W, wc), lambda i, j: (0, 0, j))],
        out_specs=pl.BlockSpec((tt, wc), lambda i, j: (i, j)),
        out_shape=jax.ShapeDtypeStruct((batch * seq, ch), F32),
        scratch_shapes=[pltpu.VMEM((tt + SUBLANES, wc), F32)],
        compiler_params=_cparams("parallel", "parallel"),
        name="gdn_conv_prompt",
    )(proj, proj, conv_buf8, conv_w)


def _conv_sample_kernel(u_ref, buf_ref, cw_ref, o_ref, *, seq, nq_tiles, scale):
    j = pl.program_id(1)
    u = u_ref[...]
    buf = buf_ref[...]
    r = u.shape[0]
    tok = lax.broadcasted_iota(jnp.int32, (r, 1), 0) % seq
    cw = cw_ref[...]
    y = cw[CONV_W - 1:CONV_W] * u
    for s in range(1, CONV_W):
        from_u = pltpu.roll(u, s, axis=0)
        up = CONV_W - 1 - s
        from_buf = pltpu.roll(buf, r - up, axis=0) if up else buf
        y = y + cw[CONV_W - 1 - s:CONV_W - s] * jnp.where(tok >= s, from_u, from_buf)
    _conv_finish(y, j < nq_tiles, (j >= nq_tiles) & (j < 2 * nq_tiles), o_ref, scale)


def _conv_sample(proj, col0, buf_rows, conv_w, seq, qk_w, ch):
    m = proj.shape[0]
    wc = _tile(qk_w, 512, HEAD_DIM)
    tr = _tile(m, 256, SUBLANES)
    c0 = col0 // wc
    assert col0 % wc == 0 and tr % seq == 0 and seq >= CONV_W - 1
    return pl.pallas_call(
        functools.partial(_conv_sample_kernel, seq=seq, nq_tiles=qk_w // wc,
                          scale=float(HEAD_DIM) ** -0.5),
        grid=(m // tr, ch // wc),
        in_specs=[pl.BlockSpec((tr, wc), lambda i, j: (i, c0 + j)),
                  pl.BlockSpec((tr, wc), lambda i, j: (i, j)),
                  pl.BlockSpec((None, CONV_W, wc), lambda i, j: (0, 0, j))],
        out_specs=pl.BlockSpec((tr, wc), lambda i, j: (i, j)),
        out_shape=jax.ShapeDtypeStruct((m, ch), F32),
        compiler_params=_cparams("parallel", "parallel"),
        name="gdn_conv_sample",
    )(proj, buf_rows, conv_w)


GDN_TILE = 64
HALF_LANES = 64


def _mm_tile(a, b):
    return _bdot(a[:, :GDN_TILE], b)


def _lane_bcast_all(x, sel):
    return _mask_dot_right(x, sel)


def _mask_dot_right(x, m_bf):
    hi, mid, lo = _split3(x)
    return (jnp.dot(hi, m_bf, preferred_element_type=F32)
            + jnp.dot(mid, m_bf, preferred_element_type=F32)
            + jnp.dot(lo, m_bf, preferred_element_type=F32))


def _head_select(hb):
    head = lax.broadcasted_iota(jnp.int32, (hb, hb * HEAD_DIM), 0)
    lane_head = lax.broadcasted_iota(jnp.int32, (hb, hb * HEAD_DIM), 1) // HEAD_DIM
    return jnp.where(head == lane_head, 1.0, 0.0).astype(BF16)


def _unit_lower_inverses(a_list, rows, cols, block):
    eye = jnp.where(rows == cols, 1.0, 0.0)
    pairs = (rows >> 1) == (cols >> 1)
    d = [eye - jnp.where(pairs, a, 0.0) for a in a_list]
    b = 2
    while b < block:
        sh = b.bit_length()
        coupling = ((rows >> sh) == (cols >> sh)) & ((rows >> (sh - 1)) != (cols >> (sh - 1)))
        m1 = [_mm_tile(jnp.where(coupling, a, 0.0), di) for a, di in zip(a_list, d)]
        m2 = [_mm_tile(di, mi) for di, mi in zip(d, m1)]
        d = [di - mi for di, mi in zip(d, m2)]
        b *= 2
    return d


def _gdn_pair_products(q, k):
    k_pad = jnp.concatenate([k, jnp.zeros_like(k)], axis=0)
    kq = _bdot_nt(jnp.concatenate([k, q], axis=0), k_pad)
    return kq[:GDN_TILE], kq[GDN_TILE:]


def _gdn_wy(tiles, rep, rows, cols, incl, strict, block):
    heads = [slice(h * HEAD_DIM, (h + 1) * HEAD_DIM) for h in range(len(tiles[0][2]))]
    prods = [[_gdn_pair_products(q, k) for q, k in zip(qs, ks)] for qs, ks, *_ in tiles]
    decays, a_list = [], []
    for (qs, ks, vs, bc, gc, gr), prod in zip(tiles, prods):
        for h, sl in enumerate(heads):
            decay = jnp.where(incl, jnp.exp(gc[:, sl] - gr[h:h + 1, :]), 0.0)
            decays.append(decay)
            a_list.append(jnp.where(strict, bc[:, sl] * prod[h // rep][0] * decay, 0.0))
    t_inv = _unit_lower_inverses(a_list, rows, cols, block)
    xs = []
    for t, (qs, ks, vs, bc, gc, gr) in enumerate(tiles):
        for h, sl in enumerate(heads):
            beta = bc[:, sl]
            rhs = jnp.concatenate([vs[h] * beta, ks[h // rep] * (beta * jnp.exp(gc[:, sl]))], axis=1)
            xs.append(_mm_tile(t_inv[t * len(heads) + h], rhs))
    out = []
    for t in range(len(tiles)):
        out.append([(xs[t * len(heads) + h][:, :HEAD_DIM], xs[t * len(heads) + h][:, HEAD_DIM:],
                     prods[t][h // rep][1] * decays[t * len(heads) + h]) for h in range(len(heads))])
    return out


def _tile_iotas():
    shape = (GDN_TILE, HEAD_DIM)
    return lax.broadcasted_iota(jnp.int32, shape, 0), lax.broadcasted_iota(jnp.int32, shape, 1)


def _causal_conv(x_scr, cw, r0, n, cols):
    y = None
    for s in range(CONV_W):
        lo = SUBLANES + r0 - s
        term = cw[CONV_W - 1 - s:CONV_W - s, cols] * x_scr[lo:lo + n, cols]
        y = term if y is None else y + term
    return y


def _gdn_prompt_kernel(uq_ref, uk_ref, uv_ref, pq_ref, pk_ref, pv_ref, cwq_ref, cwk_ref, cwv_ref,
                       z_ref, bc_ref, gc_ref, gt_ref, gr_ref, ng_ref,
                       o_ref, st_ref, s_scr, u_scr, w_scr, a_scr, xq_scr, xk_scr, xv_scr, q_scr, k_scr,
                       *, hb, rep, scale):
    t = pl.program_id(2)
    c = GDN_CHUNK
    tt = uq_ref.shape[0]
    n_chunks = tt // c

    @pl.when(t == 0)
    def _():
        s_scr[...] = jnp.zeros_like(s_scr)

    keep = jnp.where(t == 0, 0.0, 1.0)
    for x_scr, prev_ref, cur_ref in ((xq_scr, pq_ref, uq_ref), (xk_scr, pk_ref, uk_ref), (xv_scr, pv_ref, uv_ref)):
        x_scr[0:SUBLANES, :] = prev_ref[...] * keep
        x_scr[SUBLANES:SUBLANES + tt, :] = cur_ref[...]

    rows_i, cols_i = _tile_iotas()
    incl = cols_i <= rows_i
    strict = cols_i < rows_i
    ng = ng_ref[...]
    sel = _head_select(hb)
    cwq, cwk, cwv = cwq_ref[...], cwk_ref[...], cwv_ref[...]
    heads = [slice(h * HEAD_DIM, (h + 1) * HEAD_DIM) for h in range(hb)]

    def chunk_operands(ci):
        rows = pl.ds(ci * c, c)
        qs = [_l2_heads(_silu(_causal_conv(xq_scr, cwq, ci * c, c, heads[i])), scale) for i in range(hb // rep)]
        ks = [_l2_heads(_silu(_causal_conv(xk_scr, cwk, ci * c, c, heads[i])), 1.0) for i in range(hb // rep)]
        vs = [_silu(_causal_conv(xv_scr, cwv, ci * c, c, heads[h])) for h in range(hb)]
        for i in range(hb // rep):
            q_scr[rows, heads[i]] = qs[i]
            k_scr[rows, heads[i]] = ks[i]
        return (qs, ks, vs, _lane_bcast_all(bc_ref[rows, :], sel),
                _lane_bcast_all(gc_ref[rows, :], sel), gr_ref[ci])

    wy = _gdn_wy([chunk_operands(ci) for ci in range(n_chunks)], rep, rows_i, cols_i, incl, strict, c)
    for ci in range(n_chunks):
        rows = pl.ds(ci * c, c)
        for h, (u, w, attn) in enumerate(wy[ci]):
            sl = slice(h * HEAD_DIM, (h + 1) * HEAD_DIM)
            u_scr[rows, sl] = u
            w_scr[rows, sl] = w.astype(w_scr.dtype)
            a_scr[rows, sl] = attn.astype(a_scr.dtype)

    def recur(ci, carry):
        rows = pl.ds(pl.multiple_of(ci * c, c), c)
        gc = _lane_bcast_all(gc_ref[rows, :], sel)
        gt = _lane_bcast_all(gt_ref[rows, :], sel)
        heads = [slice(h * HEAD_DIM, (h + 1) * HEAD_DIM) for h in range(hb)]
        pairs = range(hb // rep)
        s_pairs = [s_scr[pr] for pr in pairs]
        qs = [_bdot(q_scr[rows, heads[pr]], s_pairs[pr]) for pr in pairs]
        v_new = [u_scr[rows, sl] - _bdot(w_scr[rows, sl], s_pairs[h // rep][:, heads[h % rep]])
                 for h, sl in enumerate(heads)]
        outs = [qs[h // rep][:, heads[h % rep]] * jnp.exp(gc[:, sl]) + _bdot(a_scr[rows, sl][:, :c], v_new[h])
                for h, sl in enumerate(heads)]
        for pr in pairs:
            k_t = _transpose_pad(k_scr[rows, heads[pr]])[:, :c]
            mine = range(pr * rep, (pr + 1) * rep)
            v_scaled = jnp.concatenate([v_new[h] * jnp.exp(gt[:, heads[h]] - gc[:, heads[h]]) for h in mine], axis=1)
            decays = jnp.concatenate([jnp.exp(gt[0:1, heads[h]]) for h in mine], axis=1)
            s_scr[pr] = decays * s_pairs[pr] + _bdot(k_t, v_scaled)
        for h, sl in enumerate(heads):
            o_ref[rows, sl] = _head_out(outs[h], z_ref[rows, sl], ng).astype(o_ref.dtype)
        return carry

    lax.fori_loop(0, n_chunks, recur, 0)

    @pl.when(t == pl.num_programs(2) - 1)
    def _():
        for h in range(hb):
            st_ref[h] = s_scr[h // rep][:, (h % rep) * HEAD_DIM:(h % rep + 1) * HEAD_DIM].astype(st_ref.dtype)


def _gdn_layouts(beta, gam, gtot, nv, hb):
    m = beta.shape[0]
    nhb = nv // hb
    as_cols = lambda a: a.reshape(m, nhb, hb).transpose(1, 0, 2)
    g_rows = gam.reshape(m // GDN_TILE, GDN_TILE, nhb, hb).transpose(2, 0, 3, 1)
    g_rows = jnp.pad(g_rows, ((0, 0), (0, 0), (0, 0), (0, HEAD_DIM - GDN_TILE)))
    return as_cols(beta), as_cols(gam), as_cols(gtot), g_rows


def _gdn_prompt(proj, conv_col0, z_col0, conv_w, gate_arrays, norm_g, batch, seq, qk_heads, v_heads,
                state_dtype):
    rep = v_heads // qk_heads
    hb = _tile(v_heads, 8, rep)
    tt = _tile(seq, 256, GDN_CHUNK)
    nt = seq // tt
    nhb = v_heads // hb
    wq = (hb // rep) * HEAD_DIM
    wv = hb * HEAD_DIM
    qk_w = qk_heads * HEAD_DIM
    bc, gc, gt, gr = _gdn_layouts(*gate_arrays, v_heads, hb)
    assert conv_col0 % wv == 0 and z_col0 % wv == 0 and GDN_CHUNK == GDN_TILE and tt % SUBLANES == 0
    q0 = conv_col0 // wq
    k0 = (conv_col0 + qk_w) // wq
    v0 = (conv_col0 + 2 * qk_w) // wv
    z0 = z_col0 // wv
    kc0 = qk_w // wq
    vc0 = 2 * qk_w // wv
    col_spec = pl.BlockSpec((None, tt, hb), lambda b, h, t: (h, b * nt + t, 0))
    sub = tt // SUBLANES

    def cur(width, c0):
        return pl.BlockSpec((tt, width), lambda b, h, t: (b * nt + t, c0 + h))

    def prev(width, c0):
        return pl.BlockSpec((SUBLANES, width), lambda b, h, t: (jnp.maximum((b * nt + t) * sub - 1, 0), c0 + h))

    def taps(width, c0):
        return pl.BlockSpec((None, CONV_W, width), lambda b, h, t: (0, 0, c0 + h))

    return pl.pallas_call(
        functools.partial(_gdn_prompt_kernel, hb=hb, rep=rep, scale=float(HEAD_DIM) ** -0.5),
        grid=(batch, nhb, nt),
        in_specs=[cur(wq, q0), cur(wq, k0), cur(wv, v0),
                  prev(wq, q0), prev(wq, k0), prev(wv, v0),
                  taps(wq, 0), taps(wq, kc0), taps(wv, vc0),
                  cur(wv, z0),
                  col_spec, col_spec, col_spec,
                  pl.BlockSpec((None, tt // GDN_TILE, hb, HEAD_DIM), lambda b, h, t: (h, b * nt + t, 0, 0)),
                  pl.BlockSpec((1, HEAD_DIM), lambda b, h, t: (0, 0))],
        out_specs=[pl.BlockSpec((tt, wv), lambda b, h, t: (b * nt + t, h)),
                   pl.BlockSpec((None, hb, HEAD_DIM, HEAD_DIM), lambda b, h, t: (b, h, 0, 0))],
        out_shape=[jax.ShapeDtypeStruct((batch * seq, v_heads * HEAD_DIM), BF16),
                   jax.ShapeDtypeStruct((batch, v_heads, HEAD_DIM, HEAD_DIM), state_dtype)],
        scratch_shapes=[pltpu.VMEM((hb // rep, HEAD_DIM, rep * HEAD_DIM), F32),
                        pltpu.VMEM((tt, wv), F32),
                        pltpu.VMEM((tt, wv), BF16),
                        pltpu.VMEM((tt, wv), BF16),
                        pltpu.VMEM((tt + SUBLANES, wq), F32),
                        pltpu.VMEM((tt + SUBLANES, wq), F32),
                        pltpu.VMEM((tt + SUBLANES, wv), F32),
                        pltpu.VMEM((tt, wq), F32),
                        pltpu.VMEM((tt, wq), F32)],
        compiler_params=_cparams("parallel", "parallel", "arbitrary"),
        name="gdn_prompt",
    )(proj, proj, proj, proj, proj, proj, conv_w, conv_w, conv_w, proj, bc, gc, gt, gr, norm_g)


def _gdn_sample_kernel(q_ref, k_ref, v_ref, z_ref, bc_ref, gc_ref, gt_ref, gr_ref, ng_ref, s0_ref,
                       o_ref, st_ref, *, hb, rep, seq):
    c = GDN_TILE
    nseq = c // seq
    per_group = SUBLANES // seq
    lsh = seq.bit_length() - 1
    rows_i, cols_i = _tile_iotas()
    same = (rows_i >> lsh) == (cols_i >> lsh)
    incl = (cols_i <= rows_i) & same
    strict = (cols_i < rows_i) & same
    row8 = lax.broadcasted_iota(jnp.int32, (SUBLANES, 1), 0)
    lane_seq = lax.broadcasted_iota(jnp.int32, (1, HEAD_DIM), 1) >> lsh
    ng = ng_ref[...]
    sel = _head_select(hb)
    bc = _lane_bcast_all(bc_ref[...], sel)
    gc = _lane_bcast_all(gc_ref[...], sel)
    gt = _lane_bcast_all(gt_ref[...], sel)
    gr = gr_ref[0]
    head = lambda ref, i: ref[:, i * HEAD_DIM:(i + 1) * HEAD_DIM]
    qs = [head(q_ref, i) for i in range(hb // rep)]
    ks = [head(k_ref, i) for i in range(hb // rep)]
    wy = _gdn_wy([(qs, ks, [head(v_ref, h) for h in range(hb)], bc, gc, gr)], rep,
                 rows_i, cols_i, incl, strict, seq)[0]
    k_t_seq = []
    for k in ks:
        k_t = _transpose_pad(k)
        k_t_seq.append(jnp.concatenate([jnp.where(lane_seq == b, k_t, 0.0)[:, :c] for b in range(nseq)],
                                       axis=0).astype(BF16))
    qe = [qs[h // rep] * jnp.exp(head(gc, h)) for h in range(hb)]
    ws_parts = [[] for _ in range(hb)]
    qs_parts = [[] for _ in range(hb)]
    for g in range(c // SUBLANES):
        gsl = slice(g * SUBLANES, (g + 1) * SUBLANES)
        for h in range(hb):
            acc = jnp.zeros((2 * SUBLANES, HEAD_DIM), F32)
            for p in range(per_group):
                valid = (row8 >= p * seq) & (row8 < (p + 1) * seq)
                lhs = jnp.concatenate([jnp.where(valid, wy[h][1][gsl], 0.0),
                                       jnp.where(valid, qe[h][gsl], 0.0)], axis=0)
                acc = acc + _bdot(lhs, s0_ref[g * per_group + p, h])
            ws_parts[h].append(acc[:SUBLANES])
            qs_parts[h].append(acc[SUBLANES:])
    v_new = [wy[h][0] - jnp.concatenate(ws_parts[h], axis=0) for h in range(hb)]
    outs = [jnp.concatenate(qs_parts[h], axis=0) + _bdot(wy[h][2][:, :c], v_new[h]) for h in range(hb)]
    upds = [jnp.dot(k_t_seq[h // rep], (v_new[h] * jnp.exp(head(gt, h) - head(gc, h))).astype(BF16),
                    preferred_element_type=F32) for h in range(hb)]
    for h in range(hb):
        o_ref[:, h * HEAD_DIM:(h + 1) * HEAD_DIM] = _head_out(outs[h], head(z_ref, h), ng).astype(o_ref.dtype)
        decay = jnp.exp(head(gt, h))
        for b in range(nseq):
            st_ref[b, h] = (decay[b * seq:b * seq + 1] * s0_ref[b, h].astype(F32)
                            + upds[h][b * HEAD_DIM:(b + 1) * HEAD_DIM]).astype(st_ref.dtype)


def _gdn_sample(qkv, proj, z_col0, gate_arrays, norm_g, s0, batch, seq, qk_heads, v_heads):
    rep = v_heads // qk_heads
    hb = _tile(v_heads, 4, rep)
    assert GDN_TILE % seq == 0 and SUBLANES % seq == 0 and seq & (seq - 1) == 0
    nb = GDN_TILE // seq
    assert batch % nb == 0
    r = GDN_TILE
    nhb = v_heads // hb
    wq = (hb // rep) * HEAD_DIM
    wv = hb * HEAD_DIM
    qk_w = qk_heads * HEAD_DIM
    bc, gc, gt, gr = _gdn_layouts(*gate_arrays, v_heads, hb)
    k0 = qk_w // wq
    v0 = 2 * qk_w // wv
    z0 = z_col0 // wv
    assert z_col0 % wv == 0
    st_spec = pl.BlockSpec((nb, hb, HEAD_DIM, HEAD_DIM), lambda b, h: (b, h, 0, 0))
    col_spec = pl.BlockSpec((None, r, hb), lambda b, h: (h, b, 0))
    return pl.pallas_call(
        functools.partial(_gdn_sample_kernel, hb=hb, rep=rep, seq=seq),
        grid=(batch // nb, nhb),
        in_specs=[pl.BlockSpec((r, wq), lambda b, h: (b, h)),
                  pl.BlockSpec((r, wq), lambda b, h: (b, k0 + h)),
                  pl.BlockSpec((r, wv), lambda b, h: (b, v0 + h)),
                  pl.BlockSpec((r, wv), lambda b, h: (b, z0 + h)),
                  col_spec, col_spec, col_spec,
                  pl.BlockSpec((None, 1, hb, HEAD_DIM), lambda b, h: (h, b, 0, 0)),
                  pl.BlockSpec((1, HEAD_DIM), lambda b, h: (0, 0)),
                  st_spec],
        out_specs=[pl.BlockSpec((r, wv), lambda b, h: (b, h)), st_spec],
        out_shape=[jax.ShapeDtypeStruct((batch * seq, v_heads * HEAD_DIM), BF16),
                   jax.ShapeDtypeStruct(s0.shape, s0.dtype)],
        compiler_params=_cparams("parallel", "parallel"),
        name="gdn_sample",
    )(qkv, qkv, qkv, proj, bc, gc, gt, gr, norm_g, s0)


def _merge_kernel(oh_ref, og_ref, wa_ref, wb_ref, ga_ref, gb_ref, o_ref):
    ya = _bdot(oh_ref[...], wa_ref[...])
    yb = _bdot(og_ref[...], wb_ref[...])
    o_ref[...] = (_sigmoid(ga_ref[...]) * ya + _sigmoid(gb_ref[...]) * yb).astype(o_ref.dtype)


def _merge(o_hg, o_gdn, w_a, w_b, gates, tm):
    m, ka = o_hg.shape
    kb = o_gdn.shape[1]
    d = w_a.shape[-1]
    tn = _tile(d, 256, 128)
    nj = d // tn
    return pl.pallas_call(
        _merge_kernel,
        grid=(m // tm, nj),
        in_specs=[pl.BlockSpec((tm, ka), lambda i, j: (i, 0), pipeline_mode=pl.Buffered(1)),
                  pl.BlockSpec((tm, kb), lambda i, j: (i, 0), pipeline_mode=pl.Buffered(1)),
                  pl.BlockSpec((None, ka, tn), lambda i, j: (0, 0, j)),
                  pl.BlockSpec((None, kb, tn), lambda i, j: (0, 0, j)),
                  pl.BlockSpec((tm, tn), lambda i, j: (i, j)),
                  pl.BlockSpec((tm, tn), lambda i, j: (i, nj + j))],
        out_specs=pl.BlockSpec((tm, tn), lambda i, j: (i, j)),
        out_shape=jax.ShapeDtypeStruct((m, d), BF16),
        compiler_params=_cparams("parallel", "arbitrary"),
        name="merge",
    )(o_hg, o_gdn, w_a, w_b, gates, gates)


def _outproj_kernel(m_ref, w_ref, x_ref, g1_ref, sc_ref, sh_ref, ng_ref, h_ref, a_ref):
    ng = ng_ref[...]
    mix = _bdot(m_ref[...], w_ref[...])
    h = x_ref[...].astype(F32) + g1_ref[...] * (_rms(mix) * ng[0:1])
    h_ref[...] = h
    a_ref[...] = ((_rms(h) * ng[1:2]) * (1.0 + sc_ref[...]) + sh_ref[...]).astype(a_ref.dtype)


def _outproj(merged, w_o_bf16, x, mod, norm_g2, rows_per_seq, tm):
    m, d = x.shape
    kdim = merged.shape[1]
    row = lambda i: (i, 0)
    return pl.pallas_call(
        _outproj_kernel,
        grid=(m // tm,),
        in_specs=[pl.BlockSpec((tm, kdim), row),
                  pl.BlockSpec((kdim, d), lambda i: (0, 0), pipeline_mode=pl.Buffered(1)),
                  pl.BlockSpec((tm, d), row),
                  _mod_spec(mod, 2, tm, rows_per_seq),
                  _mod_spec(mod, 4, tm, rows_per_seq),
                  _mod_spec(mod, 3, tm, rows_per_seq),
                  pl.BlockSpec((2, d), lambda i: (0, 0))],
        out_specs=[pl.BlockSpec((tm, d), row), pl.BlockSpec((tm, d), row)],
        out_shape=[jax.ShapeDtypeStruct((m, d), F32), jax.ShapeDtypeStruct((m, d), BF16)],
        compiler_params=_cparams("parallel"),
        name="out_proj",
    )(merged, w_o_bf16, x, mod, mod, mod, norm_g2)


def _cast_kernel(x_ref, o_ref):
    o_ref[...] = x_ref[...].astype(o_ref.dtype)


def _to_bf16(w):
    kdim, n = w.shape
    tk = _tile(kdim, 512, 16)
    return pl.pallas_call(
        _cast_kernel,
        grid=(kdim // tk,),
        in_specs=[pl.BlockSpec((tk, n), lambda i: (i, 0))],
        out_specs=pl.BlockSpec((tk, n), lambda i: (i, 0)),
        out_shape=jax.ShapeDtypeStruct((kdim, n), BF16),
        compiler_params=_cparams("parallel"),
        name="weight_to_bf16",
    )(w)


def _ffn_up_kernel(a_ref, w_ref, o_ref):
    o_ref[...] = jnp.square(jnp.maximum(_bdot(a_ref[...], w_ref[...]), 0.0)).astype(o_ref.dtype)


def _ffn_up(a, w_up, tm):
    m, d = a.shape
    dff = w_up.shape[-1]
    tf = _tile(dff, 512, 128)
    return pl.pallas_call(
        _ffn_up_kernel,
        grid=(m // tm, dff // tf),
        in_specs=[pl.BlockSpec((tm, d), lambda i, f: (i, 0)),
                  pl.BlockSpec((None, d, tf), lambda i, f: (0, 0, f))],
        out_specs=pl.BlockSpec((tm, tf), lambda i, f: (i, f)),
        out_shape=jax.ShapeDtypeStruct((m, dff), BF16),
        compiler_params=_cparams("parallel", "arbitrary"),
        name="ffn_up",
    )(a, w_up)


EPILOGUE_ROWS = 128


def _ffn_down_kernel(hid_ref, w_ref, h_ref, g2_ref, ng_ref, y_ref):
    j = pl.program_id(1)
    tn = w_ref.shape[1]
    y_ref[:, pl.ds(pl.multiple_of(j * tn, tn), tn)] = _bdot(hid_ref[...], w_ref[...])

    @pl.when(j == pl.num_programs(1) - 1)
    def _():
        step = min(EPILOGUE_ROWS, y_ref.shape[0])

        def body(r, carry):
            rows = pl.ds(pl.multiple_of(r * step, step), step)
            g2 = g2_ref[rows, :] if g2_ref.shape[0] > 1 else g2_ref[...]
            y_ref[rows, :] = h_ref[rows, :] + g2 * (_rms(y_ref[rows, :]) * ng_ref[...])
            return carry

        lax.fori_loop(0, y_ref.shape[0] // step, body, 0)


def _ffn_down(hid, w_down, h, mod, norm_g, rows_per_seq, tm, out_dtype):
    m, d = h.shape
    dff = hid.shape[1]
    tn = _tile(d, 256, 128)
    row = lambda i, j: (i, 0)
    assert out_dtype == F32, "the resident output block holds the f32 pre-norm result"
    return pl.pallas_call(
        _ffn_down_kernel,
        grid=(m // tm, d // tn),
        in_specs=[pl.BlockSpec((tm, dff), row, pipeline_mode=pl.Buffered(1)),
                  pl.BlockSpec((dff, tn), lambda i, j: (0, j)),
                  pl.BlockSpec((tm, d), row, pipeline_mode=pl.Buffered(1)),
                  _mod_spec(mod, 5, tm, rows_per_seq),
                  pl.BlockSpec((1, d), lambda i, j: (0, 0))],
        out_specs=pl.BlockSpec((tm, d), row),
        out_shape=jax.ShapeDtypeStruct((m, d), out_dtype),
        compiler_params=_cparams("parallel", "arbitrary", vmem_limit=VMEM_LIMIT_FFN_DOWN),
        name="ffn_down",
    )(hid, w_down, h, mod, norm_g)


def _run_group(x3, mod, rows_per_seq_for_mod, conv_buf, s_hg, s_gdn, lb, p, prompt):
    batch, seq, d = x3.shape
    m = batch * seq
    x = x3.reshape(m, d)
    hg_heads, gv_heads = p["hg_heads"], p["gv_heads"]
    gq_heads = p["gq_heads"]
    hg_w = hg_heads * HEAD_DIM
    gq_w = gq_heads * HEAD_DIM
    gv_w = gv_heads * HEAD_DIM
    ch = 2 * gq_w + gv_w
    main = 4 * hg_w + ch + gv_w
    conv0 = 4 * hg_w
    z0 = conv0 + ch
    rows_unit = seq if prompt else m
    tm = _tile(rows_unit, 512, SUBLANES)
    tm_proj = _tile(rows_unit, 2048, SUBLANES)
    tm_wide = _tile(rows_unit, 1024, SUBLANES)

    a1 = _prenorm(x, p["norm_g"][0:1], mod, 1, 0, rows_per_seq_for_mod, tm)
    proj = _project(a1, p["w_in"], main, tm_proj, _tile(main, 512, 128))
    tail = _project(a1, p["w_tail"], p["w_tail"].shape[0], tm_proj, p["w_tail"].shape[0])
    gates = _project(a1, p["w_gates"], 2 * d, tm_proj, _tile(2 * d, 512, 128))

    if prompt:
        o_hg, s_hg_new = _hgrn_prompt(proj, lb, p["hg_norm_g"], batch, seq, hg_heads, p["hg_dtype"])
        gate_arrays = _gdn_gates(tail, p["a_log"], p["dt_bias"], GDN_CHUNK)
        o_gdn, s_gdn_new = _gdn_prompt(proj, conv0, z0, p["conv_w"], gate_arrays, p["gdn_norm_g"], batch, seq,
                                       gq_heads, gv_heads, p["gdn_dtype"])
    else:
        o_hg, s_hg_new = _hgrn_sample(proj, lb, p["hg_norm_g"], s_hg, batch, seq, hg_heads)
        gate_arrays = _gdn_gates(tail, p["a_log"], p["dt_bias"], seq)
        buf_rows = jnp.pad(conv_buf.astype(F32), ((0, 0), (0, seq - (CONV_W - 1)), (0, 0))).reshape(m, ch)
        qkv = _conv_sample(proj, conv0, buf_rows, p["conv_w"], seq, gq_w, ch)
        o_gdn, s_gdn_new = _gdn_sample(qkv, proj, z0, gate_arrays, p["gdn_norm_g"], s_gdn, batch, seq,
                                       gq_heads, gv_heads)

    if seq % SUBLANES == 0:
        conv_new = proj.reshape(batch, seq, main)[:, seq - (CONV_W - 1):, conv0:conv0 + ch]
    else:
        conv_new = proj[:, conv0:conv0 + ch].reshape(batch, seq, ch)[:, seq - (CONV_W - 1):]

    merged = _merge(o_hg, o_gdn, p["w_out_hg"], p["w_out_gdn"], gates, tm_wide)
    h1, a2 = _outproj(merged, p["w_o_bf16"], x, mod, p["norm_g"][1:3], rows_per_seq_for_mod, tm)
    hid = _ffn_up(a2, p["w_up"], tm_proj)
    y = _ffn_down(hid, p["w_down_bf16"], h1, mod, p["norm_g"][3:4], rows_per_seq_for_mod, tm_wide, x3.dtype)
    return y.reshape(batch, seq, d), conv_new, s_hg_new, s_gdn_new


def kernel(x_prompt, x_sample, state_hgrn, state_gdn, cache_conv, c_prompt, c_sample, lb_logits, w_ada,
           b_ada, norm_g, w_in, conv_w, A_log, dt_bias, hg_norm_g, gdn_norm_g, w_out_hg, w_out_gdn, w_o,
           w_up, w_down):
    depth = w_in.shape[0]
    assert depth == 1, "single-layer trunk"
    bp, tp, d = x_prompt.shape
    bs, ts, _ = x_sample.shape
    hg_heads = state_hgrn.shape[2]
    gv_heads = state_gdn.shape[2]
    ch = cache_conv.shape[-1]
    gq_heads = (ch // HEAD_DIM - gv_heads) // 2
    assert tp >= CONV_W - 1 and ts >= CONV_W - 1
    assert tp % GDN_CHUNK == 0 and tp % HG_CHUNK == 0

    main = 4 * hg_heads * HEAD_DIM + ch + gv_heads * HEAD_DIM
    w_in_t = jnp.swapaxes(w_in[0], 0, 1)
    p = dict(
        hg_heads=hg_heads, gv_heads=gv_heads, gq_heads=gq_heads,
        hg_dtype=state_hgrn.dtype, gdn_dtype=state_gdn.dtype,
        norm_g=norm_g[0].astype(F32), w_in=w_in_t,
        w_tail=w_in_t[main:main + 2 * gv_heads],
        w_gates=w_in_t[main + 2 * gv_heads:],
        conv_w=conv_w.astype(F32), a_log=A_log.astype(F32), dt_bias=dt_bias.astype(F32),
        hg_norm_g=hg_norm_g.astype(F32), gdn_norm_g=gdn_norm_g.astype(F32),
        w_out_hg=w_out_hg, w_out_gdn=w_out_gdn, w_o_bf16=_to_bf16(w_o[0]), w_up=w_up,
        w_down_bf16=_to_bf16(w_down[0]),
    )

    lbs = jnp.cumsum(jax.nn.softmax(lb_logits.astype(F32), axis=0), axis=0)
    lb = lbs[0:1]

    n_c = bp + bs
    pad = (-n_c) % SUBLANES
    c_all = jnp.concatenate([c_prompt.astype(F32), c_sample.astype(F32), jnp.zeros((pad, d), F32)], axis=0)
    mod_all = _ada(c_all, w_ada, b_ada.astype(F32))
    mod_p = mod_all[:bp].reshape(bp * 6, 1, d)
    mod_s = jnp.repeat(mod_all[bp:bp + bs].reshape(bs, 6, d), ts, axis=0).transpose(1, 0, 2)

    y_p, conv_p, hg_p, gdn_p = _run_group(x_prompt, mod_p, tp, None, None, None, lb, p, prompt=True)
    y_s, conv_s, hg_s, gdn_s = _run_group(x_sample, mod_s, ts, cache_conv[0], state_hgrn[0], state_gdn[0],
                                          lb, p, prompt=False)
    return (y_p, y_s,
            hg_p[None], gdn_p[None], conv_p.astype(cache_conv.dtype)[None],
            hg_s[None], gdn_s[None], conv_s.astype(cache_conv.dtype)[None])
```

```python
import functools

import jax
import jax.numpy as jnp
from jax import lax
from jax.experimental import pallas as pl
from jax.experimental.pallas import tpu as pltpu

F32 = jnp.float32
BF16 = jnp.bfloat16
HIGHEST = lax.Precision.HIGHEST

HEAD_DIM = 128
HG_CHUNK = 32
GDN_CHUNK = 64
CONV_W = 4
NORM_EPS = 1e-6
SUBLANES = 8
VMEM_LIMIT = 48 * 1024 * 1024
VMEM_LIMIT_FFN_DOWN = 54 * 1024 * 1024


def _cparams(*sem, vmem_limit=VMEM_LIMIT):
    return pltpu.CompilerParams(dimension_semantics=sem, vmem_limit_bytes=vmem_limit)


def _tile(n, pref, quantum):
    if n <= pref:
        return n
    t = (pref // quantum) * quantum
    while t > quantum and n % t:
        t -= quantum
    assert n % t == 0, (n, pref, quantum)
    return t


def _bdot(a, b):
    return jnp.dot(a.astype(BF16), b.astype(BF16), preferred_element_type=F32)


def _bdot_nt(a, b):
    return lax.dot_general(a.astype(BF16), b.astype(BF16), (((1,), (1,)), ((), ())),
                           preferred_element_type=F32)


def _fdot(a, b):
    return jnp.dot(a, b, precision=HIGHEST, preferred_element_type=F32)


def _split3(x):
    hi = x.astype(BF16)
    r = x - hi.astype(F32)
    mid = r.astype(BF16)
    lo = (r - mid.astype(F32)).astype(BF16)
    return hi, mid, lo


def _mask_dot(m_bf, x):
    hi, mid, lo = _split3(x)
    return (jnp.dot(m_bf, hi, preferred_element_type=F32)
            + jnp.dot(m_bf, mid, preferred_element_type=F32)
            + jnp.dot(m_bf, lo, preferred_element_type=F32))


def _sigmoid(x):
    return 1.0 / (1.0 + jnp.exp(-x))


def _silu(x):
    return x * _sigmoid(x)


def _rms(x):
    return x * lax.rsqrt(jnp.mean(x * x, axis=-1, keepdims=True) + NORM_EPS)


def _transpose_pad(x):
    r = x.shape[0]
    if r < HEAD_DIM:
        x = jnp.concatenate([x, jnp.zeros((HEAD_DIM - r, HEAD_DIM), x.dtype)], axis=0)
    return x.T


def _tri_masks(c):
    row = lax.broadcasted_iota(jnp.int32, (c, c), 0)
    col = lax.broadcasted_iota(jnp.int32, (c, c), 1)
    return col <= row, col < row


def _ada_kernel(c_ref, w_ref, b_ref, o_ref):
    cs = _silu(c_ref[...])
    w = w_ref[...]
    c_hi = cs.astype(BF16)
    c_lo = (cs - c_hi.astype(F32)).astype(BF16)
    w_hi = w.astype(BF16)
    w_lo = (w - w_hi.astype(F32)).astype(BF16)
    acc = jnp.dot(c_hi, w_hi, preferred_element_type=F32)
    acc += jnp.dot(c_lo, w_hi, preferred_element_type=F32)
    acc += jnp.dot(c_hi, w_lo, preferred_element_type=F32)
    o_ref[...] = acc + b_ref[...]


def _ada(c_all, w_ada, b_ada):
    m, d = c_all.shape
    n = w_ada.shape[-1]
    tn = _tile(n, 512, 128)
    return pl.pallas_call(
        _ada_kernel,
        grid=(n // tn,),
        in_specs=[pl.BlockSpec((m, d), lambda j: (0, 0)),
                  pl.BlockSpec((None, d, tn), lambda j: (0, 0, j)),
                  pl.BlockSpec((1, tn), lambda j: (0, j))],
        out_specs=pl.BlockSpec((m, tn), lambda j: (0, j)),
        out_shape=jax.ShapeDtypeStruct((m, n), F32),
        compiler_params=_cparams("arbitrary"),
        name="ada_mod",
    )(c_all, w_ada, b_ada)


def _mod_spec(mod, k, tm, rows_per_seq):
    d = mod.shape[-1]
    if mod.shape[1] == 1:
        tiles_per_seq = rows_per_seq // tm
        return pl.BlockSpec((None, 1, d), lambda i, *_: ((i // tiles_per_seq) * 6 + k, 0, 0))
    return pl.BlockSpec((None, tm, d), lambda i, *_: (k, i, 0))


def _prenorm_kernel(x_ref, g_ref, sc_ref, sh_ref, o_ref):
    xn = _rms(x_ref[...].astype(F32)) * g_ref[...]
    o_ref[...] = (xn * (1.0 + sc_ref[...]) + sh_ref[...]).astype(o_ref.dtype)


def _prenorm(x, g, mod, k_scale, k_shift, rows_per_seq, tm):
    m, d = x.shape
    return pl.pallas_call(
        _prenorm_kernel,
        grid=(m // tm,),
        in_specs=[pl.BlockSpec((tm, d), lambda i: (i, 0)),
                  pl.BlockSpec((1, d), lambda i: (0, 0)),
                  _mod_spec(mod, k_scale, tm, rows_per_seq),
                  _mod_spec(mod, k_shift, tm, rows_per_seq)],
        out_specs=pl.BlockSpec((tm, d), lambda i: (i, 0)),
        out_shape=jax.ShapeDtypeStruct((m, d), BF16),
        compiler_params=_cparams("parallel"),
        name="prenorm",
    )(x, g, mod, mod)


def _mm_nt_kernel(a_ref, w_ref, o_ref):
    o_ref[...] = _bdot_nt(a_ref[...], w_ref[...])


def _project(a, w_t, n_cols, tm, tn):
    m, kdim = a.shape
    return pl.pallas_call(
        _mm_nt_kernel,
        grid=(m // tm, n_cols // tn),
        in_specs=[pl.BlockSpec((tm, kdim), lambda i, j: (i, 0)),
                  pl.BlockSpec((tn, kdim), lambda i, j: (j, 0))],
        out_specs=pl.BlockSpec((tm, tn), lambda i, j: (i, j)),
        out_shape=jax.ShapeDtypeStruct((m, n_cols), F32),
        compiler_params=_cparams("parallel", "arbitrary"),
        name="in_proj",
    )(a, w_t)


def _hgrn_elementwise(hq, hf, lb):
    q = _silu(hq)
    f = lb + (1.0 - lb) * _sigmoid(hf)
    return q, jnp.log(f), 1.0 - f


def _hgrn_chunk_free(q, g, k, inp, n, tril, tril_bf):
    c = q.shape[0]
    heads = [slice(h * HEAD_DIM, (h + 1) * HEAD_DIM) for h in range(n)]
    big_g = _mask_dot(tril_bf, g)
    g_ref = big_g[c // 2:c // 2 + 1]
    g_last = big_g[c - 1:c]
    qe = q * jnp.exp(big_g - g_ref)
    ke = k * jnp.exp(g_ref - big_g)
    qs = q * jnp.exp(big_g)
    kl = k * jnp.exp(g_last - big_g)
    d_hi, d_mid, d_lo = [p.astype(F32) for p in _split3(jnp.exp(g_last))]
    extra = jnp.concatenate([d_hi, d_mid, d_lo, jnp.zeros((SUBLANES - 3, q.shape[1]), F32)], axis=0)
    a = [jnp.where(tril, _bdot_nt(qe[:, sl], ke[:, sl]), 0.0) for sl in heads]
    o_intra = [_bdot(a[h], inp[:, sl]) for h, sl in enumerate(heads)]
    t = [_transpose_pad(jnp.concatenate([kl[:, sl], extra[:, sl]], axis=0))[:, :c + SUBLANES] for sl in heads]
    piece_rows = lax.broadcasted_iota(jnp.int32, (SUBLANES, HEAD_DIM), 0) < 3
    ones_rows = jnp.where(piece_rows, 1.0, 0.0)
    zeros_c = jnp.zeros((c, HEAD_DIM), F32)
    uds = []
    for h, sl in enumerate(heads):
        rhs = jnp.concatenate([jnp.concatenate([inp[:, sl], zeros_c], axis=1),
                               jnp.concatenate([jnp.zeros((SUBLANES, HEAD_DIM), F32), ones_rows], axis=1)], axis=0)
        uds.append(_bdot(t[h], rhs))
    return o_intra, qs, uds


def _hgrn_chunk_heads(q, g, k, inp, states, tril, tril_bf):
    o_intra, qs, uds = _hgrn_chunk_free(q, g, k, inp, len(states), tril, tril_bf)
    out = []
    for h, s in enumerate(states):
        sl = slice(h * HEAD_DIM, (h + 1) * HEAD_DIM)
        out.append((o_intra[h] + _bdot(qs[:, sl], s), uds[h][:, HEAD_DIM:] * s + uds[h][:, :HEAD_DIM]))
    return out


def _head_out(o, gate, norm_g):
    mean_sq = _lane_sum_bcast(o * o) * (1.0 / HEAD_DIM)
    return o * lax.rsqrt(mean_sq + NORM_EPS) * norm_g * _silu(gate)


HGRN_CHUNKS_PER_PASS = 4


def _hgrn_prompt_kernel(q_ref, f_ref, i_ref, g_ref, lb_ref, ng_ref, o_ref, st_ref,
                        s_scr, oi_scr, qs_scr, ud_scr, *, hb):
    t = pl.program_id(2)
    c = HG_CHUNK
    n_chunks = q_ref.shape[0] // c
    per = HGRN_CHUNKS_PER_PASS if n_chunks % HGRN_CHUNKS_PER_PASS == 0 else 1
    heads = [slice(h * HEAD_DIM, (h + 1) * HEAD_DIM) for h in range(hb)]

    @pl.when(t == 0)
    def _():
        s_scr[...] = jnp.zeros_like(s_scr)

    tril, _ = _tri_masks(c)
    tril_bf = jnp.where(tril, 1.0, 0.0).astype(BF16)
    lb = lb_ref[...]
    ng = ng_ref[...]

    def free_part(it, carry):
        rows = [pl.ds(pl.multiple_of((it * per + j) * c, c), c) for j in range(per)]
        parts = [_hgrn_elementwise(q_ref[r, :], f_ref[r, :], lb) for r in rows]
        cat = lambda xs: jnp.concatenate(xs, axis=1)
        o_intra, qs, uds = _hgrn_chunk_free(cat([p[0] for p in parts]), cat([p[1] for p in parts]),
                                            cat([p[2] for p in parts]), cat([i_ref[r, :] for r in rows]),
                                            per * hb, tril, tril_bf)
        for j, r in enumerate(rows):
            qs_scr[r, :] = qs[:, j * hb * HEAD_DIM:(j + 1) * hb * HEAD_DIM].astype(qs_scr.dtype)
            for h, sl in enumerate(heads):
                oi_scr[r, sl] = o_intra[j * hb + h]
                ud_scr[(it * per + j) * hb + h] = uds[j * hb + h]
        return carry

    lax.fori_loop(0, n_chunks // per, free_part, 0)

    states = [s_scr[h] for h in range(hb)]
    outs = []
    for ci in range(n_chunks):
        rows = pl.ds(ci * c, c)
        outs.append([oi_scr[rows, sl] + _bdot(qs_scr[rows, sl], states[h]) for h, sl in enumerate(heads)])
        for h in range(hb):
            ud = ud_scr[ci * hb + h]
            states[h] = ud[:, HEAD_DIM:] * states[h] + ud[:, :HEAD_DIM]
    for h in range(hb):
        s_scr[h] = states[h]
    for ci in range(n_chunks):
        rows = pl.ds(ci * c, c)
        for h, sl in enumerate(heads):
            o_ref[rows, sl] = _head_out(outs[ci][h], g_ref[rows, sl], ng).astype(o_ref.dtype)

    @pl.when(t == pl.num_programs(2) - 1)
    def _():
        st_ref[...] = s_scr[...].astype(st_ref.dtype)


def _hgrn_prompt(proj, lb, norm_g, batch, seq, heads, state_dtype):
    hb = _tile(heads, 8, 1)
    tt = _tile(seq, 256, HG_CHUNK)
    w = hb * HEAD_DIM
    nhb = heads // hb
    nt = seq // tt

    def col(seg):
        return pl.BlockSpec((tt, w), lambda b, h, t: (b * nt + t, seg * nhb + h))

    return pl.pallas_call(
        functools.partial(_hgrn_prompt_kernel, hb=hb),
        grid=(batch, nhb, nt),
        in_specs=[col(0), col(1), col(2), col(3),
                  pl.BlockSpec((1, w), lambda b, h, t: (0, h)),
                  pl.BlockSpec((1, HEAD_DIM), lambda b, h, t: (0, 0))],
        out_specs=[pl.BlockSpec((tt, w), lambda b, h, t: (b * nt + t, h)),
                   pl.BlockSpec((None, hb, HEAD_DIM, HEAD_DIM), lambda b, h, t: (b, h, 0, 0))],
        out_shape=[jax.ShapeDtypeStruct((batch * seq, heads * HEAD_DIM), BF16),
                   jax.ShapeDtypeStruct((batch, heads, HEAD_DIM, HEAD_DIM), state_dtype)],
        scratch_shapes=[pltpu.VMEM((hb, HEAD_DIM, HEAD_DIM), F32),
                        pltpu.VMEM((tt, w), F32),
                        pltpu.VMEM((tt, w), BF16),
                        pltpu.VMEM(((tt // HG_CHUNK) * hb, HEAD_DIM, 2 * HEAD_DIM), F32)],
        compiler_params=_cparams("parallel", "parallel", "arbitrary"),
        name="hgrn_prompt",
    )(proj, proj, proj, proj, lb, norm_g)


def _hgrn_sample_kernel(q_ref, f_ref, i_ref, g_ref, lb_ref, ng_ref, s0_ref, o_ref, st_ref, *, hb, seq):
    c = SUBLANES
    per_group = c // seq
    tril, _ = _tri_masks(c)
    tril_bf = jnp.where(tril, 1.0, 0.0).astype(BF16)
    lb = lb_ref[...]
    ng = ng_ref[...]
    row = lax.broadcasted_iota(jnp.int32, (c, 1), 0)

    n_groups = q_ref.shape[0] // c
    valid = [(row >= p * seq) & (row < (p + 1) * seq) for p in range(per_group)]
    masked = lambda x: [jnp.where(v, x, 0.0) for v in valid]
    qs, gs, ks, inps, states = [], [], [], [], []
    for gi in range(n_groups):
        rows = pl.ds(gi * c, c)
        q, g, k = _hgrn_elementwise(q_ref[rows, :], f_ref[rows, :], lb)
        qs += masked(q)
        gs += masked(g)
        ks += masked(k)
        inps += [i_ref[rows, :]] * per_group
        states += [s0_ref[gi * per_group + p, h].astype(F32) for p in range(per_group) for h in range(hb)]
    cat = lambda xs: jnp.concatenate(xs, axis=1)
    res = _hgrn_chunk_heads(cat(qs), cat(gs), cat(ks), cat(inps), states, tril, tril_bf)
    for gi in range(n_groups):
        rows = pl.ds(gi * c, c)
        for h in range(hb):
            sl = slice(h * HEAD_DIM, (h + 1) * HEAD_DIM)
            o_all = jnp.zeros((c, HEAD_DIM), F32)
            for p in range(per_group):
                o, s_new = res[(gi * per_group + p) * hb + h]
                st_ref[gi * per_group + p, h] = s_new.astype(st_ref.dtype)
                o_all = jnp.where(valid[p], o, o_all)
            o_ref[rows, sl] = _head_out(o_all, g_ref[rows, sl], ng).astype(o_ref.dtype)


def _hgrn_sample(proj, lb, norm_g, s0, batch, seq, heads):
    assert SUBLANES % seq == 0
    hb = _tile(heads, 4, 1)
    nb = _tile(batch, 8, SUBLANES // seq)
    w = hb * HEAD_DIM
    nhb = heads // hb
    r = nb * seq

    def col(seg):
        return pl.BlockSpec((r, w), lambda b, h: (b, seg * nhb + h))

    st_spec = pl.BlockSpec((nb, hb, HEAD_DIM, HEAD_DIM), lambda b, h: (b, h, 0, 0))
    return pl.pallas_call(
        functools.partial(_hgrn_sample_kernel, hb=hb, seq=seq),
        grid=(batch // nb, nhb),
        in_specs=[col(0), col(1), col(2), col(3),
                  pl.BlockSpec((1, w), lambda b, h: (0, h)),
                  pl.BlockSpec((1, HEAD_DIM), lambda b, h: (0, 0)),
                  st_spec],
        out_specs=[pl.BlockSpec((r, w), lambda b, h: (b, h)), st_spec],
        out_shape=[jax.ShapeDtypeStruct((batch * seq, heads * HEAD_DIM), BF16),
                   jax.ShapeDtypeStruct(s0.shape, s0.dtype)],
        compiler_params=_cparams("parallel", "parallel"),
        name="hgrn_sample",
    )(proj, proj, proj, proj, lb, norm_g, s0)


def _gdn_gates_kernel(x_ref, alog_ref, dtb_ref, beta_ref, gam_ref, gtot_ref, *, nv, chunk):
    x = x_ref[...]
    r = x.shape[0]
    beta_ref[...] = _sigmoid(x[:, :nv])
    z = x[:, nv:] + dtb_ref[...]
    softplus = jnp.maximum(z, 0.0) + jnp.log(1.0 + jnp.exp(-jnp.abs(z)))
    g = -jnp.exp(alog_ref[...]) * softplus
    row = lax.broadcasted_iota(jnp.int32, (r, r), 0)
    col = lax.broadcasted_iota(jnp.int32, (r, r), 1)
    same = (row // chunk) == (col // chunk)
    gam_ref[...] = _mask_dot(jnp.where(same & (col <= row), 1.0, 0.0).astype(BF16), g)
    gtot_ref[...] = _mask_dot(jnp.where(same, 1.0, 0.0).astype(BF16), g)


def _gdn_gates(x, a_log, dt_bias, chunk):
    m, two_nv = x.shape
    nv = two_nv // 2
    tr = _tile(m, 256, max(chunk, SUBLANES))
    out = jax.ShapeDtypeStruct((m, nv), F32)
    return pl.pallas_call(
        functools.partial(_gdn_gates_kernel, nv=nv, chunk=chunk),
        grid=(m // tr,),
        in_specs=[pl.BlockSpec((tr, two_nv), lambda i: (i, 0)),
                  pl.BlockSpec((1, nv), lambda i: (0, 0)),
                  pl.BlockSpec((1, nv), lambda i: (0, 0))],
        out_specs=[pl.BlockSpec((tr, nv), lambda i: (i, 0))] * 3,
        out_shape=[out, out, out],
        compiler_params=_cparams("parallel"),
        name="gdn_gates",
    )(x, a_log, dt_bias)


def _lane_sum_bcast(x):
    hi = x.astype(BF16)
    lo = (x - hi.astype(F32)).astype(BF16)
    return jnp.dot(jnp.concatenate([hi, lo], axis=1), jnp.ones((2 * HEAD_DIM, HEAD_DIM), BF16),
                   preferred_element_type=F32)


def _l2_heads(y, mult):
    parts = []
    for h in range(y.shape[1] // HEAD_DIM):
        yh = y[:, h * HEAD_DIM:(h + 1) * HEAD_DIM]
        parts.append(yh * (lax.rsqrt(_lane_sum_bcast(yh * yh) + NORM_EPS) * mult))
    return jnp.concatenate(parts, axis=1)


def _conv_finish(y, kind_q, kind_k, o_ref, scale):
    y = _silu(y)

    @pl.when(kind_q)
    def _():
        o_ref[...] = _l2_heads(y, scale)

    @pl.when(kind_k)
    def _():
        o_ref[...] = _l2_heads(y, 1.0)

    @pl.when(jnp.logical_not(kind_q | kind_k))
    def _():
        o_ref[...] = y


CONV_ROWS = 32


def _conv_prompt_kernel(cur_ref, prev_ref, buf_ref, cw_ref, o_ref, x_scr, *, tiles_per_seq, nq_tiles, scale):
    i = pl.program_id(0)
    j = pl.program_id(1)
    first = (i % tiles_per_seq) == 0
    tt = cur_ref.shape[0]
    x_scr[0:SUBLANES, :] = jnp.where(first, buf_ref[...], prev_ref[...])
    x_scr[SUBLANES:SUBLANES + tt, :] = cur_ref[...]
    cw = cw_ref[...]

    def run(finish):
        for r0 in range(0, tt, CONV_ROWS):
            y = None
            for s in range(CONV_W):
                lo = SUBLANES + r0 - s
                term = cw[CONV_W - 1 - s:CONV_W - s] * x_scr[lo:lo + CONV_ROWS, :]
                y = term if y is None else y + term
            o_ref[r0:r0 + CONV_ROWS, :] = finish(_silu(y))

    kind_q = j < nq_tiles
    kind_k = (j >= nq_tiles) & (j < 2 * nq_tiles)
    pl.when(kind_q)(lambda: run(lambda y: _l2_heads(y, scale)))
    pl.when(kind_k)(lambda: run(lambda y: _l2_heads(y, 1.0)))
    pl.when(jnp.logical_not(kind_q | kind_k))(lambda: run(lambda y: y))


def _conv_prompt(proj, col0, conv_buf8, conv_w, batch, seq, qk_w, ch):
    wc = _tile(qk_w, 512, HEAD_DIM)
    tt = _tile(seq, 256, SUBLANES)
    nt = seq // tt
    c0 = col0 // wc
    assert col0 % wc == 0
    return pl.pallas_call(
        functools.partial(_conv_prompt_kernel, tiles_per_seq=nt, nq_tiles=qk_w // wc,
                          scale=float(HEAD_DIM) ** -0.5),
        grid=(batch * nt, ch // wc),
        in_specs=[pl.BlockSpec((tt, wc), lambda i, j: (i, c0 + j)),
                  pl.BlockSpec((SUBLANES, wc),
                               lambda i, j: (jnp.maximum(i * (tt // SUBLANES) - 1, 0), c0 + j)),
                  pl.BlockSpec((SUBLANES, wc), lambda i, j: (i // nt, j)),
                  pl.BlockSpec((None, CONV_W, wc), lambda i, j: (0, 0, j))],
        out_specs=pl.BlockSpec((tt, wc), lambda i, j: (i, j)),
        out_shape=jax.ShapeDtypeStruct((batch * seq, ch), F32),
        scratch_shapes=[pltpu.VMEM((tt + SUBLANES, wc), F32)],
        compiler_params=_cparams("parallel", "parallel"),
        name="gdn_conv_prompt",
    )(proj, proj, conv_buf8, conv_w)


def _conv_sample_kernel(u_ref, buf_ref, cw_ref, o_ref, *, seq, nq_tiles, scale):
    j = pl.program_id(1)
    u = u_ref[...]
    buf = buf_ref[...]
    r = u.shape[0]
    tok = lax.broadcasted_iota(jnp.int32, (r, 1), 0) % seq
    cw = cw_ref[...]
    y = cw[CONV_W - 1:CONV_W] * u
    for s in range(1, CONV_W):
        from_u = pltpu.roll(u, s, axis=0)
        up = CONV_W - 1 - s
        from_buf = pltpu.roll(buf, r - up, axis=0) if up else buf
        y = y + cw[CONV_W - 1 - s:CONV_W - s] * jnp.where(tok >= s, from_u, from_buf)
    _conv_finish(y, j < nq_tiles, (j >= nq_tiles) & (j < 2 * nq_tiles), o_ref, scale)


def _conv_sample(proj, col0, buf_rows, conv_w, seq, qk_w, ch):
    m = proj.shape[0]
    wc = _tile(qk_w, 512, HEAD_DIM)
    tr = _tile(m, 256, SUBLANES)
    c0 = col0 // wc
    assert col0 % wc == 0 and tr % seq == 0 and seq >= CONV_W - 1
    return pl.pallas_call(
        functools.partial(_conv_sample_kernel, seq=seq, nq_tiles=qk_w // wc,
                          scale=float(HEAD_DIM) ** -0.5),
        grid=(m // tr, ch // wc),
        in_specs=[pl.BlockSpec((tr, wc), lambda i, j: (i, c0 + j)),
                  pl.BlockSpec((tr, wc), lambda i, j: (i, j)),
                  pl.BlockSpec((None, CONV_W, wc), lambda i, j: (0, 0, j))],
        out_specs=pl.BlockSpec((tr, wc), lambda i, j: (i, j)),
        out_shape=jax.ShapeDtypeStruct((m, ch), F32),
        compiler_params=_cparams("parallel", "parallel"),
        name="gdn_conv_sample",
    )(proj, buf_rows, conv_w)


GDN_TILE = 64
HALF_LANES = 64


def _mm_tile(a, b):
    return _bdot(a[:, :GDN_TILE], b)


def _lane_bcast_all(x, sel):
    return _mask_dot_right(x, sel)


def _mask_dot_right(x, m_bf):
    hi, mid, lo = _split3(x)
    return (jnp.dot(hi, m_bf, preferred_element_type=F32)
            + jnp.dot(mid, m_bf, preferred_element_type=F32)
            + jnp.dot(lo, m_bf, preferred_element_type=F32))


def _head_select(hb):
    head = lax.broadcasted_iota(jnp.int32, (hb, hb * HEAD_DIM), 0)
    lane_head = lax.broadcasted_iota(jnp.int32, (hb, hb * HEAD_DIM), 1) // HEAD_DIM
    return jnp.where(head == lane_head, 1.0, 0.0).astype(BF16)


def _unit_lower_inverses(a_list, rows, cols, block):
    eye = jnp.where(rows == cols, 1.0, 0.0)
    pairs = (rows >> 1) == (cols >> 1)
    d = [eye - jnp.where(pairs, a, 0.0) for a in a_list]
    b = 2
    while b < block:
        sh = b.bit_length()
        coupling = ((rows >> sh) == (cols >> sh)) & ((rows >> (sh - 1)) != (cols >> (sh - 1)))
        m1 = [_mm_tile(jnp.where(coupling, a, 0.0), di) for a, di in zip(a_list, d)]
        m2 = [_mm_tile(di, mi) for di, mi in zip(d, m1)]
        d = [di - mi for di, mi in zip(d, m2)]
        b *= 2
    return d


def _gdn_pair_products(q, k):
    k_pad = jnp.concatenate([k, jnp.zeros_like(k)], axis=0)
    kq = _bdot_nt(jnp.concatenate([k, q], axis=0), k_pad)
    return kq[:GDN_TILE], kq[GDN_TILE:]


def _gdn_wy(tiles, rep, rows, cols, incl, strict, block):
    heads = [slice(h * HEAD_DIM, (h + 1) * HEAD_DIM) for h in range(len(tiles[0][2]))]
    prods = [[_gdn_pair_products(q, k) for q, k in zip(qs, ks)] for qs, ks, *_ in tiles]
    decays, a_list = [], []
    for (qs, ks, vs, bc, gc, gr), prod in zip(tiles, prods):
        for h, sl in enumerate(heads):
            decay = jnp.where(incl, jnp.exp(gc[:, sl] - gr[h:h + 1, :]), 0.0)
            decays.append(decay)
            a_list.append(jnp.where(strict, bc[:, sl] * prod[h // rep][0] * decay, 0.0))
    t_inv = _unit_lower_inverses(a_list, rows, cols, block)
    xs = []
    for t, (qs, ks, vs, bc, gc, gr) in enumerate(tiles):
        for h, sl in enumerate(heads):
            beta = bc[:, sl]
            rhs = jnp.concatenate([vs[h] * beta, ks[h // rep] * (beta * jnp.exp(gc[:, sl]))], axis=1)
            xs.append(_mm_tile(t_inv[t * len(heads) + h], rhs))
    out = []
    for t in range(len(tiles)):
        out.append([(xs[t * len(heads) + h][:, :HEAD_DIM], xs[t * len(heads) + h][:, HEAD_DIM:],
                     prods[t][h // rep][1] * decays[t * len(heads) + h]) for h in range(len(heads))])
    return out


def _tile_iotas():
    shape = (GDN_TILE, HEAD_DIM)
    return lax.broadcasted_iota(jnp.int32, shape, 0), lax.broadcasted_iota(jnp.int32, shape, 1)


GDN_CHUNKS_PER_PASS = 4


def _causal_conv(x_scr, cw, r0, n, cols):
    y = None
    for s in range(CONV_W):
        lo = SUBLANES + r0 - s
        term = cw[CONV_W - 1 - s:CONV_W - s, cols] * x_scr[lo:lo + n, cols]
        y = term if y is None else y + term
    return y


def _gdn_prompt_kernel(uq_ref, uk_ref, uv_ref, pq_ref, pk_ref, pv_ref, cwq_ref, cwk_ref, cwv_ref,
                       z_ref, bc_ref, gc_ref, gt_ref, gr_ref, ng_ref,
                       o_ref, st_ref, s_scr, u_scr, w_scr, a_scr, xq_scr, xk_scr, xv_scr, q_scr, k_scr,
                       *, hb, rep, scale):
    t = pl.program_id(2)
    c = GDN_CHUNK
    tt = uq_ref.shape[0]
    n_chunks = tt // c

    @pl.when(t == 0)
    def _():
        s_scr[...] = jnp.zeros_like(s_scr)

    keep = jnp.where(t == 0, 0.0, 1.0)
    for x_scr, prev_ref, cur_ref in ((xq_scr, pq_ref, uq_ref), (xk_scr, pk_ref, uk_ref), (xv_scr, pv_ref, uv_ref)):
        x_scr[0:SUBLANES, :] = prev_ref[...] * keep
        x_scr[SUBLANES:SUBLANES + tt, :] = cur_ref[...]

    rows_i, cols_i = _tile_iotas()
    incl = cols_i <= rows_i
    strict = cols_i < rows_i
    ng = ng_ref[...]
    sel = _head_select(hb)
    cwq, cwk, cwv = cwq_ref[...], cwk_ref[...], cwv_ref[...]
    heads = [slice(h * HEAD_DIM, (h + 1) * HEAD_DIM) for h in range(hb)]

    def chunk_operands(ci):
        rows = pl.ds(ci * c, c)
        qs = [_l2_heads(_silu(_causal_conv(xq_scr, cwq, ci * c, c, heads[i])), scale) for i in range(hb // rep)]
        ks = [_l2_heads(_silu(_causal_conv(xk_scr, cwk, ci * c, c, heads[i])), 1.0) for i in range(hb // rep)]
        vs = [_silu(_causal_conv(xv_scr, cwv, ci * c, c, heads[h])) for h in range(hb)]
        for i in range(hb // rep):
            q_scr[rows, heads[i]] = qs[i]
            k_scr[rows, heads[i]] = ks[i]
        return (qs, ks, vs, _lane_bcast_all(bc_ref[rows, :], sel),
                _lane_bcast_all(gc_ref[rows, :], sel), gr_ref[ci])

    per = GDN_CHUNKS_PER_PASS if n_chunks % GDN_CHUNKS_PER_PASS == 0 else 1
    for c0 in range(0, n_chunks, per):
        wy = _gdn_wy([chunk_operands(ci) for ci in range(c0, c0 + per)], rep, rows_i, cols_i, incl, strict, c)
        for j in range(per):
            rows = pl.ds((c0 + j) * c, c)
            for h, (u, w, attn) in enumerate(wy[j]):
                sl = slice(h * HEAD_DIM, (h + 1) * HEAD_DIM)
                u_scr[rows, sl] = u
                w_scr[rows, sl] = w.astype(w_scr.dtype)
                a_scr[rows, sl] = attn.astype(a_scr.dtype)

    pairs = range(hb // rep)
    s_pairs = [s_scr[pr] for pr in pairs]
    outs = []
    for ci in range(n_chunks):
        rows = pl.ds(ci * c, c)
        gc = _lane_bcast_all(gc_ref[rows, :], sel)
        gt = _lane_bcast_all(gt_ref[rows, :], sel)
        qs = [_bdot(q_scr[rows, heads[pr]], s_pairs[pr]) for pr in pairs]
        v_new = [u_scr[rows, sl] - _bdot(w_scr[rows, sl], s_pairs[h // rep][:, heads[h % rep]])
                 for h, sl in enumerate(heads)]
        outs.append([qs[h // rep][:, heads[h % rep]] * jnp.exp(gc[:, sl])
                     + _bdot(a_scr[rows, sl][:, :c], v_new[h]) for h, sl in enumerate(heads)])
        for pr in pairs:
            k_t = _transpose_pad(k_scr[rows, heads[pr]])[:, :c]
            mine = range(pr * rep, (pr + 1) * rep)
            v_scaled = jnp.concatenate([v_new[h] * jnp.exp(gt[:, heads[h]] - gc[:, heads[h]]) for h in mine], axis=1)
            decays = jnp.concatenate([jnp.exp(gt[0:1, heads[h]]) for h in mine], axis=1)
            s_pairs[pr] = decays * s_pairs[pr] + _bdot(k_t, v_scaled)
    for pr in pairs:
        s_scr[pr] = s_pairs[pr]
    for ci in range(n_chunks):
        rows = pl.ds(ci * c, c)
        for h, sl in enumerate(heads):
            o_ref[rows, sl] = _head_out(outs[ci][h], z_ref[rows, sl], ng).astype(o_ref.dtype)

    @pl.when(t == pl.num_programs(2) - 1)
    def _():
        for h in range(hb):
            st_ref[h] = s_scr[h // rep][:, (h % rep) * HEAD_DIM:(h % rep + 1) * HEAD_DIM].astype(st_ref.dtype)


def _gdn_layouts(beta, gam, gtot, nv, hb):
    m = beta.shape[0]
    nhb = nv // hb
    as_cols = lambda a: a.reshape(m, nhb, hb).transpose(1, 0, 2)
    g_rows = gam.reshape(m // GDN_TILE, GDN_TILE, nhb, hb).transpose(2, 0, 3, 1)
    g_rows = jnp.pad(g_rows, ((0, 0), (0, 0), (0, 0), (0, HEAD_DIM - GDN_TILE)))
    return as_cols(beta), as_cols(gam), as_cols(gtot), g_rows


def _gdn_prompt(proj, conv_col0, z_col0, conv_w, gate_arrays, norm_g, batch, seq, qk_heads, v_heads,
                state_dtype):
    rep = v_heads // qk_heads
    hb = _tile(v_heads, 8, rep)
    tt = _tile(seq, 256, GDN_CHUNK)
    nt = seq // tt
    nhb = v_heads // hb
    wq = (hb // rep) * HEAD_DIM
    wv = hb * HEAD_DIM
    qk_w = qk_heads * HEAD_DIM
    bc, gc, gt, gr = _gdn_layouts(*gate_arrays, v_heads, hb)
    assert conv_col0 % wv == 0 and z_col0 % wv == 0 and GDN_CHUNK == GDN_TILE and tt % SUBLANES == 0
    q0 = conv_col0 // wq
    k0 = (conv_col0 + qk_w) // wq
    v0 = (conv_col0 + 2 * qk_w) // wv
    z0 = z_col0 // wv
    kc0 = qk_w // wq
    vc0 = 2 * qk_w // wv
    col_spec = pl.BlockSpec((None, tt, hb), lambda b, h, t: (h, b * nt + t, 0))
    sub = tt // SUBLANES

    def cur(width, c0):
        return pl.BlockSpec((tt, width), lambda b, h, t: (b * nt + t, c0 + h))

    def prev(width, c0):
        return pl.BlockSpec((SUBLANES, width), lambda b, h, t: (jnp.maximum((b * nt + t) * sub - 1, 0), c0 + h))

    def taps(width, c0):
        return pl.BlockSpec((None, CONV_W, width), lambda b, h, t: (0, 0, c0 + h))

    return pl.pallas_call(
        functools.partial(_gdn_prompt_kernel, hb=hb, rep=rep, scale=float(HEAD_DIM) ** -0.5),
        grid=(batch, nhb, nt),
        in_specs=[cur(wq, q0), cur(wq, k0), cur(wv, v0),
                  prev(wq, q0), prev(wq, k0), prev(wv, v0),
                  taps(wq, 0), taps(wq, kc0), taps(wv, vc0),
                  cur(wv, z0),
                  col_spec, col_spec, col_spec,
                  pl.BlockSpec((None, tt // GDN_TILE, hb, HEAD_DIM), lambda b, h, t: (h, b * nt + t, 0, 0)),
                  pl.BlockSpec((1, HEAD_DIM), lambda b, h, t: (0, 0))],
        out_specs=[pl.BlockSpec((tt, wv), lambda b, h, t: (b * nt + t, h)),
                   pl.BlockSpec((None, hb, HEAD_DIM, HEAD_DIM), lambda b, h, t: (b, h, 0, 0))],
        out_shape=[jax.ShapeDtypeStruct((batch * seq, v_heads * HEAD_DIM), BF16),
                   jax.ShapeDtypeStruct((batch, v_heads, HEAD_DIM, HEAD_DIM), state_dtype)],
        scratch_shapes=[pltpu.VMEM((hb // rep, HEAD_DIM, rep * HEAD_DIM), F32),
                        pltpu.VMEM((tt, wv), F32),
                        pltpu.VMEM((tt, wv), BF16),
                        pltpu.VMEM((tt, wv), BF16),
                        pltpu.VMEM((tt + SUBLANES, wq), F32),
                        pltpu.VMEM((tt + SUBLANES, wq), F32),
                        pltpu.VMEM((tt + SUBLANES, wv), F32),
                        pltpu.VMEM((tt, wq), F32),
                        pltpu.VMEM((tt, wq), F32)],
        compiler_params=_cparams("parallel", "parallel", "arbitrary"),
        name="gdn_prompt",
    )(proj, proj, proj, proj, proj, proj, conv_w, conv_w, conv_w, proj, bc, gc, gt, gr, norm_g)


def _gdn_sample_kernel(q_ref, k_ref, v_ref, z_ref, bc_ref, gc_ref, gt_ref, gr_ref, ng_ref, s0_ref,
                       o_ref, st_ref, *, hb, rep, seq):
    c = GDN_TILE
    nseq = c // seq
    per_group = SUBLANES // seq
    lsh = seq.bit_length() - 1
    rows_i, cols_i = _tile_iotas()
    same = (rows_i >> lsh) == (cols_i >> lsh)
    incl = (cols_i <= rows_i) & same
    strict = (cols_i < rows_i) & same
    row8 = lax.broadcasted_iota(jnp.int32, (SUBLANES, 1), 0)
    lane_seq = lax.broadcasted_iota(jnp.int32, (1, HEAD_DIM), 1) >> lsh
    ng = ng_ref[...]
    sel = _head_select(hb)
    bc = _lane_bcast_all(bc_ref[...], sel)
    gc = _lane_bcast_all(gc_ref[...], sel)
    gt = _lane_bcast_all(gt_ref[...], sel)
    gr = gr_ref[0]
    head = lambda ref, i: ref[:, i * HEAD_DIM:(i + 1) * HEAD_DIM]
    qs = [head(q_ref, i) for i in range(hb // rep)]
    ks = [head(k_ref, i) for i in range(hb // rep)]
    wy = _gdn_wy([(qs, ks, [head(v_ref, h) for h in range(hb)], bc, gc, gr)], rep,
                 rows_i, cols_i, incl, strict, seq)[0]
    k_t_seq = []
    for k in ks:
        k_t = _transpose_pad(k)
        k_t_seq.append(jnp.concatenate([jnp.where(lane_seq == b, k_t, 0.0)[:, :c] for b in range(nseq)],
                                       axis=0).astype(BF16))
    qe = [qs[h // rep] * jnp.exp(head(gc, h)) for h in range(hb)]
    ws_parts = [[] for _ in range(hb)]
    qs_parts = [[] for _ in range(hb)]
    for g in range(c // SUBLANES):
        gsl = slice(g * SUBLANES, (g + 1) * SUBLANES)
        for h in range(hb):
            acc = jnp.zeros((2 * SUBLANES, HEAD_DIM), F32)
            for p in range(per_group):
                valid = (row8 >= p * seq) & (row8 < (p + 1) * seq)
                lhs = jnp.concatenate([jnp.where(valid, wy[h][1][gsl], 0.0),
                                       jnp.where(valid, qe[h][gsl], 0.0)], axis=0)
                acc = acc + _bdot(lhs, s0_ref[g * per_group + p, h])
            ws_parts[h].append(acc[:SUBLANES])
            qs_parts[h].append(acc[SUBLANES:])
    v_new = [wy[h][0] - jnp.concatenate(ws_parts[h], axis=0) for h in range(hb)]
    outs = [jnp.concatenate(qs_parts[h], axis=0) + _bdot(wy[h][2][:, :c], v_new[h]) for h in range(hb)]
    upds = [jnp.dot(k_t_seq[h // rep], (v_new[h] * jnp.exp(head(gt, h) - head(gc, h))).astype(BF16),
                    preferred_element_type=F32) for h in range(hb)]
    for h in range(hb):
        o_ref[:, h * HEAD_DIM:(h + 1) * HEAD_DIM] = _head_out(outs[h], head(z_ref, h), ng).astype(o_ref.dtype)
        decay = jnp.exp(head(gt, h))
        for b in range(nseq):
            st_ref[b, h] = (decay[b * seq:b * seq + 1] * s0_ref[b, h].astype(F32)
                            + upds[h][b * HEAD_DIM:(b + 1) * HEAD_DIM]).astype(st_ref.dtype)


def _gdn_sample(qkv, proj, z_col0, gate_arrays, norm_g, s0, batch, seq, qk_heads, v_heads):
    rep = v_heads // qk_heads
    hb = _tile(v_heads, 4, rep)
    assert GDN_TILE % seq == 0 and SUBLANES % seq == 0 and seq & (seq - 1) == 0
    nb = GDN_TILE // seq
    assert batch % nb == 0
    r = GDN_TILE
    nhb = v_heads // hb
    wq = (hb // rep) * HEAD_DIM
    wv = hb * HEAD_DIM
    qk_w = qk_heads * HEAD_DIM
    bc, gc, gt, gr = _gdn_layouts(*gate_arrays, v_heads, hb)
    k0 = qk_w // wq
    v0 = 2 * qk_w // wv
    z0 = z_col0 // wv
    assert z_col0 % wv == 0
    st_spec = pl.BlockSpec((nb, hb, HEAD_DIM, HEAD_DIM), lambda b, h: (b, h, 0, 0))
    col_spec = pl.BlockSpec((None, r, hb), lambda b, h: (h, b, 0))
    return pl.pallas_call(
        functools.partial(_gdn_sample_kernel, hb=hb, rep=rep, seq=seq),
        grid=(batch // nb, nhb),
        in_specs=[pl.BlockSpec((r, wq), lambda b, h: (b, h)),
                  pl.BlockSpec((r, wq), lambda b, h: (b, k0 + h)),
                  pl.BlockSpec((r, wv), lambda b, h: (b, v0 + h)),
                  pl.BlockSpec((r, wv), lambda b, h: (b, z0 + h)),
                  col_spec, col_spec, col_spec,
                  pl.BlockSpec((None, 1, hb, HEAD_DIM), lambda b, h: (h, b, 0, 0)),
                  pl.BlockSpec((1, HEAD_DIM), lambda b, h: (0, 0)),
                  st_spec],
        out_specs=[pl.BlockSpec((r, wv), lambda b, h: (b, h)), st_spec],
        out_shape=[jax.ShapeDtypeStruct((batch * seq, v_heads * HEAD_DIM), BF16),
                   jax.ShapeDtypeStruct(s0.shape, s0.dtype)],
        compiler_params=_cparams("parallel", "parallel"),
        name="gdn_sample",
    )(qkv, qkv, qkv, proj, bc, gc, gt, gr, norm_g, s0)


def _merge_kernel(oh_ref, og_ref, wa_ref, wb_ref, ga_ref, gb_ref, o_ref):
    ya = _bdot(oh_ref[...], wa_ref[...])
    yb = _bdot(og_ref[...], wb_ref[...])
    o_ref[...] = (_sigmoid(ga_ref[...]) * ya + _sigmoid(gb_ref[...]) * yb).astype(o_ref.dtype)


def _merge(o_hg, o_gdn, w_a, w_b, gates, tm):
    m, ka = o_hg.shape
    kb = o_gdn.shape[1]
    d = w_a.shape[-1]
    tn = _tile(d, 256, 128)
    nj = d // tn
    return pl.pallas_call(
        _merge_kernel,
        grid=(m // tm, nj),
        in_specs=[pl.BlockSpec((tm, ka), lambda i, j: (i, 0)),
                  pl.BlockSpec((tm, kb), lambda i, j: (i, 0)),
                  pl.BlockSpec((None, ka, tn), lambda i, j: (0, 0, j)),
                  pl.BlockSpec((None, kb, tn), lambda i, j: (0, 0, j)),
                  pl.BlockSpec((tm, tn), lambda i, j: (i, j)),
                  pl.BlockSpec((tm, tn), lambda i, j: (i, nj + j))],
        out_specs=pl.BlockSpec((tm, tn), lambda i, j: (i, j)),
        out_shape=jax.ShapeDtypeStruct((m, d), BF16),
        compiler_params=_cparams("parallel", "arbitrary"),
        name="merge",
    )(o_hg, o_gdn, w_a, w_b, gates, gates)


def _outproj_kernel(m_ref, w_ref, x_ref, g1_ref, sc_ref, sh_ref, ng_ref, h_ref, a_ref):
    ng = ng_ref[...]
    mix = _bdot(m_ref[...], w_ref[...])
    h = x_ref[...].astype(F32) + g1_ref[...] * (_rms(mix) * ng[0:1])
    h_ref[...] = h
    a_ref[...] = ((_rms(h) * ng[1:2]) * (1.0 + sc_ref[...]) + sh_ref[...]).astype(a_ref.dtype)


def _outproj(merged, w_o_bf16, x, mod, norm_g2, rows_per_seq, tm):
    m, d = x.shape
    kdim = merged.shape[1]
    row = lambda i: (i, 0)
    return pl.pallas_call(
        _outproj_kernel,
        grid=(m // tm,),
        in_specs=[pl.BlockSpec((tm, kdim), row),
                  pl.BlockSpec((kdim, d), lambda i: (0, 0), pipeline_mode=pl.Buffered(1)),
                  pl.BlockSpec((tm, d), row),
                  _mod_spec(mod, 2, tm, rows_per_seq),
                  _mod_spec(mod, 4, tm, rows_per_seq),
                  _mod_spec(mod, 3, tm, rows_per_seq),
                  pl.BlockSpec((2, d), lambda i: (0, 0))],
        out_specs=[pl.BlockSpec((tm, d), row), pl.BlockSpec((tm, d), row)],
        out_shape=[jax.ShapeDtypeStruct((m, d), F32), jax.ShapeDtypeStruct((m, d), BF16)],
        compiler_params=_cparams("parallel"),
        name="out_proj",
    )(merged, w_o_bf16, x, mod, mod, mod, norm_g2)


def _cast_kernel(x_ref, o_ref):
    o_ref[...] = x_ref[...].astype(o_ref.dtype)


def _to_bf16(w):
    kdim, n = w.shape
    tk = _tile(kdim, 512, 16)
    return pl.pallas_call(
        _cast_kernel,
        grid=(kdim // tk,),
        in_specs=[pl.BlockSpec((tk, n), lambda i: (i, 0))],
        out_specs=pl.BlockSpec((tk, n), lambda i: (i, 0)),
        out_shape=jax.ShapeDtypeStruct((kdim, n), BF16),
        compiler_params=_cparams("parallel"),
        name="weight_to_bf16",
    )(w)


def _ffn_up_kernel(a_ref, w_ref, o_ref):
    o_ref[...] = jnp.square(jnp.maximum(_bdot(a_ref[...], w_ref[...]), 0.0)).astype(o_ref.dtype)


def _ffn_up(a, w_up, tm):
    m, d = a.shape
    dff = w_up.shape[-1]
    tf = _tile(dff, 512, 128)
    return pl.pallas_call(
        _ffn_up_kernel,
        grid=(m // tm, dff // tf),
        in_specs=[pl.BlockSpec((tm, d), lambda i, f: (i, 0)),
                  pl.BlockSpec((None, d, tf), lambda i, f: (0, 0, f))],
        out_specs=pl.BlockSpec((tm, tf), lambda i, f: (i, f)),
        out_shape=jax.ShapeDtypeStruct((m, dff), BF16),
        compiler_params=_cparams("parallel", "arbitrary"),
        name="ffn_up",
    )(a, w_up)


EPILOGUE_ROWS = 128


def _ffn_down_kernel(hid_ref, w_ref, h_ref, g2_ref, ng_ref, y_ref):
    j = pl.program_id(1)
    tn = w_ref.shape[1]
    y_ref[:, pl.ds(pl.multiple_of(j * tn, tn), tn)] = _bdot(hid_ref[...], w_ref[...])

    @pl.when(j == pl.num_programs(1) - 1)
    def _():
        step = min(EPILOGUE_ROWS, y_ref.shape[0])

        def body(r, carry):
            rows = pl.ds(pl.multiple_of(r * step, step), step)
            g2 = g2_ref[rows, :] if g2_ref.shape[0] > 1 else g2_ref[...]
            y_ref[rows, :] = h_ref[rows, :] + g2 * (_rms(y_ref[rows, :]) * ng_ref[...])
            return carry

        lax.fori_loop(0, y_ref.shape[0] // step, body, 0)


def _ffn_down(hid, w_down, h, mod, norm_g, rows_per_seq, tm, out_dtype):
    m, d = h.shape
    dff = hid.shape[1]
    tn = _tile(d, 256, 128)
    row = lambda i, j: (i, 0)
    assert out_dtype == F32, "the resident output block holds the f32 pre-norm result"
    return pl.pallas_call(
        _ffn_down_kernel,
        grid=(m // tm, d // tn),
        in_specs=[pl.BlockSpec((tm, dff), row),
                  pl.BlockSpec((dff, tn), lambda i, j: (0, j)),
                  pl.BlockSpec((tm, d), row),
                  _mod_spec(mod, 5, tm, rows_per_seq),
                  pl.BlockSpec((1, d), lambda i, j: (0, 0))],
        out_specs=pl.BlockSpec((tm, d), row),
        out_shape=jax.ShapeDtypeStruct((m, d), out_dtype),
        compiler_params=_cparams("parallel", "arbitrary", vmem_limit=VMEM_LIMIT_FFN_DOWN),
        name="ffn_down",
    )(hid, w_down, h, mod, norm_g)


def _run_group(x3, mod, rows_per_seq_for_mod, conv_buf, s_hg, s_gdn, lb, p, prompt):
    batch, seq, d = x3.shape
    m = batch * seq
    x = x3.reshape(m, d)
    hg_heads, gv_heads = p["hg_heads"], p["gv_heads"]
    gq_heads = p["gq_heads"]
    hg_w = hg_heads * HEAD_DIM
    gq_w = gq_heads * HEAD_DIM
    gv_w = gv_heads * HEAD_DIM
    ch = 2 * gq_w + gv_w
    main = 4 * hg_w + ch + gv_w
    conv0 = 4 * hg_w
    z0 = conv0 + ch
    rows_unit = seq if prompt else m
    tm = _tile(rows_unit, 512, SUBLANES)
    tm_proj = _tile(rows_unit, 2048, SUBLANES)
    tm_wide = _tile(rows_unit, 1024, SUBLANES)

    a1 = _prenorm(x, p["norm_g"][0:1], mod, 1, 0, rows_per_seq_for_mod, tm)
    proj = _project(a1, p["w_in"], main, tm_proj, _tile(main, 512, 128))
    tail = _project(a1, p["w_tail"], p["w_tail"].shape[0], tm_proj, p["w_tail"].shape[0])
    gates = _project(a1, p["w_gates"], 2 * d, tm_proj, _tile(2 * d, 512, 128))

    if prompt:
        o_hg, s_hg_new = _hgrn_prompt(proj, lb, p["hg_norm_g"], batch, seq, hg_heads, p["hg_dtype"])
        gate_arrays = _gdn_gates(tail, p["a_log"], p["dt_bias"], GDN_CHUNK)
        o_gdn, s_gdn_new = _gdn_prompt(proj, conv0, z0, p["conv_w"], gate_arrays, p["gdn_norm_g"], batch, seq,
                                       gq_heads, gv_heads, p["gdn_dtype"])
    else:
        o_hg, s_hg_new = _hgrn_sample(proj, lb, p["hg_norm_g"], s_hg, batch, seq, hg_heads)
        gate_arrays = _gdn_gates(tail, p["a_log"], p["dt_bias"], seq)
        buf_rows = jnp.pad(conv_buf.astype(F32), ((0, 0), (0, seq - (CONV_W - 1)), (0, 0))).reshape(m, ch)
        qkv = _conv_sample(proj, conv0, buf_rows, p["conv_w"], seq, gq_w, ch)
        o_gdn, s_gdn_new = _gdn_sample(qkv, proj, z0, gate_arrays, p["gdn_norm_g"], s_gdn, batch, seq,
                                       gq_heads, gv_heads)

    if seq % SUBLANES == 0:
        conv_new = proj.reshape(batch, seq, main)[:, seq - (CONV_W - 1):, conv0:conv0 + ch]
    else:
        conv_new = proj[:, conv0:conv0 + ch].reshape(batch, seq, ch)[:, seq - (CONV_W - 1):]

    merged = _merge(o_hg, o_gdn, p["w_out_hg"], p["w_out_gdn"], gates, tm_wide)
    h1, a2 = _outproj(merged, p["w_o_bf16"], x, mod, p["norm_g"][1:3], rows_per_seq_for_mod, tm)
    hid = _ffn_up(a2, p["w_up"], tm_proj)
    y = _ffn_down(hid, p["w_down_bf16"], h1, mod, p["norm_g"][3:4], rows_per_seq_for_mod, tm, x3.dtype)
    return y.reshape(batch, seq, d), conv_new, s_hg_new, s_gdn_new


def kernel(x_prompt, x_sample, state_hgrn, state_gdn, cache_conv, c_prompt, c_sample, lb_logits, w_ada,
           b_ada, norm_g, w_in, conv_w, A_log, dt_bias, hg_norm_g, gdn_norm_g, w_out_hg, w_out_gdn, w_o,
           w_up, w_down):
    depth = w_in.shape[0]
    assert depth == 1, "single-layer trunk"
    bp, tp, d = x_prompt.shape
    bs, ts, _ = x_sample.shape
    hg_heads = state_hgrn.shape[2]
    gv_heads = state_gdn.shape[2]
    ch = cache_conv.shape[-1]
    gq_heads = (ch // HEAD_DIM - gv_heads) // 2
    assert tp >= CONV_W - 1 and ts >= CONV_W - 1
    assert tp % GDN_CHUNK == 0 and tp % HG_CHUNK == 0

    main = 4 * hg_heads * HEAD_DIM + ch + gv_heads * HEAD_DIM
    w_in_t = jnp.swapaxes(w_in[0], 0, 1)
    p = dict(
        hg_heads=hg_heads, gv_heads=gv_heads, gq_heads=gq_heads,
        hg_dtype=state_hgrn.dtype, gdn_dtype=state_gdn.dtype,
        norm_g=norm_g[0].astype(F32), w_in=w_in_t,
        w_tail=w_in_t[main:main + 2 * gv_heads],
        w_gates=w_in_t[main + 2 * gv_heads:],
        conv_w=conv_w.astype(F32), a_log=A_log.astype(F32), dt_bias=dt_bias.astype(F32),
        hg_norm_g=hg_norm_g.astype(F32), gdn_norm_g=gdn_norm_g.astype(F32),
        w_out_hg=w_out_hg, w_out_gdn=w_out_gdn, w_o_bf16=_to_bf16(w_o[0]), w_up=w_up,
        w_down_bf16=_to_bf16(w_down[0]),
    )

    lbs = jnp.cumsum(jax.nn.softmax(lb_logits.astype(F32), axis=0), axis=0)
    lb = lbs[0:1]

    n_c = bp + bs
    pad = (-n_c) % SUBLANES
    c_all = jnp.concatenate([c_prompt.astype(F32), c_sample.astype(F32), jnp.zeros((pad, d), F32)], axis=0)
    mod_all = _ada(c_all, w_ada, b_ada.astype(F32))
    mod_p = mod_all[:bp].reshape(bp * 6, 1, d)
    mod_s = jnp.repeat(mod_all[bp:bp + bs].reshape(bs, 6, d), ts, axis=0).transpose(1, 0, 2)

    y_p, conv_p, hg_p, gdn_p = _run_group(x_prompt, mod_p, tp, None, None, None, lb, p, prompt=True)
    y_s, conv_s, hg_s, gdn_s = _run_group(x_sample, mod_s, ts, cache_conv[0], state_hgrn[0], state_gdn[0],
                                          lb, p, prompt=False)
    return (y_p, y_s,
            hg_p[None], gdn_p[None], conv_p.astype(cache_conv.dtype)[None],
            hg_s[None], gdn_s[None], conv_s.astype(cache_conv.dtype)[None])
```

```python
import functools

import jax
import jax.numpy as jnp
from jax import lax
from jax.experimental import pallas as pl
from jax.experimental.pallas import tpu as pltpu

F32 = jnp.float32
BF16 = jnp.bfloat16

HEAD_DIM = 128
HG_CHUNK = 32
GDN_CHUNK = 64
CONV_W = 4
NORM_EPS = 1e-6
SUBLANES = 8
V7X_VMEM_BYTES = 64 * 1024 * 1024
VMEM_LIMIT = V7X_VMEM_BYTES * 3 // 4
VMEM_LIMIT_FFN_DOWN = V7X_VMEM_BYTES * 27 // 32


def _cparams(*sem, vmem_limit=VMEM_LIMIT):
    return pltpu.CompilerParams(dimension_semantics=sem, vmem_limit_bytes=vmem_limit)


def _tile(n, pref, quantum):
    if n <= pref:
        return n
    t = (pref // quantum) * quantum
    while t > quantum and n % t:
        t -= quantum
    assert n % t == 0, (n, pref, quantum)
    return t


def _bdot(a, b):
    return jnp.dot(a.astype(BF16), b.astype(BF16), preferred_element_type=F32)


def _bdot_nt(a, b):
    return lax.dot_general(a.astype(BF16), b.astype(BF16), (((1,), (1,)), ((), ())),
                           preferred_element_type=F32)


def _split3(x):
    hi = x.astype(BF16)
    r = x - hi.astype(F32)
    mid = r.astype(BF16)
    lo = (r - mid.astype(F32)).astype(BF16)
    return hi, mid, lo


def _mask_dot(m_bf, x):
    hi, mid, lo = _split3(x)
    return (jnp.dot(m_bf, hi, preferred_element_type=F32)
            + jnp.dot(m_bf, mid, preferred_element_type=F32)
            + jnp.dot(m_bf, lo, preferred_element_type=F32))


def _sigmoid(x):
    return 1.0 / (1.0 + jnp.exp(-x))


def _silu(x):
    return x * _sigmoid(x)


def _rms(x):
    return x * lax.rsqrt(jnp.mean(x * x, axis=-1, keepdims=True) + NORM_EPS)


def _transpose_pad(x):
    r = x.shape[0]
    if r < HEAD_DIM:
        x = jnp.concatenate([x, jnp.zeros((HEAD_DIM - r, HEAD_DIM), x.dtype)], axis=0)
    return x.T


def _tri_masks(c):
    row = lax.broadcasted_iota(jnp.int32, (c, c), 0)
    col = lax.broadcasted_iota(jnp.int32, (c, c), 1)
    return col <= row, col < row


def _ada_kernel(c_ref, w_ref, b_ref, o_ref):
    cs = _silu(c_ref[...])
    w = w_ref[...]
    c_hi = cs.astype(BF16)
    c_lo = (cs - c_hi.astype(F32)).astype(BF16)
    w_hi = w.astype(BF16)
    w_lo = (w - w_hi.astype(F32)).astype(BF16)
    acc = jnp.dot(c_hi, w_hi, preferred_element_type=F32)
    acc += jnp.dot(c_lo, w_hi, preferred_element_type=F32)
    acc += jnp.dot(c_hi, w_lo, preferred_element_type=F32)
    o_ref[...] = acc + b_ref[...]


def _ada(c_all, w_ada, b_ada):
    m, d = c_all.shape
    n = w_ada.shape[-1]
    tn = _tile(n, 512, 128)
    return pl.pallas_call(
        _ada_kernel,
        grid=(n // tn,),
        in_specs=[pl.BlockSpec((m, d), lambda j: (0, 0)),
                  pl.BlockSpec((None, d, tn), lambda j: (0, 0, j)),
                  pl.BlockSpec((1, tn), lambda j: (0, j))],
        out_specs=pl.BlockSpec((m, tn), lambda j: (0, j)),
        out_shape=jax.ShapeDtypeStruct((m, n), F32),
        compiler_params=_cparams("arbitrary"),
        name="ada_mod",
    )(c_all, w_ada, b_ada)


def _mod_spec(mod, k, tm, rows_per_seq):
    d = mod.shape[-1]
    if mod.shape[1] == 1:
        tiles_per_seq = rows_per_seq // tm
        return pl.BlockSpec((None, 1, d), lambda i, *_: ((i // tiles_per_seq) * 6 + k, 0, 0))
    return pl.BlockSpec((None, tm // rows_per_seq, d), lambda i, *_: (k, i, 0))


def _mod_rows(v, n):
    b = v.shape[0]
    if b == 1 or b == n:
        return v
    rep = n // b
    row = lax.broadcasted_iota(jnp.int32, (n, b), 0)
    col = lax.broadcasted_iota(jnp.int32, (n, b), 1)
    owner = (row >= col * rep) & (row < (col + 1) * rep)
    return _mask_dot(jnp.where(owner, 1.0, 0.0).astype(BF16), v)


def _prenorm_kernel(x_ref, g_ref, sc_ref, sh_ref, o_ref):
    xn = _rms(x_ref[...].astype(F32)) * g_ref[...]
    n = xn.shape[0]
    o_ref[...] = (xn * (1.0 + _mod_rows(sc_ref[...], n)) + _mod_rows(sh_ref[...], n)).astype(o_ref.dtype)


def _prenorm(x, g, mod, k_scale, k_shift, rows_per_seq, tm):
    m, d = x.shape
    return pl.pallas_call(
        _prenorm_kernel,
        grid=(m // tm,),
        in_specs=[pl.BlockSpec((tm, d), lambda i: (i, 0)),
                  pl.BlockSpec((1, d), lambda i: (0, 0)),
                  _mod_spec(mod, k_scale, tm, rows_per_seq),
                  _mod_spec(mod, k_shift, tm, rows_per_seq)],
        out_specs=pl.BlockSpec((tm, d), lambda i: (i, 0)),
        out_shape=jax.ShapeDtypeStruct((m, d), BF16),
        compiler_params=_cparams("parallel"),
        name="prenorm",
    )(x, g, mod, mod)


def _mm_nt_kernel(a_ref, w_ref, o_ref):
    o_ref[...] = _bdot_nt(a_ref[...], w_ref[...])


def _project(a, w_t, n_cols, tm, tn):
    m, kdim = a.shape
    return pl.pallas_call(
        _mm_nt_kernel,
        grid=(m // tm, n_cols // tn),
        in_specs=[pl.BlockSpec((tm, kdim), lambda i, j: (i, 0)),
                  pl.BlockSpec((tn, kdim), lambda i, j: (j, 0))],
        out_specs=pl.BlockSpec((tm, tn), lambda i, j: (i, j)),
        out_shape=jax.ShapeDtypeStruct((m, n_cols), F32),
        compiler_params=_cparams("parallel", "arbitrary"),
        name="in_proj",
    )(a, w_t)


def _hgrn_elementwise(hq, hf, lb):
    q = _silu(hq)
    f = lb + (1.0 - lb) * _sigmoid(hf)
    return q, jnp.log(f), 1.0 - f


def _hgrn_chunk_free(q, g, k, inp, n, tril, tril_bf):
    c = q.shape[0]
    heads = [slice(h * HEAD_DIM, (h + 1) * HEAD_DIM) for h in range(n)]
    big_g = _mask_dot(tril_bf, g)
    g_ref = big_g[c // 2:c // 2 + 1]
    g_last = big_g[c - 1:c]
    qe = q * jnp.exp(big_g - g_ref)
    ke = k * jnp.exp(g_ref - big_g)
    qs = q * jnp.exp(big_g)
    kl = k * jnp.exp(g_last - big_g)
    d_hi, d_mid, d_lo = [p.astype(F32) for p in _split3(jnp.exp(g_last))]
    extra = jnp.concatenate([d_hi, d_mid, d_lo, jnp.zeros((SUBLANES - 3, q.shape[1]), F32)], axis=0)
    a = [jnp.where(tril, _bdot_nt(qe[:, sl], ke[:, sl]), 0.0) for sl in heads]
    o_intra = [_bdot(a[h], inp[:, sl]) for h, sl in enumerate(heads)]
    t = [_transpose_pad(jnp.concatenate([kl[:, sl], extra[:, sl]], axis=0))[:, :c + SUBLANES] for sl in heads]
    piece_rows = lax.broadcasted_iota(jnp.int32, (SUBLANES, HEAD_DIM), 0) < 3
    ones_rows = jnp.where(piece_rows, 1.0, 0.0)
    zeros_c = jnp.zeros((c, HEAD_DIM), F32)
    uds = []
    for h, sl in enumerate(heads):
        rhs = jnp.concatenate([jnp.concatenate([inp[:, sl], zeros_c], axis=1),
                               jnp.concatenate([jnp.zeros((SUBLANES, HEAD_DIM), F32), ones_rows], axis=1)], axis=0)
        uds.append(_bdot(t[h], rhs))
    return o_intra, qs, uds


def _hgrn_chunk_heads(q, g, k, inp, states, tril, tril_bf):
    o_intra, qs, uds = _hgrn_chunk_free(q, g, k, inp, len(states), tril, tril_bf)
    out = []
    for h, s in enumerate(states):
        sl = slice(h * HEAD_DIM, (h + 1) * HEAD_DIM)
        out.append((o_intra[h] + _bdot(qs[:, sl], s), uds[h][:, HEAD_DIM:] * s + uds[h][:, :HEAD_DIM]))
    return out


def _head_out(o, gate, norm_g):
    mean_sq = _lane_sum_bcast(o * o) * (1.0 / HEAD_DIM)
    return o * lax.rsqrt(mean_sq + NORM_EPS) * norm_g * _silu(gate)


HGRN_CHUNKS_PER_PASS = 4


def _hgrn_prompt_kernel(q_ref, f_ref, i_ref, g_ref, lb_ref, ng_ref, o_ref, st_ref,
                        s_scr, oi_scr, qs_scr, ud_scr, *, hb):
    t = pl.program_id(2)
    c = HG_CHUNK
    n_chunks = q_ref.shape[0] // c
    per = HGRN_CHUNKS_PER_PASS if n_chunks % HGRN_CHUNKS_PER_PASS == 0 else 1
    heads = [slice(h * HEAD_DIM, (h + 1) * HEAD_DIM) for h in range(hb)]

    @pl.when(t == 0)
    def _():
        s_scr[...] = jnp.zeros_like(s_scr)

    tril, _ = _tri_masks(c)
    tril_bf = jnp.where(tril, 1.0, 0.0).astype(BF16)
    lb = lb_ref[...]
    ng = ng_ref[...]

    def free_part(it, carry):
        rows = [pl.ds(pl.multiple_of((it * per + j) * c, c), c) for j in range(per)]
        parts = [_hgrn_elementwise(q_ref[r, :], f_ref[r, :], lb) for r in rows]
        cat = lambda xs: jnp.concatenate(xs, axis=1)
        o_intra, qs, uds = _hgrn_chunk_free(cat([p[0] for p in parts]), cat([p[1] for p in parts]),
                                            cat([p[2] for p in parts]), cat([i_ref[r, :] for r in rows]),
                                            per * hb, tril, tril_bf)
        for j, r in enumerate(rows):
            qs_scr[r, :] = qs[:, j * hb * HEAD_DIM:(j + 1) * hb * HEAD_DIM].astype(qs_scr.dtype)
            for h, sl in enumerate(heads):
                oi_scr[r, sl] = o_intra[j * hb + h]
                ud_scr[(it * per + j) * hb + h] = uds[j * hb + h]
        return carry

    lax.fori_loop(0, n_chunks // per, free_part, 0)

    states = [s_scr[h] for h in range(hb)]
    outs = []
    for ci in range(n_chunks):
        rows = pl.ds(ci * c, c)
        outs.append([oi_scr[rows, sl] + _bdot(qs_scr[rows, sl], states[h]) for h, sl in enumerate(heads)])
        for h in range(hb):
            ud = ud_scr[ci * hb + h]
            states[h] = ud[:, HEAD_DIM:] * states[h] + ud[:, :HEAD_DIM]
    for h in range(hb):
        s_scr[h] = states[h]
    for ci in range(n_chunks):
        rows = pl.ds(ci * c, c)
        for h, sl in enumerate(heads):
            o_ref[rows, sl] = _head_out(outs[ci][h], g_ref[rows, sl], ng).astype(o_ref.dtype)

    @pl.when(t == pl.num_programs(2) - 1)
    def _():
        st_ref[...] = s_scr[...].astype(st_ref.dtype)


def _hgrn_prompt(proj, lb, norm_g, batch, seq, heads, state_dtype):
    hb = _tile(heads, 8, 1)
    tt = _tile(seq, 256, HG_CHUNK)
    w = hb * HEAD_DIM
    nhb = heads // hb
    nt = seq // tt

    def col(seg):
        return pl.BlockSpec((tt, w), lambda b, h, t: (b * nt + t, seg * nhb + h))

    return pl.pallas_call(
        functools.partial(_hgrn_prompt_kernel, hb=hb),
        grid=(batch, nhb, nt),
        in_specs=[col(0), col(1), col(2), col(3),
                  pl.BlockSpec((1, w), lambda b, h, t: (0, h)),
                  pl.BlockSpec((1, HEAD_DIM), lambda b, h, t: (0, 0))],
        out_specs=[pl.BlockSpec((tt, w), lambda b, h, t: (b * nt + t, h)),
                   pl.BlockSpec((None, hb, HEAD_DIM, HEAD_DIM), lambda b, h, t: (b, h, 0, 0))],
        out_shape=[jax.ShapeDtypeStruct((batch * seq, heads * HEAD_DIM), BF16),
                   jax.ShapeDtypeStruct((batch, heads, HEAD_DIM, HEAD_DIM), state_dtype)],
        scratch_shapes=[pltpu.VMEM((hb, HEAD_DIM, HEAD_DIM), F32),
                        pltpu.VMEM((tt, w), F32),
                        pltpu.VMEM((tt, w), BF16),
                        pltpu.VMEM(((tt // HG_CHUNK) * hb, HEAD_DIM, 2 * HEAD_DIM), F32)],
        compiler_params=_cparams("parallel", "parallel", "arbitrary"),
        name="hgrn_prompt",
    )(proj, proj, proj, proj, lb, norm_g)


def _hgrn_sample_kernel(q_ref, f_ref, i_ref, g_ref, lb_ref, ng_ref, s0_ref, o_ref, st_ref, *, hb, seq):
    c = SUBLANES
    per_group = c // seq
    tril, _ = _tri_masks(c)
    tril_bf = jnp.where(tril, 1.0, 0.0).astype(BF16)
    lb = lb_ref[...]
    ng = ng_ref[...]
    row = lax.broadcasted_iota(jnp.int32, (c, 1), 0)

    n_groups = q_ref.shape[0] // c
    valid = [(row >= p * seq) & (row < (p + 1) * seq) for p in range(per_group)]
    masked = lambda x: [jnp.where(v, x, 0.0) for v in valid]
    qs, gs, ks, inps, states = [], [], [], [], []
    for gi in range(n_groups):
        rows = pl.ds(gi * c, c)
        q, g, k = _hgrn_elementwise(q_ref[rows, :], f_ref[rows, :], lb)
        qs += masked(q)
        gs += masked(g)
        ks += masked(k)
        inps += [i_ref[rows, :]] * per_group
        states += [s0_ref[gi * per_group + p, h].astype(F32) for p in range(per_group) for h in range(hb)]
    cat = lambda xs: jnp.concatenate(xs, axis=1)
    res = _hgrn_chunk_heads(cat(qs), cat(gs), cat(ks), cat(inps), states, tril, tril_bf)
    for gi in range(n_groups):
        rows = pl.ds(gi * c, c)
        for h in range(hb):
            sl = slice(h * HEAD_DIM, (h + 1) * HEAD_DIM)
            o_all = jnp.zeros((c, HEAD_DIM), F32)
            for p in range(per_group):
                o, s_new = res[(gi * per_group + p) * hb + h]
                st_ref[gi * per_group + p, h] = s_new.astype(st_ref.dtype)
                o_all = jnp.where(valid[p], o, o_all)
            o_ref[rows, sl] = _head_out(o_all, g_ref[rows, sl], ng).astype(o_ref.dtype)


def _hgrn_sample(proj, lb, norm_g, s0, batch, seq, heads):
    assert SUBLANES % seq == 0
    hb = _tile(heads, 4, 1)
    nb = _tile(batch, 8, SUBLANES // seq)
    w = hb * HEAD_DIM
    nhb = heads // hb
    r = nb * seq

    def col(seg):
        return pl.BlockSpec((r, w), lambda b, h: (b, seg * nhb + h))

    st_spec = pl.BlockSpec((nb, hb, HEAD_DIM, HEAD_DIM), lambda b, h: (b, h, 0, 0))
    return pl.pallas_call(
        functools.partial(_hgrn_sample_kernel, hb=hb, seq=seq),
        grid=(batch // nb, nhb),
        in_specs=[col(0), col(1), col(2), col(3),
                  pl.BlockSpec((1, w), lambda b, h: (0, h)),
                  pl.BlockSpec((1, HEAD_DIM), lambda b, h: (0, 0)),
                  st_spec],
        out_specs=[pl.BlockSpec((r, w), lambda b, h: (b, h)), st_spec],
        out_shape=[jax.ShapeDtypeStruct((batch * seq, heads * HEAD_DIM), BF16),
                   jax.ShapeDtypeStruct(s0.shape, s0.dtype)],
        compiler_params=_cparams("parallel", "parallel"),
        name="hgrn_sample",
    )(proj, proj, proj, proj, lb, norm_g, s0)


def _gdn_gates_kernel(x_ref, alog_ref, dtb_ref, beta_ref, gam_ref, gtot_ref, *, nv, chunk):
    x = x_ref[...]
    r = x.shape[0]
    beta_ref[...] = _sigmoid(x[:, :nv])
    z = x[:, nv:] + dtb_ref[...]
    softplus = jnp.maximum(z, 0.0) + jnp.log(1.0 + jnp.exp(-jnp.abs(z)))
    g = -jnp.exp(alog_ref[...]) * softplus
    row = lax.broadcasted_iota(jnp.int32, (r, r), 0)
    col = lax.broadcasted_iota(jnp.int32, (r, r), 1)
    same = (row // chunk) == (col // chunk)
    gam_ref[...] = _mask_dot(jnp.where(same & (col <= row), 1.0, 0.0).astype(BF16), g)
    gtot_ref[...] = _mask_dot(jnp.where(same, 1.0, 0.0).astype(BF16), g)


def _gdn_gates(x, a_log, dt_bias, chunk):
    m, two_nv = x.shape
    nv = two_nv // 2
    tr = _tile(m, 256, max(chunk, SUBLANES))
    out = jax.ShapeDtypeStruct((m, nv), F32)
    return pl.pallas_call(
        functools.partial(_gdn_gates_kernel, nv=nv, chunk=chunk),
        grid=(m // tr,),
        in_specs=[pl.BlockSpec((tr, two_nv), lambda i: (i, 0)),
                  pl.BlockSpec((1, nv), lambda i: (0, 0)),
                  pl.BlockSpec((1, nv), lambda i: (0, 0))],
        out_specs=[pl.BlockSpec((tr, nv), lambda i: (i, 0))] * 3,
        out_shape=[out, out, out],
        compiler_params=_cparams("parallel"),
        name="gdn_gates",
    )(x, a_log, dt_bias)


def _lane_sum_bcast(x):
    hi = x.astype(BF16)
    lo = (x - hi.astype(F32)).astype(BF16)
    return jnp.dot(jnp.concatenate([hi, lo], axis=1), jnp.ones((2 * HEAD_DIM, HEAD_DIM), BF16),
                   preferred_element_type=F32)


def _l2_heads(y, mult):
    parts = []
    for h in range(y.shape[1] // HEAD_DIM):
        yh = y[:, h * HEAD_DIM:(h + 1) * HEAD_DIM]
        parts.append(yh * (lax.rsqrt(_lane_sum_bcast(yh * yh) + NORM_EPS) * mult))
    return jnp.concatenate(parts, axis=1)


def _conv_finish(y, kind_q, kind_k, o_ref, scale):
    y = _silu(y)

    @pl.when(kind_q)
    def _():
        o_ref[...] = _l2_heads(y, scale)

    @pl.when(kind_k)
    def _():
        o_ref[...] = _l2_heads(y, 1.0)

    @pl.when(jnp.logical_not(kind_q | kind_k))
    def _():
        o_ref[...] = y


def _conv_sample_kernel(u_ref, buf_ref, cw_ref, o_ref, *, seq, nq_tiles, scale):
    j = pl.program_id(1)
    u = u_ref[...]
    buf = buf_ref[...]
    r = u.shape[0]
    tok = lax.broadcasted_iota(jnp.int32, (r, 1), 0) % seq
    cw = cw_ref[...]
    y = cw[CONV_W - 1:CONV_W] * u
    for s in range(1, CONV_W):
        from_u = pltpu.roll(u, s, axis=0)
        up = CONV_W - 1 - s
        from_buf = pltpu.roll(buf, r - up, axis=0) if up else buf
        y = y + cw[CONV_W - 1 - s:CONV_W - s] * jnp.where(tok >= s, from_u, from_buf)
    _conv_finish(y, j < nq_tiles, (j >= nq_tiles) & (j < 2 * nq_tiles), o_ref, scale)


def _conv_sample(proj, col0, buf_rows, conv_w, seq, qk_w, ch):
    m = proj.shape[0]
    wc = _tile(qk_w, 512, HEAD_DIM)
    tr = _tile(m, 256, SUBLANES)
    c0 = col0 // wc
    assert col0 % wc == 0 and tr % seq == 0 and seq >= CONV_W - 1
    return pl.pallas_call(
        functools.partial(_conv_sample_kernel, seq=seq, nq_tiles=qk_w // wc,
                          scale=float(HEAD_DIM) ** -0.5),
        grid=(m // tr, ch // wc),
        in_specs=[pl.BlockSpec((tr, wc), lambda i, j: (i, c0 + j)),
                  pl.BlockSpec((tr, wc), lambda i, j: (i, j)),
                  pl.BlockSpec((None, CONV_W, wc), lambda i, j: (0, 0, j))],
        out_specs=pl.BlockSpec((tr, wc), lambda i, j: (i, j)),
        out_shape=jax.ShapeDtypeStruct((m, ch), F32),
        compiler_params=_cparams("parallel", "parallel"),
        name="gdn_conv_sample",
    )(proj, buf_rows, conv_w)


GDN_TILE = 64


def _mm_tile(a, b):
    return _bdot(a[:, :GDN_TILE], b)


def _lane_bcast_all(x, sel):
    return _mask_dot_right(x, sel)


def _mask_dot_right(x, m_bf):
    hi, mid, lo = _split3(x)
    return (jnp.dot(hi, m_bf, preferred_element_type=F32)
            + jnp.dot(mid, m_bf, preferred_element_type=F32)
            + jnp.dot(lo, m_bf, preferred_element_type=F32))


def _head_select(hb):
    head = lax.broadcasted_iota(jnp.int32, (hb, hb * HEAD_DIM), 0)
    lane_head = lax.broadcasted_iota(jnp.int32, (hb, hb * HEAD_DIM), 1) // HEAD_DIM
    return jnp.where(head == lane_head, 1.0, 0.0).astype(BF16)


def _unit_lower_inverses(a_list, rows, cols, block):
    eye = jnp.where(rows == cols, 1.0, 0.0)
    pairs = (rows >> 1) == (cols >> 1)
    d = [eye - jnp.where(pairs, a, 0.0) for a in a_list]
    b = 2
    while b < block:
        sh = b.bit_length()
        coupling = ((rows >> sh) == (cols >> sh)) & ((rows >> (sh - 1)) != (cols >> (sh - 1)))
        m1 = [_mm_tile(jnp.where(coupling, a, 0.0), di) for a, di in zip(a_list, d)]
        m2 = [_mm_tile(di, mi) for di, mi in zip(d, m1)]
        d = [di - mi for di, mi in zip(d, m2)]
        b *= 2
    return d


def _gdn_pair_products(q, k):
    k_pad = jnp.concatenate([k, jnp.zeros_like(k)], axis=0)
    kq = _bdot_nt(jnp.concatenate([k, q], axis=0), k_pad)
    return kq[:GDN_TILE], kq[GDN_TILE:]


def _gdn_wy(tiles, rep, rows, cols, incl, strict, block):
    heads = [slice(h * HEAD_DIM, (h + 1) * HEAD_DIM) for h in range(len(tiles[0][2]))]
    prods = [[_gdn_pair_products(q, k) for q, k in zip(qs, ks)] for qs, ks, *_ in tiles]
    decays, a_list = [], []
    for (qs, ks, vs, bc, gc, gr), prod in zip(tiles, prods):
        for h, sl in enumerate(heads):
            decay = jnp.where(incl, jnp.exp(gc[:, sl] - gr[h:h + 1, :]), 0.0)
            decays.append(decay)
            a_list.append(jnp.where(strict, bc[:, sl] * prod[h // rep][0] * decay, 0.0))
    t_inv = _unit_lower_inverses(a_list, rows, cols, block)
    xs = []
    for t, (qs, ks, vs, bc, gc, gr) in enumerate(tiles):
        for h, sl in enumerate(heads):
            beta = bc[:, sl]
            rhs = jnp.concatenate([vs[h] * beta, ks[h // rep] * (beta * jnp.exp(gc[:, sl]))], axis=1)
            xs.append(_mm_tile(t_inv[t * len(heads) + h], rhs))
    out = []
    for t in range(len(tiles)):
        out.append([(xs[t * len(heads) + h][:, :HEAD_DIM], xs[t * len(heads) + h][:, HEAD_DIM:],
                     prods[t][h // rep][1] * decays[t * len(heads) + h]) for h in range(len(heads))])
    return out


def _tile_iotas():
    shape = (GDN_TILE, HEAD_DIM)
    return lax.broadcasted_iota(jnp.int32, shape, 0), lax.broadcasted_iota(jnp.int32, shape, 1)


GDN_CHUNKS_PER_PASS = 4


def _causal_conv(x_scr, cw, r0, n, cols):
    y = None
    for s in range(CONV_W):
        lo = SUBLANES + r0 - s
        term = cw[CONV_W - 1 - s:CONV_W - s, cols] * x_scr[lo:lo + n, cols]
        y = term if y is None else y + term
    return y


def _gdn_prompt_kernel(uq_ref, uk_ref, uv_ref, pq_ref, pk_ref, pv_ref, cwq_ref, cwk_ref, cwv_ref,
                       z_ref, bc_ref, gc_ref, gt_ref, gr_ref, ng_ref,
                       o_ref, st_ref, s_scr, u_scr, w_scr, a_scr, xq_scr, xk_scr, xv_scr, q_scr, k_scr,
                       *, hb, rep, scale):
    t = pl.program_id(2)
    c = GDN_CHUNK
    tt = uq_ref.shape[0]
    n_chunks = tt // c

    @pl.when(t == 0)
    def _():
        s_scr[...] = jnp.zeros_like(s_scr)

    keep = jnp.where(t == 0, 0.0, 1.0)
    for x_scr, prev_ref, cur_ref in ((xq_scr, pq_ref, uq_ref), (xk_scr, pk_ref, uk_ref), (xv_scr, pv_ref, uv_ref)):
        x_scr[0:SUBLANES, :] = prev_ref[...] * keep
        x_scr[SUBLANES:SUBLANES + tt, :] = cur_ref[...]

    rows_i, cols_i = _tile_iotas()
    incl = cols_i <= rows_i
    strict = cols_i < rows_i
    ng = ng_ref[...]
    sel = _head_select(hb)
    cwq, cwk, cwv = cwq_ref[...], cwk_ref[...], cwv_ref[...]
    heads = [slice(h * HEAD_DIM, (h + 1) * HEAD_DIM) for h in range(hb)]

    def chunk_operands(ci):
        rows = pl.ds(ci * c, c)
        qs = [_l2_heads(_silu(_causal_conv(xq_scr, cwq, ci * c, c, heads[i])), scale) for i in range(hb // rep)]
        ks = [_l2_heads(_silu(_causal_conv(xk_scr, cwk, ci * c, c, heads[i])), 1.0) for i in range(hb // rep)]
        vs = [_silu(_causal_conv(xv_scr, cwv, ci * c, c, heads[h])) for h in range(hb)]
        for i in range(hb // rep):
            q_scr[rows, heads[i]] = qs[i]
            k_scr[rows, heads[i]] = ks[i]
        return (qs, ks, vs, _lane_bcast_all(bc_ref[rows, :], sel),
                _lane_bcast_all(gc_ref[rows, :], sel), gr_ref[ci])

    per = GDN_CHUNKS_PER_PASS if n_chunks % GDN_CHUNKS_PER_PASS == 0 else 1
    for c0 in range(0, n_chunks, per):
        wy = _gdn_wy([chunk_operands(ci) for ci in range(c0, c0 + per)], rep, rows_i, cols_i, incl, strict, c)
        for j in range(per):
            rows = pl.ds((c0 + j) * c, c)
            for h, (u, w, attn) in enumerate(wy[j]):
                sl = slice(h * HEAD_DIM, (h + 1) * HEAD_DIM)
                u_scr[rows, sl] = u
                w_scr[rows, sl] = w.astype(w_scr.dtype)
                a_scr[rows, sl] = attn.astype(a_scr.dtype)

    pairs = range(hb // rep)
    s_pairs = [s_scr[pr] for pr in pairs]
    outs = []
    for ci in range(n_chunks):
        rows = pl.ds(ci * c, c)
        gc = _lane_bcast_all(gc_ref[rows, :], sel)
        gt = _lane_bcast_all(gt_ref[rows, :], sel)
        qs = [_bdot(q_scr[rows, heads[pr]], s_pairs[pr]) for pr in pairs]
        v_new = [u_scr[rows, sl] - _bdot(w_scr[rows, sl], s_pairs[h // rep][:, heads[h % rep]])
                 for h, sl in enumerate(heads)]
        outs.append([qs[h // rep][:, heads[h % rep]] * jnp.exp(gc[:, sl])
                     + _bdot(a_scr[rows, sl][:, :c], v_new[h]) for h, sl in enumerate(heads)])
        for pr in pairs:
            k_t = _transpose_pad(k_scr[rows, heads[pr]])[:, :c]
            mine = range(pr * rep, (pr + 1) * rep)
            v_scaled = jnp.concatenate([v_new[h] * jnp.exp(gt[:, heads[h]] - gc[:, heads[h]]) for h in mine], axis=1)
            decays = jnp.concatenate([jnp.exp(gt[0:1, heads[h]]) for h in mine], axis=1)
            s_pairs[pr] = decays * s_pairs[pr] + _bdot(k_t, v_scaled)
    for pr in pairs:
        s_scr[pr] = s_pairs[pr]
    for ci in range(n_chunks):
        rows = pl.ds(ci * c, c)
        for h, sl in enumerate(heads):
            o_ref[rows, sl] = _head_out(outs[ci][h], z_ref[rows, sl], ng).astype(o_ref.dtype)

    @pl.when(t == pl.num_programs(2) - 1)
    def _():
        for h in range(hb):
            st_ref[h] = s_scr[h // rep][:, (h % rep) * HEAD_DIM:(h % rep + 1) * HEAD_DIM].astype(st_ref.dtype)


def _gdn_layouts(beta, gam, gtot, nv, hb):
    m = beta.shape[0]
    nhb = nv // hb
    as_cols = lambda a: a.reshape(m, nhb, hb).transpose(1, 0, 2)
    g_rows = gam.reshape(m // GDN_TILE, GDN_TILE, nhb, hb).transpose(2, 0, 3, 1)
    g_rows = jnp.pad(g_rows, ((0, 0), (0, 0), (0, 0), (0, HEAD_DIM - GDN_TILE)))
    return as_cols(beta), as_cols(gam), as_cols(gtot), g_rows


def _gdn_prompt(proj, conv_col0, z_col0, conv_w, gate_arrays, norm_g, batch, seq, qk_heads, v_heads,
                state_dtype):
    rep = v_heads // qk_heads
    hb = _tile(v_heads, 8, rep)
    tt = _tile(seq, 256, GDN_CHUNK)
    nt = seq // tt
    nhb = v_heads // hb
    wq = (hb // rep) * HEAD_DIM
    wv = hb * HEAD_DIM
    qk_w = qk_heads * HEAD_DIM
    bc, gc, gt, gr = _gdn_layouts(*gate_arrays, v_heads, hb)
    assert conv_col0 % wv == 0 and z_col0 % wv == 0 and GDN_CHUNK == GDN_TILE and tt % SUBLANES == 0
    q0 = conv_col0 // wq
    k0 = (conv_col0 + qk_w) // wq
    v0 = (conv_col0 + 2 * qk_w) // wv
    z0 = z_col0 // wv
    kc0 = qk_w // wq
    vc0 = 2 * qk_w // wv
    col_spec = pl.BlockSpec((None, tt, hb), lambda b, h, t: (h, b * nt + t, 0))
    sub = tt // SUBLANES

    def cur(width, c0):
        return pl.BlockSpec((tt, width), lambda b, h, t: (b * nt + t, c0 + h))

    def prev(width, c0):
        return pl.BlockSpec((SUBLANES, width), lambda b, h, t: (jnp.maximum((b * nt + t) * sub - 1, 0), c0 + h))

    def taps(width, c0):
        return pl.BlockSpec((None, CONV_W, width), lambda b, h, t: (0, 0, c0 + h))

    return pl.pallas_call(
        functools.partial(_gdn_prompt_kernel, hb=hb, rep=rep, scale=float(HEAD_DIM) ** -0.5),
        grid=(batch, nhb, nt),
        in_specs=[cur(wq, q0), cur(wq, k0), cur(wv, v0),
                  prev(wq, q0), prev(wq, k0), prev(wv, v0),
                  taps(wq, 0), taps(wq, kc0), taps(wv, vc0),
                  cur(wv, z0),
                  col_spec, col_spec, col_spec,
                  pl.BlockSpec((None, tt // GDN_TILE, hb, HEAD_DIM), lambda b, h, t: (h, b * nt + t, 0, 0)),
                  pl.BlockSpec((1, HEAD_DIM), lambda b, h, t: (0, 0))],
        out_specs=[pl.BlockSpec((tt, wv), lambda b, h, t: (b * nt + t, h)),
                   pl.BlockSpec((None, hb, HEAD_DIM, HEAD_DIM), lambda b, h, t: (b, h, 0, 0))],
        out_shape=[jax.ShapeDtypeStruct((batch * seq, v_heads * HEAD_DIM), BF16),
                   jax.ShapeDtypeStruct((batch, v_heads, HEAD_DIM, HEAD_DIM), state_dtype)],
        scratch_shapes=[pltpu.VMEM((hb // rep, HEAD_DIM, rep * HEAD_DIM), F32),
                        pltpu.VMEM((tt, wv), F32),
                        pltpu.VMEM((tt, wv), BF16),
                        pltpu.VMEM((tt, wv), BF16),
                        pltpu.VMEM((tt + SUBLANES, wq), F32),
                        pltpu.VMEM((tt + SUBLANES, wq), F32),
                        pltpu.VMEM((tt + SUBLANES, wv), F32),
                        pltpu.VMEM((tt, wq), F32),
                        pltpu.VMEM((tt, wq), F32)],
        compiler_params=_cparams("parallel", "parallel", "arbitrary"),
        name="gdn_prompt",
    )(proj, proj, proj, proj, proj, proj, conv_w, conv_w, conv_w, proj, bc, gc, gt, gr, norm_g)


def _gdn_sample_kernel(q_ref, k_ref, v_ref, z_ref, bc_ref, gc_ref, gt_ref, gr_ref, ng_ref, s0_ref,
                       o_ref, st_ref, *, hb, rep, seq):
    c = GDN_TILE
    nseq = c // seq
    per_group = SUBLANES // seq
    lsh = seq.bit_length() - 1
    rows_i, cols_i = _tile_iotas()
    same = (rows_i >> lsh) == (cols_i >> lsh)
    incl = (cols_i <= rows_i) & same
    strict = (cols_i < rows_i) & same
    row8 = lax.broadcasted_iota(jnp.int32, (SUBLANES, 1), 0)
    lane_seq = lax.broadcasted_iota(jnp.int32, (1, HEAD_DIM), 1) >> lsh
    ng = ng_ref[...]
    sel = _head_select(hb)
    bc = _lane_bcast_all(bc_ref[...], sel)
    gc = _lane_bcast_all(gc_ref[...], sel)
    gt = _lane_bcast_all(gt_ref[...], sel)
    gr = gr_ref[0]
    head = lambda ref, i: ref[:, i * HEAD_DIM:(i + 1) * HEAD_DIM]
    qs = [head(q_ref, i) for i in range(hb // rep)]
    ks = [head(k_ref, i) for i in range(hb // rep)]
    wy = _gdn_wy([(qs, ks, [head(v_ref, h) for h in range(hb)], bc, gc, gr)], rep,
                 rows_i, cols_i, incl, strict, seq)[0]
    k_t_seq = []
    for k in ks:
        k_t = _transpose_pad(k)
        k_t_seq.append(jnp.concatenate([jnp.where(lane_seq == b, k_t, 0.0)[:, :c] for b in range(nseq)],
                                       axis=0).astype(BF16))
    qe = [qs[h // rep] * jnp.exp(head(gc, h)) for h in range(hb)]
    ws_parts = [[] for _ in range(hb)]
    qs_parts = [[] for _ in range(hb)]
    for g in range(c // SUBLANES):
        gsl = slice(g * SUBLANES, (g + 1) * SUBLANES)
        for h in range(hb):
            acc = jnp.zeros((2 * SUBLANES, HEAD_DIM), F32)
            for p in range(per_group):
                valid = (row8 >= p * seq) & (row8 < (p + 1) * seq)
                lhs = jnp.concatenate([jnp.where(valid, wy[h][1][gsl], 0.0),
                                       jnp.where(valid, qe[h][gsl], 0.0)], axis=0)
                acc = acc + _bdot(lhs, s0_ref[g * per_group + p, h])
            ws_parts[h].append(acc[:SUBLANES])
            qs_parts[h].append(acc[SUBLANES:])
    v_new = [wy[h][0] - jnp.concatenate(ws_parts[h], axis=0) for h in range(hb)]
    outs = [jnp.concatenate(qs_parts[h], axis=0) + _bdot(wy[h][2][:, :c], v_new[h]) for h in range(hb)]
    upds = [jnp.dot(k_t_seq[h // rep], (v_new[h] * jnp.exp(head(gt, h) - head(gc, h))).astype(BF16),
                    preferred_element_type=F32) for h in range(hb)]
    for h in range(hb):
        o_ref[:, h * HEAD_DIM:(h + 1) * HEAD_DIM] = _head_out(outs[h], head(z_ref, h), ng).astype(o_ref.dtype)
        decay = jnp.exp(head(gt, h))
        for b in range(nseq):
            st_ref[b, h] = (decay[b * seq:b * seq + 1] * s0_ref[b, h].astype(F32)
                            + upds[h][b * HEAD_DIM:(b + 1) * HEAD_DIM]).astype(st_ref.dtype)


def _gdn_sample(qkv, proj, z_col0, gate_arrays, norm_g, s0, batch, seq, qk_heads, v_heads):
    rep = v_heads // qk_heads
    hb = _tile(v_heads, 4, rep)
    assert GDN_TILE % seq == 0 and SUBLANES % seq == 0 and seq & (seq - 1) == 0
    nb = GDN_TILE // seq
    assert batch % nb == 0
    r = GDN_TILE
    nhb = v_heads // hb
    wq = (hb // rep) * HEAD_DIM
    wv = hb * HEAD_DIM
    qk_w = qk_heads * HEAD_DIM
    bc, gc, gt, gr = _gdn_layouts(*gate_arrays, v_heads, hb)
    k0 = qk_w // wq
    v0 = 2 * qk_w // wv
    z0 = z_col0 // wv
    assert z_col0 % wv == 0
    st_spec = pl.BlockSpec((nb, hb, HEAD_DIM, HEAD_DIM), lambda b, h: (b, h, 0, 0))
    col_spec = pl.BlockSpec((None, r, hb), lambda b, h: (h, b, 0))
    return pl.pallas_call(
        functools.partial(_gdn_sample_kernel, hb=hb, rep=rep, seq=seq),
        grid=(batch // nb, nhb),
        in_specs=[pl.BlockSpec((r, wq), lambda b, h: (b, h)),
                  pl.BlockSpec((r, wq), lambda b, h: (b, k0 + h)),
                  pl.BlockSpec((r, wv), lambda b, h: (b, v0 + h)),
                  pl.BlockSpec((r, wv), lambda b, h: (b, z0 + h)),
                  col_spec, col_spec, col_spec,
                  pl.BlockSpec((None, 1, hb, HEAD_DIM), lambda b, h: (h, b, 0, 0)),
                  pl.BlockSpec((1, HEAD_DIM), lambda b, h: (0, 0)),
                  st_spec],
        out_specs=[pl.BlockSpec((r, wv), lambda b, h: (b, h)), st_spec],
        out_shape=[jax.ShapeDtypeStruct((batch * seq, v_heads * HEAD_DIM), BF16),
                   jax.ShapeDtypeStruct(s0.shape, s0.dtype)],
        compiler_params=_cparams("parallel", "parallel"),
        name="gdn_sample",
    )(qkv, qkv, qkv, proj, bc, gc, gt, gr, norm_g, s0)


def _merge_kernel(oh_ref, og_ref, wa_ref, wb_ref, ga_ref, gb_ref, o_ref):
    ya = _bdot(oh_ref[...], wa_ref[...])
    yb = _bdot(og_ref[...], wb_ref[...])
    o_ref[...] = (_sigmoid(ga_ref[...]) * ya + _sigmoid(gb_ref[...]) * yb).astype(o_ref.dtype)


def _merge(o_hg, o_gdn, w_a, w_b, gates, tm):
    m, ka = o_hg.shape
    kb = o_gdn.shape[1]
    d = w_a.shape[-1]
    tn = _tile(d, 256, 128)
    nj = d // tn
    return pl.pallas_call(
        _merge_kernel,
        grid=(m // tm, nj),
        in_specs=[pl.BlockSpec((tm, ka), lambda i, j: (i, 0)),
                  pl.BlockSpec((tm, kb), lambda i, j: (i, 0)),
                  pl.BlockSpec((None, ka, tn), lambda i, j: (0, 0, j)),
                  pl.BlockSpec((None, kb, tn), lambda i, j: (0, 0, j)),
                  pl.BlockSpec((tm, tn), lambda i, j: (i, j)),
                  pl.BlockSpec((tm, tn), lambda i, j: (i, nj + j))],
        out_specs=pl.BlockSpec((tm, tn), lambda i, j: (i, j)),
        out_shape=jax.ShapeDtypeStruct((m, d), BF16),
        compiler_params=_cparams("parallel", "arbitrary"),
        name="merge",
    )(o_hg, o_gdn, w_a, w_b, gates, gates)


def _outproj_kernel(m_ref, w_ref, x_ref, g1_ref, sc_ref, sh_ref, ng_ref, h_ref, a_ref):
    ng = ng_ref[...]
    mix = _bdot(m_ref[...], w_ref[...])
    n = mix.shape[0]
    h = x_ref[...].astype(F32) + _mod_rows(g1_ref[...], n) * (_rms(mix) * ng[0:1])
    h_ref[...] = h
    a_ref[...] = ((_rms(h) * ng[1:2]) * (1.0 + _mod_rows(sc_ref[...], n))
                  + _mod_rows(sh_ref[...], n)).astype(a_ref.dtype)


def _outproj(merged, w_o_bf16, x, mod, norm_g2, rows_per_seq, tm):
    m, d = x.shape
    kdim = merged.shape[1]
    row = lambda i: (i, 0)
    return pl.pallas_call(
        _outproj_kernel,
        grid=(m // tm,),
        in_specs=[pl.BlockSpec((tm, kdim), row),
                  pl.BlockSpec((kdim, d), lambda i: (0, 0), pipeline_mode=pl.Buffered(1)),
                  pl.BlockSpec((tm, d), row),
                  _mod_spec(mod, 2, tm, rows_per_seq),
                  _mod_spec(mod, 4, tm, rows_per_seq),
                  _mod_spec(mod, 3, tm, rows_per_seq),
                  pl.BlockSpec((2, d), lambda i: (0, 0))],
        out_specs=[pl.BlockSpec((tm, d), row), pl.BlockSpec((tm, d), row)],
        out_shape=[jax.ShapeDtypeStruct((m, d), F32), jax.ShapeDtypeStruct((m, d), BF16)],
        compiler_params=_cparams("parallel"),
        name="out_proj",
    )(merged, w_o_bf16, x, mod, mod, mod, norm_g2)


def _cast_kernel(x_ref, o_ref):
    o_ref[...] = x_ref[...].astype(o_ref.dtype)


def _to_bf16(w):
    kdim, n = w.shape
    tk = _tile(kdim, 512, 16)
    return pl.pallas_call(
        _cast_kernel,
        grid=(kdim // tk,),
        in_specs=[pl.BlockSpec((tk, n), lambda i: (i, 0))],
        out_specs=pl.BlockSpec((tk, n), lambda i: (i, 0)),
        out_shape=jax.ShapeDtypeStruct((kdim, n), BF16),
        compiler_params=_cparams("parallel"),
        name="weight_to_bf16",
    )(w)


def _ffn_up_kernel(a_ref, w_ref, o_ref):
    o_ref[...] = jnp.square(jnp.maximum(_bdot(a_ref[...], w_ref[...]), 0.0)).astype(o_ref.dtype)


def _ffn_up(a, w_up, tm):
    m, d = a.shape
    dff = w_up.shape[-1]
    tf = _tile(dff, 512, 128)
    return pl.pallas_call(
        _ffn_up_kernel,
        grid=(m // tm, dff // tf),
        in_specs=[pl.BlockSpec((tm, d), lambda i, f: (i, 0)),
                  pl.BlockSpec((None, d, tf), lambda i, f: (0, 0, f))],
        out_specs=pl.BlockSpec((tm, tf), lambda i, f: (i, f)),
        out_shape=jax.ShapeDtypeStruct((m, dff), BF16),
        compiler_params=_cparams("parallel", "arbitrary"),
        name="ffn_up",
    )(a, w_up)


EPILOGUE_ROWS = 128


def _ffn_down_kernel(hid_ref, w_ref, h_ref, g2_ref, ng_ref, y_ref):
    j = pl.program_id(1)
    tn = w_ref.shape[1]
    y_ref[:, pl.ds(pl.multiple_of(j * tn, tn), tn)] = _bdot(hid_ref[...], w_ref[...])

    @pl.when(j == pl.num_programs(1) - 1)
    def _():
        tm = y_ref.shape[0]
        step = min(EPILOGUE_ROWS, tm)
        g_rows = g2_ref.shape[0]
        g_step = max(g_rows * step // tm, 1)

        def body(r, carry):
            rows = pl.ds(pl.multiple_of(r * step, step), step)
            g2 = g2_ref[...] if g_rows == 1 else _mod_rows(
                g2_ref[pl.ds(pl.multiple_of(r * g_step, g_step), g_step), :], step)
            y_ref[rows, :] = h_ref[rows, :] + g2 * (_rms(y_ref[rows, :]) * ng_ref[...])
            return carry

        lax.fori_loop(0, y_ref.shape[0] // step, body, 0)


def _ffn_down(hid, w_down, h, mod, norm_g, rows_per_seq, tm, out_dtype):
    m, d = h.shape
    dff = hid.shape[1]
    tn = _tile(d, 256, 128)
    row = lambda i, j: (i, 0)
    assert out_dtype == F32, "the resident output block holds the f32 pre-norm result"
    return pl.pallas_call(
        _ffn_down_kernel,
        grid=(m // tm, d // tn),
        in_specs=[pl.BlockSpec((tm, dff), row),
                  pl.BlockSpec((dff, tn), lambda i, j: (0, j)),
                  pl.BlockSpec((tm, d), row),
                  _mod_spec(mod, 5, tm, rows_per_seq),
                  pl.BlockSpec((1, d), lambda i, j: (0, 0))],
        out_specs=pl.BlockSpec((tm, d), row),
        out_shape=jax.ShapeDtypeStruct((m, d), out_dtype),
        compiler_params=_cparams("parallel", "arbitrary", vmem_limit=VMEM_LIMIT_FFN_DOWN),
        name="ffn_down",
    )(hid, w_down, h, mod, norm_g)


def _run_group(x3, mod, rows_per_seq_for_mod, conv_buf, s_hg, s_gdn, lb, p, prompt):
    batch, seq, d = x3.shape
    m = batch * seq
    x = x3.reshape(m, d)
    hg_heads, gv_heads = p["hg_heads"], p["gv_heads"]
    gq_heads = p["gq_heads"]
    hg_w = hg_heads * HEAD_DIM
    gq_w = gq_heads * HEAD_DIM
    gv_w = gv_heads * HEAD_DIM
    ch = 2 * gq_w + gv_w
    main = 4 * hg_w + ch + gv_w
    conv0 = 4 * hg_w
    z0 = conv0 + ch
    rows_unit = seq if prompt else m
    tm = _tile(rows_unit, 512, SUBLANES)
    tm_proj = _tile(rows_unit, 2048, SUBLANES)
    tm_wide = _tile(rows_unit, 1024, SUBLANES)

    a1 = _prenorm(x, p["norm_g"][0:1], mod, 1, 0, rows_per_seq_for_mod, tm)
    tn_pref = 512 if tm_proj > 512 else 1024
    proj = _project(a1, p["w_in"], main, tm_proj, _tile(main, tn_pref, 128))
    tail = _project(a1, p["w_tail"], p["w_tail"].shape[0], tm_proj, p["w_tail"].shape[0])
    gates = _project(a1, p["w_gates"], 2 * d, tm_proj, _tile(2 * d, tn_pref, 128))

    if prompt:
        o_hg, s_hg_new = _hgrn_prompt(proj, lb, p["hg_norm_g"], batch, seq, hg_heads, p["hg_dtype"])
        gate_arrays = _gdn_gates(tail, p["a_log"], p["dt_bias"], GDN_CHUNK)
        o_gdn, s_gdn_new = _gdn_prompt(proj, conv0, z0, p["conv_w"], gate_arrays, p["gdn_norm_g"], batch, seq,
                                       gq_heads, gv_heads, p["gdn_dtype"])
    else:
        o_hg, s_hg_new = _hgrn_sample(proj, lb, p["hg_norm_g"], s_hg, batch, seq, hg_heads)
        gate_arrays = _gdn_gates(tail, p["a_log"], p["dt_bias"], seq)
        buf_rows = jnp.pad(conv_buf.astype(F32), ((0, 0), (0, seq - (CONV_W - 1)), (0, 0))).reshape(m, ch)
        qkv = _conv_sample(proj, conv0, buf_rows, p["conv_w"], seq, gq_w, ch)
        o_gdn, s_gdn_new = _gdn_sample(qkv, proj, z0, gate_arrays, p["gdn_norm_g"], s_gdn, batch, seq,
                                       gq_heads, gv_heads)

    if seq % SUBLANES == 0:
        conv_new = proj.reshape(batch, seq, main)[:, seq - (CONV_W - 1):, conv0:conv0 + ch]
    else:
        conv_new = proj[:, conv0:conv0 + ch].reshape(batch, seq, ch)[:, seq - (CONV_W - 1):]

    merged = _merge(o_hg, o_gdn, p["w_out_hg"], p["w_out_gdn"], gates, tm_wide)
    h1, a2 = _outproj(merged, p["w_o_bf16"], x, mod, p["norm_g"][1:3], rows_per_seq_for_mod, tm)
    hid = _ffn_up(a2, p["w_up"], tm_proj)
    y = _ffn_down(hid, p["w_down_bf16"], h1, mod, p["norm_g"][3:4], rows_per_seq_for_mod, tm, x3.dtype)
    return y.reshape(batch, seq, d), conv_new, s_hg_new, s_gdn_new


def kernel(x_prompt, x_sample, state_hgrn, state_gdn, cache_conv, c_prompt, c_sample, lb_logits, w_ada,
           b_ada, norm_g, w_in, conv_w, A_log, dt_bias, hg_norm_g, gdn_norm_g, w_out_hg, w_out_gdn, w_o,
           w_up, w_down):
    depth = w_in.shape[0]
    assert depth == 1, "single-layer trunk"
    bp, tp, d = x_prompt.shape
    bs, ts, _ = x_sample.shape
    hg_heads = state_hgrn.shape[2]
    gv_heads = state_gdn.shape[2]
    ch = cache_conv.shape[-1]
    gq_heads = (ch // HEAD_DIM - gv_heads) // 2
    assert tp >= CONV_W - 1 and ts >= CONV_W - 1
    assert tp % GDN_CHUNK == 0 and tp % HG_CHUNK == 0

    main = 4 * hg_heads * HEAD_DIM + ch + gv_heads * HEAD_DIM
    w_in_t = jnp.swapaxes(w_in[0], 0, 1)
    p = dict(
        hg_heads=hg_heads, gv_heads=gv_heads, gq_heads=gq_heads,
        hg_dtype=state_hgrn.dtype, gdn_dtype=state_gdn.dtype,
        norm_g=norm_g[0].astype(F32), w_in=w_in_t,
        w_tail=w_in_t[main:main + 2 * gv_heads],
        w_gates=w_in_t[main + 2 * gv_heads:],
        conv_w=conv_w.astype(F32), a_log=A_log.astype(F32), dt_bias=dt_bias.astype(F32),
        hg_norm_g=hg_norm_g.astype(F32), gdn_norm_g=gdn_norm_g.astype(F32),
        w_out_hg=w_out_hg, w_out_gdn=w_out_gdn, w_o_bf16=_to_bf16(w_o[0]), w_up=w_up,
        w_down_bf16=_to_bf16(w_down[0]),
    )

    lbs = jnp.cumsum(jax.nn.softmax(lb_logits.astype(F32), axis=0), axis=0)
    lb = lbs[0:1]

    n_c = bp + bs
    pad = (-n_c) % SUBLANES
    c_all = jnp.concatenate([c_prompt.astype(F32), c_sample.astype(F32), jnp.zeros((pad, d), F32)], axis=0)
    mod_all = _ada(c_all, w_ada, b_ada.astype(F32))
    mod_p = mod_all[:bp].reshape(bp * 6, 1, d)
    mod_s = mod_all[bp:bp + bs].reshape(bs, 6, d).transpose(1, 0, 2)

    y_p, conv_p, hg_p, gdn_p = _run_group(x_prompt, mod_p, tp, None, None, None, lb, p, prompt=True)
    y_s, conv_s, hg_s, gdn_s = _run_group(x_sample, mod_s, ts, cache_conv[0], state_hgrn[0], state_gdn[0],
                                          lb, p, prompt=False)
    return (y_p, y_s,
            hg_p[None], gdn_p[None], conv_p.astype(cache_conv.dtype)[None],
            hg_s[None], gdn_s[None], conv_s.astype(cache_conv.dtype)[None])
```

```python
import functools

import jax
import jax.numpy as jnp
from jax import lax
from jax.experimental import pallas as pl
from jax.experimental.pallas import tpu as pltpu

F32 = jnp.float32
BF16 = jnp.bfloat16

HEAD_DIM = 128
HG_CHUNK = 32
GDN_CHUNK = 64
CONV_W = 4
NORM_EPS = 1e-6
SUBLANES = 8
V7X_VMEM_BYTES = 64 * 1024 * 1024
VMEM_LIMIT = V7X_VMEM_BYTES * 3 // 4
VMEM_LIMIT_FFN_DOWN = V7X_VMEM_BYTES * 27 // 32


def _cparams(*sem, vmem_limit=VMEM_LIMIT):
    return pltpu.CompilerParams(dimension_semantics=sem, vmem_limit_bytes=vmem_limit)


def _tile(n, pref, quantum):
    if n <= pref:
        return n
    t = (pref // quantum) * quantum
    while t > quantum and n % t:
        t -= quantum
    assert n % t == 0, (n, pref, quantum)
    return t


def _bdot(a, b):
    return jnp.dot(a.astype(BF16), b.astype(BF16), preferred_element_type=F32)


def _bdot_nt(a, b):
    return lax.dot_general(a.astype(BF16), b.astype(BF16), (((1,), (1,)), ((), ())),
                           preferred_element_type=F32)


def _split3(x):
    hi = x.astype(BF16)
    r = x - hi.astype(F32)
    mid = r.astype(BF16)
    lo = (r - mid.astype(F32)).astype(BF16)
    return hi, mid, lo


def _mask_dot(m_bf, x):
    hi, mid, lo = _split3(x)
    return (jnp.dot(m_bf, hi, preferred_element_type=F32)
            + jnp.dot(m_bf, mid, preferred_element_type=F32)
            + jnp.dot(m_bf, lo, preferred_element_type=F32))


def _sigmoid(x):
    return 1.0 / (1.0 + jnp.exp(-x))


def _silu(x):
    return x * _sigmoid(x)


def _rms(x):
    return x * lax.rsqrt(jnp.mean(x * x, axis=-1, keepdims=True) + NORM_EPS)


def _transpose_pad(x):
    r = x.shape[0]
    if r < HEAD_DIM:
        x = jnp.concatenate([x, jnp.zeros((HEAD_DIM - r, HEAD_DIM), x.dtype)], axis=0)
    return x.T


def _tri_masks(c):
    row = lax.broadcasted_iota(jnp.int32, (c, c), 0)
    col = lax.broadcasted_iota(jnp.int32, (c, c), 1)
    return col <= row, col < row


def _ada_kernel(c_ref, w_ref, b_ref, o_ref):
    cs = _silu(c_ref[...])
    w = w_ref[...]
    c_hi = cs.astype(BF16)
    c_lo = (cs - c_hi.astype(F32)).astype(BF16)
    w_hi = w.astype(BF16)
    w_lo = (w - w_hi.astype(F32)).astype(BF16)
    acc = jnp.dot(c_hi, w_hi, preferred_element_type=F32)
    acc += jnp.dot(c_lo, w_hi, preferred_element_type=F32)
    acc += jnp.dot(c_hi, w_lo, preferred_element_type=F32)
    o_ref[...] = acc + b_ref[...]


def _ada(c_all, w_ada, b_ada):
    m, d = c_all.shape
    n = w_ada.shape[-1]
    tn = _tile(n, 512, 128)
    return pl.pallas_call(
        _ada_kernel,
        grid=(n // tn,),
        in_specs=[pl.BlockSpec((m, d), lambda j: (0, 0)),
                  pl.BlockSpec((None, d, tn), lambda j: (0, 0, j)),
                  pl.BlockSpec((1, tn), lambda j: (0, j))],
        out_specs=pl.BlockSpec((m, tn), lambda j: (0, j)),
        out_shape=jax.ShapeDtypeStruct((m, n), F32),
        compiler_params=_cparams("arbitrary"),
        name="ada_mod",
    )(c_all, w_ada, b_ada)


def _mod_spec(mod, k, tm, rows_per_seq):
    d = mod.shape[-1]
    if mod.shape[1] == 1:
        tiles_per_seq = rows_per_seq // tm
        return pl.BlockSpec((None, 1, d), lambda i, *_: ((i // tiles_per_seq) * 6 + k, 0, 0))
    return pl.BlockSpec((None, tm // rows_per_seq, d), lambda i, *_: (k, i, 0))


def _mod_rows(v, n):
    b = v.shape[0]
    if b == 1 or b == n:
        return v
    rep = n // b
    row = lax.broadcasted_iota(jnp.int32, (n, b), 0)
    col = lax.broadcasted_iota(jnp.int32, (n, b), 1)
    owner = (row >= col * rep) & (row < (col + 1) * rep)
    return _mask_dot(jnp.where(owner, 1.0, 0.0).astype(BF16), v)


def _prenorm_kernel(x_ref, g_ref, sc_ref, sh_ref, o_ref):
    xn = _rms(x_ref[...].astype(F32)) * g_ref[...]
    n = xn.shape[0]
    o_ref[...] = (xn * (1.0 + _mod_rows(sc_ref[...], n)) + _mod_rows(sh_ref[...], n)).astype(o_ref.dtype)


def _prenorm(x, g, mod, k_scale, k_shift, rows_per_seq, tm):
    m, d = x.shape
    return pl.pallas_call(
        _prenorm_kernel,
        grid=(m // tm,),
        in_specs=[pl.BlockSpec((tm, d), lambda i: (i, 0)),
                  pl.BlockSpec((1, d), lambda i: (0, 0)),
                  _mod_spec(mod, k_scale, tm, rows_per_seq),
                  _mod_spec(mod, k_shift, tm, rows_per_seq)],
        out_specs=pl.BlockSpec((tm, d), lambda i: (i, 0)),
        out_shape=jax.ShapeDtypeStruct((m, d), BF16),
        compiler_params=_cparams("parallel"),
        name="prenorm",
    )(x, g, mod, mod)


def _mm_nt_kernel(a_ref, w_ref, o_ref):
    o_ref[...] = _bdot_nt(a_ref[...], w_ref[...])


def _project(a, w_t, n_cols, tm, tn):
    m, kdim = a.shape
    return pl.pallas_call(
        _mm_nt_kernel,
        grid=(m // tm, n_cols // tn),
        in_specs=[pl.BlockSpec((tm, kdim), lambda i, j: (i, 0)),
                  pl.BlockSpec((tn, kdim), lambda i, j: (j, 0))],
        out_specs=pl.BlockSpec((tm, tn), lambda i, j: (i, j)),
        out_shape=jax.ShapeDtypeStruct((m, n_cols), F32),
        compiler_params=_cparams("parallel", "arbitrary"),
        name="in_proj",
    )(a, w_t)


def _hgrn_elementwise(hq, hf, lb):
    q = _silu(hq)
    f = lb + (1.0 - lb) * _sigmoid(hf)
    return q, jnp.log(f), 1.0 - f


def _hgrn_chunk_free(q, g, k, inp, n, tril, tril_bf):
    c = q.shape[0]
    heads = [slice(h * HEAD_DIM, (h + 1) * HEAD_DIM) for h in range(n)]
    big_g = _mask_dot(tril_bf, g)
    g_ref = big_g[c // 2:c // 2 + 1]
    g_last = big_g[c - 1:c]
    qe = q * jnp.exp(big_g - g_ref)
    ke = k * jnp.exp(g_ref - big_g)
    qs = q * jnp.exp(big_g)
    kl = k * jnp.exp(g_last - big_g)
    d_hi, d_mid, d_lo = [p.astype(F32) for p in _split3(jnp.exp(g_last))]
    extra = jnp.concatenate([d_hi, d_mid, d_lo, jnp.zeros((SUBLANES - 3, q.shape[1]), F32)], axis=0)
    a = [jnp.where(tril, _bdot_nt(qe[:, sl], ke[:, sl]), 0.0) for sl in heads]
    o_intra = [_bdot(a[h], inp[:, sl]) for h, sl in enumerate(heads)]
    t = [_transpose_pad(jnp.concatenate([kl[:, sl], extra[:, sl]], axis=0))[:, :c + SUBLANES] for sl in heads]
    piece_rows = lax.broadcasted_iota(jnp.int32, (SUBLANES, HEAD_DIM), 0) < 3
    ones_rows = jnp.where(piece_rows, 1.0, 0.0)
    zeros_c = jnp.zeros((c, HEAD_DIM), F32)
    uds = []
    for h, sl in enumerate(heads):
        rhs = jnp.concatenate([jnp.concatenate([inp[:, sl], zeros_c], axis=1),
                               jnp.concatenate([jnp.zeros((SUBLANES, HEAD_DIM), F32), ones_rows], axis=1)], axis=0)
        uds.append(_bdot(t[h], rhs))
    return o_intra, qs, uds


def _hgrn_chunk_heads(q, g, k, inp, states, tril, tril_bf):
    o_intra, qs, uds = _hgrn_chunk_free(q, g, k, inp, len(states), tril, tril_bf)
    out = []
    for h, s in enumerate(states):
        sl = slice(h * HEAD_DIM, (h + 1) * HEAD_DIM)
        out.append((o_intra[h] + _bdot(qs[:, sl], s), uds[h][:, HEAD_DIM:] * s + uds[h][:, :HEAD_DIM]))
    return out


def _head_out(o, gate, norm_g):
    mean_sq = _lane_sum_bcast(o * o) * (1.0 / HEAD_DIM)
    return o * lax.rsqrt(mean_sq + NORM_EPS) * norm_g * _silu(gate)


HGRN_CHUNKS_PER_PASS = 8


def _hgrn_prompt_kernel(q_ref, f_ref, i_ref, g_ref, lb_ref, ng_ref, o_ref, st_ref,
                        s_scr, oi_scr, qs_scr, ud_scr, *, hb):
    t = pl.program_id(2)
    c = HG_CHUNK
    n_chunks = q_ref.shape[0] // c
    per = HGRN_CHUNKS_PER_PASS if n_chunks % HGRN_CHUNKS_PER_PASS == 0 else 1
    heads = [slice(h * HEAD_DIM, (h + 1) * HEAD_DIM) for h in range(hb)]

    @pl.when(t == 0)
    def _():
        s_scr[...] = jnp.zeros_like(s_scr)

    tril, _ = _tri_masks(c)
    tril_bf = jnp.where(tril, 1.0, 0.0).astype(BF16)
    lb = lb_ref[...]
    ng = ng_ref[...]

    def free_part(it, carry):
        rows = [pl.ds(pl.multiple_of((it * per + j) * c, c), c) for j in range(per)]
        parts = [_hgrn_elementwise(q_ref[r, :], f_ref[r, :], lb) for r in rows]
        cat = lambda xs: jnp.concatenate(xs, axis=1)
        o_intra, qs, uds = _hgrn_chunk_free(cat([p[0] for p in parts]), cat([p[1] for p in parts]),
                                            cat([p[2] for p in parts]), cat([i_ref[r, :] for r in rows]),
                                            per * hb, tril, tril_bf)
        for j, r in enumerate(rows):
            qs_scr[r, :] = qs[:, j * hb * HEAD_DIM:(j + 1) * hb * HEAD_DIM].astype(qs_scr.dtype)
            for h, sl in enumerate(heads):
                oi_scr[r, sl] = o_intra[j * hb + h]
                ud_scr[(it * per + j) * hb + h] = uds[j * hb + h]
        return carry

    lax.fori_loop(0, n_chunks // per, free_part, 0)

    states = [s_scr[h] for h in range(hb)]
    outs = []
    for ci in range(n_chunks):
        rows = pl.ds(ci * c, c)
        outs.append([oi_scr[rows, sl] + _bdot(qs_scr[rows, sl], states[h]) for h, sl in enumerate(heads)])
        for h in range(hb):
            ud = ud_scr[ci * hb + h]
            states[h] = ud[:, HEAD_DIM:] * states[h] + ud[:, :HEAD_DIM]
    for h in range(hb):
        s_scr[h] = states[h]
    for ci in range(n_chunks):
        rows = pl.ds(ci * c, c)
        for h, sl in enumerate(heads):
            o_ref[rows, sl] = _head_out(outs[ci][h], g_ref[rows, sl], ng).astype(o_ref.dtype)

    @pl.when(t == pl.num_programs(2) - 1)
    def _():
        st_ref[...] = s_scr[...].astype(st_ref.dtype)


def _hgrn_prompt(proj, lb, norm_g, batch, seq, heads, state_dtype):
    hb = _tile(heads, 8, 1)
    tt = _tile(seq, 256, HG_CHUNK)
    w = hb * HEAD_DIM
    nhb = heads // hb
    nt = seq // tt

    def col(seg):
        return pl.BlockSpec((tt, w), lambda b, h, t: (b * nt + t, seg * nhb + h))

    return pl.pallas_call(
        functools.partial(_hgrn_prompt_kernel, hb=hb),
        grid=(batch, nhb, nt),
        in_specs=[col(0), col(1), col(2), col(3),
                  pl.BlockSpec((1, w), lambda b, h, t: (0, h)),
                  pl.BlockSpec((1, HEAD_DIM), lambda b, h, t: (0, 0))],
        out_specs=[pl.BlockSpec((tt, w), lambda b, h, t: (b * nt + t, h)),
                   pl.BlockSpec((None, hb, HEAD_DIM, HEAD_DIM), lambda b, h, t: (b, h, 0, 0))],
        out_shape=[jax.ShapeDtypeStruct((batch * seq, heads * HEAD_DIM), BF16),
                   jax.ShapeDtypeStruct((batch, heads, HEAD_DIM, HEAD_DIM), state_dtype)],
        scratch_shapes=[pltpu.VMEM((hb, HEAD_DIM, HEAD_DIM), F32),
                        pltpu.VMEM((tt, w), F32),
                        pltpu.VMEM((tt, w), BF16),
                        pltpu.VMEM(((tt // HG_CHUNK) * hb, HEAD_DIM, 2 * HEAD_DIM), F32)],
        compiler_params=_cparams("parallel", "parallel", "arbitrary"),
        name="hgrn_prompt",
    )(proj, proj, proj, proj, lb, norm_g)


def _hgrn_sample_kernel(q_ref, f_ref, i_ref, g_ref, lb_ref, ng_ref, s0_ref, o_ref, st_ref, *, hb, seq):
    c = SUBLANES
    per_group = c // seq
    tril, _ = _tri_masks(c)
    tril_bf = jnp.where(tril, 1.0, 0.0).astype(BF16)
    lb = lb_ref[...]
    ng = ng_ref[...]
    row = lax.broadcasted_iota(jnp.int32, (c, 1), 0)

    n_groups = q_ref.shape[0] // c
    valid = [(row >= p * seq) & (row < (p + 1) * seq) for p in range(per_group)]
    masked = lambda x: [jnp.where(v, x, 0.0) for v in valid]
    qs, gs, ks, inps, states = [], [], [], [], []
    for gi in range(n_groups):
        rows = pl.ds(gi * c, c)
        q, g, k = _hgrn_elementwise(q_ref[rows, :], f_ref[rows, :], lb)
        qs += masked(q)
        gs += masked(g)
        ks += masked(k)
        inps += [i_ref[rows, :]] * per_group
        states += [s0_ref[gi * per_group + p, h].astype(F32) for p in range(per_group) for h in range(hb)]
    cat = lambda xs: jnp.concatenate(xs, axis=1)
    res = _hgrn_chunk_heads(cat(qs), cat(gs), cat(ks), cat(inps), states, tril, tril_bf)
    for gi in range(n_groups):
        rows = pl.ds(gi * c, c)
        for h in range(hb):
            sl = slice(h * HEAD_DIM, (h + 1) * HEAD_DIM)
            o_all = jnp.zeros((c, HEAD_DIM), F32)
            for p in range(per_group):
                o, s_new = res[(gi * per_group + p) * hb + h]
                st_ref[gi * per_group + p, h] = s_new.astype(st_ref.dtype)
                o_all = jnp.where(valid[p], o, o_all)
            o_ref[rows, sl] = _head_out(o_all, g_ref[rows, sl], ng).astype(o_ref.dtype)


def _hgrn_sample(proj, lb, norm_g, s0, batch, seq, heads):
    assert SUBLANES % seq == 0
    hb = _tile(heads, 4, 1)
    nb = _tile(batch, 16, SUBLANES // seq)
    w = hb * HEAD_DIM
    nhb = heads // hb
    r = nb * seq

    def col(seg):
        return pl.BlockSpec((r, w), lambda b, h: (b, seg * nhb + h))

    st_spec = pl.BlockSpec((nb, hb, HEAD_DIM, HEAD_DIM), lambda b, h: (b, h, 0, 0))
    return pl.pallas_call(
        functools.partial(_hgrn_sample_kernel, hb=hb, seq=seq),
        grid=(batch // nb, nhb),
        in_specs=[col(0), col(1), col(2), col(3),
                  pl.BlockSpec((1, w), lambda b, h: (0, h)),
                  pl.BlockSpec((1, HEAD_DIM), lambda b, h: (0, 0)),
                  st_spec],
        out_specs=[pl.BlockSpec((r, w), lambda b, h: (b, h)), st_spec],
        out_shape=[jax.ShapeDtypeStruct((batch * seq, heads * HEAD_DIM), BF16),
                   jax.ShapeDtypeStruct(s0.shape, s0.dtype)],
        compiler_params=_cparams("parallel", "parallel"),
        name="hgrn_sample",
    )(proj, proj, proj, proj, lb, norm_g, s0)


def _gdn_gates_kernel(x_ref, alog_ref, dtb_ref, beta_ref, gam_ref, gtot_ref, *, nv, chunk):
    x = x_ref[...]
    r = x.shape[0]
    beta_ref[...] = _sigmoid(x[:, :nv])
    z = x[:, nv:] + dtb_ref[...]
    softplus = jnp.maximum(z, 0.0) + jnp.log(1.0 + jnp.exp(-jnp.abs(z)))
    g = -jnp.exp(alog_ref[...]) * softplus
    row = lax.broadcasted_iota(jnp.int32, (r, r), 0)
    col = lax.broadcasted_iota(jnp.int32, (r, r), 1)
    same = (row // chunk) == (col // chunk)
    gam_ref[...] = _mask_dot(jnp.where(same & (col <= row), 1.0, 0.0).astype(BF16), g)
    gtot_ref[...] = _mask_dot(jnp.where(same, 1.0, 0.0).astype(BF16), g)


def _gdn_gates(x, a_log, dt_bias, chunk):
    m, two_nv = x.shape
    nv = two_nv // 2
    tr = _tile(m, 256, max(chunk, SUBLANES))
    out = jax.ShapeDtypeStruct((m, nv), F32)
    return pl.pallas_call(
        functools.partial(_gdn_gates_kernel, nv=nv, chunk=chunk),
        grid=(m // tr,),
        in_specs=[pl.BlockSpec((tr, two_nv), lambda i: (i, 0)),
                  pl.BlockSpec((1, nv), lambda i: (0, 0)),
                  pl.BlockSpec((1, nv), lambda i: (0, 0))],
        out_specs=[pl.BlockSpec((tr, nv), lambda i: (i, 0))] * 3,
        out_shape=[out, out, out],
        compiler_params=_cparams("parallel"),
        name="gdn_gates",
    )(x, a_log, dt_bias)


def _lane_sum_bcast(x):
    hi = x.astype(BF16)
    lo = (x - hi.astype(F32)).astype(BF16)
    return jnp.dot(jnp.concatenate([hi, lo], axis=1), jnp.ones((2 * HEAD_DIM, HEAD_DIM), BF16),
                   preferred_element_type=F32)


def _l2_heads(y, mult):
    parts = []
    for h in range(y.shape[1] // HEAD_DIM):
        yh = y[:, h * HEAD_DIM:(h + 1) * HEAD_DIM]
        parts.append(yh * (lax.rsqrt(_lane_sum_bcast(yh * yh) + NORM_EPS) * mult))
    return jnp.concatenate(parts, axis=1)


def _conv_finish(y, kind_q, kind_k, o_ref, scale):
    y = _silu(y)

    @pl.when(kind_q)
    def _():
        o_ref[...] = _l2_heads(y, scale)

    @pl.when(kind_k)
    def _():
        o_ref[...] = _l2_heads(y, 1.0)

    @pl.when(jnp.logical_not(kind_q | kind_k))
    def _():
        o_ref[...] = y


def _conv_sample_kernel(u_ref, buf_ref, cw_ref, o_ref, cache_ref, *, seq, nq_tiles, scale):
    j = pl.program_id(1)
    u = u_ref[...]
    r = u.shape[0]
    nb = r // seq
    tok = lax.broadcasted_iota(jnp.int32, (r, 1), 0) % seq
    hist = [_mod_rows(buf_ref[slot], r) for slot in range(CONV_W - 1)]
    cw = cw_ref[...]
    y = cw[CONV_W - 1:CONV_W] * u
    for s in range(1, CONV_W):
        tap = pltpu.roll(u, s, axis=0)
        for t in range(min(s, seq)):
            tap = jnp.where(tok == t, hist[t + CONV_W - 1 - s], tap)
        y = y + cw[CONV_W - 1 - s:CONV_W - s] * tap
    _conv_finish(y, j < nq_tiles, (j >= nq_tiles) & (j < 2 * nq_tiles), o_ref, scale)
    seq_i = lax.broadcasted_iota(jnp.int32, (nb, r), 0)
    row_i = lax.broadcasted_iota(jnp.int32, (nb, r), 1)
    for slot in range(CONV_W - 1):
        pick = row_i == seq_i * seq + (seq - (CONV_W - 1) + slot)
        cache_ref[slot] = _mask_dot(jnp.where(pick, 1.0, 0.0).astype(BF16), u).astype(cache_ref.dtype)


def _conv_sample(proj, col0, conv_buf, conv_w, seq, qk_w, ch):
    m = proj.shape[0]
    wc = _tile(qk_w, 512, HEAD_DIM)
    tr = _tile(m, 256, SUBLANES * seq)
    c0 = col0 // wc
    assert col0 % wc == 0 and tr % seq == 0 and seq >= CONV_W - 1
    nseq = tr // seq
    slots = pl.BlockSpec((CONV_W - 1, nseq, wc), lambda i, j: (0, i, j))
    buf = jnp.transpose(conv_buf, (1, 0, 2))
    qkv, cache = pl.pallas_call(
        functools.partial(_conv_sample_kernel, seq=seq, nq_tiles=qk_w // wc,
                          scale=float(HEAD_DIM) ** -0.5),
        grid=(m // tr, ch // wc),
        in_specs=[pl.BlockSpec((tr, wc), lambda i, j: (i, c0 + j)),
                  slots,
                  pl.BlockSpec((None, CONV_W, wc), lambda i, j: (0, 0, j))],
        out_specs=[pl.BlockSpec((tr, wc), lambda i, j: (i, j)), slots],
        out_shape=[jax.ShapeDtypeStruct((m, ch), F32), jax.ShapeDtypeStruct(buf.shape, conv_buf.dtype)],
        compiler_params=_cparams("parallel", "parallel"),
        name="gdn_conv_sample",
    )(proj, buf, conv_w)
    return qkv, jnp.transpose(cache, (1, 0, 2))


GDN_TILE = 64


def _mm_tile(a, b):
    return _bdot(a[:, :GDN_TILE], b)


def _lane_bcast_all(x, sel):
    return _mask_dot_right(x, sel)


def _mask_dot_right(x, m_bf):
    hi, mid, lo = _split3(x)
    return (jnp.dot(hi, m_bf, preferred_element_type=F32)
            + jnp.dot(mid, m_bf, preferred_element_type=F32)
            + jnp.dot(lo, m_bf, preferred_element_type=F32))


def _head_select(hb):
    head = lax.broadcasted_iota(jnp.int32, (hb, hb * HEAD_DIM), 0)
    lane_head = lax.broadcasted_iota(jnp.int32, (hb, hb * HEAD_DIM), 1) // HEAD_DIM
    return jnp.where(head == lane_head, 1.0, 0.0).astype(BF16)


def _unit_lower_inverses(a_list, rows, cols, block):
    eye = jnp.where(rows == cols, 1.0, 0.0)
    pairs = (rows >> 1) == (cols >> 1)
    d = [eye - jnp.where(pairs, a, 0.0) for a in a_list]
    b = 2
    while b < block:
        sh = b.bit_length()
        coupling = ((rows >> sh) == (cols >> sh)) & ((rows >> (sh - 1)) != (cols >> (sh - 1)))
        m1 = [_mm_tile(jnp.where(coupling, a, 0.0), di) for a, di in zip(a_list, d)]
        m2 = [_mm_tile(di, mi) for di, mi in zip(d, m1)]
        d = [di - mi for di, mi in zip(d, m2)]
        b *= 2
    return d


def _gdn_pair_products(q, k):
    k_pad = jnp.concatenate([k, jnp.zeros_like(k)], axis=0)
    kq = _bdot_nt(jnp.concatenate([k, q], axis=0), k_pad)
    return kq[:GDN_TILE], kq[GDN_TILE:]


def _gdn_wy(tiles, rep, rows, cols, incl, strict, block):
    heads = [slice(h * HEAD_DIM, (h + 1) * HEAD_DIM) for h in range(len(tiles[0][2]))]
    prods = [[_gdn_pair_products(q, k) for q, k in zip(qs, ks)] for qs, ks, *_ in tiles]
    decays, a_list = [], []
    for (qs, ks, vs, bc, gc, gr), prod in zip(tiles, prods):
        for h, sl in enumerate(heads):
            decay = jnp.where(incl, jnp.exp(gc[:, sl] - gr[h:h + 1, :]), 0.0)
            decays.append(decay)
            a_list.append(jnp.where(strict, bc[:, sl] * prod[h // rep][0] * decay, 0.0))
    t_inv = _unit_lower_inverses(a_list, rows, cols, block)
    xs = []
    for t, (qs, ks, vs, bc, gc, gr) in enumerate(tiles):
        for h, sl in enumerate(heads):
            beta = bc[:, sl]
            rhs = jnp.concatenate([vs[h] * beta, ks[h // rep] * (beta * jnp.exp(gc[:, sl]))], axis=1)
            xs.append(_mm_tile(t_inv[t * len(heads) + h], rhs))
    out = []
    for t in range(len(tiles)):
        out.append([(xs[t * len(heads) + h][:, :HEAD_DIM], xs[t * len(heads) + h][:, HEAD_DIM:],
                     prods[t][h // rep][1] * decays[t * len(heads) + h]) for h in range(len(heads))])
    return out


def _tile_iotas():
    shape = (GDN_TILE, HEAD_DIM)
    return lax.broadcasted_iota(jnp.int32, shape, 0), lax.broadcasted_iota(jnp.int32, shape, 1)


GDN_CHUNKS_PER_PASS = 4


def _causal_conv(x_scr, cw, r0, n, cols):
    y = None
    for s in range(CONV_W):
        lo = SUBLANES + r0 - s
        term = cw[CONV_W - 1 - s:CONV_W - s, cols] * x_scr[lo:lo + n, cols]
        y = term if y is None else y + term
    return y


def _gdn_prompt_kernel(uq_ref, uk_ref, uv_ref, pq_ref, pk_ref, pv_ref, cwq_ref, cwk_ref, cwv_ref,
                       z_ref, bc_ref, gc_ref, gt_ref, gr_ref, ng_ref,
                       o_ref, st_ref, s_scr, u_scr, w_scr, a_scr, xq_scr, xk_scr, xv_scr, q_scr, k_scr,
                       *, hb, rep, scale):
    t = pl.program_id(2)
    c = GDN_CHUNK
    tt = uq_ref.shape[0]
    n_chunks = tt // c

    @pl.when(t == 0)
    def _():
        s_scr[...] = jnp.zeros_like(s_scr)

    keep = jnp.where(t == 0, 0.0, 1.0)
    for x_scr, prev_ref, cur_ref in ((xq_scr, pq_ref, uq_ref), (xk_scr, pk_ref, uk_ref), (xv_scr, pv_ref, uv_ref)):
        x_scr[0:SUBLANES, :] = prev_ref[...] * keep
        x_scr[SUBLANES:SUBLANES + tt, :] = cur_ref[...]

    rows_i, cols_i = _tile_iotas()
    incl = cols_i <= rows_i
    strict = cols_i < rows_i
    ng = ng_ref[...]
    sel = _head_select(hb)
    cwq, cwk, cwv = cwq_ref[...], cwk_ref[...], cwv_ref[...]
    heads = [slice(h * HEAD_DIM, (h + 1) * HEAD_DIM) for h in range(hb)]

    def chunk_operands(ci):
        rows = pl.ds(ci * c, c)
        qs = [_l2_heads(_silu(_causal_conv(xq_scr, cwq, ci * c, c, heads[i])), scale) for i in range(hb // rep)]
        ks = [_l2_heads(_silu(_causal_conv(xk_scr, cwk, ci * c, c, heads[i])), 1.0) for i in range(hb // rep)]
        vs = [_silu(_causal_conv(xv_scr, cwv, ci * c, c, heads[h])) for h in range(hb)]
        for i in range(hb // rep):
            q_scr[rows, heads[i]] = qs[i]
            k_scr[rows, heads[i]] = ks[i]
        return (qs, ks, vs, _lane_bcast_all(bc_ref[rows, :], sel),
                _lane_bcast_all(gc_ref[rows, :], sel), gr_ref[ci])

    per = GDN_CHUNKS_PER_PASS if n_chunks % GDN_CHUNKS_PER_PASS == 0 else 1
    for c0 in range(0, n_chunks, per):
        wy = _gdn_wy([chunk_operands(ci) for ci in range(c0, c0 + per)], rep, rows_i, cols_i, incl, strict, c)
        for j in range(per):
            rows = pl.ds((c0 + j) * c, c)
            for h, (u, w, attn) in enumerate(wy[j]):
                sl = slice(h * HEAD_DIM, (h + 1) * HEAD_DIM)
                u_scr[rows, sl] = u
                w_scr[rows, sl] = w.astype(w_scr.dtype)
                a_scr[rows, sl] = attn.astype(a_scr.dtype)

    pairs = range(hb // rep)
    s_pairs = [s_scr[pr] for pr in pairs]
    outs = []
    for ci in range(n_chunks):
        rows = pl.ds(ci * c, c)
        gc = _lane_bcast_all(gc_ref[rows, :], sel)
        gt = _lane_bcast_all(gt_ref[rows, :], sel)
        qs = [_bdot(q_scr[rows, heads[pr]], s_pairs[pr]) for pr in pairs]
        v_new = [u_scr[rows, sl] - _bdot(w_scr[rows, sl], s_pairs[h // rep][:, heads[h % rep]])
                 for h, sl in enumerate(heads)]
        outs.append([qs[h // rep][:, heads[h % rep]] * jnp.exp(gc[:, sl])
                     + _bdot(a_scr[rows, sl][:, :c], v_new[h]) for h, sl in enumerate(heads)])
        for pr in pairs:
            k_t = _transpose_pad(k_scr[rows, heads[pr]])[:, :c]
            mine = range(pr * rep, (pr + 1) * rep)
            v_scaled = jnp.concatenate([v_new[h] * jnp.exp(gt[:, heads[h]] - gc[:, heads[h]]) for h in mine], axis=1)
            decays = jnp.concatenate([jnp.exp(gt[0:1, heads[h]]) for h in mine], axis=1)
            s_pairs[pr] = decays * s_pairs[pr] + _bdot(k_t, v_scaled)
    for pr in pairs:
        s_scr[pr] = s_pairs[pr]
    for ci in range(n_chunks):
        rows = pl.ds(ci * c, c)
        for h, sl in enumerate(heads):
            o_ref[rows, sl] = _head_out(outs[ci][h], z_ref[rows, sl], ng).astype(o_ref.dtype)

    @pl.when(t == pl.num_programs(2) - 1)
    def _():
        for h in range(hb):
            st_ref[h] = s_scr[h // rep][:, (h % rep) * HEAD_DIM:(h % rep + 1) * HEAD_DIM].astype(st_ref.dtype)


def _gdn_layouts(beta, gam, gtot, nv, hb):
    m = beta.shape[0]
    nhb = nv // hb
    as_cols = lambda a: a.reshape(m, nhb, hb).transpose(1, 0, 2)
    g_rows = gam.reshape(m // GDN_TILE, GDN_TILE, nhb, hb).transpose(2, 0, 3, 1)
    g_rows = jnp.pad(g_rows, ((0, 0), (0, 0), (0, 0), (0, HEAD_DIM - GDN_TILE)))
    return as_cols(beta), as_cols(gam), as_cols(gtot), g_rows


def _gdn_prompt(proj, conv_col0, z_col0, conv_w, gate_arrays, norm_g, batch, seq, qk_heads, v_heads,
                state_dtype):
    rep = v_heads // qk_heads
    hb = _tile(v_heads, 8, rep)
    tt = _tile(seq, 256, GDN_CHUNK)
    nt = seq // tt
    nhb = v_heads // hb
    wq = (hb // rep) * HEAD_DIM
    wv = hb * HEAD_DIM
    qk_w = qk_heads * HEAD_DIM
    bc, gc, gt, gr = _gdn_layouts(*gate_arrays, v_heads, hb)
    assert conv_col0 % wv == 0 and z_col0 % wv == 0 and GDN_CHUNK == GDN_TILE and tt % SUBLANES == 0
    q0 = conv_col0 // wq
    k0 = (conv_col0 + qk_w) // wq
    v0 = (conv_col0 + 2 * qk_w) // wv
    z0 = z_col0 // wv
    kc0 = qk_w // wq
    vc0 = 2 * qk_w // wv
    col_spec = pl.BlockSpec((None, tt, hb), lambda b, h, t: (h, b * nt + t, 0))
    sub = tt // SUBLANES

    def cur(width, c0):
        return pl.BlockSpec((tt, width), lambda b, h, t: (b * nt + t, c0 + h))

    def prev(width, c0):
        return pl.BlockSpec((SUBLANES, width), lambda b, h, t: (jnp.maximum((b * nt + t) * sub - 1, 0), c0 + h))

    def taps(width, c0):
        return pl.BlockSpec((None, CONV_W, width), lambda b, h, t: (0, 0, c0 + h))

    return pl.pallas_call(
        functools.partial(_gdn_prompt_kernel, hb=hb, rep=rep, scale=float(HEAD_DIM) ** -0.5),
        grid=(batch, nhb, nt),
        in_specs=[cur(wq, q0), cur(wq, k0), cur(wv, v0),
                  prev(wq, q0), prev(wq, k0), prev(wv, v0),
                  taps(wq, 0), taps(wq, kc0), taps(wv, vc0),
                  cur(wv, z0),
                  col_spec, col_spec, col_spec,
                  pl.BlockSpec((None, tt // GDN_TILE, hb, HEAD_DIM), lambda b, h, t: (h, b * nt + t, 0, 0)),
                  pl.BlockSpec((1, HEAD_DIM), lambda b, h, t: (0, 0))],
        out_specs=[pl.BlockSpec((tt, wv), lambda b, h, t: (b * nt + t, h)),
                   pl.BlockSpec((None, hb, HEAD_DIM, HEAD_DIM), lambda b, h, t: (b, h, 0, 0))],
        out_shape=[jax.ShapeDtypeStruct((batch * seq, v_heads * HEAD_DIM), BF16),
                   jax.ShapeDtypeStruct((batch, v_heads, HEAD_DIM, HEAD_DIM), state_dtype)],
        scratch_shapes=[pltpu.VMEM((hb // rep, HEAD_DIM, rep * HEAD_DIM), F32),
                        pltpu.VMEM((tt, wv), F32),
                        pltpu.VMEM((tt, wv), BF16),
                        pltpu.VMEM((tt, wv), BF16),
                        pltpu.VMEM((tt + SUBLANES, wq), F32),
                        pltpu.VMEM((tt + SUBLANES, wq), F32),
                        pltpu.VMEM((tt + SUBLANES, wv), F32),
                        pltpu.VMEM((tt, wq), F32),
                        pltpu.VMEM((tt, wq), F32)],
        compiler_params=_cparams("parallel", "parallel", "arbitrary"),
        name="gdn_prompt",
    )(proj, proj, proj, proj, proj, proj, conv_w, conv_w, conv_w, proj, bc, gc, gt, gr, norm_g)


def _gdn_sample_kernel(q_ref, k_ref, v_ref, z_ref, bc_ref, gc_ref, gt_ref, gr_ref, ng_ref, s0_ref,
                       o_ref, st_ref, *, hb, rep, seq):
    c = GDN_TILE
    nseq = c // seq
    per_group = SUBLANES // seq
    lsh = seq.bit_length() - 1
    rows_i, cols_i = _tile_iotas()
    same = (rows_i >> lsh) == (cols_i >> lsh)
    incl = (cols_i <= rows_i) & same
    strict = (cols_i < rows_i) & same
    row8 = lax.broadcasted_iota(jnp.int32, (SUBLANES, 1), 0)
    lane_seq = lax.broadcasted_iota(jnp.int32, (1, HEAD_DIM), 1) >> lsh
    ng = ng_ref[...]
    sel = _head_select(hb)
    bc = _lane_bcast_all(bc_ref[...], sel)
    gc = _lane_bcast_all(gc_ref[...], sel)
    gt = _lane_bcast_all(gt_ref[...], sel)
    gr = gr_ref[0]
    head = lambda ref, i: ref[:, i * HEAD_DIM:(i + 1) * HEAD_DIM]
    qs = [head(q_ref, i) for i in range(hb // rep)]
    ks = [head(k_ref, i) for i in range(hb // rep)]
    wy = _gdn_wy([(qs, ks, [head(v_ref, h) for h in range(hb)], bc, gc, gr)], rep,
                 rows_i, cols_i, incl, strict, seq)[0]
    k_t_seq = []
    for k in ks:
        k_t = _transpose_pad(k)
        k_t_seq.append(jnp.concatenate([jnp.where(lane_seq == b, k_t, 0.0)[:, :c] for b in range(nseq)],
                                       axis=0).astype(BF16))
    qe = [qs[h // rep] * jnp.exp(head(gc, h)) for h in range(hb)]
    ws_parts = [[] for _ in range(hb)]
    qs_parts = [[] for _ in range(hb)]
    for g in range(c // SUBLANES):
        gsl = slice(g * SUBLANES, (g + 1) * SUBLANES)
        for h in range(hb):
            acc = jnp.zeros((2 * SUBLANES, HEAD_DIM), F32)
            for p in range(per_group):
                valid = (row8 >= p * seq) & (row8 < (p + 1) * seq)
                lhs = jnp.concatenate([jnp.where(valid, wy[h][1][gsl], 0.0),
                                       jnp.where(valid, qe[h][gsl], 0.0)], axis=0)
                acc = acc + _bdot(lhs, s0_ref[g * per_group + p, h])
            ws_parts[h].append(acc[:SUBLANES])
            qs_parts[h].append(acc[SUBLANES:])
    v_new = [wy[h][0] - jnp.concatenate(ws_parts[h], axis=0) for h in range(hb)]
    outs = [jnp.concatenate(qs_parts[h], axis=0) + _bdot(wy[h][2][:, :c], v_new[h]) for h in range(hb)]
    upds = [jnp.dot(k_t_seq[h // rep], (v_new[h] * jnp.exp(head(gt, h) - head(gc, h))).astype(BF16),
                    preferred_element_type=F32) for h in range(hb)]
    for h in range(hb):
        o_ref[:, h * HEAD_DIM:(h + 1) * HEAD_DIM] = _head_out(outs[h], head(z_ref, h), ng).astype(o_ref.dtype)
        decay = jnp.exp(head(gt, h))
        for b in range(nseq):
            st_ref[b, h] = (decay[b * seq:b * seq + 1] * s0_ref[b, h].astype(F32)
                            + upds[h][b * HEAD_DIM:(b + 1) * HEAD_DIM]).astype(st_ref.dtype)


def _gdn_sample(qkv, proj, z_col0, gate_arrays, norm_g, s0, batch, seq, qk_heads, v_heads):
    rep = v_heads // qk_heads
    hb = _tile(v_heads, 4, rep)
    assert GDN_TILE % seq == 0 and SUBLANES % seq == 0 and seq & (seq - 1) == 0
    nb = GDN_TILE // seq
    assert batch % nb == 0
    r = GDN_TILE
    nhb = v_heads // hb
    wq = (hb // rep) * HEAD_DIM
    wv = hb * HEAD_DIM
    qk_w = qk_heads * HEAD_DIM
    bc, gc, gt, gr = _gdn_layouts(*gate_arrays, v_heads, hb)
    k0 = qk_w // wq
    v0 = 2 * qk_w // wv
    z0 = z_col0 // wv
    assert z_col0 % wv == 0
    st_spec = pl.BlockSpec((nb, hb, HEAD_DIM, HEAD_DIM), lambda b, h: (b, h, 0, 0))
    col_spec = pl.BlockSpec((None, r, hb), lambda b, h: (h, b, 0))
    return pl.pallas_call(
        functools.partial(_gdn_sample_kernel, hb=hb, rep=rep, seq=seq),
        grid=(batch // nb, nhb),
        in_specs=[pl.BlockSpec((r, wq), lambda b, h: (b, h)),
                  pl.BlockSpec((r, wq), lambda b, h: (b, k0 + h)),
                  pl.BlockSpec((r, wv), lambda b, h: (b, v0 + h)),
                  pl.BlockSpec((r, wv), lambda b, h: (b, z0 + h)),
                  col_spec, col_spec, col_spec,
                  pl.BlockSpec((None, 1, hb, HEAD_DIM), lambda b, h: (h, b, 0, 0)),
                  pl.BlockSpec((1, HEAD_DIM), lambda b, h: (0, 0)),
                  st_spec],
        out_specs=[pl.BlockSpec((r, wv), lambda b, h: (b, h)), st_spec],
        out_shape=[jax.ShapeDtypeStruct((batch * seq, v_heads * HEAD_DIM), BF16),
                   jax.ShapeDtypeStruct(s0.shape, s0.dtype)],
        compiler_params=_cparams("parallel", "parallel"),
        name="gdn_sample",
    )(qkv, qkv, qkv, proj, bc, gc, gt, gr, norm_g, s0)


def _merge_kernel(oh_ref, og_ref, wa_ref, wb_ref, ga_ref, gb_ref, o_ref):
    ya = _bdot(oh_ref[...], wa_ref[...])
    yb = _bdot(og_ref[...], wb_ref[...])
    o_ref[...] = (_sigmoid(ga_ref[...]) * ya + _sigmoid(gb_ref[...]) * yb).astype(o_ref.dtype)


def _merge(o_hg, o_gdn, w_a, w_b, gates, tm):
    m, ka = o_hg.shape
    kb = o_gdn.shape[1]
    d = w_a.shape[-1]
    tn = _tile(d, 256, 128)
    nj = d // tn
    return pl.pallas_call(
        _merge_kernel,
        grid=(m // tm, nj),
        in_specs=[pl.BlockSpec((tm, ka), lambda i, j: (i, 0)),
                  pl.BlockSpec((tm, kb), lambda i, j: (i, 0)),
                  pl.BlockSpec((None, ka, tn), lambda i, j: (0, 0, j)),
                  pl.BlockSpec((None, kb, tn), lambda i, j: (0, 0, j)),
                  pl.BlockSpec((tm, tn), lambda i, j: (i, j)),
                  pl.BlockSpec((tm, tn), lambda i, j: (i, nj + j))],
        out_specs=pl.BlockSpec((tm, tn), lambda i, j: (i, j)),
        out_shape=jax.ShapeDtypeStruct((m, d), BF16),
        compiler_params=_cparams("parallel", "arbitrary"),
        name="merge",
    )(o_hg, o_gdn, w_a, w_b, gates, gates)


def _outproj_kernel(m_ref, w_ref, x_ref, g1_ref, sc_ref, sh_ref, ng_ref, h_ref, a_ref):
    ng = ng_ref[...]
    mix = _bdot(m_ref[...], w_ref[...])
    n = mix.shape[0]
    h = x_ref[...].astype(F32) + _mod_rows(g1_ref[...], n) * (_rms(mix) * ng[0:1])
    h_ref[...] = h
    a_ref[...] = ((_rms(h) * ng[1:2]) * (1.0 + _mod_rows(sc_ref[...], n))
                  + _mod_rows(sh_ref[...], n)).astype(a_ref.dtype)


def _outproj(merged, w_o_bf16, x, mod, norm_g2, rows_per_seq, tm):
    m, d = x.shape
    kdim = merged.shape[1]
    row = lambda i: (i, 0)
    return pl.pallas_call(
        _outproj_kernel,
        grid=(m // tm,),
        in_specs=[pl.BlockSpec((tm, kdim), row),
                  pl.BlockSpec((kdim, d), lambda i: (0, 0), pipeline_mode=pl.Buffered(1)),
                  pl.BlockSpec((tm, d), row),
                  _mod_spec(mod, 2, tm, rows_per_seq),
                  _mod_spec(mod, 4, tm, rows_per_seq),
                  _mod_spec(mod, 3, tm, rows_per_seq),
                  pl.BlockSpec((2, d), lambda i: (0, 0))],
        out_specs=[pl.BlockSpec((tm, d), row), pl.BlockSpec((tm, d), row)],
        out_shape=[jax.ShapeDtypeStruct((m, d), F32), jax.ShapeDtypeStruct((m, d), BF16)],
        compiler_params=_cparams("parallel"),
        name="out_proj",
    )(merged, w_o_bf16, x, mod, mod, mod, norm_g2)


def _cast_kernel(x_ref, o_ref):
    o_ref[...] = x_ref[...].astype(o_ref.dtype)


def _to_bf16(w):
    kdim, n = w.shape
    tk = _tile(kdim, 512, 16)
    return pl.pallas_call(
        _cast_kernel,
        grid=(kdim // tk,),
        in_specs=[pl.BlockSpec((tk, n), lambda i: (i, 0))],
        out_specs=pl.BlockSpec((tk, n), lambda i: (i, 0)),
        out_shape=jax.ShapeDtypeStruct((kdim, n), BF16),
        compiler_params=_cparams("parallel"),
        name="weight_to_bf16",
    )(w)


def _ffn_up_kernel(a_ref, w_ref, o_ref):
    o_ref[...] = jnp.square(jnp.maximum(_bdot(a_ref[...], w_ref[...]), 0.0)).astype(o_ref.dtype)


def _ffn_up(a, w_up, tm):
    m, d = a.shape
    dff = w_up.shape[-1]
    tf = _tile(dff, 512, 128)
    return pl.pallas_call(
        _ffn_up_kernel,
        grid=(m // tm, dff // tf),
        in_specs=[pl.BlockSpec((tm, d), lambda i, f: (i, 0)),
                  pl.BlockSpec((None, d, tf), lambda i, f: (0, 0, f))],
        out_specs=pl.BlockSpec((tm, tf), lambda i, f: (i, f)),
        out_shape=jax.ShapeDtypeStruct((m, dff), BF16),
        compiler_params=_cparams("parallel", "arbitrary"),
        name="ffn_up",
    )(a, w_up)


EPILOGUE_ROWS = 128


def _ffn_down_kernel(hid_ref, w_ref, h_ref, g2_ref, ng_ref, y_ref):
    j = pl.program_id(1)
    tn = w_ref.shape[1]
    y_ref[:, pl.ds(pl.multiple_of(j * tn, tn), tn)] = _bdot(hid_ref[...], w_ref[...])

    @pl.when(j == pl.num_programs(1) - 1)
    def _():
        tm = y_ref.shape[0]
        step = min(EPILOGUE_ROWS, tm)
        g_rows = g2_ref.shape[0]
        g_step = max(g_rows * step // tm, 1)

        def body(r, carry):
            rows = pl.ds(pl.multiple_of(r * step, step), step)
            g2 = g2_ref[...] if g_rows == 1 else _mod_rows(
                g2_ref[pl.ds(pl.multiple_of(r * g_step, g_step), g_step), :], step)
            y_ref[rows, :] = h_ref[rows, :] + g2 * (_rms(y_ref[rows, :]) * ng_ref[...])
            return carry

        lax.fori_loop(0, y_ref.shape[0] // step, body, 0)


def _ffn_down(hid, w_down, h, mod, norm_g, rows_per_seq, tm, out_dtype):
    m, d = h.shape
    dff = hid.shape[1]
    tn = _tile(d, 256, 128)
    row = lambda i, j: (i, 0)
    assert out_dtype == F32, "the resident output block holds the f32 pre-norm result"
    return pl.pallas_call(
        _ffn_down_kernel,
        grid=(m // tm, d // tn),
        in_specs=[pl.BlockSpec((tm, dff), row),
                  pl.BlockSpec((dff, tn), lambda i, j: (0, j)),
                  pl.BlockSpec((tm, d), row),
                  _mod_spec(mod, 5, tm, rows_per_seq),
                  pl.BlockSpec((1, d), lambda i, j: (0, 0))],
        out_specs=pl.BlockSpec((tm, d), row),
        out_shape=jax.ShapeDtypeStruct((m, d), out_dtype),
        compiler_params=_cparams("parallel", "arbitrary", vmem_limit=VMEM_LIMIT_FFN_DOWN),
        name="ffn_down",
    )(hid, w_down, h, mod, norm_g)


def _run_group(x3, mod, rows_per_seq_for_mod, conv_buf, s_hg, s_gdn, lb, p, prompt):
    batch, seq, d = x3.shape
    m = batch * seq
    x = x3.reshape(m, d)
    hg_heads, gv_heads = p["hg_heads"], p["gv_heads"]
    gq_heads = p["gq_heads"]
    hg_w = hg_heads * HEAD_DIM
    gq_w = gq_heads * HEAD_DIM
    gv_w = gv_heads * HEAD_DIM
    ch = 2 * gq_w + gv_w
    main = 4 * hg_w + ch + gv_w
    conv0 = 4 * hg_w
    z0 = conv0 + ch
    rows_unit = seq if prompt else m
    tm = _tile(rows_unit, 512, SUBLANES)
    tm_proj = _tile(rows_unit, 2048, SUBLANES)
    tm_wide = _tile(rows_unit, 1024, SUBLANES)

    a1 = _prenorm(x, p["norm_g"][0:1], mod, 1, 0, rows_per_seq_for_mod, tm)
    tn_pref = 512 if tm_proj > 512 else 1024
    proj = _project(a1, p["w_in"], main, tm_proj, _tile(main, tn_pref, 128))
    tail = _project(a1, p["w_tail"], p["w_tail"].shape[0], tm_proj, p["w_tail"].shape[0])
    gates = _project(a1, p["w_gates"], 2 * d, tm_proj, _tile(2 * d, tn_pref, 128))

    if prompt:
        o_hg, s_hg_new = _hgrn_prompt(proj, lb, p["hg_norm_g"], batch, seq, hg_heads, p["hg_dtype"])
        gate_arrays = _gdn_gates(tail, p["a_log"], p["dt_bias"], GDN_CHUNK)
        o_gdn, s_gdn_new = _gdn_prompt(proj, conv0, z0, p["conv_w"], gate_arrays, p["gdn_norm_g"], batch, seq,
                                       gq_heads, gv_heads, p["gdn_dtype"])
        assert seq % SUBLANES == 0
        conv_new = proj.reshape(batch, seq, main)[:, seq - (CONV_W - 1):, conv0:conv0 + ch]
    else:
        o_hg, s_hg_new = _hgrn_sample(proj, lb, p["hg_norm_g"], s_hg, batch, seq, hg_heads)
        gate_arrays = _gdn_gates(tail, p["a_log"], p["dt_bias"], seq)
        qkv, conv_new = _conv_sample(proj, conv0, conv_buf.astype(F32), p["conv_w"], seq, gq_w, ch)
        o_gdn, s_gdn_new = _gdn_sample(qkv, proj, z0, gate_arrays, p["gdn_norm_g"], s_gdn, batch, seq,
                                       gq_heads, gv_heads)

    merged = _merge(o_hg, o_gdn, p["w_out_hg"], p["w_out_gdn"], gates, tm_wide)
    h1, a2 = _outproj(merged, p["w_o_bf16"], x, mod, p["norm_g"][1:3], rows_per_seq_for_mod, tm)
    hid = _ffn_up(a2, p["w_up"], tm_proj)
    y = _ffn_down(hid, p["w_down_bf16"], h1, mod, p["norm_g"][3:4], rows_per_seq_for_mod, tm, x3.dtype)
    return y.reshape(batch, seq, d), conv_new, s_hg_new, s_gdn_new


def kernel(x_prompt, x_sample, state_hgrn, state_gdn, cache_conv, c_prompt, c_sample, lb_logits, w_ada,
           b_ada, norm_g, w_in, conv_w, A_log, dt_bias, hg_norm_g, gdn_norm_g, w_out_hg, w_out_gdn, w_o,
           w_up, w_down):
    depth = w_in.shape[0]
    assert depth == 1, "single-layer trunk"
    bp, tp, d = x_prompt.shape
    bs, ts, _ = x_sample.shape
    hg_heads = state_hgrn.shape[2]
    gv_heads = state_gdn.shape[2]
    ch = cache_conv.shape[-1]
    gq_heads = (ch // HEAD_DIM - gv_heads) // 2
    assert tp >= CONV_W - 1 and ts >= CONV_W - 1
    assert tp % GDN_CHUNK == 0 and tp % HG_CHUNK == 0

    main = 4 * hg_heads * HEAD_DIM + ch + gv_heads * HEAD_DIM
    w_in_t = jnp.swapaxes(w_in[0], 0, 1)
    p = dict(
        hg_heads=hg_heads, gv_heads=gv_heads, gq_heads=gq_heads,
        hg_dtype=state_hgrn.dtype, gdn_dtype=state_gdn.dtype,
        norm_g=norm_g[0].astype(F32), w_in=w_in_t,
        w_tail=w_in_t[main:main + 2 * gv_heads],
        w_gates=w_in_t[main + 2 * gv_heads:],
        conv_w=conv_w.astype(F32), a_log=A_log.astype(F32), dt_bias=dt_bias.astype(F32),
        hg_norm_g=hg_norm_g.astype(F32), gdn_norm_g=gdn_norm_g.astype(F32),
        w_out_hg=w_out_hg, w_out_gdn=w_out_gdn, w_o_bf16=_to_bf16(w_o[0]), w_up=w_up,
        w_down_bf16=_to_bf16(w_down[0]),
    )

    lbs = jnp.cumsum(jax.nn.softmax(lb_logits.astype(F32), axis=0), axis=0)
    lb = lbs[0:1]

    n_c = bp + bs
    pad = (-n_c) % SUBLANES
    c_all = jnp.concatenate([c_prompt.astype(F32), c_sample.astype(F32), jnp.zeros((pad, d), F32)], axis=0)
    mod_all = _ada(c_all, w_ada, b_ada.astype(F32))
    mod_p = mod_all[:bp].reshape(bp * 6, 1, d)
    mod_s = mod_all[bp:bp + bs].reshape(bs, 6, d).transpose(1, 0, 2)

    y_p, conv_p, hg_p, gdn_p = _run_group(x_prompt, mod_p, tp, None, None, None, lb, p, prompt=True)
    y_s, conv_s, hg_s, gdn_s = _run_group(x_sample, mod_s, ts, cache_conv[0], state_hgrn[0], state_gdn[0],
                                          lb, p, prompt=False)
    return (y_p, y_s,
            hg_p[None], gdn_p[None], conv_p.astype(cache_conv.dtype)[None],
            hg_s[None], gdn_s[None], conv_s.astype(cache_conv.dtype)[None])
```

```python
import functools

import jax
import jax.numpy as jnp
from jax import lax
from jax.experimental import pallas as pl
from jax.experimental.pallas import tpu as pltpu

F32 = jnp.float32
BF16 = jnp.bfloat16

HEAD_DIM = 128
HG_CHUNK = 32
GDN_CHUNK = 64
CONV_W = 4
NORM_EPS = 1e-6
SUBLANES = 8
V7X_VMEM_BYTES = 64 * 1024 * 1024
VMEM_LIMIT = V7X_VMEM_BYTES * 3 // 4
VMEM_LIMIT_FFN_DOWN = V7X_VMEM_BYTES * 27 // 32


def _cparams(*sem, vmem_limit=VMEM_LIMIT):
    return pltpu.CompilerParams(dimension_semantics=sem, vmem_limit_bytes=vmem_limit)


def _tile(n, pref, quantum):
    if n <= pref:
        return n
    t = (pref // quantum) * quantum
    while t > quantum and n % t:
        t -= quantum
    assert n % t == 0, (n, pref, quantum)
    return t


def _bdot(a, b):
    return jnp.dot(a.astype(BF16), b.astype(BF16), preferred_element_type=F32)


def _bdot_nt(a, b):
    return lax.dot_general(a.astype(BF16), b.astype(BF16), (((1,), (1,)), ((), ())),
                           preferred_element_type=F32)


def _split3(x):
    hi = x.astype(BF16)
    r = x - hi.astype(F32)
    mid = r.astype(BF16)
    lo = (r - mid.astype(F32)).astype(BF16)
    return hi, mid, lo


def _mask_dot(m_bf, x):
    hi, mid, lo = _split3(x)
    return (jnp.dot(m_bf, hi, preferred_element_type=F32)
            + jnp.dot(m_bf, mid, preferred_element_type=F32)
            + jnp.dot(m_bf, lo, preferred_element_type=F32))


def _sigmoid(x):
    return 1.0 / (1.0 + jnp.exp(-x))


def _silu(x):
    return x * _sigmoid(x)


def _rms(x):
    return x * lax.rsqrt(jnp.mean(x * x, axis=-1, keepdims=True) + NORM_EPS)


def _transpose_pad(x):
    r = x.shape[0]
    if r < HEAD_DIM:
        x = jnp.concatenate([x, jnp.zeros((HEAD_DIM - r, HEAD_DIM), x.dtype)], axis=0)
    return x.T


def _tri_masks(c):
    row = lax.broadcasted_iota(jnp.int32, (c, c), 0)
    col = lax.broadcasted_iota(jnp.int32, (c, c), 1)
    return col <= row, col < row


def _ada_kernel(c_ref, w_ref, b_ref, o_ref):
    cs = _silu(c_ref[...])
    w = w_ref[...]
    c_hi = cs.astype(BF16)
    c_lo = (cs - c_hi.astype(F32)).astype(BF16)
    w_hi = w.astype(BF16)
    w_lo = (w - w_hi.astype(F32)).astype(BF16)
    acc = jnp.dot(c_hi, w_hi, preferred_element_type=F32)
    acc += jnp.dot(c_lo, w_hi, preferred_element_type=F32)
    acc += jnp.dot(c_hi, w_lo, preferred_element_type=F32)
    o_ref[...] = acc + b_ref[...]


def _ada(c_all, w_ada, b_ada):
    m, d = c_all.shape
    n = w_ada.shape[-1]
    tn = _tile(n, 512, 128)
    return pl.pallas_call(
        _ada_kernel,
        grid=(n // tn,),
        in_specs=[pl.BlockSpec((m, d), lambda j: (0, 0)),
                  pl.BlockSpec((None, d, tn), lambda j: (0, 0, j)),
                  pl.BlockSpec((1, tn), lambda j: (0, j))],
        out_specs=pl.BlockSpec((m, tn), lambda j: (0, j)),
        out_shape=jax.ShapeDtypeStruct((m, n), F32),
        compiler_params=_cparams("arbitrary"),
        name="ada_mod",
    )(c_all, w_ada, b_ada)


def _mod_spec(mod, k, tm, rows_per_seq):
    d = mod.shape[-1]
    if mod.shape[1] == 1:
        tiles_per_seq = rows_per_seq // tm
        return pl.BlockSpec((None, 1, d), lambda i, *_: ((i // tiles_per_seq) * 6 + k, 0, 0))
    return pl.BlockSpec((None, tm // rows_per_seq, d), lambda i, *_: (k, i, 0))


def _mod_rows(v, n):
    b = v.shape[0]
    if b == 1 or b == n:
        return v
    rep = n // b
    row = lax.broadcasted_iota(jnp.int32, (n, b), 0)
    col = lax.broadcasted_iota(jnp.int32, (n, b), 1)
    owner = (row >= col * rep) & (row < (col + 1) * rep)
    return _mask_dot(jnp.where(owner, 1.0, 0.0).astype(BF16), v)


def _prenorm_kernel(x_ref, g_ref, sc_ref, sh_ref, o_ref):
    xn = _rms(x_ref[...].astype(F32)) * g_ref[...]
    n = xn.shape[0]
    o_ref[...] = (xn * (1.0 + _mod_rows(sc_ref[...], n)) + _mod_rows(sh_ref[...], n)).astype(o_ref.dtype)


def _prenorm(x, g, mod, k_scale, k_shift, rows_per_seq, tm):
    m, d = x.shape
    return pl.pallas_call(
        _prenorm_kernel,
        grid=(m // tm,),
        in_specs=[pl.BlockSpec((tm, d), lambda i: (i, 0)),
                  pl.BlockSpec((1, d), lambda i: (0, 0)),
                  _mod_spec(mod, k_scale, tm, rows_per_seq),
                  _mod_spec(mod, k_shift, tm, rows_per_seq)],
        out_specs=pl.BlockSpec((tm, d), lambda i: (i, 0)),
        out_shape=jax.ShapeDtypeStruct((m, d), BF16),
        compiler_params=_cparams("parallel"),
        name="prenorm",
    )(x, g, mod, mod)


def _mm_nt_kernel(a_ref, w_ref, o_ref):
    o_ref[...] = _bdot_nt(a_ref[...], w_ref[...])


def _project(a, w_t, n_cols, tm, tn):
    m, kdim = a.shape
    return pl.pallas_call(
        _mm_nt_kernel,
        grid=(m // tm, n_cols // tn),
        in_specs=[pl.BlockSpec((tm, kdim), lambda i, j: (i, 0)),
                  pl.BlockSpec((tn, kdim), lambda i, j: (j, 0))],
        out_specs=pl.BlockSpec((tm, tn), lambda i, j: (i, j)),
        out_shape=jax.ShapeDtypeStruct((m, n_cols), F32),
        compiler_params=_cparams("parallel", "arbitrary"),
        name="in_proj",
    )(a, w_t)


def _hgrn_elementwise(hq, hf, lb):
    q = _silu(hq)
    f = lb + (1.0 - lb) * _sigmoid(hf)
    return q, jnp.log(f), 1.0 - f


def _hgrn_chunk_free(q, g, k, inp, n, tril, tril_bf):
    c = q.shape[0]
    heads = [slice(h * HEAD_DIM, (h + 1) * HEAD_DIM) for h in range(n)]
    big_g = _mask_dot(tril_bf, g)
    g_ref = big_g[c // 2:c // 2 + 1]
    g_last = big_g[c - 1:c]
    qe = q * jnp.exp(big_g - g_ref)
    ke = k * jnp.exp(g_ref - big_g)
    qs = q * jnp.exp(big_g)
    kl = k * jnp.exp(g_last - big_g)
    d_hi, d_mid, d_lo = [p.astype(F32) for p in _split3(jnp.exp(g_last))]
    extra = jnp.concatenate([d_hi, d_mid, d_lo, jnp.zeros((SUBLANES - 3, q.shape[1]), F32)], axis=0)
    a = [jnp.where(tril, _bdot_nt(qe[:, sl], ke[:, sl]), 0.0) for sl in heads]
    o_intra = [_bdot(a[h], inp[:, sl]) for h, sl in enumerate(heads)]
    t = [_transpose_pad(jnp.concatenate([kl[:, sl], extra[:, sl]], axis=0))[:, :c + SUBLANES] for sl in heads]
    piece_rows = lax.broadcasted_iota(jnp.int32, (SUBLANES, HEAD_DIM), 0) < 3
    ones_rows = jnp.where(piece_rows, 1.0, 0.0)
    zeros_c = jnp.zeros((c, HEAD_DIM), F32)
    uds = []
    for h, sl in enumerate(heads):
        rhs = jnp.concatenate([jnp.concatenate([inp[:, sl], zeros_c], axis=1),
                               jnp.concatenate([jnp.zeros((SUBLANES, HEAD_DIM), F32), ones_rows], axis=1)], axis=0)
        uds.append(_bdot(t[h], rhs))
    return o_intra, qs, uds


def _hgrn_chunk_heads(q, g, k, inp, states, tril, tril_bf):
    o_intra, qs, uds = _hgrn_chunk_free(q, g, k, inp, len(states), tril, tril_bf)
    out = []
    for h, s in enumerate(states):
        sl = slice(h * HEAD_DIM, (h + 1) * HEAD_DIM)
        out.append((o_intra[h] + _bdot(qs[:, sl], s), uds[h][:, HEAD_DIM:] * s + uds[h][:, :HEAD_DIM]))
    return out


def _head_out(o, gate, norm_g):
    mean_sq = _lane_sum_bcast(o * o) * (1.0 / HEAD_DIM)
    return o * lax.rsqrt(mean_sq + NORM_EPS) * norm_g * _silu(gate)


HGRN_CHUNKS_PER_PASS = 8


def _hgrn_prompt_kernel(q_ref, f_ref, i_ref, g_ref, lb_ref, ng_ref, o_ref, st_ref,
                        s_scr, oi_scr, qs_scr, ud_scr, *, hb):
    t = pl.program_id(2)
    c = HG_CHUNK
    n_chunks = q_ref.shape[0] // c
    per = HGRN_CHUNKS_PER_PASS if n_chunks % HGRN_CHUNKS_PER_PASS == 0 else 1
    heads = [slice(h * HEAD_DIM, (h + 1) * HEAD_DIM) for h in range(hb)]

    @pl.when(t == 0)
    def _():
        s_scr[...] = jnp.zeros_like(s_scr)

    tril, _ = _tri_masks(c)
    tril_bf = jnp.where(tril, 1.0, 0.0).astype(BF16)
    lb = lb_ref[...]
    ng = ng_ref[...]

    def free_part(it, carry):
        rows = [pl.ds(pl.multiple_of((it * per + j) * c, c), c) for j in range(per)]
        parts = [_hgrn_elementwise(q_ref[r, :], f_ref[r, :], lb) for r in rows]
        cat = lambda xs: jnp.concatenate(xs, axis=1)
        o_intra, qs, uds = _hgrn_chunk_free(cat([p[0] for p in parts]), cat([p[1] for p in parts]),
                                            cat([p[2] for p in parts]), cat([i_ref[r, :] for r in rows]),
                                            per * hb, tril, tril_bf)
        for j, r in enumerate(rows):
            qs_scr[r, :] = qs[:, j * hb * HEAD_DIM:(j + 1) * hb * HEAD_DIM].astype(qs_scr.dtype)
            for h, sl in enumerate(heads):
                oi_scr[r, sl] = o_intra[j * hb + h]
                ud_scr[(it * per + j) * hb + h] = uds[j * hb + h]
        return carry

    lax.fori_loop(0, n_chunks // per, free_part, 0)

    states = [s_scr[h] for h in range(hb)]
    outs = []
    for ci in range(n_chunks):
        rows = pl.ds(ci * c, c)
        outs.append([oi_scr[rows, sl] + _bdot(qs_scr[rows, sl], states[h]) for h, sl in enumerate(heads)])
        for h in range(hb):
            ud = ud_scr[ci * hb + h]
            states[h] = ud[:, HEAD_DIM:] * states[h] + ud[:, :HEAD_DIM]
    for h in range(hb):
        s_scr[h] = states[h]
    for ci in range(n_chunks):
        rows = pl.ds(ci * c, c)
        for h, sl in enumerate(heads):
            o_ref[rows, sl] = _head_out(outs[ci][h], g_ref[rows, sl], ng).astype(o_ref.dtype)

    @pl.when(t == pl.num_programs(2) - 1)
    def _():
        st_ref[...] = s_scr[...].astype(st_ref.dtype)


def _hgrn_prompt(proj, lb, norm_g, batch, seq, heads, state_dtype):
    hb = _tile(heads, 8, 1)
    tt = _tile(seq, 256, HG_CHUNK)
    w = hb * HEAD_DIM
    nhb = heads // hb
    nt = seq // tt

    def col(seg):
        return pl.BlockSpec((tt, w), lambda b, h, t: (b * nt + t, seg * nhb + h))

    return pl.pallas_call(
        functools.partial(_hgrn_prompt_kernel, hb=hb),
        grid=(batch, nhb, nt),
        in_specs=[col(0), col(1), col(2), col(3),
                  pl.BlockSpec((1, w), lambda b, h, t: (0, h)),
                  pl.BlockSpec((1, HEAD_DIM), lambda b, h, t: (0, 0))],
        out_specs=[pl.BlockSpec((tt, w), lambda b, h, t: (b * nt + t, h)),
                   pl.BlockSpec((None, hb, HEAD_DIM, HEAD_DIM), lambda b, h, t: (b, h, 0, 0))],
        out_shape=[jax.ShapeDtypeStruct((batch * seq, heads * HEAD_DIM), BF16),
                   jax.ShapeDtypeStruct((batch, heads, HEAD_DIM, HEAD_DIM), state_dtype)],
        scratch_shapes=[pltpu.VMEM((hb, HEAD_DIM, HEAD_DIM), F32),
                        pltpu.VMEM((tt, w), F32),
                        pltpu.VMEM((tt, w), BF16),
                        pltpu.VMEM(((tt // HG_CHUNK) * hb, HEAD_DIM, 2 * HEAD_DIM), F32)],
        compiler_params=_cparams("parallel", "parallel", "arbitrary"),
        name="hgrn_prompt",
    )(proj, proj, proj, proj, lb, norm_g)


def _hgrn_sample_kernel(q_ref, f_ref, i_ref, g_ref, lb_ref, ng_ref, s0_ref, o_ref, st_ref, *, hb, seq):
    c = SUBLANES
    per_group = c // seq
    tril, _ = _tri_masks(c)
    tril_bf = jnp.where(tril, 1.0, 0.0).astype(BF16)
    lb = lb_ref[...]
    ng = ng_ref[...]
    row = lax.broadcasted_iota(jnp.int32, (c, 1), 0)

    n_groups = q_ref.shape[0] // c
    valid = [(row >= p * seq) & (row < (p + 1) * seq) for p in range(per_group)]
    masked = lambda x: [jnp.where(v, x, 0.0) for v in valid]
    qs, gs, ks, inps, states = [], [], [], [], []
    for gi in range(n_groups):
        rows = pl.ds(gi * c, c)
        q, g, k = _hgrn_elementwise(q_ref[rows, :], f_ref[rows, :], lb)
        qs += masked(q)
        gs += masked(g)
        ks += masked(k)
        inps += [i_ref[rows, :]] * per_group
        states += [s0_ref[gi * per_group + p, h].astype(F32) for p in range(per_group) for h in range(hb)]
    cat = lambda xs: jnp.concatenate(xs, axis=1)
    res = _hgrn_chunk_heads(cat(qs), cat(gs), cat(ks), cat(inps), states, tril, tril_bf)
    for gi in range(n_groups):
        rows = pl.ds(gi * c, c)
        for h in range(hb):
            sl = slice(h * HEAD_DIM, (h + 1) * HEAD_DIM)
            o_all = jnp.zeros((c, HEAD_DIM), F32)
            for p in range(per_group):
                o, s_new = res[(gi * per_group + p) * hb + h]
                st_ref[gi * per_group + p, h] = s_new.astype(st_ref.dtype)
                o_all = jnp.where(valid[p], o, o_all)
            o_ref[rows, sl] = _head_out(o_all, g_ref[rows, sl], ng).astype(o_ref.dtype)


def _hgrn_sample(proj, lb, norm_g, s0, batch, seq, heads):
    assert SUBLANES % seq == 0
    hb = _tile(heads, 4, 1)
    nb = _tile(batch, 16, SUBLANES // seq)
    w = hb * HEAD_DIM
    nhb = heads // hb
    r = nb * seq

    def col(seg):
        return pl.BlockSpec((r, w), lambda b, h: (b, seg * nhb + h))

    st_spec = pl.BlockSpec((nb, hb, HEAD_DIM, HEAD_DIM), lambda b, h: (b, h, 0, 0))
    return pl.pallas_call(
        functools.partial(_hgrn_sample_kernel, hb=hb, seq=seq),
        grid=(batch // nb, nhb),
        in_specs=[col(0), col(1), col(2), col(3),
                  pl.BlockSpec((1, w), lambda b, h: (0, h)),
                  pl.BlockSpec((1, HEAD_DIM), lambda b, h: (0, 0)),
                  st_spec],
        out_specs=[pl.BlockSpec((r, w), lambda b, h: (b, h)), st_spec],
        out_shape=[jax.ShapeDtypeStruct((batch * seq, heads * HEAD_DIM), BF16),
                   jax.ShapeDtypeStruct(s0.shape, s0.dtype)],
        compiler_params=_cparams("parallel", "parallel"),
        name="hgrn_sample",
    )(proj, proj, proj, proj, lb, norm_g, s0)


def _gdn_gates_kernel(x_ref, alog_ref, dtb_ref, beta_ref, gam_ref, gtot_ref, *, nv, chunk):
    x = x_ref[...]
    r = x.shape[0]
    beta_ref[...] = _sigmoid(x[:, :nv])
    z = x[:, nv:] + dtb_ref[...]
    softplus = jnp.maximum(z, 0.0) + jnp.log(1.0 + jnp.exp(-jnp.abs(z)))
    g = -jnp.exp(alog_ref[...]) * softplus
    row = lax.broadcasted_iota(jnp.int32, (r, r), 0)
    col = lax.broadcasted_iota(jnp.int32, (r, r), 1)
    same = (row // chunk) == (col // chunk)
    gam_ref[...] = _mask_dot(jnp.where(same & (col <= row), 1.0, 0.0).astype(BF16), g)
    gtot_ref[...] = _mask_dot(jnp.where(same, 1.0, 0.0).astype(BF16), g)


def _gdn_gates(x, a_log, dt_bias, chunk):
    m, two_nv = x.shape
    nv = two_nv // 2
    tr = _tile(m, 256, max(chunk, SUBLANES))
    out = jax.ShapeDtypeStruct((m, nv), F32)
    return pl.pallas_call(
        functools.partial(_gdn_gates_kernel, nv=nv, chunk=chunk),
        grid=(m // tr,),
        in_specs=[pl.BlockSpec((tr, two_nv), lambda i: (i, 0)),
                  pl.BlockSpec((1, nv), lambda i: (0, 0)),
                  pl.BlockSpec((1, nv), lambda i: (0, 0))],
        out_specs=[pl.BlockSpec((tr, nv), lambda i: (i, 0))] * 3,
        out_shape=[out, out, out],
        compiler_params=_cparams("parallel"),
        name="gdn_gates",
    )(x, a_log, dt_bias)


def _lane_sum_bcast(x):
    hi = x.astype(BF16)
    lo = (x - hi.astype(F32)).astype(BF16)
    return jnp.dot(jnp.concatenate([hi, lo], axis=1), jnp.ones((2 * HEAD_DIM, HEAD_DIM), BF16),
                   preferred_element_type=F32)


def _l2_heads(y, mult):
    parts = []
    for h in range(y.shape[1] // HEAD_DIM):
        yh = y[:, h * HEAD_DIM:(h + 1) * HEAD_DIM]
        parts.append(yh * (lax.rsqrt(_lane_sum_bcast(yh * yh) + NORM_EPS) * mult))
    return jnp.concatenate(parts, axis=1)


def _conv_finish(y, kind_q, kind_k, o_ref, scale):
    y = _silu(y)

    @pl.when(kind_q)
    def _():
        o_ref[...] = _l2_heads(y, scale)

    @pl.when(kind_k)
    def _():
        o_ref[...] = _l2_heads(y, 1.0)

    @pl.when(jnp.logical_not(kind_q | kind_k))
    def _():
        o_ref[...] = y


def _conv_sample_kernel(u_ref, buf_ref, cw_ref, o_ref, cache_ref, *, seq, nq_tiles, scale):
    j = pl.program_id(1)
    u = u_ref[...]
    r = u.shape[0]
    nb = r // seq
    tok = lax.broadcasted_iota(jnp.int32, (r, 1), 0) % seq
    hist = [_mod_rows(buf_ref[slot], r) for slot in range(CONV_W - 1)]
    cw = cw_ref[...]
    y = cw[CONV_W - 1:CONV_W] * u
    for s in range(1, CONV_W):
        tap = pltpu.roll(u, s, axis=0)
        for t in range(min(s, seq)):
            tap = jnp.where(tok == t, hist[t + CONV_W - 1 - s], tap)
        y = y + cw[CONV_W - 1 - s:CONV_W - s] * tap
    _conv_finish(y, j < nq_tiles, (j >= nq_tiles) & (j < 2 * nq_tiles), o_ref, scale)
    seq_i = lax.broadcasted_iota(jnp.int32, (nb, r), 0)
    row_i = lax.broadcasted_iota(jnp.int32, (nb, r), 1)
    for slot in range(CONV_W - 1):
        pick = row_i == seq_i * seq + (seq - (CONV_W - 1) + slot)
        cache_ref[slot] = _mask_dot(jnp.where(pick, 1.0, 0.0).astype(BF16), u).astype(cache_ref.dtype)


def _conv_sample(proj, col0, conv_buf, conv_w, seq, qk_w, ch):
    m = proj.shape[0]
    wc = _tile(qk_w, 512, HEAD_DIM)
    tr = _tile(m, 256, SUBLANES * seq)
    c0 = col0 // wc
    assert col0 % wc == 0 and tr % seq == 0 and seq >= CONV_W - 1
    nseq = tr // seq
    slots = pl.BlockSpec((CONV_W - 1, nseq, wc), lambda i, j: (0, i, j))
    buf = jnp.transpose(conv_buf, (1, 0, 2))
    qkv, cache = pl.pallas_call(
        functools.partial(_conv_sample_kernel, seq=seq, nq_tiles=qk_w // wc,
                          scale=float(HEAD_DIM) ** -0.5),
        grid=(m // tr, ch // wc),
        in_specs=[pl.BlockSpec((tr, wc), lambda i, j: (i, c0 + j)),
                  slots,
                  pl.BlockSpec((None, CONV_W, wc), lambda i, j: (0, 0, j))],
        out_specs=[pl.BlockSpec((tr, wc), lambda i, j: (i, j)), slots],
        out_shape=[jax.ShapeDtypeStruct((m, ch), F32), jax.ShapeDtypeStruct(buf.shape, conv_buf.dtype)],
        compiler_params=_cparams("parallel", "parallel"),
        name="gdn_conv_sample",
    )(proj, buf, conv_w)
    return qkv, jnp.transpose(cache, (1, 0, 2))


GDN_TILE = 64


def _mm_tile(a, b):
    return _bdot(a[:, :GDN_TILE], b)


def _lane_bcast_all(x, sel):
    return _mask_dot_right(x, sel)


def _mask_dot_right(x, m_bf):
    hi, mid, lo = _split3(x)
    return (jnp.dot(hi, m_bf, preferred_element_type=F32)
            + jnp.dot(mid, m_bf, preferred_element_type=F32)
            + jnp.dot(lo, m_bf, preferred_element_type=F32))


def _head_select(hb):
    head = lax.broadcasted_iota(jnp.int32, (hb, hb * HEAD_DIM), 0)
    lane_head = lax.broadcasted_iota(jnp.int32, (hb, hb * HEAD_DIM), 1) // HEAD_DIM
    return jnp.where(head == lane_head, 1.0, 0.0).astype(BF16)


def _unit_lower_inverses(a_list, rows, cols, block):
    eye = jnp.where(rows == cols, 1.0, 0.0)
    pairs = (rows >> 1) == (cols >> 1)
    d = [eye - jnp.where(pairs, a, 0.0) for a in a_list]
    b = 2
    while b < block:
        sh = b.bit_length()
        coupling = ((rows >> sh) == (cols >> sh)) & ((rows >> (sh - 1)) != (cols >> (sh - 1)))
        m1 = [_mm_tile(jnp.where(coupling, a, 0.0), di) for a, di in zip(a_list, d)]
        m2 = [_mm_tile(di, mi) for di, mi in zip(d, m1)]
        d = [di - mi for di, mi in zip(d, m2)]
        b *= 2
    return d


def _gdn_pair_products(q, k):
    k_pad = jnp.concatenate([k, jnp.zeros_like(k)], axis=0)
    kq = _bdot_nt(jnp.concatenate([k, q], axis=0), k_pad)
    return kq[:GDN_TILE], kq[GDN_TILE:]


def _gdn_wy(tiles, rep, rows, cols, incl, strict, block):
    heads = [slice(h * HEAD_DIM, (h + 1) * HEAD_DIM) for h in range(len(tiles[0][2]))]
    prods = [[_gdn_pair_products(q, k) for q, k in zip(qs, ks)] for qs, ks, *_ in tiles]
    decays, a_list = [], []
    for (qs, ks, vs, bc, gc, gr), prod in zip(tiles, prods):
        for h, sl in enumerate(heads):
            decay = jnp.where(incl, jnp.exp(gc[:, sl] - gr[h:h + 1, :]), 0.0)
            decays.append(decay)
            a_list.append(jnp.where(strict, bc[:, sl] * prod[h // rep][0] * decay, 0.0))
    t_inv = _unit_lower_inverses(a_list, rows, cols, block)
    xs = []
    for t, (qs, ks, vs, bc, gc, gr) in enumerate(tiles):
        for h, sl in enumerate(heads):
            beta = bc[:, sl]
            rhs = jnp.concatenate([vs[h] * beta, ks[h // rep] * (beta * jnp.exp(gc[:, sl]))], axis=1)
            xs.append(_mm_tile(t_inv[t * len(heads) + h], rhs))
    out = []
    for t in range(len(tiles)):
        out.append([(xs[t * len(heads) + h][:, :HEAD_DIM], xs[t * len(heads) + h][:, HEAD_DIM:],
                     prods[t][h // rep][1] * decays[t * len(heads) + h]) for h in range(len(heads))])
    return out


def _tile_iotas():
    shape = (GDN_TILE, HEAD_DIM)
    return lax.broadcasted_iota(jnp.int32, shape, 0), lax.broadcasted_iota(jnp.int32, shape, 1)


GDN_CHUNKS_PER_PASS = 4


def _causal_conv(x_scr, cw, r0, n, cols):
    y = None
    for s in range(CONV_W):
        lo = SUBLANES + r0 - s
        term = cw[CONV_W - 1 - s:CONV_W - s, cols] * x_scr[lo:lo + n, cols]
        y = term if y is None else y + term
    return y


def _gdn_prompt_kernel(uq_ref, uk_ref, uv_ref, pq_ref, pk_ref, pv_ref, cwq_ref, cwk_ref, cwv_ref,
                       z_ref, bc_ref, gc_ref, gt_ref, gr_ref, ng_ref,
                       o_ref, st_ref, s_scr, u_scr, w_scr, a_scr, xq_scr, xk_scr, xv_scr, q_scr, k_scr,
                       *, hb, rep, scale):
    t = pl.program_id(2)
    c = GDN_CHUNK
    tt = uq_ref.shape[0]
    n_chunks = tt // c

    @pl.when(t == 0)
    def _():
        s_scr[...] = jnp.zeros_like(s_scr)

    keep = jnp.where(t == 0, 0.0, 1.0)
    for x_scr, prev_ref, cur_ref in ((xq_scr, pq_ref, uq_ref), (xk_scr, pk_ref, uk_ref), (xv_scr, pv_ref, uv_ref)):
        x_scr[0:SUBLANES, :] = prev_ref[...] * keep
        x_scr[SUBLANES:SUBLANES + tt, :] = cur_ref[...]

    rows_i, cols_i = _tile_iotas()
    incl = cols_i <= rows_i
    strict = cols_i < rows_i
    ng = ng_ref[...]
    sel = _head_select(hb)
    cwq, cwk, cwv = cwq_ref[...], cwk_ref[...], cwv_ref[...]
    heads = [slice(h * HEAD_DIM, (h + 1) * HEAD_DIM) for h in range(hb)]

    def chunk_operands(ci):
        rows = pl.ds(ci * c, c)
        qs = [_l2_heads(_silu(_causal_conv(xq_scr, cwq, ci * c, c, heads[i])), scale) for i in range(hb // rep)]
        ks = [_l2_heads(_silu(_causal_conv(xk_scr, cwk, ci * c, c, heads[i])), 1.0) for i in range(hb // rep)]
        vs = [_silu(_causal_conv(xv_scr, cwv, ci * c, c, heads[h])) for h in range(hb)]
        for i in range(hb // rep):
            q_scr[rows, heads[i]] = qs[i]
            k_scr[rows, heads[i]] = ks[i]
        return (qs, ks, vs, _lane_bcast_all(bc_ref[rows, :], sel),
                _lane_bcast_all(gc_ref[rows, :], sel), gr_ref[ci])

    per = GDN_CHUNKS_PER_PASS if n_chunks % GDN_CHUNKS_PER_PASS == 0 else 1
    for c0 in range(0, n_chunks, per):
        wy = _gdn_wy([chunk_operands(ci) for ci in range(c0, c0 + per)], rep, rows_i, cols_i, incl, strict, c)
        for j in range(per):
            rows = pl.ds((c0 + j) * c, c)
            for h, (u, w, attn) in enumerate(wy[j]):
                sl = slice(h * HEAD_DIM, (h + 1) * HEAD_DIM)
                u_scr[rows, sl] = u
                w_scr[rows, sl] = w.astype(w_scr.dtype)
                a_scr[rows, sl] = attn.astype(a_scr.dtype)

    pairs = range(hb // rep)
    s_pairs = [s_scr[pr] for pr in pairs]
    outs = []
    for ci in range(n_chunks):
        rows = pl.ds(ci * c, c)
        gc = _lane_bcast_all(gc_ref[rows, :], sel)
        gt = _lane_bcast_all(gt_ref[rows, :], sel)
        qs = [_bdot(q_scr[rows, heads[pr]], s_pairs[pr]) for pr in pairs]
        v_new = [u_scr[rows, sl] - _bdot(w_scr[rows, sl], s_pairs[h // rep][:, heads[h % rep]])
                 for h, sl in enumerate(heads)]
        outs.append([qs[h // rep][:, heads[h % rep]] * jnp.exp(gc[:, sl])
                     + _bdot(a_scr[rows, sl][:, :c], v_new[h]) for h, sl in enumerate(heads)])
        for pr in pairs:
            k_t = _transpose_pad(k_scr[rows, heads[pr]])[:, :c]
            mine = range(pr * rep, (pr + 1) * rep)
            v_scaled = jnp.concatenate([v_new[h] * jnp.exp(gt[:, heads[h]] - gc[:, heads[h]]) for h in mine], axis=1)
            decays = jnp.concatenate([jnp.exp(gt[0:1, heads[h]]) for h in mine], axis=1)
            s_pairs[pr] = decays * s_pairs[pr] + _bdot(k_t, v_scaled)
    for pr in pairs:
        s_scr[pr] = s_pairs[pr]
    for ci in range(n_chunks):
        rows = pl.ds(ci * c, c)
        for h, sl in enumerate(heads):
            o_ref[rows, sl] = _head_out(outs[ci][h], z_ref[rows, sl], ng).astype(o_ref.dtype)

    @pl.when(t == pl.num_programs(2) - 1)
    def _():
        for h in range(hb):
            st_ref[h] = s_scr[h // rep][:, (h % rep) * HEAD_DIM:(h % rep + 1) * HEAD_DIM].astype(st_ref.dtype)


def _gdn_layouts(beta, gam, gtot, nv, hb):
    m = beta.shape[0]
    nhb = nv // hb
    as_cols = lambda a: a.reshape(m, nhb, hb).transpose(1, 0, 2)
    g_rows = gam.reshape(m // GDN_TILE, GDN_TILE, nhb, hb).transpose(2, 0, 3, 1)
    g_rows = jnp.pad(g_rows, ((0, 0), (0, 0), (0, 0), (0, HEAD_DIM - GDN_TILE)))
    return as_cols(beta), as_cols(gam), as_cols(gtot), g_rows


def _gdn_prompt(proj, conv_col0, z_col0, conv_w, gate_arrays, norm_g, batch, seq, qk_heads, v_heads,
                state_dtype):
    rep = v_heads // qk_heads
    hb = _tile(v_heads, 8, rep)
    tt = _tile(seq, 256, GDN_CHUNK)
    nt = seq // tt
    nhb = v_heads // hb
    wq = (hb // rep) * HEAD_DIM
    wv = hb * HEAD_DIM
    qk_w = qk_heads * HEAD_DIM
    bc, gc, gt, gr = _gdn_layouts(*gate_arrays, v_heads, hb)
    assert conv_col0 % wv == 0 and z_col0 % wv == 0 and GDN_CHUNK == GDN_TILE and tt % SUBLANES == 0
    q0 = conv_col0 // wq
    k0 = (conv_col0 + qk_w) // wq
    v0 = (conv_col0 + 2 * qk_w) // wv
    z0 = z_col0 // wv
    kc0 = qk_w // wq
    vc0 = 2 * qk_w // wv
    col_spec = pl.BlockSpec((None, tt, hb), lambda b, h, t: (h, b * nt + t, 0))
    sub = tt // SUBLANES

    def cur(width, c0):
        return pl.BlockSpec((tt, width), lambda b, h, t: (b * nt + t, c0 + h))

    def prev(width, c0):
        return pl.BlockSpec((SUBLANES, width), lambda b, h, t: (jnp.maximum((b * nt + t) * sub - 1, 0), c0 + h))

    def taps(width, c0):
        return pl.BlockSpec((None, CONV_W, width), lambda b, h, t: (0, 0, c0 + h))

    return pl.pallas_call(
        functools.partial(_gdn_prompt_kernel, hb=hb, rep=rep, scale=float(HEAD_DIM) ** -0.5),
        grid=(batch, nhb, nt),
        in_specs=[cur(wq, q0), cur(wq, k0), cur(wv, v0),
                  prev(wq, q0), prev(wq, k0), prev(wv, v0),
                  taps(wq, 0), taps(wq, kc0), taps(wv, vc0),
                  cur(wv, z0),
                  col_spec, col_spec, col_spec,
                  pl.BlockSpec((None, tt // GDN_TILE, hb, HEAD_DIM), lambda b, h, t: (h, b * nt + t, 0, 0)),
                  pl.BlockSpec((1, HEAD_DIM), lambda b, h, t: (0, 0))],
        out_specs=[pl.BlockSpec((tt, wv), lambda b, h, t: (b * nt + t, h)),
                   pl.BlockSpec((None, hb, HEAD_DIM, HEAD_DIM), lambda b, h, t: (b, h, 0, 0))],
        out_shape=[jax.ShapeDtypeStruct((batch * seq, v_heads * HEAD_DIM), BF16),
                   jax.ShapeDtypeStruct((batch, v_heads, HEAD_DIM, HEAD_DIM), state_dtype)],
        scratch_shapes=[pltpu.VMEM((hb // rep, HEAD_DIM, rep * HEAD_DIM), F32),
                        pltpu.VMEM((tt, wv), F32),
                        pltpu.VMEM((tt, wv), BF16),
                        pltpu.VMEM((tt, wv), BF16),
                        pltpu.VMEM((tt + SUBLANES, wq), F32),
                        pltpu.VMEM((tt + SUBLANES, wq), F32),
                        pltpu.VMEM((tt + SUBLANES, wv), F32),
                        pltpu.VMEM((tt, wq), F32),
                        pltpu.VMEM((tt, wq), F32)],
        compiler_params=_cparams("parallel", "parallel", "arbitrary"),
        name="gdn_prompt",
    )(proj, proj, proj, proj, proj, proj, conv_w, conv_w, conv_w, proj, bc, gc, gt, gr, norm_g)


def _gdn_sample_kernel(q_ref, k_ref, v_ref, z_ref, bc_ref, gc_ref, gt_ref, gr_ref, ng_ref, s0_ref,
                       o_ref, st_ref, *, hb, rep, seq):
    c = GDN_TILE
    nseq = c // seq
    per_group = SUBLANES // seq
    lsh = seq.bit_length() - 1
    rows_i, cols_i = _tile_iotas()
    same = (rows_i >> lsh) == (cols_i >> lsh)
    incl = (cols_i <= rows_i) & same
    strict = (cols_i < rows_i) & same
    row8 = lax.broadcasted_iota(jnp.int32, (SUBLANES, 1), 0)
    lane_seq = lax.broadcasted_iota(jnp.int32, (1, HEAD_DIM), 1) >> lsh
    ng = ng_ref[...]
    sel = _head_select(hb)
    bc = _lane_bcast_all(bc_ref[...], sel)
    gc = _lane_bcast_all(gc_ref[...], sel)
    gt = _lane_bcast_all(gt_ref[...], sel)
    gr = gr_ref[0]
    head = lambda ref, i: ref[:, i * HEAD_DIM:(i + 1) * HEAD_DIM]
    qs = [head(q_ref, i) for i in range(hb // rep)]
    ks = [head(k_ref, i) for i in range(hb // rep)]
    wy = _gdn_wy([(qs, ks, [head(v_ref, h) for h in range(hb)], bc, gc, gr)], rep,
                 rows_i, cols_i, incl, strict, seq)[0]
    k_t_seq = []
    for k in ks:
        k_t = _transpose_pad(k)
        k_t_seq.append(jnp.concatenate([jnp.where(lane_seq == b, k_t, 0.0)[:, :c] for b in range(nseq)],
                                       axis=0).astype(BF16))
    qe = [qs[h // rep] * jnp.exp(head(gc, h)) for h in range(hb)]
    ws_parts = [[] for _ in range(hb)]
    qs_parts = [[] for _ in range(hb)]
    for g in range(c // SUBLANES):
        gsl = slice(g * SUBLANES, (g + 1) * SUBLANES)
        for h in range(hb):
            acc = jnp.zeros((2 * SUBLANES, HEAD_DIM), F32)
            for p in range(per_group):
                valid = (row8 >= p * seq) & (row8 < (p + 1) * seq)
                lhs = jnp.concatenate([jnp.where(valid, wy[h][1][gsl], 0.0),
                                       jnp.where(valid, qe[h][gsl], 0.0)], axis=0)
                acc = acc + _bdot(lhs, s0_ref[g * per_group + p, h])
            ws_parts[h].append(acc[:SUBLANES])
            qs_parts[h].append(acc[SUBLANES:])
    v_new = [wy[h][0] - jnp.concatenate(ws_parts[h], axis=0) for h in range(hb)]
    outs = [jnp.concatenate(qs_parts[h], axis=0) + _bdot(wy[h][2][:, :c], v_new[h]) for h in range(hb)]
    upds = [jnp.dot(k_t_seq[h // rep], (v_new[h] * jnp.exp(head(gt, h) - head(gc, h))).astype(BF16),
                    preferred_element_type=F32) for h in range(hb)]
    for h in range(hb):
        o_ref[:, h * HEAD_DIM:(h + 1) * HEAD_DIM] = _head_out(outs[h], head(z_ref, h), ng).astype(o_ref.dtype)
        decay = jnp.exp(head(gt, h))
        for b in range(nseq):
            st_ref[b, h] = (decay[b * seq:b * seq + 1] * s0_ref[b, h].astype(F32)
                            + upds[h][b * HEAD_DIM:(b + 1) * HEAD_DIM]).astype(st_ref.dtype)


def _gdn_sample(qkv, proj, z_col0, gate_arrays, norm_g, s0, batch, seq, qk_heads, v_heads):
    rep = v_heads // qk_heads
    hb = _tile(v_heads, 4, rep)
    assert GDN_TILE % seq == 0 and SUBLANES % seq == 0 and seq & (seq - 1) == 0
    nb = GDN_TILE // seq
    assert batch % nb == 0
    r = GDN_TILE
    nhb = v_heads // hb
    wq = (hb // rep) * HEAD_DIM
    wv = hb * HEAD_DIM
    qk_w = qk_heads * HEAD_DIM
    bc, gc, gt, gr = _gdn_layouts(*gate_arrays, v_heads, hb)
    k0 = qk_w // wq
    v0 = 2 * qk_w // wv
    z0 = z_col0 // wv
    assert z_col0 % wv == 0
    st_spec = pl.BlockSpec((nb, hb, HEAD_DIM, HEAD_DIM), lambda b, h: (b, h, 0, 0))
    col_spec = pl.BlockSpec((None, r, hb), lambda b, h: (h, b, 0))
    return pl.pallas_call(
        functools.partial(_gdn_sample_kernel, hb=hb, rep=rep, seq=seq),
        grid=(batch // nb, nhb),
        in_specs=[pl.BlockSpec((r, wq), lambda b, h: (b, h)),
                  pl.BlockSpec((r, wq), lambda b, h: (b, k0 + h)),
                  pl.BlockSpec((r, wv), lambda b, h: (b, v0 + h)),
                  pl.BlockSpec((r, wv), lambda b, h: (b, z0 + h)),
                  col_spec, col_spec, col_spec,
                  pl.BlockSpec((None, 1, hb, HEAD_DIM), lambda b, h: (h, b, 0, 0)),
                  pl.BlockSpec((1, HEAD_DIM), lambda b, h: (0, 0)),
                  st_spec],
        out_specs=[pl.BlockSpec((r, wv), lambda b, h: (b, h)), st_spec],
        out_shape=[jax.ShapeDtypeStruct((batch * seq, v_heads * HEAD_DIM), BF16),
                   jax.ShapeDtypeStruct(s0.shape, s0.dtype)],
        compiler_params=_cparams("parallel", "parallel"),
        name="gdn_sample",
    )(qkv, qkv, qkv, proj, bc, gc, gt, gr, norm_g, s0)


def _merge_kernel(oh_ref, og_ref, wa_ref, wb_ref, ga_ref, gb_ref, o_ref):
    ya = _bdot(oh_ref[...], wa_ref[...])
    yb = _bdot(og_ref[...], wb_ref[...])
    o_ref[...] = (_sigmoid(ga_ref[...]) * ya + _sigmoid(gb_ref[...]) * yb).astype(o_ref.dtype)


def _merge(o_hg, o_gdn, w_a, w_b, gates, tm):
    m, ka = o_hg.shape
    kb = o_gdn.shape[1]
    d = w_a.shape[-1]
    tn = _tile(d, 256, 128)
    nj = d // tn
    return pl.pallas_call(
        _merge_kernel,
        grid=(m // tm, nj),
        in_specs=[pl.BlockSpec((tm, ka), lambda i, j: (i, 0)),
                  pl.BlockSpec((tm, kb), lambda i, j: (i, 0)),
                  pl.BlockSpec((None, ka, tn), lambda i, j: (0, 0, j)),
                  pl.BlockSpec((None, kb, tn), lambda i, j: (0, 0, j)),
                  pl.BlockSpec((tm, tn), lambda i, j: (i, j)),
                  pl.BlockSpec((tm, tn), lambda i, j: (i, nj + j))],
        out_specs=pl.BlockSpec((tm, tn), lambda i, j: (i, j)),
        out_shape=jax.ShapeDtypeStruct((m, d), BF16),
        compiler_params=_cparams("parallel", "arbitrary"),
        name="merge",
    )(o_hg, o_gdn, w_a, w_b, gates, gates)


def _outproj_kernel(m_ref, w_ref, x_ref, g1_ref, sc_ref, sh_ref, ng_ref, h_ref, a_ref):
    ng = ng_ref[...]
    mix = _bdot(m_ref[...], w_ref[...])
    n = mix.shape[0]
    h = x_ref[...].astype(F32) + _mod_rows(g1_ref[...], n) * (_rms(mix) * ng[0:1])
    h_ref[...] = h
    a_ref[...] = ((_rms(h) * ng[1:2]) * (1.0 + _mod_rows(sc_ref[...], n))
                  + _mod_rows(sh_ref[...], n)).astype(a_ref.dtype)


def _outproj(merged, w_o_bf16, x, mod, norm_g2, rows_per_seq, tm):
    m, d = x.shape
    kdim = merged.shape[1]
    row = lambda i: (i, 0)
    return pl.pallas_call(
        _outproj_kernel,
        grid=(m // tm,),
        in_specs=[pl.BlockSpec((tm, kdim), row),
                  pl.BlockSpec((kdim, d), lambda i: (0, 0), pipeline_mode=pl.Buffered(1)),
                  pl.BlockSpec((tm, d), row),
                  _mod_spec(mod, 2, tm, rows_per_seq),
                  _mod_spec(mod, 4, tm, rows_per_seq),
                  _mod_spec(mod, 3, tm, rows_per_seq),
                  pl.BlockSpec((2, d), lambda i: (0, 0))],
        out_specs=[pl.BlockSpec((tm, d), row), pl.BlockSpec((tm, d), row)],
        out_shape=[jax.ShapeDtypeStruct((m, d), F32), jax.ShapeDtypeStruct((m, d), BF16)],
        compiler_params=_cparams("parallel"),
        name="out_proj",
    )(merged, w_o_bf16, x, mod, mod, mod, norm_g2)


def _cast_kernel(x_ref, o_ref):
    o_ref[...] = x_ref[...].astype(o_ref.dtype)


def _to_bf16(w):
    kdim, n = w.shape
    tk = _tile(kdim, 512, 16)
    return pl.pallas_call(
        _cast_kernel,
        grid=(kdim // tk,),
        in_specs=[pl.BlockSpec((tk, n), lambda i: (i, 0))],
        out_specs=pl.BlockSpec((tk, n), lambda i: (i, 0)),
        out_shape=jax.ShapeDtypeStruct((kdim, n), BF16),
        compiler_params=_cparams("parallel"),
        name="weight_to_bf16",
    )(w)


def _ffn_up_kernel(a_ref, w_ref, o_ref):
    o_ref[...] = jnp.square(jnp.maximum(_bdot(a_ref[...], w_ref[...]), 0.0)).astype(o_ref.dtype)


def _ffn_up(a, w_up, tm):
    m, d = a.shape
    dff = w_up.shape[-1]
    tf = _tile(dff, 512, 128)
    return pl.pallas_call(
        _ffn_up_kernel,
        grid=(m // tm, dff // tf),
        in_specs=[pl.BlockSpec((tm, d), lambda i, f: (i, 0)),
                  pl.BlockSpec((None, d, tf), lambda i, f: (0, 0, f))],
        out_specs=pl.BlockSpec((tm, tf), lambda i, f: (i, f)),
        out_shape=jax.ShapeDtypeStruct((m, dff), BF16),
        compiler_params=_cparams("parallel", "arbitrary"),
        name="ffn_up",
    )(a, w_up)


EPILOGUE_ROWS = 128


def _ffn_down_kernel(hid_ref, w_ref, h_ref, g2_ref, ng_ref, y_ref):
    j = pl.program_id(1)
    tn = w_ref.shape[1]
    y_ref[:, pl.ds(pl.multiple_of(j * tn, tn), tn)] = _bdot(hid_ref[...], w_ref[...])

    @pl.when(j == pl.num_programs(1) - 1)
    def _():
        tm = y_ref.shape[0]
        step = min(EPILOGUE_ROWS, tm)
        g_rows = g2_ref.shape[0]
        g_step = max(g_rows * step // tm, 1)

        def body(r, carry):
            rows = pl.ds(pl.multiple_of(r * step, step), step)
            g2 = g2_ref[...] if g_rows == 1 else _mod_rows(
                g2_ref[pl.ds(pl.multiple_of(r * g_step, g_step), g_step), :], step)
            y_ref[rows, :] = h_ref[rows, :] + g2 * (_rms(y_ref[rows, :]) * ng_ref[...])
            return carry

        lax.fori_loop(0, y_ref.shape[0] // step, body, 0)


def _ffn_down(hid, w_down, h, mod, norm_g, rows_per_seq, tm, out_dtype):
    m, d = h.shape
    dff = hid.shape[1]
    tn = _tile(d, 512, 128)
    row = lambda i, j: (i, 0)
    assert out_dtype == F32, "the resident output block holds the f32 pre-norm result"
    return pl.pallas_call(
        _ffn_down_kernel,
        grid=(m // tm, d // tn),
        in_specs=[pl.BlockSpec((tm, dff), row),
                  pl.BlockSpec((dff, tn), lambda i, j: (0, j)),
                  pl.BlockSpec((tm, d), row),
                  _mod_spec(mod, 5, tm, rows_per_seq),
                  pl.BlockSpec((1, d), lambda i, j: (0, 0))],
        out_specs=pl.BlockSpec((tm, d), row),
        out_shape=jax.ShapeDtypeStruct((m, d), out_dtype),
        compiler_params=_cparams("parallel", "arbitrary", vmem_limit=VMEM_LIMIT_FFN_DOWN),
        name="ffn_down",
    )(hid, w_down, h, mod, norm_g)


def _run_group(x3, mod, rows_per_seq_for_mod, conv_buf, s_hg, s_gdn, lb, p, prompt):
    batch, seq, d = x3.shape
    m = batch * seq
    x = x3.reshape(m, d)
    hg_heads, gv_heads = p["hg_heads"], p["gv_heads"]
    gq_heads = p["gq_heads"]
    hg_w = hg_heads * HEAD_DIM
    gq_w = gq_heads * HEAD_DIM
    gv_w = gv_heads * HEAD_DIM
    ch = 2 * gq_w + gv_w
    main = 4 * hg_w + ch + gv_w
    conv0 = 4 * hg_w
    z0 = conv0 + ch
    rows_unit = seq if prompt else m
    tm = _tile(rows_unit, 512, SUBLANES)
    tm_proj = _tile(rows_unit, 2048, SUBLANES)
    tm_wide = _tile(rows_unit, 1024, SUBLANES)

    a1 = _prenorm(x, p["norm_g"][0:1], mod, 1, 0, rows_per_seq_for_mod, tm)
    tn_pref = 512 if tm_proj > 512 else 1024
    proj = _project(a1, p["w_in"], main, tm_proj, _tile(main, tn_pref, 128))
    tail = _project(a1, p["w_tail"], p["w_tail"].shape[0], tm_proj, p["w_tail"].shape[0])
    gates = _project(a1, p["w_gates"], 2 * d, tm_proj, _tile(2 * d, tn_pref, 128))

    if prompt:
        o_hg, s_hg_new = _hgrn_prompt(proj, lb, p["hg_norm_g"], batch, seq, hg_heads, p["hg_dtype"])
        gate_arrays = _gdn_gates(tail, p["a_log"], p["dt_bias"], GDN_CHUNK)
        o_gdn, s_gdn_new = _gdn_prompt(proj, conv0, z0, p["conv_w"], gate_arrays, p["gdn_norm_g"], batch, seq,
                                       gq_heads, gv_heads, p["gdn_dtype"])
        assert seq % SUBLANES == 0
        conv_new = proj.reshape(batch, seq, main)[:, seq - (CONV_W - 1):, conv0:conv0 + ch]
    else:
        o_hg, s_hg_new = _hgrn_sample(proj, lb, p["hg_norm_g"], s_hg, batch, seq, hg_heads)
        gate_arrays = _gdn_gates(tail, p["a_log"], p["dt_bias"], seq)
        qkv, conv_new = _conv_sample(proj, conv0, conv_buf.astype(F32), p["conv_w"], seq, gq_w, ch)
        o_gdn, s_gdn_new = _gdn_sample(qkv, proj, z0, gate_arrays, p["gdn_norm_g"], s_gdn, batch, seq,
                                       gq_heads, gv_heads)

    merged = _merge(o_hg, o_gdn, p["w_out_hg"], p["w_out_gdn"], gates, tm_wide)
    h1, a2 = _outproj(merged, p["w_o_bf16"], x, mod, p["norm_g"][1:3], rows_per_seq_for_mod, tm)
    hid = _ffn_up(a2, p["w_up"], tm_proj)
    y = _ffn_down(hid, p["w_down_bf16"], h1, mod, p["norm_g"][3:4], rows_per_seq_for_mod, tm, x3.dtype)
    return y.reshape(batch, seq, d), conv_new, s_hg_new, s_gdn_new


def kernel(x_prompt, x_sample, state_hgrn, state_gdn, cache_conv, c_prompt, c_sample, lb_logits, w_ada,
           b_ada, norm_g, w_in, conv_w, A_log, dt_bias, hg_norm_g, gdn_norm_g, w_out_hg, w_out_gdn, w_o,
           w_up, w_down):
    depth = w_in.shape[0]
    assert depth == 1, "single-layer trunk"
    bp, tp, d = x_prompt.shape
    bs, ts, _ = x_sample.shape
    hg_heads = state_hgrn.shape[2]
    gv_heads = state_gdn.shape[2]
    ch = cache_conv.shape[-1]
    gq_heads = (ch // HEAD_DIM - gv_heads) // 2
    assert tp >= CONV_W - 1 and ts >= CONV_W - 1
    assert tp % GDN_CHUNK == 0 and tp % HG_CHUNK == 0

    main = 4 * hg_heads * HEAD_DIM + ch + gv_heads * HEAD_DIM
    w_in_t = jnp.swapaxes(w_in[0], 0, 1)
    p = dict(
        hg_heads=hg_heads, gv_heads=gv_heads, gq_heads=gq_heads,
        hg_dtype=state_hgrn.dtype, gdn_dtype=state_gdn.dtype,
        norm_g=norm_g[0].astype(F32), w_in=w_in_t,
        w_tail=w_in_t[main:main + 2 * gv_heads],
        w_gates=w_in_t[main + 2 * gv_heads:],
        conv_w=conv_w.astype(F32), a_log=A_log.astype(F32), dt_bias=dt_bias.astype(F32),
        hg_norm_g=hg_norm_g.astype(F32), gdn_norm_g=gdn_norm_g.astype(F32),
        w_out_hg=w_out_hg, w_out_gdn=w_out_gdn, w_o_bf16=_to_bf16(w_o[0]), w_up=w_up,
        w_down_bf16=_to_bf16(w_down[0]),
    )

    lbs = jnp.cumsum(jax.nn.softmax(lb_logits.astype(F32), axis=0), axis=0)
    lb = lbs[0:1]

    n_c = bp + bs
    pad = (-n_c) % SUBLANES
    c_all = jnp.concatenate([c_prompt.astype(F32), c_sample.astype(F32), jnp.zeros((pad, d), F32)], axis=0)
    mod_all = _ada(c_all, w_ada, b_ada.astype(F32))
    mod_p = mod_all[:bp].reshape(bp * 6, 1, d)
    mod_s = mod_all[bp:bp + bs].reshape(bs, 6, d).transpose(1, 0, 2)

    y_p, conv_p, hg_p, gdn_p = _run_group(x_prompt, mod_p, tp, None, None, None, lb, p, prompt=True)
    y_s, conv_s, hg_s, gdn_s = _run_group(x_sample, mod_s, ts, cache_conv[0], state_hgrn[0], state_gdn[0],
                                          lb, p, prompt=False)
    return (y_p, y_s,
            hg_p[None], gdn_p[None], conv_p.astype(cache_conv.dtype)[None],
            hg_s[None], gdn_s[None], conv_s.astype(cache_conv.dtype)[None])
```

```python
import functools

import jax
import jax.numpy as jnp
from jax import lax
from jax.experimental import pallas as pl
from jax.experimental.pallas import tpu as pltpu

F32 = jnp.float32
BF16 = jnp.bfloat16

HEAD_DIM = 128
HG_CHUNK = 32
GDN_CHUNK = 64
CONV_W = 4
NORM_EPS = 1e-6
SUBLANES = 8
V7X_VMEM_BYTES = 64 * 1024 * 1024
VMEM_LIMIT = V7X_VMEM_BYTES * 3 // 4
VMEM_LIMIT_FFN_DOWN = V7X_VMEM_BYTES * 27 // 32


def _cparams(*sem, vmem_limit=VMEM_LIMIT):
    return pltpu.CompilerParams(dimension_semantics=sem, vmem_limit_bytes=vmem_limit)


def _tile(n, pref, quantum):
    if n <= pref:
        return n
    t = (pref // quantum) * quantum
    while t > quantum and n % t:
        t -= quantum
    assert n % t == 0, (n, pref, quantum)
    return t


def _bdot(a, b):
    return jnp.dot(a.astype(BF16), b.astype(BF16), preferred_element_type=F32)


def _bdot_nt(a, b):
    return lax.dot_general(a.astype(BF16), b.astype(BF16), (((1,), (1,)), ((), ())),
                           preferred_element_type=F32)


def _split3(x):
    hi = x.astype(BF16)
    r = x - hi.astype(F32)
    mid = r.astype(BF16)
    lo = (r - mid.astype(F32)).astype(BF16)
    return hi, mid, lo


def _mask_dot(m_bf, x):
    hi, mid, lo = _split3(x)
    return (jnp.dot(m_bf, hi, preferred_element_type=F32)
            + jnp.dot(m_bf, mid, preferred_element_type=F32)
            + jnp.dot(m_bf, lo, preferred_element_type=F32))


def _sigmoid(x):
    return 1.0 / (1.0 + jnp.exp(-x))


def _silu(x):
    return x * _sigmoid(x)


def _rms(x):
    return x * lax.rsqrt(jnp.mean(x * x, axis=-1, keepdims=True) + NORM_EPS)


def _transpose_pad(x):
    r = x.shape[0]
    if r < HEAD_DIM:
        x = jnp.concatenate([x, jnp.zeros((HEAD_DIM - r, HEAD_DIM), x.dtype)], axis=0)
    return x.T


def _tri_masks(c):
    row = lax.broadcasted_iota(jnp.int32, (c, c), 0)
    col = lax.broadcasted_iota(jnp.int32, (c, c), 1)
    return col <= row, col < row


def _ada_kernel(c_ref, w_ref, b_ref, o_ref):
    cs = _silu(c_ref[...])
    w = w_ref[...]
    c_hi = cs.astype(BF16)
    c_lo = (cs - c_hi.astype(F32)).astype(BF16)
    w_hi = w.astype(BF16)
    w_lo = (w - w_hi.astype(F32)).astype(BF16)
    acc = jnp.dot(c_hi, w_hi, preferred_element_type=F32)
    acc += jnp.dot(c_lo, w_hi, preferred_element_type=F32)
    acc += jnp.dot(c_hi, w_lo, preferred_element_type=F32)
    o_ref[...] = acc + b_ref[...]


def _ada(c_all, w_ada, b_ada):
    m, d = c_all.shape
    n = w_ada.shape[-1]
    tn = _tile(n, 512, 128)
    return pl.pallas_call(
        _ada_kernel,
        grid=(n // tn,),
        in_specs=[pl.BlockSpec((m, d), lambda j: (0, 0)),
                  pl.BlockSpec((None, d, tn), lambda j: (0, 0, j)),
                  pl.BlockSpec((1, tn), lambda j: (0, j))],
        out_specs=pl.BlockSpec((m, tn), lambda j: (0, j)),
        out_shape=jax.ShapeDtypeStruct((m, n), F32),
        compiler_params=_cparams("arbitrary"),
        name="ada_mod",
    )(c_all, w_ada, b_ada)


def _mod_spec(mod, k, tm, rows_per_seq):
    d = mod.shape[-1]
    if mod.shape[1] == 1:
        tiles_per_seq = rows_per_seq // tm
        return pl.BlockSpec((None, 1, d), lambda i, *_: ((i // tiles_per_seq) * 6 + k, 0, 0))
    return pl.BlockSpec((None, tm // rows_per_seq, d), lambda i, *_: (k, i, 0))


def _mod_rows(v, n):
    b = v.shape[0]
    if b == 1 or b == n:
        return v
    rep = n // b
    row = lax.broadcasted_iota(jnp.int32, (n, b), 0)
    col = lax.broadcasted_iota(jnp.int32, (n, b), 1)
    owner = (row >= col * rep) & (row < (col + 1) * rep)
    return _mask_dot(jnp.where(owner, 1.0, 0.0).astype(BF16), v)


def _prenorm_kernel(x_ref, g_ref, sc_ref, sh_ref, o_ref):
    xn = _rms(x_ref[...].astype(F32)) * g_ref[...]
    n = xn.shape[0]
    o_ref[...] = (xn * (1.0 + _mod_rows(sc_ref[...], n)) + _mod_rows(sh_ref[...], n)).astype(o_ref.dtype)


def _prenorm(x, g, mod, k_scale, k_shift, rows_per_seq, tm):
    m, d = x.shape
    return pl.pallas_call(
        _prenorm_kernel,
        grid=(m // tm,),
        in_specs=[pl.BlockSpec((tm, d), lambda i: (i, 0)),
                  pl.BlockSpec((1, d), lambda i: (0, 0)),
                  _mod_spec(mod, k_scale, tm, rows_per_seq),
                  _mod_spec(mod, k_shift, tm, rows_per_seq)],
        out_specs=pl.BlockSpec((tm, d), lambda i: (i, 0)),
        out_shape=jax.ShapeDtypeStruct((m, d), BF16),
        compiler_params=_cparams("parallel"),
        name="prenorm",
    )(x, g, mod, mod)


def _mm_nt_kernel(a_ref, w_ref, o_ref):
    o_ref[...] = _bdot_nt(a_ref[...], w_ref[...])


def _project(a, w_t, n_cols, tm, tn):
    m, kdim = a.shape
    return pl.pallas_call(
        _mm_nt_kernel,
        grid=(m // tm, n_cols // tn),
        in_specs=[pl.BlockSpec((tm, kdim), lambda i, j: (i, 0)),
                  pl.BlockSpec((tn, kdim), lambda i, j: (j, 0))],
        out_specs=pl.BlockSpec((tm, tn), lambda i, j: (i, j)),
        out_shape=jax.ShapeDtypeStruct((m, n_cols), F32),
        compiler_params=_cparams("parallel", "arbitrary"),
        name="in_proj",
    )(a, w_t)


def _hgrn_elementwise(hq, hf, lb):
    q = _silu(hq)
    f = lb + (1.0 - lb) * _sigmoid(hf)
    return q, jnp.log(f), 1.0 - f


def _hgrn_chunk_free(q, g, k, inp, n, tril, tril_bf):
    c = q.shape[0]
    heads = [slice(h * HEAD_DIM, (h + 1) * HEAD_DIM) for h in range(n)]
    big_g = _mask_dot(tril_bf, g)
    g_ref = big_g[c // 2:c // 2 + 1]
    g_last = big_g[c - 1:c]
    qe = q * jnp.exp(big_g - g_ref)
    ke = k * jnp.exp(g_ref - big_g)
    qs = q * jnp.exp(big_g)
    kl = k * jnp.exp(g_last - big_g)
    d_hi, d_mid, d_lo = [p.astype(F32) for p in _split3(jnp.exp(g_last))]
    extra = jnp.concatenate([d_hi, d_mid, d_lo, jnp.zeros((SUBLANES - 3, q.shape[1]), F32)], axis=0)
    a = [jnp.where(tril, _bdot_nt(qe[:, sl], ke[:, sl]), 0.0) for sl in heads]
    o_intra = [_bdot(a[h], inp[:, sl]) for h, sl in enumerate(heads)]
    t = [_transpose_pad(jnp.concatenate([kl[:, sl], extra[:, sl]], axis=0))[:, :c + SUBLANES] for sl in heads]
    piece_rows = lax.broadcasted_iota(jnp.int32, (SUBLANES, HEAD_DIM), 0) < 3
    ones_rows = jnp.where(piece_rows, 1.0, 0.0)
    zeros_c = jnp.zeros((c, HEAD_DIM), F32)
    uds = []
    for h, sl in enumerate(heads):
        rhs = jnp.concatenate([jnp.concatenate([inp[:, sl], zeros_c], axis=1),
                               jnp.concatenate([jnp.zeros((SUBLANES, HEAD_DIM), F32), ones_rows], axis=1)], axis=0)
        uds.append(_bdot(t[h], rhs))
    return o_intra, qs, uds


def _hgrn_chunk_heads(q, g, k, inp, states, tril, tril_bf):
    o_intra, qs, uds = _hgrn_chunk_free(q, g, k, inp, len(states), tril, tril_bf)
    out = []
    for h, s in enumerate(states):
        sl = slice(h * HEAD_DIM, (h + 1) * HEAD_DIM)
        out.append((o_intra[h] + _bdot(qs[:, sl], s), uds[h][:, HEAD_DIM:] * s + uds[h][:, :HEAD_DIM]))
    return out


def _head_out(o, gate, norm_g):
    mean_sq = _lane_sum_bcast(o * o) * (1.0 / HEAD_DIM)
    return o * lax.rsqrt(mean_sq + NORM_EPS) * norm_g * _silu(gate)


HGRN_CHUNKS_PER_PASS = 8


def _hgrn_prompt_kernel(q_ref, f_ref, i_ref, g_ref, lb_ref, ng_ref, o_ref, st_ref,
                        s_scr, oi_scr, qs_scr, ud_scr, *, hb):
    t = pl.program_id(2)
    c = HG_CHUNK
    n_chunks = q_ref.shape[0] // c
    per = HGRN_CHUNKS_PER_PASS if n_chunks % HGRN_CHUNKS_PER_PASS == 0 else 1
    heads = [slice(h * HEAD_DIM, (h + 1) * HEAD_DIM) for h in range(hb)]

    @pl.when(t == 0)
    def _():
        s_scr[...] = jnp.zeros_like(s_scr)

    tril, _ = _tri_masks(c)
    tril_bf = jnp.where(tril, 1.0, 0.0).astype(BF16)
    lb = lb_ref[...]
    ng = ng_ref[...]

    def free_part(it, carry):
        rows = [pl.ds(pl.multiple_of((it * per + j) * c, c), c) for j in range(per)]
        parts = [_hgrn_elementwise(q_ref[r, :], f_ref[r, :], lb) for r in rows]
        cat = lambda xs: jnp.concatenate(xs, axis=1)
        o_intra, qs, uds = _hgrn_chunk_free(cat([p[0] for p in parts]), cat([p[1] for p in parts]),
                                            cat([p[2] for p in parts]), cat([i_ref[r, :] for r in rows]),
                                            per * hb, tril, tril_bf)
        for j, r in enumerate(rows):
            qs_scr[r, :] = qs[:, j * hb * HEAD_DIM:(j + 1) * hb * HEAD_DIM].astype(qs_scr.dtype)
            for h, sl in enumerate(heads):
                oi_scr[r, sl] = o_intra[j * hb + h]
                ud_scr[(it * per + j) * hb + h] = uds[j * hb + h]
        return carry

    lax.fori_loop(0, n_chunks // per, free_part, 0)

    states = [s_scr[h] for h in range(hb)]
    outs = []
    for ci in range(n_chunks):
        rows = pl.ds(ci * c, c)
        outs.append([oi_scr[rows, sl] + _bdot(qs_scr[rows, sl], states[h]) for h, sl in enumerate(heads)])
        for h in range(hb):
            ud = ud_scr[ci * hb + h]
            states[h] = ud[:, HEAD_DIM:] * states[h] + ud[:, :HEAD_DIM]
    for h in range(hb):
        s_scr[h] = states[h]
    for ci in range(n_chunks):
        rows = pl.ds(ci * c, c)
        for h, sl in enumerate(heads):
            o_ref[rows, sl] = _head_out(outs[ci][h], g_ref[rows, sl], ng).astype(o_ref.dtype)

    @pl.when(t == pl.num_programs(2) - 1)
    def _():
        st_ref[...] = s_scr[...].astype(st_ref.dtype)


def _hgrn_prompt(proj, lb, norm_g, batch, seq, heads, state_dtype):
    hb = _tile(heads, 8, 1)
    tt = _tile(seq, 256, HG_CHUNK)
    w = hb * HEAD_DIM
    nhb = heads // hb
    nt = seq // tt

    def col(seg):
        return pl.BlockSpec((tt, w), lambda b, h, t: (b * nt + t, seg * nhb + h))

    return pl.pallas_call(
        functools.partial(_hgrn_prompt_kernel, hb=hb),
        grid=(batch, nhb, nt),
        in_specs=[col(0), col(1), col(2), col(3),
                  pl.BlockSpec((1, w), lambda b, h, t: (0, h)),
                  pl.BlockSpec((1, HEAD_DIM), lambda b, h, t: (0, 0))],
        out_specs=[pl.BlockSpec((tt, w), lambda b, h, t: (b * nt + t, h)),
                   pl.BlockSpec((None, hb, HEAD_DIM, HEAD_DIM), lambda b, h, t: (b, h, 0, 0))],
        out_shape=[jax.ShapeDtypeStruct((batch * seq, heads * HEAD_DIM), BF16),
                   jax.ShapeDtypeStruct((batch, heads, HEAD_DIM, HEAD_DIM), state_dtype)],
        scratch_shapes=[pltpu.VMEM((hb, HEAD_DIM, HEAD_DIM), F32),
                        pltpu.VMEM((tt, w), F32),
                        pltpu.VMEM((tt, w), BF16),
                        pltpu.VMEM(((tt // HG_CHUNK) * hb, HEAD_DIM, 2 * HEAD_DIM), F32)],
        compiler_params=_cparams("parallel", "parallel", "arbitrary"),
        name="hgrn_prompt",
    )(proj, proj, proj, proj, lb, norm_g)


def _hgrn_sample_kernel(q_ref, f_ref, i_ref, g_ref, lb_ref, ng_ref, s0_ref, o_ref, st_ref, *, hb, seq):
    c = SUBLANES
    per_group = c // seq
    tril, _ = _tri_masks(c)
    tril_bf = jnp.where(tril, 1.0, 0.0).astype(BF16)
    lb = lb_ref[...]
    ng = ng_ref[...]
    row = lax.broadcasted_iota(jnp.int32, (c, 1), 0)

    n_groups = q_ref.shape[0] // c
    valid = [(row >= p * seq) & (row < (p + 1) * seq) for p in range(per_group)]
    masked = lambda x: [jnp.where(v, x, 0.0) for v in valid]
    qs, gs, ks, inps, states = [], [], [], [], []
    for gi in range(n_groups):
        rows = pl.ds(gi * c, c)
        q, g, k = _hgrn_elementwise(q_ref[rows, :], f_ref[rows, :], lb)
        qs += masked(q)
        gs += masked(g)
        ks += masked(k)
        inps += [i_ref[rows, :]] * per_group
        states += [s0_ref[gi * per_group + p, h].astype(F32) for p in range(per_group) for h in range(hb)]
    cat = lambda xs: jnp.concatenate(xs, axis=1)
    res = _hgrn_chunk_heads(cat(qs), cat(gs), cat(ks), cat(inps), states, tril, tril_bf)
    for gi in range(n_groups):
        rows = pl.ds(gi * c, c)
        for h in range(hb):
            sl = slice(h * HEAD_DIM, (h + 1) * HEAD_DIM)
            o_all = jnp.zeros((c, HEAD_DIM), F32)
            for p in range(per_group):
                o, s_new = res[(gi * per_group + p) * hb + h]
                st_ref[gi * per_group + p, h] = s_new.astype(st_ref.dtype)
                o_all = jnp.where(valid[p], o, o_all)
            o_ref[rows, sl] = _head_out(o_all, g_ref[rows, sl], ng).astype(o_ref.dtype)


def _hgrn_sample(proj, lb, norm_g, s0, batch, seq, heads):
    assert SUBLANES % seq == 0
    hb = _tile(heads, 4, 1)
    nb = _tile(batch, 16, SUBLANES // seq)
    w = hb * HEAD_DIM
    nhb = heads // hb
    r = nb * seq

    def col(seg):
        return pl.BlockSpec((r, w), lambda b, h: (b, seg * nhb + h))

    st_spec = pl.BlockSpec((nb, hb, HEAD_DIM, HEAD_DIM), lambda b, h: (b, h, 0, 0))
    return pl.pallas_call(
        functools.partial(_hgrn_sample_kernel, hb=hb, seq=seq),
        grid=(batch // nb, nhb),
        in_specs=[col(0), col(1), col(2), col(3),
                  pl.BlockSpec((1, w), lambda b, h: (0, h)),
                  pl.BlockSpec((1, HEAD_DIM), lambda b, h: (0, 0)),
                  st_spec],
        out_specs=[pl.BlockSpec((r, w), lambda b, h: (b, h)), st_spec],
        out_shape=[jax.ShapeDtypeStruct((batch * seq, heads * HEAD_DIM), BF16),
                   jax.ShapeDtypeStruct(s0.shape, s0.dtype)],
        compiler_params=_cparams("parallel", "parallel"),
        name="hgrn_sample",
    )(proj, proj, proj, proj, lb, norm_g, s0)


def _gdn_gates_kernel(x_ref, alog_ref, dtb_ref, beta_ref, gam_ref, gtot_ref, *, nv, chunk):
    x = x_ref[...]
    r = x.shape[0]
    beta_ref[...] = _sigmoid(x[:, :nv])
    z = x[:, nv:] + dtb_ref[...]
    softplus = jnp.maximum(z, 0.0) + jnp.log(1.0 + jnp.exp(-jnp.abs(z)))
    g = -jnp.exp(alog_ref[...]) * softplus
    row = lax.broadcasted_iota(jnp.int32, (r, r), 0)
    col = lax.broadcasted_iota(jnp.int32, (r, r), 1)
    same = (row // chunk) == (col // chunk)
    gam_ref[...] = _mask_dot(jnp.where(same & (col <= row), 1.0, 0.0).astype(BF16), g)
    gtot_ref[...] = _mask_dot(jnp.where(same, 1.0, 0.0).astype(BF16), g)


def _gdn_gates(x, a_log, dt_bias, chunk):
    m, two_nv = x.shape
    nv = two_nv // 2
    tr = _tile(m, 256, max(chunk, SUBLANES))
    out = jax.ShapeDtypeStruct((m, nv), F32)
    return pl.pallas_call(
        functools.partial(_gdn_gates_kernel, nv=nv, chunk=chunk),
        grid=(m // tr,),
        in_specs=[pl.BlockSpec((tr, two_nv), lambda i: (i, 0)),
                  pl.BlockSpec((1, nv), lambda i: (0, 0)),
                  pl.BlockSpec((1, nv), lambda i: (0, 0))],
        out_specs=[pl.BlockSpec((tr, nv), lambda i: (i, 0))] * 3,
        out_shape=[out, out, out],
        compiler_params=_cparams("parallel"),
        name="gdn_gates",
    )(x, a_log, dt_bias)


def _lane_sum_bcast(x):
    hi = x.astype(BF16)
    lo = (x - hi.astype(F32)).astype(BF16)
    return jnp.dot(jnp.concatenate([hi, lo], axis=1), jnp.ones((2 * HEAD_DIM, HEAD_DIM), BF16),
                   preferred_element_type=F32)


def _l2_heads(y, mult):
    parts = []
    for h in range(y.shape[1] // HEAD_DIM):
        yh = y[:, h * HEAD_DIM:(h + 1) * HEAD_DIM]
        parts.append(yh * (lax.rsqrt(_lane_sum_bcast(yh * yh) + NORM_EPS) * mult))
    return jnp.concatenate(parts, axis=1)


def _conv_finish(y, kind_q, kind_k, o_ref, scale):
    y = _silu(y)

    @pl.when(kind_q)
    def _():
        o_ref[...] = _l2_heads(y, scale)

    @pl.when(kind_k)
    def _():
        o_ref[...] = _l2_heads(y, 1.0)

    @pl.when(jnp.logical_not(kind_q | kind_k))
    def _():
        o_ref[...] = y


def _conv_sample_kernel(u_ref, buf_ref, cw_ref, o_ref, cache_ref, *, seq, nq_tiles, scale):
    j = pl.program_id(1)
    u = u_ref[...]
    r = u.shape[0]
    nb = r // seq
    tok = lax.broadcasted_iota(jnp.int32, (r, 1), 0) % seq
    hist = [_mod_rows(buf_ref[slot], r) for slot in range(CONV_W - 1)]
    cw = cw_ref[...]
    y = cw[CONV_W - 1:CONV_W] * u
    for s in range(1, CONV_W):
        tap = pltpu.roll(u, s, axis=0)
        for t in range(min(s, seq)):
            tap = jnp.where(tok == t, hist[t + CONV_W - 1 - s], tap)
        y = y + cw[CONV_W - 1 - s:CONV_W - s] * tap
    _conv_finish(y, j < nq_tiles, (j >= nq_tiles) & (j < 2 * nq_tiles), o_ref, scale)
    seq_i = lax.broadcasted_iota(jnp.int32, (nb, r), 0)
    row_i = lax.broadcasted_iota(jnp.int32, (nb, r), 1)
    for slot in range(CONV_W - 1):
        pick = row_i == seq_i * seq + (seq - (CONV_W - 1) + slot)
        cache_ref[slot] = _mask_dot(jnp.where(pick, 1.0, 0.0).astype(BF16), u).astype(cache_ref.dtype)


def _conv_sample(proj, col0, conv_buf, conv_w, seq, qk_w, ch):
    m = proj.shape[0]
    wc = _tile(qk_w, 512, HEAD_DIM)
    tr = _tile(m, 256, SUBLANES * seq)
    c0 = col0 // wc
    assert col0 % wc == 0 and tr % seq == 0 and seq >= CONV_W - 1
    nseq = tr // seq
    slots = pl.BlockSpec((CONV_W - 1, nseq, wc), lambda i, j: (0, i, j))
    buf = jnp.transpose(conv_buf, (1, 0, 2))
    qkv, cache = pl.pallas_call(
        functools.partial(_conv_sample_kernel, seq=seq, nq_tiles=qk_w // wc,
                          scale=float(HEAD_DIM) ** -0.5),
        grid=(m // tr, ch // wc),
        in_specs=[pl.BlockSpec((tr, wc), lambda i, j: (i, c0 + j)),
                  slots,
                  pl.BlockSpec((None, CONV_W, wc), lambda i, j: (0, 0, j))],
        out_specs=[pl.BlockSpec((tr, wc), lambda i, j: (i, j)), slots],
        out_shape=[jax.ShapeDtypeStruct((m, ch), F32), jax.ShapeDtypeStruct(buf.shape, conv_buf.dtype)],
        compiler_params=_cparams("parallel", "parallel"),
        name="gdn_conv_sample",
    )(proj, buf, conv_w)
    return qkv, jnp.transpose(cache, (1, 0, 2))


GDN_TILE = 64


def _mm_tile(a, b):
    return _bdot(a[:, :GDN_TILE], b)


def _lane_bcast_all(x, sel):
    return _mask_dot_right(x, sel)


def _mask_dot_right(x, m_bf):
    hi, mid, lo = _split3(x)
    return (jnp.dot(hi, m_bf, preferred_element_type=F32)
            + jnp.dot(mid, m_bf, preferred_element_type=F32)
            + jnp.dot(lo, m_bf, preferred_element_type=F32))


def _head_select(hb):
    head = lax.broadcasted_iota(jnp.int32, (hb, hb * HEAD_DIM), 0)
    lane_head = lax.broadcasted_iota(jnp.int32, (hb, hb * HEAD_DIM), 1) // HEAD_DIM
    return jnp.where(head == lane_head, 1.0, 0.0).astype(BF16)


def _unit_lower_inverses(a_list, rows, cols, block):
    eye = jnp.where(rows == cols, 1.0, 0.0)
    pairs = (rows >> 1) == (cols >> 1)
    d = [eye - jnp.where(pairs, a, 0.0) for a in a_list]
    b = 2
    while b < block:
        sh = b.bit_length()
        coupling = ((rows >> sh) == (cols >> sh)) & ((rows >> (sh - 1)) != (cols >> (sh - 1)))
        m1 = [_mm_tile(jnp.where(coupling, a, 0.0), di) for a, di in zip(a_list, d)]
        m2 = [_mm_tile(di, mi) for di, mi in zip(d, m1)]
        d = [di - mi for di, mi in zip(d, m2)]
        b *= 2
    return d


def _gdn_pair_products(q, k):
    k_pad = jnp.concatenate([k, jnp.zeros_like(k)], axis=0)
    kq = _bdot_nt(jnp.concatenate([k, q], axis=0), k_pad)
    return kq[:GDN_TILE], kq[GDN_TILE:]


def _gdn_wy(tiles, rep, rows, cols, incl, strict, block):
    heads = [slice(h * HEAD_DIM, (h + 1) * HEAD_DIM) for h in range(len(tiles[0][2]))]
    prods = [[_gdn_pair_products(q, k) for q, k in zip(qs, ks)] for qs, ks, *_ in tiles]
    decays, a_list = [], []
    for (qs, ks, vs, bc, gc, gr), prod in zip(tiles, prods):
        for h, sl in enumerate(heads):
            decay = jnp.where(incl, jnp.exp(gc[:, sl] - gr[h:h + 1, :]), 0.0)
            decays.append(decay)
            a_list.append(jnp.where(strict, bc[:, sl] * prod[h // rep][0] * decay, 0.0))
    t_inv = _unit_lower_inverses(a_list, rows, cols, block)
    xs = []
    for t, (qs, ks, vs, bc, gc, gr) in enumerate(tiles):
        for h, sl in enumerate(heads):
            beta = bc[:, sl]
            rhs = jnp.concatenate([vs[h] * beta, ks[h // rep] * (beta * jnp.exp(gc[:, sl]))], axis=1)
            xs.append(_mm_tile(t_inv[t * len(heads) + h], rhs))
    out = []
    for t in range(len(tiles)):
        out.append([(xs[t * len(heads) + h][:, :HEAD_DIM], xs[t * len(heads) + h][:, HEAD_DIM:],
                     prods[t][h // rep][1] * decays[t * len(heads) + h]) for h in range(len(heads))])
    return out


def _tile_iotas():
    shape = (GDN_TILE, HEAD_DIM)
    return lax.broadcasted_iota(jnp.int32, shape, 0), lax.broadcasted_iota(jnp.int32, shape, 1)


GDN_CHUNKS_PER_PASS = 4


def _causal_conv(x_scr, cw, r0, n, cols):
    y = None
    for s in range(CONV_W):
        lo = SUBLANES + r0 - s
        term = cw[CONV_W - 1 - s:CONV_W - s, cols] * x_scr[lo:lo + n, cols]
        y = term if y is None else y + term
    return y


def _gdn_prompt_kernel(uq_ref, uk_ref, uv_ref, pq_ref, pk_ref, pv_ref, cwq_ref, cwk_ref, cwv_ref,
                       z_ref, bc_ref, gc_ref, gt_ref, gr_ref, ng_ref,
                       o_ref, st_ref, s_scr, u_scr, w_scr, a_scr, xq_scr, xk_scr, xv_scr, q_scr, k_scr,
                       *, hb, rep, scale):
    t = pl.program_id(2)
    c = GDN_CHUNK
    tt = uq_ref.shape[0]
    n_chunks = tt // c

    @pl.when(t == 0)
    def _():
        s_scr[...] = jnp.zeros_like(s_scr)

    keep = jnp.where(t == 0, 0.0, 1.0)
    for x_scr, prev_ref, cur_ref in ((xq_scr, pq_ref, uq_ref), (xk_scr, pk_ref, uk_ref), (xv_scr, pv_ref, uv_ref)):
        x_scr[0:SUBLANES, :] = prev_ref[...] * keep
        x_scr[SUBLANES:SUBLANES + tt, :] = cur_ref[...]

    rows_i, cols_i = _tile_iotas()
    incl = cols_i <= rows_i
    strict = cols_i < rows_i
    ng = ng_ref[...]
    sel = _head_select(hb)
    cwq, cwk, cwv = cwq_ref[...], cwk_ref[...], cwv_ref[...]
    heads = [slice(h * HEAD_DIM, (h + 1) * HEAD_DIM) for h in range(hb)]

    def chunk_operands(ci):
        rows = pl.ds(ci * c, c)
        qs = [_l2_heads(_silu(_causal_conv(xq_scr, cwq, ci * c, c, heads[i])), scale) for i in range(hb // rep)]
        ks = [_l2_heads(_silu(_causal_conv(xk_scr, cwk, ci * c, c, heads[i])), 1.0) for i in range(hb // rep)]
        vs = [_silu(_causal_conv(xv_scr, cwv, ci * c, c, heads[h])) for h in range(hb)]
        for i in range(hb // rep):
            q_scr[rows, heads[i]] = qs[i]
            k_scr[rows, heads[i]] = ks[i]
        return (qs, ks, vs, _lane_bcast_all(bc_ref[rows, :], sel),
                _lane_bcast_all(gc_ref[rows, :], sel), gr_ref[ci])

    per = GDN_CHUNKS_PER_PASS if n_chunks % GDN_CHUNKS_PER_PASS == 0 else 1
    for c0 in range(0, n_chunks, per):
        wy = _gdn_wy([chunk_operands(ci) for ci in range(c0, c0 + per)], rep, rows_i, cols_i, incl, strict, c)
        for j in range(per):
            rows = pl.ds((c0 + j) * c, c)
            for h, (u, w, attn) in enumerate(wy[j]):
                sl = slice(h * HEAD_DIM, (h + 1) * HEAD_DIM)
                u_scr[rows, sl] = u
                w_scr[rows, sl] = w.astype(w_scr.dtype)
                a_scr[rows, sl] = attn.astype(a_scr.dtype)

    pairs = range(hb // rep)
    s_pairs = [s_scr[pr] for pr in pairs]
    outs = []
    for ci in range(n_chunks):
        rows = pl.ds(ci * c, c)
        gc = _lane_bcast_all(gc_ref[rows, :], sel)
        gt = _lane_bcast_all(gt_ref[rows, :], sel)
        qs = [_bdot(q_scr[rows, heads[pr]], s_pairs[pr]) for pr in pairs]
        v_new = [u_scr[rows, sl] - _bdot(w_scr[rows, sl], s_pairs[h // rep][:, heads[h % rep]])
                 for h, sl in enumerate(heads)]
        outs.append([qs[h // rep][:, heads[h % rep]] * jnp.exp(gc[:, sl])
                     + _bdot(a_scr[rows, sl][:, :c], v_new[h]) for h, sl in enumerate(heads)])
        for pr in pairs:
            k_t = _transpose_pad(k_scr[rows, heads[pr]])[:, :c]
            mine = range(pr * rep, (pr + 1) * rep)
            v_scaled = jnp.concatenate([v_new[h] * jnp.exp(gt[:, heads[h]] - gc[:, heads[h]]) for h in mine], axis=1)
            decays = jnp.concatenate([jnp.exp(gt[0:1, heads[h]]) for h in mine], axis=1)
            s_pairs[pr] = decays * s_pairs[pr] + _bdot(k_t, v_scaled)
    for pr in pairs:
        s_scr[pr] = s_pairs[pr]
    for ci in range(n_chunks):
        rows = pl.ds(ci * c, c)
        for h, sl in enumerate(heads):
            o_ref[rows, sl] = _head_out(outs[ci][h], z_ref[rows, sl], ng).astype(o_ref.dtype)

    @pl.when(t == pl.num_programs(2) - 1)
    def _():
        for h in range(hb):
            st_ref[h] = s_scr[h // rep][:, (h % rep) * HEAD_DIM:(h % rep + 1) * HEAD_DIM].astype(st_ref.dtype)


def _gdn_layouts(beta, gam, gtot, nv, hb):
    m = beta.shape[0]
    nhb = nv // hb
    as_cols = lambda a: a.reshape(m, nhb, hb).transpose(1, 0, 2)
    g_rows = gam.reshape(m // GDN_TILE, GDN_TILE, nhb, hb).transpose(2, 0, 3, 1)
    g_rows = jnp.pad(g_rows, ((0, 0), (0, 0), (0, 0), (0, HEAD_DIM - GDN_TILE)))
    return as_cols(beta), as_cols(gam), as_cols(gtot), g_rows


def _gdn_prompt(proj, conv_col0, z_col0, conv_w, gate_arrays, norm_g, batch, seq, qk_heads, v_heads,
                state_dtype):
    rep = v_heads // qk_heads
    hb = _tile(v_heads, 8, rep)
    tt = _tile(seq, 256, GDN_CHUNK)
    nt = seq // tt
    nhb = v_heads // hb
    wq = (hb // rep) * HEAD_DIM
    wv = hb * HEAD_DIM
    qk_w = qk_heads * HEAD_DIM
    bc, gc, gt, gr = _gdn_layouts(*gate_arrays, v_heads, hb)
    assert conv_col0 % wv == 0 and z_col0 % wv == 0 and GDN_CHUNK == GDN_TILE and tt % SUBLANES == 0
    q0 = conv_col0 // wq
    k0 = (conv_col0 + qk_w) // wq
    v0 = (conv_col0 + 2 * qk_w) // wv
    z0 = z_col0 // wv
    kc0 = qk_w // wq
    vc0 = 2 * qk_w // wv
    col_spec = pl.BlockSpec((None, tt, hb), lambda b, h, t: (h, b * nt + t, 0))
    sub = tt // SUBLANES

    def cur(width, c0):
        return pl.BlockSpec((tt, width), lambda b, h, t: (b * nt + t, c0 + h))

    def prev(width, c0):
        return pl.BlockSpec((SUBLANES, width), lambda b, h, t: (jnp.maximum((b * nt + t) * sub - 1, 0), c0 + h))

    def taps(width, c0):
        return pl.BlockSpec((None, CONV_W, width), lambda b, h, t: (0, 0, c0 + h))

    return pl.pallas_call(
        functools.partial(_gdn_prompt_kernel, hb=hb, rep=rep, scale=float(HEAD_DIM) ** -0.5),
        grid=(batch, nhb, nt),
        in_specs=[cur(wq, q0), cur(wq, k0), cur(wv, v0),
                  prev(wq, q0), prev(wq, k0), prev(wv, v0),
                  taps(wq, 0), taps(wq, kc0), taps(wv, vc0),
                  cur(wv, z0),
                  col_spec, col_spec, col_spec,
                  pl.BlockSpec((None, tt // GDN_TILE, hb, HEAD_DIM), lambda b, h, t: (h, b * nt + t, 0, 0)),
                  pl.BlockSpec((1, HEAD_DIM), lambda b, h, t: (0, 0))],
        out_specs=[pl.BlockSpec((tt, wv), lambda b, h, t: (b * nt + t, h)),
                   pl.BlockSpec((None, hb, HEAD_DIM, HEAD_DIM), lambda b, h, t: (b, h, 0, 0))],
        out_shape=[jax.ShapeDtypeStruct((batch * seq, v_heads * HEAD_DIM), BF16),
                   jax.ShapeDtypeStruct((batch, v_heads, HEAD_DIM, HEAD_DIM), state_dtype)],
        scratch_shapes=[pltpu.VMEM((hb // rep, HEAD_DIM, rep * HEAD_DIM), F32),
                        pltpu.VMEM((tt, wv), F32),
                        pltpu.VMEM((tt, wv), BF16),
                        pltpu.VMEM((tt, wv), BF16),
                        pltpu.VMEM((tt + SUBLANES, wq), F32),
                        pltpu.VMEM((tt + SUBLANES, wq), F32),
                        pltpu.VMEM((tt + SUBLANES, wv), F32),
                        pltpu.VMEM((tt, wq), F32),
                        pltpu.VMEM((tt, wq), F32)],
        compiler_params=_cparams("parallel", "parallel", "arbitrary"),
        name="gdn_prompt",
    )(proj, proj, proj, proj, proj, proj, conv_w, conv_w, conv_w, proj, bc, gc, gt, gr, norm_g)


def _gdn_sample_kernel(q_ref, k_ref, v_ref, z_ref, bc_ref, gc_ref, gt_ref, gr_ref, ng_ref, s0_ref,
                       o_ref, st_ref, *, hb, rep, seq):
    c = GDN_TILE
    nseq = c // seq
    per_group = SUBLANES // seq
    lsh = seq.bit_length() - 1
    rows_i, cols_i = _tile_iotas()
    same = (rows_i >> lsh) == (cols_i >> lsh)
    incl = (cols_i <= rows_i) & same
    strict = (cols_i < rows_i) & same
    row8 = lax.broadcasted_iota(jnp.int32, (SUBLANES, 1), 0)
    lane_seq = lax.broadcasted_iota(jnp.int32, (1, HEAD_DIM), 1) >> lsh
    ng = ng_ref[...]
    sel = _head_select(hb)
    bc = _lane_bcast_all(bc_ref[...], sel)
    gc = _lane_bcast_all(gc_ref[...], sel)
    gt = _lane_bcast_all(gt_ref[...], sel)
    gr = gr_ref[0]
    head = lambda ref, i: ref[:, i * HEAD_DIM:(i + 1) * HEAD_DIM]
    qs = [head(q_ref, i) for i in range(hb // rep)]
    ks = [head(k_ref, i) for i in range(hb // rep)]
    wy = _gdn_wy([(qs, ks, [head(v_ref, h) for h in range(hb)], bc, gc, gr)], rep,
                 rows_i, cols_i, incl, strict, seq)[0]
    k_t_seq = []
    for k in ks:
        k_t = _transpose_pad(k)
        k_t_seq.append(jnp.concatenate([jnp.where(lane_seq == b, k_t, 0.0)[:, :c] for b in range(nseq)],
                                       axis=0).astype(BF16))
    qe = [qs[h // rep] * jnp.exp(head(gc, h)) for h in range(hb)]
    ws_parts = [[] for _ in range(hb)]
    qs_parts = [[] for _ in range(hb)]
    for g in range(c // SUBLANES):
        gsl = slice(g * SUBLANES, (g + 1) * SUBLANES)
        for h in range(hb):
            acc = jnp.zeros((2 * SUBLANES, HEAD_DIM), F32)
            for p in range(per_group):
                valid = (row8 >= p * seq) & (row8 < (p + 1) * seq)
                lhs = jnp.concatenate([jnp.where(valid, wy[h][1][gsl], 0.0),
                                       jnp.where(valid, qe[h][gsl], 0.0)], axis=0)
                acc = acc + _bdot(lhs, s0_ref[g * per_group + p, h])
            ws_parts[h].append(acc[:SUBLANES])
            qs_parts[h].append(acc[SUBLANES:])
    v_new = [wy[h][0] - jnp.concatenate(ws_parts[h], axis=0) for h in range(hb)]
    outs = [jnp.concatenate(qs_parts[h], axis=0) + _bdot(wy[h][2][:, :c], v_new[h]) for h in range(hb)]
    upds = [jnp.dot(k_t_seq[h // rep], (v_new[h] * jnp.exp(head(gt, h) - head(gc, h))).astype(BF16),
                    preferred_element_type=F32) for h in range(hb)]
    for h in range(hb):
        o_ref[:, h * HEAD_DIM:(h + 1) * HEAD_DIM] = _head_out(outs[h], head(z_ref, h), ng).astype(o_ref.dtype)
        decay = jnp.exp(head(gt, h))
        for b in range(nseq):
            st_ref[b, h] = (decay[b * seq:b * seq + 1] * s0_ref[b, h].astype(F32)
                            + upds[h][b * HEAD_DIM:(b + 1) * HEAD_DIM]).astype(st_ref.dtype)


def _gdn_sample(qkv, proj, z_col0, gate_arrays, norm_g, s0, batch, seq, qk_heads, v_heads):
    rep = v_heads // qk_heads
    hb = _tile(v_heads, 8, rep)
    assert GDN_TILE % seq == 0 and SUBLANES % seq == 0 and seq & (seq - 1) == 0
    nb = GDN_TILE // seq
    assert batch % nb == 0
    r = GDN_TILE
    nhb = v_heads // hb
    wq = (hb // rep) * HEAD_DIM
    wv = hb * HEAD_DIM
    qk_w = qk_heads * HEAD_DIM
    bc, gc, gt, gr = _gdn_layouts(*gate_arrays, v_heads, hb)
    k0 = qk_w // wq
    v0 = 2 * qk_w // wv
    z0 = z_col0 // wv
    assert z_col0 % wv == 0
    st_spec = pl.BlockSpec((nb, hb, HEAD_DIM, HEAD_DIM), lambda b, h: (b, h, 0, 0))
    col_spec = pl.BlockSpec((None, r, hb), lambda b, h: (h, b, 0))
    return pl.pallas_call(
        functools.partial(_gdn_sample_kernel, hb=hb, rep=rep, seq=seq),
        grid=(batch // nb, nhb),
        in_specs=[pl.BlockSpec((r, wq), lambda b, h: (b, h)),
                  pl.BlockSpec((r, wq), lambda b, h: (b, k0 + h)),
                  pl.BlockSpec((r, wv), lambda b, h: (b, v0 + h)),
                  pl.BlockSpec((r, wv), lambda b, h: (b, z0 + h)),
                  col_spec, col_spec, col_spec,
                  pl.BlockSpec((None, 1, hb, HEAD_DIM), lambda b, h: (h, b, 0, 0)),
                  pl.BlockSpec((1, HEAD_DIM), lambda b, h: (0, 0)),
                  st_spec],
        out_specs=[pl.BlockSpec((r, wv), lambda b, h: (b, h)), st_spec],
        out_shape=[jax.ShapeDtypeStruct((batch * seq, v_heads * HEAD_DIM), BF16),
                   jax.ShapeDtypeStruct(s0.shape, s0.dtype)],
        compiler_params=_cparams("parallel", "parallel"),
        name="gdn_sample",
    )(qkv, qkv, qkv, proj, bc, gc, gt, gr, norm_g, s0)


def _merge_kernel(oh_ref, og_ref, wa_ref, wb_ref, ga_ref, gb_ref, o_ref):
    ya = _bdot(oh_ref[...], wa_ref[...])
    yb = _bdot(og_ref[...], wb_ref[...])
    o_ref[...] = (_sigmoid(ga_ref[...]) * ya + _sigmoid(gb_ref[...]) * yb).astype(o_ref.dtype)


def _merge(o_hg, o_gdn, w_a, w_b, gates, tm):
    m, ka = o_hg.shape
    kb = o_gdn.shape[1]
    d = w_a.shape[-1]
    tn = _tile(d, 256, 128)
    nj = d // tn
    return pl.pallas_call(
        _merge_kernel,
        grid=(m // tm, nj),
        in_specs=[pl.BlockSpec((tm, ka), lambda i, j: (i, 0)),
                  pl.BlockSpec((tm, kb), lambda i, j: (i, 0)),
                  pl.BlockSpec((None, ka, tn), lambda i, j: (0, 0, j)),
                  pl.BlockSpec((None, kb, tn), lambda i, j: (0, 0, j)),
                  pl.BlockSpec((tm, tn), lambda i, j: (i, j)),
                  pl.BlockSpec((tm, tn), lambda i, j: (i, nj + j))],
        out_specs=pl.BlockSpec((tm, tn), lambda i, j: (i, j)),
        out_shape=jax.ShapeDtypeStruct((m, d), BF16),
        compiler_params=_cparams("parallel", "arbitrary"),
        name="merge",
    )(o_hg, o_gdn, w_a, w_b, gates, gates)


def _outproj_kernel(m_ref, w_ref, x_ref, g1_ref, sc_ref, sh_ref, ng_ref, h_ref, a_ref):
    ng = ng_ref[...]
    mix = _bdot(m_ref[...], w_ref[...])
    n = mix.shape[0]
    h = x_ref[...].astype(F32) + _mod_rows(g1_ref[...], n) * (_rms(mix) * ng[0:1])
    h_ref[...] = h
    a_ref[...] = ((_rms(h) * ng[1:2]) * (1.0 + _mod_rows(sc_ref[...], n))
                  + _mod_rows(sh_ref[...], n)).astype(a_ref.dtype)


def _outproj(merged, w_o_bf16, x, mod, norm_g2, rows_per_seq, tm):
    m, d = x.shape
    kdim = merged.shape[1]
    row = lambda i: (i, 0)
    return pl.pallas_call(
        _outproj_kernel,
        grid=(m // tm,),
        in_specs=[pl.BlockSpec((tm, kdim), row),
                  pl.BlockSpec((kdim, d), lambda i: (0, 0), pipeline_mode=pl.Buffered(1)),
                  pl.BlockSpec((tm, d), row),
                  _mod_spec(mod, 2, tm, rows_per_seq),
                  _mod_spec(mod, 4, tm, rows_per_seq),
                  _mod_spec(mod, 3, tm, rows_per_seq),
                  pl.BlockSpec((2, d), lambda i: (0, 0))],
        out_specs=[pl.BlockSpec((tm, d), row), pl.BlockSpec((tm, d), row)],
        out_shape=[jax.ShapeDtypeStruct((m, d), F32), jax.ShapeDtypeStruct((m, d), BF16)],
        compiler_params=_cparams("parallel"),
        name="out_proj",
    )(merged, w_o_bf16, x, mod, mod, mod, norm_g2)


def _cast_kernel(x_ref, o_ref):
    o_ref[...] = x_ref[...].astype(o_ref.dtype)


def _to_bf16(w):
    kdim, n = w.shape
    tk = _tile(kdim, 512, 16)
    return pl.pallas_call(
        _cast_kernel,
        grid=(kdim // tk,),
        in_specs=[pl.BlockSpec((tk, n), lambda i: (i, 0))],
        out_specs=pl.BlockSpec((tk, n), lambda i: (i, 0)),
        out_shape=jax.ShapeDtypeStruct((kdim, n), BF16),
        compiler_params=_cparams("parallel"),
        name="weight_to_bf16",
    )(w)


def _ffn_up_kernel(a_ref, w_ref, o_ref):
    o_ref[...] = jnp.square(jnp.maximum(_bdot(a_ref[...], w_ref[...]), 0.0)).astype(o_ref.dtype)


def _ffn_up(a, w_up, tm):
    m, d = a.shape
    dff = w_up.shape[-1]
    tf = _tile(dff, 512, 128)
    return pl.pallas_call(
        _ffn_up_kernel,
        grid=(m // tm, dff // tf),
        in_specs=[pl.BlockSpec((tm, d), lambda i, f: (i, 0)),
                  pl.BlockSpec((None, d, tf), lambda i, f: (0, 0, f))],
        out_specs=pl.BlockSpec((tm, tf), lambda i, f: (i, f)),
        out_shape=jax.ShapeDtypeStruct((m, dff), BF16),
        compiler_params=_cparams("parallel", "arbitrary"),
        name="ffn_up",
    )(a, w_up)


EPILOGUE_ROWS = 128


def _ffn_down_kernel(hid_ref, w_ref, h_ref, g2_ref, ng_ref, y_ref):
    j = pl.program_id(1)
    tn = w_ref.shape[1]
    y_ref[:, pl.ds(pl.multiple_of(j * tn, tn), tn)] = _bdot(hid_ref[...], w_ref[...])

    @pl.when(j == pl.num_programs(1) - 1)
    def _():
        tm = y_ref.shape[0]
        step = min(EPILOGUE_ROWS, tm)
        g_rows = g2_ref.shape[0]
        g_step = max(g_rows * step // tm, 1)

        def body(r, carry):
            rows = pl.ds(pl.multiple_of(r * step, step), step)
            g2 = g2_ref[...] if g_rows == 1 else _mod_rows(
                g2_ref[pl.ds(pl.multiple_of(r * g_step, g_step), g_step), :], step)
            y_ref[rows, :] = h_ref[rows, :] + g2 * (_rms(y_ref[rows, :]) * ng_ref[...])
            return carry

        lax.fori_loop(0, y_ref.shape[0] // step, body, 0)


def _ffn_down(hid, w_down, h, mod, norm_g, rows_per_seq, tm, out_dtype):
    m, d = h.shape
    dff = hid.shape[1]
    tn = _tile(d, 512, 128)
    row = lambda i, j: (i, 0)
    assert out_dtype == F32, "the resident output block holds the f32 pre-norm result"
    return pl.pallas_call(
        _ffn_down_kernel,
        grid=(m // tm, d // tn),
        in_specs=[pl.BlockSpec((tm, dff), row),
                  pl.BlockSpec((dff, tn), lambda i, j: (0, j)),
                  pl.BlockSpec((tm, d), row),
                  _mod_spec(mod, 5, tm, rows_per_seq),
                  pl.BlockSpec((1, d), lambda i, j: (0, 0))],
        out_specs=pl.BlockSpec((tm, d), row),
        out_shape=jax.ShapeDtypeStruct((m, d), out_dtype),
        compiler_params=_cparams("parallel", "arbitrary", vmem_limit=VMEM_LIMIT_FFN_DOWN),
        name="ffn_down",
    )(hid, w_down, h, mod, norm_g)


def _run_group(x3, mod, rows_per_seq_for_mod, conv_buf, s_hg, s_gdn, lb, p, prompt):
    batch, seq, d = x3.shape
    m = batch * seq
    x = x3.reshape(m, d)
    hg_heads, gv_heads = p["hg_heads"], p["gv_heads"]
    gq_heads = p["gq_heads"]
    hg_w = hg_heads * HEAD_DIM
    gq_w = gq_heads * HEAD_DIM
    gv_w = gv_heads * HEAD_DIM
    ch = 2 * gq_w + gv_w
    main = 4 * hg_w + ch + gv_w
    conv0 = 4 * hg_w
    z0 = conv0 + ch
    rows_unit = seq if prompt else m
    tm = _tile(rows_unit, 512, SUBLANES)
    tm_proj = _tile(rows_unit, 2048, SUBLANES)
    tm_wide = _tile(rows_unit, 1024, SUBLANES)

    a1 = _prenorm(x, p["norm_g"][0:1], mod, 1, 0, rows_per_seq_for_mod, tm)
    tn_pref = 512 if tm_proj > 512 else 1024
    proj = _project(a1, p["w_in"], main, tm_proj, _tile(main, tn_pref, 128))
    tail = _project(a1, p["w_tail"], p["w_tail"].shape[0], tm_proj, p["w_tail"].shape[0])
    gates = _project(a1, p["w_gates"], 2 * d, tm_proj, _tile(2 * d, tn_pref, 128))

    if prompt:
        o_hg, s_hg_new = _hgrn_prompt(proj, lb, p["hg_norm_g"], batch, seq, hg_heads, p["hg_dtype"])
        gate_arrays = _gdn_gates(tail, p["a_log"], p["dt_bias"], GDN_CHUNK)
        o_gdn, s_gdn_new = _gdn_prompt(proj, conv0, z0, p["conv_w"], gate_arrays, p["gdn_norm_g"], batch, seq,
                                       gq_heads, gv_heads, p["gdn_dtype"])
        assert seq % SUBLANES == 0
        conv_new = proj.reshape(batch, seq, main)[:, seq - (CONV_W - 1):, conv0:conv0 + ch]
    else:
        o_hg, s_hg_new = _hgrn_sample(proj, lb, p["hg_norm_g"], s_hg, batch, seq, hg_heads)
        gate_arrays = _gdn_gates(tail, p["a_log"], p["dt_bias"], seq)
        qkv, conv_new = _conv_sample(proj, conv0, conv_buf.astype(F32), p["conv_w"], seq, gq_w, ch)
        o_gdn, s_gdn_new = _gdn_sample(qkv, proj, z0, gate_arrays, p["gdn_norm_g"], s_gdn, batch, seq,
                                       gq_heads, gv_heads)

    merged = _merge(o_hg, o_gdn, p["w_out_hg"], p["w_out_gdn"], gates, tm_wide)
    h1, a2 = _outproj(merged, p["w_o_bf16"], x, mod, p["norm_g"][1:3], rows_per_seq_for_mod, tm)
    hid = _ffn_up(a2, p["w_up"], tm_proj)
    y = _ffn_down(hid, p["w_down_bf16"], h1, mod, p["norm_g"][3:4], rows_per_seq_for_mod, tm, x3.dtype)
    return y.reshape(batch, seq, d), conv_new, s_hg_new, s_gdn_new


def kernel(x_prompt, x_sample, state_hgrn, state_gdn, cache_conv, c_prompt, c_sample, lb_logits, w_ada,
           b_ada, norm_g, w_in, conv_w, A_log, dt_bias, hg_norm_g, gdn_norm_g, w_out_hg, w_out_gdn, w_o,
           w_up, w_down):
    depth = w_in.shape[0]
    assert depth == 1, "single-layer trunk"
    bp, tp, d = x_prompt.shape
    bs, ts, _ = x_sample.shape
    hg_heads = state_hgrn.shape[2]
    gv_heads = state_gdn.shape[2]
    ch = cache_conv.shape[-1]
    gq_heads = (ch // HEAD_DIM - gv_heads) // 2
    assert tp >= CONV_W - 1 and ts >= CONV_W - 1
    assert tp % GDN_CHUNK == 0 and tp % HG_CHUNK == 0

    main = 4 * hg_heads * HEAD_DIM + ch + gv_heads * HEAD_DIM
    w_in_t = jnp.swapaxes(w_in[0], 0, 1)
    p = dict(
        hg_heads=hg_heads, gv_heads=gv_heads, gq_heads=gq_heads,
        hg_dtype=state_hgrn.dtype, gdn_dtype=state_gdn.dtype,
        norm_g=norm_g[0].astype(F32), w_in=w_in_t,
        w_tail=w_in_t[main:main + 2 * gv_heads],
        w_gates=w_in_t[main + 2 * gv_heads:],
        conv_w=conv_w.astype(F32), a_log=A_log.astype(F32), dt_bias=dt_bias.astype(F32),
        hg_norm_g=hg_norm_g.astype(F32), gdn_norm_g=gdn_norm_g.astype(F32),
        w_out_hg=w_out_hg, w_out_gdn=w_out_gdn, w_o_bf16=_to_bf16(w_o[0]), w_up=w_up,
        w_down_bf16=_to_bf16(w_down[0]),
    )

    lbs = jnp.cumsum(jax.nn.softmax(lb_logits.astype(F32), axis=0), axis=0)
    lb = lbs[0:1]

    n_c = bp + bs
    pad = (-n_c) % SUBLANES
    c_all = jnp.concatenate([c_prompt.astype(F32), c_sample.astype(F32), jnp.zeros((pad, d), F32)], axis=0)
    mod_all = _ada(c_all, w_ada, b_ada.astype(F32))
    mod_p = mod_all[:bp].reshape(bp * 6, 1, d)
    mod_s = mod_all[bp:bp + bs].reshape(bs, 6, d).transpose(1, 0, 2)

    y_p, conv_p, hg_p, gdn_p = _run_group(x_prompt, mod_p, tp, None, None, None, lb, p, prompt=True)
    y_s, conv_s, hg_s, gdn_s = _run_group(x_sample, mod_s, ts, cache_conv[0], state_hgrn[0], state_gdn[0],
                                          lb, p, prompt=False)
    return (y_p, y_s,
            hg_p[None], gdn_p[None], conv_p.astype(cache_conv.dtype)[None],
            hg_s[None], gdn_s[None], conv_s.astype(cache_conv.dtype)[None])
```

```python
import functools

import jax
import jax.numpy as jnp
from jax import lax
from jax.experimental import pallas as pl
from jax.experimental.pallas import tpu as pltpu

F32 = jnp.float32
BF16 = jnp.bfloat16

HEAD_DIM = 128
HG_CHUNK = 32
GDN_CHUNK = 64
CONV_W = 4
NORM_EPS = 1e-6
SUBLANES = 8
V7X_VMEM_BYTES = 64 * 1024 * 1024
VMEM_LIMIT = V7X_VMEM_BYTES * 3 // 4
VMEM_LIMIT_FFN_DOWN = V7X_VMEM_BYTES * 27 // 32


def _cparams(*sem, vmem_limit=VMEM_LIMIT):
    return pltpu.CompilerParams(dimension_semantics=sem, vmem_limit_bytes=vmem_limit)


def _tile(n, pref, quantum):
    if n <= pref:
        return n
    t = (pref // quantum) * quantum
    while t > quantum and n % t:
        t -= quantum
    assert n % t == 0, (n, pref, quantum)
    return t


def _bdot(a, b):
    return jnp.dot(a.astype(BF16), b.astype(BF16), preferred_element_type=F32)


def _bdot_nt(a, b):
    return lax.dot_general(a.astype(BF16), b.astype(BF16), (((1,), (1,)), ((), ())),
                           preferred_element_type=F32)


def _split3(x):
    hi = x.astype(BF16)
    r = x - hi.astype(F32)
    mid = r.astype(BF16)
    lo = (r - mid.astype(F32)).astype(BF16)
    return hi, mid, lo


def _mask_dot(m_bf, x):
    hi, mid, lo = _split3(x)
    return (jnp.dot(m_bf, hi, preferred_element_type=F32)
            + jnp.dot(m_bf, mid, preferred_element_type=F32)
            + jnp.dot(m_bf, lo, preferred_element_type=F32))


def _sigmoid(x):
    return 1.0 / (1.0 + jnp.exp(-x))


def _silu(x):
    return x * _sigmoid(x)


def _rms(x):
    return x * lax.rsqrt(jnp.mean(x * x, axis=-1, keepdims=True) + NORM_EPS)


def _transpose_pad(x):
    r = x.shape[0]
    if r < HEAD_DIM:
        x = jnp.concatenate([x, jnp.zeros((HEAD_DIM - r, HEAD_DIM), x.dtype)], axis=0)
    return x.T


def _tri_masks(c):
    row = lax.broadcasted_iota(jnp.int32, (c, c), 0)
    col = lax.broadcasted_iota(jnp.int32, (c, c), 1)
    return col <= row, col < row


def _ada_kernel(c_ref, w_ref, b_ref, o_ref):
    cs = _silu(c_ref[...])
    w = w_ref[...]
    c_hi = cs.astype(BF16)
    c_lo = (cs - c_hi.astype(F32)).astype(BF16)
    w_hi = w.astype(BF16)
    w_lo = (w - w_hi.astype(F32)).astype(BF16)
    acc = jnp.dot(c_hi, w_hi, preferred_element_type=F32)
    acc += jnp.dot(c_lo, w_hi, preferred_element_type=F32)
    acc += jnp.dot(c_hi, w_lo, preferred_element_type=F32)
    o_ref[...] = acc + b_ref[...]


def _ada(c_all, w_ada, b_ada):
    m, d = c_all.shape
    n = w_ada.shape[-1]
    tn = _tile(n, 512, 128)
    return pl.pallas_call(
        _ada_kernel,
        grid=(n // tn,),
        in_specs=[pl.BlockSpec((m, d), lambda j: (0, 0)),
                  pl.BlockSpec((None, d, tn), lambda j: (0, 0, j)),
                  pl.BlockSpec((1, tn), lambda j: (0, j))],
        out_specs=pl.BlockSpec((m, tn), lambda j: (0, j)),
        out_shape=jax.ShapeDtypeStruct((m, n), F32),
        compiler_params=_cparams("arbitrary"),
        name="ada_mod",
    )(c_all, w_ada, b_ada)


def _mod_spec(mod, k, tm, rows_per_seq):
    d = mod.shape[-1]
    if mod.shape[1] == 1:
        tiles_per_seq = rows_per_seq // tm
        return pl.BlockSpec((None, 1, d), lambda i, *_: ((i // tiles_per_seq) * 6 + k, 0, 0))
    return pl.BlockSpec((None, tm // rows_per_seq, d), lambda i, *_: (k, i, 0))


def _mod_rows(v, n):
    b = v.shape[0]
    if b == 1 or b == n:
        return v
    rep = n // b
    row = lax.broadcasted_iota(jnp.int32, (n, b), 0)
    col = lax.broadcasted_iota(jnp.int32, (n, b), 1)
    owner = (row >= col * rep) & (row < (col + 1) * rep)
    return _mask_dot(jnp.where(owner, 1.0, 0.0).astype(BF16), v)


def _prenorm_kernel(x_ref, g_ref, sc_ref, sh_ref, o_ref):
    xn = _rms(x_ref[...].astype(F32)) * g_ref[...]
    n = xn.shape[0]
    o_ref[...] = (xn * (1.0 + _mod_rows(sc_ref[...], n)) + _mod_rows(sh_ref[...], n)).astype(o_ref.dtype)


def _prenorm(x, g, mod, k_scale, k_shift, rows_per_seq, tm):
    m, d = x.shape
    return pl.pallas_call(
        _prenorm_kernel,
        grid=(m // tm,),
        in_specs=[pl.BlockSpec((tm, d), lambda i: (i, 0)),
                  pl.BlockSpec((1, d), lambda i: (0, 0)),
                  _mod_spec(mod, k_scale, tm, rows_per_seq),
                  _mod_spec(mod, k_shift, tm, rows_per_seq)],
        out_specs=pl.BlockSpec((tm, d), lambda i: (i, 0)),
        out_shape=jax.ShapeDtypeStruct((m, d), BF16),
        compiler_params=_cparams("parallel"),
        name="prenorm",
    )(x, g, mod, mod)


def _mm_nt_kernel(a_ref, w_ref, o_ref):
    o_ref[...] = _bdot_nt(a_ref[...], w_ref[...])


def _project(a, w_t, n_cols, tm, tn):
    m, kdim = a.shape
    return pl.pallas_call(
        _mm_nt_kernel,
        grid=(m // tm, n_cols // tn),
        in_specs=[pl.BlockSpec((tm, kdim), lambda i, j: (i, 0)),
                  pl.BlockSpec((tn, kdim), lambda i, j: (j, 0))],
        out_specs=pl.BlockSpec((tm, tn), lambda i, j: (i, j)),
        out_shape=jax.ShapeDtypeStruct((m, n_cols), F32),
        compiler_params=_cparams("parallel", "arbitrary"),
        name="in_proj",
    )(a, w_t)


def _hgrn_elementwise(hq, hf, lb):
    q = _silu(hq)
    f = lb + (1.0 - lb) * _sigmoid(hf)
    return q, jnp.log(f), 1.0 - f


def _hgrn_chunk_free(q, g, k, inp, n, tril, tril_bf):
    c = q.shape[0]
    heads = [slice(h * HEAD_DIM, (h + 1) * HEAD_DIM) for h in range(n)]
    big_g = _mask_dot(tril_bf, g)
    g_ref = big_g[c // 2:c // 2 + 1]
    g_last = big_g[c - 1:c]
    qe = q * jnp.exp(big_g - g_ref)
    ke = k * jnp.exp(g_ref - big_g)
    qs = q * jnp.exp(big_g)
    kl = k * jnp.exp(g_last - big_g)
    d_hi, d_mid, d_lo = [p.astype(F32) for p in _split3(jnp.exp(g_last))]
    extra = jnp.concatenate([d_hi, d_mid, d_lo, jnp.zeros((SUBLANES - 3, q.shape[1]), F32)], axis=0)
    a = [jnp.where(tril, _bdot_nt(qe[:, sl], ke[:, sl]), 0.0) for sl in heads]
    o_intra = [_bdot(a[h], inp[:, sl]) for h, sl in enumerate(heads)]
    t = [_transpose_pad(jnp.concatenate([kl[:, sl], extra[:, sl]], axis=0))[:, :c + SUBLANES] for sl in heads]
    piece_rows = lax.broadcasted_iota(jnp.int32, (SUBLANES, HEAD_DIM), 0) < 3
    ones_rows = jnp.where(piece_rows, 1.0, 0.0)
    zeros_c = jnp.zeros((c, HEAD_DIM), F32)
    uds = []
    for h, sl in enumerate(heads):
        rhs = jnp.concatenate([jnp.concatenate([inp[:, sl], zeros_c], axis=1),
                               jnp.concatenate([jnp.zeros((SUBLANES, HEAD_DIM), F32), ones_rows], axis=1)], axis=0)
        uds.append(_bdot(t[h], rhs))
    return o_intra, qs, uds


def _hgrn_chunk_heads(q, g, k, inp, states, tril, tril_bf):
    o_intra, qs, uds = _hgrn_chunk_free(q, g, k, inp, len(states), tril, tril_bf)
    out = []
    for h, s in enumerate(states):
        sl = slice(h * HEAD_DIM, (h + 1) * HEAD_DIM)
        out.append((o_intra[h] + _bdot(qs[:, sl], s), uds[h][:, HEAD_DIM:] * s + uds[h][:, :HEAD_DIM]))
    return out


def _head_out(o, gate, norm_g):
    mean_sq = _lane_sum_bcast(o * o) * (1.0 / HEAD_DIM)
    return o * lax.rsqrt(mean_sq + NORM_EPS) * norm_g * _silu(gate)


HGRN_CHUNKS_PER_PASS = 8


def _hgrn_prompt_kernel(q_ref, f_ref, i_ref, g_ref, lb_ref, ng_ref, o_ref, st_ref,
                        s_scr, oi_scr, qs_scr, ud_scr, *, hb):
    t = pl.program_id(2)
    c = HG_CHUNK
    n_chunks = q_ref.shape[0] // c
    per = HGRN_CHUNKS_PER_PASS if n_chunks % HGRN_CHUNKS_PER_PASS == 0 else 1
    heads = [slice(h * HEAD_DIM, (h + 1) * HEAD_DIM) for h in range(hb)]

    @pl.when(t == 0)
    def _():
        s_scr[...] = jnp.zeros_like(s_scr)

    tril, _ = _tri_masks(c)
    tril_bf = jnp.where(tril, 1.0, 0.0).astype(BF16)
    lb = lb_ref[...]
    ng = ng_ref[...]

    def free_part(it, carry):
        rows = [pl.ds(pl.multiple_of((it * per + j) * c, c), c) for j in range(per)]
        parts = [_hgrn_elementwise(q_ref[r, :], f_ref[r, :], lb) for r in rows]
        cat = lambda xs: jnp.concatenate(xs, axis=1)
        o_intra, qs, uds = _hgrn_chunk_free(cat([p[0] for p in parts]), cat([p[1] for p in parts]),
                                            cat([p[2] for p in parts]), cat([i_ref[r, :] for r in rows]),
                                            per * hb, tril, tril_bf)
        for j, r in enumerate(rows):
            qs_scr[r, :] = qs[:, j * hb * HEAD_DIM:(j + 1) * hb * HEAD_DIM].astype(qs_scr.dtype)
            for h, sl in enumerate(heads):
                oi_scr[r, sl] = o_intra[j * hb + h]
                ud_scr[(it * per + j) * hb + h] = uds[j * hb + h]
        return carry

    lax.fori_loop(0, n_chunks // per, free_part, 0)

    states = [s_scr[h] for h in range(hb)]
    outs = []
    for ci in range(n_chunks):
        rows = pl.ds(ci * c, c)
        outs.append([oi_scr[rows, sl] + _bdot(qs_scr[rows, sl], states[h]) for h, sl in enumerate(heads)])
        for h in range(hb):
            ud = ud_scr[ci * hb + h]
            states[h] = ud[:, HEAD_DIM:] * states[h] + ud[:, :HEAD_DIM]
    for h in range(hb):
        s_scr[h] = states[h]
    for ci in range(n_chunks):
        rows = pl.ds(ci * c, c)
        for h, sl in enumerate(heads):
            o_ref[rows, sl] = _head_out(outs[ci][h], g_ref[rows, sl], ng).astype(o_ref.dtype)

    @pl.when(t == pl.num_programs(2) - 1)
    def _():
        st_ref[...] = s_scr[...].astype(st_ref.dtype)


def _hgrn_prompt(proj, lb, norm_g, batch, seq, heads, state_dtype):
    hb = _tile(heads, 8, 1)
    tt = _tile(seq, 256, HG_CHUNK)
    w = hb * HEAD_DIM
    nhb = heads // hb
    nt = seq // tt

    def col(seg):
        return pl.BlockSpec((tt, w), lambda b, h, t: (b * nt + t, seg * nhb + h))

    return pl.pallas_call(
        functools.partial(_hgrn_prompt_kernel, hb=hb),
        grid=(batch, nhb, nt),
        in_specs=[col(0), col(1), col(2), col(3),
                  pl.BlockSpec((1, w), lambda b, h, t: (0, h)),
                  pl.BlockSpec((1, HEAD_DIM), lambda b, h, t: (0, 0))],
        out_specs=[pl.BlockSpec((tt, w), lambda b, h, t: (b * nt + t, h)),
                   pl.BlockSpec((None, hb, HEAD_DIM, HEAD_DIM), lambda b, h, t: (b, h, 0, 0))],
        out_shape=[jax.ShapeDtypeStruct((batch * seq, heads * HEAD_DIM), BF16),
                   jax.ShapeDtypeStruct((batch, heads, HEAD_DIM, HEAD_DIM), state_dtype)],
        scratch_shapes=[pltpu.VMEM((hb, HEAD_DIM, HEAD_DIM), F32),
                        pltpu.VMEM((tt, w), F32),
                        pltpu.VMEM((tt, w), BF16),
                        pltpu.VMEM(((tt // HG_CHUNK) * hb, HEAD_DIM, 2 * HEAD_DIM), F32)],
        compiler_params=_cparams("parallel", "parallel", "arbitrary"),
        name="hgrn_prompt",
    )(proj, proj, proj, proj, lb, norm_g)


def _hgrn_sample_kernel(q_ref, f_ref, i_ref, g_ref, lb_ref, ng_ref, s0_ref, o_ref, st_ref, *, hb, seq):
    c = SUBLANES
    per_group = c // seq
    tril, _ = _tri_masks(c)
    tril_bf = jnp.where(tril, 1.0, 0.0).astype(BF16)
    lb = lb_ref[...]
    ng = ng_ref[...]
    row = lax.broadcasted_iota(jnp.int32, (c, 1), 0)

    n_groups = q_ref.shape[0] // c
    valid = [(row >= p * seq) & (row < (p + 1) * seq) for p in range(per_group)]
    masked = lambda x: [jnp.where(v, x, 0.0) for v in valid]
    qs, gs, ks, inps, states = [], [], [], [], []
    for gi in range(n_groups):
        rows = pl.ds(gi * c, c)
        q, g, k = _hgrn_elementwise(q_ref[rows, :], f_ref[rows, :], lb)
        qs += masked(q)
        gs += masked(g)
        ks += masked(k)
        inps += [i_ref[rows, :]] * per_group
        states += [s0_ref[gi * per_group + p, h].astype(F32) for p in range(per_group) for h in range(hb)]
    cat = lambda xs: jnp.concatenate(xs, axis=1)
    res = _hgrn_chunk_heads(cat(qs), cat(gs), cat(ks), cat(inps), states, tril, tril_bf)
    for gi in range(n_groups):
        rows = pl.ds(gi * c, c)
        for h in range(hb):
            sl = slice(h * HEAD_DIM, (h + 1) * HEAD_DIM)
            o_all = jnp.zeros((c, HEAD_DIM), F32)
            for p in range(per_group):
                o, s_new = res[(gi * per_group + p) * hb + h]
                st_ref[gi * per_group + p, h] = s_new.astype(st_ref.dtype)
                o_all = jnp.where(valid[p], o, o_all)
            o_ref[rows, sl] = _head_out(o_all, g_ref[rows, sl], ng).astype(o_ref.dtype)


def _hgrn_sample(proj, lb, norm_g, s0, batch, seq, heads):
    assert SUBLANES % seq == 0
    hb = _tile(heads, 4, 1)
    nb = _tile(batch, 16, SUBLANES // seq)
    w = hb * HEAD_DIM
    nhb = heads // hb
    r = nb * seq

    def col(seg):
        return pl.BlockSpec((r, w), lambda b, h: (b, seg * nhb + h))

    st_spec = pl.BlockSpec((nb, hb, HEAD_DIM, HEAD_DIM), lambda b, h: (b, h, 0, 0))
    return pl.pallas_call(
        functools.partial(_hgrn_sample_kernel, hb=hb, seq=seq),
        grid=(batch // nb, nhb),
        in_specs=[col(0), col(1), col(2), col(3),
                  pl.BlockSpec((1, w), lambda b, h: (0, h)),
                  pl.BlockSpec((1, HEAD_DIM), lambda b, h: (0, 0)),
                  st_spec],
        out_specs=[pl.BlockSpec((r, w), lambda b, h: (b, h)), st_spec],
        out_shape=[jax.ShapeDtypeStruct((batch * seq, heads * HEAD_DIM), BF16),
                   jax.ShapeDtypeStruct(s0.shape, s0.dtype)],
        compiler_params=_cparams("parallel", "parallel"),
        name="hgrn_sample",
    )(proj, proj, proj, proj, lb, norm_g, s0)


def _gdn_gates_kernel(x_ref, alog_ref, dtb_ref, beta_ref, gam_ref, gtot_ref, *, nv, chunk):
    x = x_ref[...]
    r = x.shape[0]
    beta_ref[...] = _sigmoid(x[:, :nv])
    z = x[:, nv:] + dtb_ref[...]
    softplus = jnp.maximum(z, 0.0) + jnp.log(1.0 + jnp.exp(-jnp.abs(z)))
    g = -jnp.exp(alog_ref[...]) * softplus
    row = lax.broadcasted_iota(jnp.int32, (r, r), 0)
    col = lax.broadcasted_iota(jnp.int32, (r, r), 1)
    same = (row // chunk) == (col // chunk)
    gam_ref[...] = _mask_dot(jnp.where(same & (col <= row), 1.0, 0.0).astype(BF16), g)
    gtot_ref[...] = _mask_dot(jnp.where(same, 1.0, 0.0).astype(BF16), g)


def _gdn_gates(x, a_log, dt_bias, chunk):
    m, two_nv = x.shape
    nv = two_nv // 2
    tr = _tile(m, 256, max(chunk, SUBLANES))
    out = jax.ShapeDtypeStruct((m, nv), F32)
    return pl.pallas_call(
        functools.partial(_gdn_gates_kernel, nv=nv, chunk=chunk),
        grid=(m // tr,),
        in_specs=[pl.BlockSpec((tr, two_nv), lambda i: (i, 0)),
                  pl.BlockSpec((1, nv), lambda i: (0, 0)),
                  pl.BlockSpec((1, nv), lambda i: (0, 0))],
        out_specs=[pl.BlockSpec((tr, nv), lambda i: (i, 0))] * 3,
        out_shape=[out, out, out],
        compiler_params=_cparams("parallel"),
        name="gdn_gates",
    )(x, a_log, dt_bias)


def _lane_sum_bcast(x):
    hi = x.astype(BF16)
    lo = (x - hi.astype(F32)).astype(BF16)
    return jnp.dot(jnp.concatenate([hi, lo], axis=1), jnp.ones((2 * HEAD_DIM, HEAD_DIM), BF16),
                   preferred_element_type=F32)


def _l2_heads(y, mult):
    parts = []
    for h in range(y.shape[1] // HEAD_DIM):
        yh = y[:, h * HEAD_DIM:(h + 1) * HEAD_DIM]
        parts.append(yh * (lax.rsqrt(_lane_sum_bcast(yh * yh) + NORM_EPS) * mult))
    return jnp.concatenate(parts, axis=1)


def _conv_finish(y, kind_q, kind_k, o_ref, scale):
    y = _silu(y)

    @pl.when(kind_q)
    def _():
        o_ref[...] = _l2_heads(y, scale)

    @pl.when(kind_k)
    def _():
        o_ref[...] = _l2_heads(y, 1.0)

    @pl.when(jnp.logical_not(kind_q | kind_k))
    def _():
        o_ref[...] = y


def _conv_sample_kernel(u_ref, buf_ref, cw_ref, o_ref, cache_ref, *, seq, nq_tiles, scale):
    j = pl.program_id(1)
    u = u_ref[...]
    r = u.shape[0]
    nb = r // seq
    tok = lax.broadcasted_iota(jnp.int32, (r, 1), 0) % seq
    hist = [_mod_rows(buf_ref[slot], r) for slot in range(CONV_W - 1)]
    cw = cw_ref[...]
    y = cw[CONV_W - 1:CONV_W] * u
    for s in range(1, CONV_W):
        tap = pltpu.roll(u, s, axis=0)
        for t in range(min(s, seq)):
            tap = jnp.where(tok == t, hist[t + CONV_W - 1 - s], tap)
        y = y + cw[CONV_W - 1 - s:CONV_W - s] * tap
    _conv_finish(y, j < nq_tiles, (j >= nq_tiles) & (j < 2 * nq_tiles), o_ref, scale)
    seq_i = lax.broadcasted_iota(jnp.int32, (nb, r), 0)
    row_i = lax.broadcasted_iota(jnp.int32, (nb, r), 1)
    for slot in range(CONV_W - 1):
        pick = row_i == seq_i * seq + (seq - (CONV_W - 1) + slot)
        cache_ref[slot] = _mask_dot(jnp.where(pick, 1.0, 0.0).astype(BF16), u).astype(cache_ref.dtype)


def _conv_sample(proj, col0, conv_buf, conv_w, seq, qk_w, ch):
    m = proj.shape[0]
    wc = _tile(qk_w, 512, HEAD_DIM)
    tr = _tile(m, 256, SUBLANES * seq)
    c0 = col0 // wc
    assert col0 % wc == 0 and tr % seq == 0 and seq >= CONV_W - 1
    nseq = tr // seq
    slots = pl.BlockSpec((CONV_W - 1, nseq, wc), lambda i, j: (0, i, j))
    buf = jnp.transpose(conv_buf, (1, 0, 2))
    qkv, cache = pl.pallas_call(
        functools.partial(_conv_sample_kernel, seq=seq, nq_tiles=qk_w // wc,
                          scale=float(HEAD_DIM) ** -0.5),
        grid=(m // tr, ch // wc),
        in_specs=[pl.BlockSpec((tr, wc), lambda i, j: (i, c0 + j)),
                  slots,
                  pl.BlockSpec((None, CONV_W, wc), lambda i, j: (0, 0, j))],
        out_specs=[pl.BlockSpec((tr, wc), lambda i, j: (i, j)), slots],
        out_shape=[jax.ShapeDtypeStruct((m, ch), F32), jax.ShapeDtypeStruct(buf.shape, conv_buf.dtype)],
        compiler_params=_cparams("parallel", "parallel"),
        name="gdn_conv_sample",
    )(proj, buf, conv_w)
    return qkv, jnp.transpose(cache, (1, 0, 2))


GDN_TILE = 64


def _mm_tile(a, b):
    return _bdot(a[:, :GDN_TILE], b)


def _lane_bcast_all(x, sel):
    return _mask_dot_right(x, sel)


def _mask_dot_right(x, m_bf):
    hi, mid, lo = _split3(x)
    return (jnp.dot(hi, m_bf, preferred_element_type=F32)
            + jnp.dot(mid, m_bf, preferred_element_type=F32)
            + jnp.dot(lo, m_bf, preferred_element_type=F32))


def _head_select(hb):
    head = lax.broadcasted_iota(jnp.int32, (hb, hb * HEAD_DIM), 0)
    lane_head = lax.broadcasted_iota(jnp.int32, (hb, hb * HEAD_DIM), 1) // HEAD_DIM
    return jnp.where(head == lane_head, 1.0, 0.0).astype(BF16)


def _unit_lower_inverses(a_list, rows, cols, block):
    eye = jnp.where(rows == cols, 1.0, 0.0)
    pairs = (rows >> 1) == (cols >> 1)
    d = [eye - jnp.where(pairs, a, 0.0) for a in a_list]
    b = 2
    while b < block:
        sh = b.bit_length()
        coupling = ((rows >> sh) == (cols >> sh)) & ((rows >> (sh - 1)) != (cols >> (sh - 1)))
        m1 = [_mm_tile(jnp.where(coupling, a, 0.0), di) for a, di in zip(a_list, d)]
        m2 = [_mm_tile(di, mi) for di, mi in zip(d, m1)]
        d = [di - mi for di, mi in zip(d, m2)]
        b *= 2
    return d


def _gdn_pair_products(q, k):
    k_pad = jnp.concatenate([k, jnp.zeros_like(k)], axis=0)
    kq = _bdot_nt(jnp.concatenate([k, q], axis=0), k_pad)
    return kq[:GDN_TILE], kq[GDN_TILE:]


def _gdn_wy(tiles, rep, rows, cols, incl, strict, block):
    heads = [slice(h * HEAD_DIM, (h + 1) * HEAD_DIM) for h in range(len(tiles[0][2]))]
    prods = [[_gdn_pair_products(q, k) for q, k in zip(qs, ks)] for qs, ks, *_ in tiles]
    decays, a_list = [], []
    for (qs, ks, vs, bc, gc, gr), prod in zip(tiles, prods):
        for h, sl in enumerate(heads):
            decay = jnp.where(incl, jnp.exp(gc[:, sl] - gr[h:h + 1, :]), 0.0)
            decays.append(decay)
            a_list.append(jnp.where(strict, bc[:, sl] * prod[h // rep][0] * decay, 0.0))
    t_inv = _unit_lower_inverses(a_list, rows, cols, block)
    xs = []
    for t, (qs, ks, vs, bc, gc, gr) in enumerate(tiles):
        for h, sl in enumerate(heads):
            beta = bc[:, sl]
            rhs = jnp.concatenate([vs[h] * beta, ks[h // rep] * (beta * jnp.exp(gc[:, sl]))], axis=1)
            xs.append(_mm_tile(t_inv[t * len(heads) + h], rhs))
    out = []
    for t in range(len(tiles)):
        out.append([(xs[t * len(heads) + h][:, :HEAD_DIM], xs[t * len(heads) + h][:, HEAD_DIM:],
                     prods[t][h // rep][1] * decays[t * len(heads) + h]) for h in range(len(heads))])
    return out


def _tile_iotas():
    shape = (GDN_TILE, HEAD_DIM)
    return lax.broadcasted_iota(jnp.int32, shape, 0), lax.broadcasted_iota(jnp.int32, shape, 1)


GDN_CHUNKS_PER_PASS = 4


def _causal_conv(x_scr, cw, r0, n, cols):
    y = None
    for s in range(CONV_W):
        lo = SUBLANES + r0 - s
        term = cw[CONV_W - 1 - s:CONV_W - s, cols] * x_scr[lo:lo + n, cols]
        y = term if y is None else y + term
    return y


def _gdn_prompt_kernel(uq_ref, uk_ref, uv_ref, pq_ref, pk_ref, pv_ref, cwq_ref, cwk_ref, cwv_ref,
                       z_ref, bc_ref, gc_ref, gt_ref, gr_ref, ng_ref,
                       o_ref, st_ref, s_scr, u_scr, w_scr, a_scr, xq_scr, xk_scr, xv_scr, q_scr, k_scr, gcb_scr,
                       *, hb, rep, scale):
    t = pl.program_id(2)
    c = GDN_CHUNK
    tt = uq_ref.shape[0]
    n_chunks = tt // c

    @pl.when(t == 0)
    def _():
        s_scr[...] = jnp.zeros_like(s_scr)

    keep = jnp.where(t == 0, 0.0, 1.0)
    for x_scr, prev_ref, cur_ref in ((xq_scr, pq_ref, uq_ref), (xk_scr, pk_ref, uk_ref), (xv_scr, pv_ref, uv_ref)):
        x_scr[0:SUBLANES, :] = prev_ref[...] * keep
        x_scr[SUBLANES:SUBLANES + tt, :] = cur_ref[...]

    rows_i, cols_i = _tile_iotas()
    incl = cols_i <= rows_i
    strict = cols_i < rows_i
    ng = ng_ref[...]
    sel = _head_select(hb)
    cwq, cwk, cwv = cwq_ref[...], cwk_ref[...], cwv_ref[...]
    heads = [slice(h * HEAD_DIM, (h + 1) * HEAD_DIM) for h in range(hb)]

    def chunk_operands(ci):
        rows = pl.ds(ci * c, c)
        qs = [_l2_heads(_silu(_causal_conv(xq_scr, cwq, ci * c, c, heads[i])), scale) for i in range(hb // rep)]
        ks = [_l2_heads(_silu(_causal_conv(xk_scr, cwk, ci * c, c, heads[i])), 1.0) for i in range(hb // rep)]
        vs = [_silu(_causal_conv(xv_scr, cwv, ci * c, c, heads[h])) for h in range(hb)]
        for i in range(hb // rep):
            q_scr[rows, heads[i]] = qs[i]
            k_scr[rows, heads[i]] = ks[i]
        gc = _lane_bcast_all(gc_ref[rows, :], sel)
        gcb_scr[rows, :] = gc
        return (qs, ks, vs, _lane_bcast_all(bc_ref[rows, :], sel), gc, gr_ref[ci])

    per = GDN_CHUNKS_PER_PASS if n_chunks % GDN_CHUNKS_PER_PASS == 0 else 1
    for c0 in range(0, n_chunks, per):
        wy = _gdn_wy([chunk_operands(ci) for ci in range(c0, c0 + per)], rep, rows_i, cols_i, incl, strict, c)
        for j in range(per):
            rows = pl.ds((c0 + j) * c, c)
            for h, (u, w, attn) in enumerate(wy[j]):
                sl = slice(h * HEAD_DIM, (h + 1) * HEAD_DIM)
                u_scr[rows, sl] = u
                w_scr[rows, sl] = w.astype(w_scr.dtype)
                a_scr[rows, sl] = attn.astype(a_scr.dtype)

    pairs = range(hb // rep)
    s_pairs = [s_scr[pr] for pr in pairs]
    outs = []
    for ci in range(n_chunks):
        rows = pl.ds(ci * c, c)
        gc = gcb_scr[rows, :]
        gt = _lane_bcast_all(gt_ref[rows, :], sel)
        qs = [_bdot(q_scr[rows, heads[pr]], s_pairs[pr]) for pr in pairs]
        v_new = [u_scr[rows, sl] - _bdot(w_scr[rows, sl], s_pairs[h // rep][:, heads[h % rep]])
                 for h, sl in enumerate(heads)]
        outs.append([qs[h // rep][:, heads[h % rep]] * jnp.exp(gc[:, sl])
                     + _bdot(a_scr[rows, sl][:, :c], v_new[h]) for h, sl in enumerate(heads)])
        for pr in pairs:
            k_t = _transpose_pad(k_scr[rows, heads[pr]])[:, :c]
            mine = range(pr * rep, (pr + 1) * rep)
            v_scaled = jnp.concatenate([v_new[h] * jnp.exp(gt[:, heads[h]] - gc[:, heads[h]]) for h in mine], axis=1)
            decays = jnp.concatenate([jnp.exp(gt[0:1, heads[h]]) for h in mine], axis=1)
            s_pairs[pr] = decays * s_pairs[pr] + _bdot(k_t, v_scaled)
    for pr in pairs:
        s_scr[pr] = s_pairs[pr]
    for ci in range(n_chunks):
        rows = pl.ds(ci * c, c)
        for h, sl in enumerate(heads):
            o_ref[rows, sl] = _head_out(outs[ci][h], z_ref[rows, sl], ng).astype(o_ref.dtype)

    @pl.when(t == pl.num_programs(2) - 1)
    def _():
        for h in range(hb):
            st_ref[h] = s_scr[h // rep][:, (h % rep) * HEAD_DIM:(h % rep + 1) * HEAD_DIM].astype(st_ref.dtype)


def _gdn_layouts(beta, gam, gtot, nv, hb):
    m = beta.shape[0]
    nhb = nv // hb
    as_cols = lambda a: a.reshape(m, nhb, hb).transpose(1, 0, 2)
    g_rows = gam.reshape(m // GDN_TILE, GDN_TILE, nhb, hb).transpose(2, 0, 3, 1)
    g_rows = jnp.pad(g_rows, ((0, 0), (0, 0), (0, 0), (0, HEAD_DIM - GDN_TILE)))
    return as_cols(beta), as_cols(gam), as_cols(gtot), g_rows


def _gdn_prompt(proj, conv_col0, z_col0, conv_w, gate_arrays, norm_g, batch, seq, qk_heads, v_heads,
                state_dtype):
    rep = v_heads // qk_heads
    hb = _tile(v_heads, 8, rep)
    tt = _tile(seq, 256, GDN_CHUNK)
    nt = seq // tt
    nhb = v_heads // hb
    wq = (hb // rep) * HEAD_DIM
    wv = hb * HEAD_DIM
    qk_w = qk_heads * HEAD_DIM
    bc, gc, gt, gr = _gdn_layouts(*gate_arrays, v_heads, hb)
    assert conv_col0 % wv == 0 and z_col0 % wv == 0 and GDN_CHUNK == GDN_TILE and tt % SUBLANES == 0
    q0 = conv_col0 // wq
    k0 = (conv_col0 + qk_w) // wq
    v0 = (conv_col0 + 2 * qk_w) // wv
    z0 = z_col0 // wv
    kc0 = qk_w // wq
    vc0 = 2 * qk_w // wv
    col_spec = pl.BlockSpec((None, tt, hb), lambda b, h, t: (h, b * nt + t, 0))
    sub = tt // SUBLANES

    def cur(width, c0):
        return pl.BlockSpec((tt, width), lambda b, h, t: (b * nt + t, c0 + h))

    def prev(width, c0):
        return pl.BlockSpec((SUBLANES, width), lambda b, h, t: (jnp.maximum((b * nt + t) * sub - 1, 0), c0 + h))

    def taps(width, c0):
        return pl.BlockSpec((None, CONV_W, width), lambda b, h, t: (0, 0, c0 + h))

    return pl.pallas_call(
        functools.partial(_gdn_prompt_kernel, hb=hb, rep=rep, scale=float(HEAD_DIM) ** -0.5),
        grid=(batch, nhb, nt),
        in_specs=[cur(wq, q0), cur(wq, k0), cur(wv, v0),
                  prev(wq, q0), prev(wq, k0), prev(wv, v0),
                  taps(wq, 0), taps(wq, kc0), taps(wv, vc0),
                  cur(wv, z0),
                  col_spec, col_spec, col_spec,
                  pl.BlockSpec((None, tt // GDN_TILE, hb, HEAD_DIM), lambda b, h, t: (h, b * nt + t, 0, 0)),
                  pl.BlockSpec((1, HEAD_DIM), lambda b, h, t: (0, 0))],
        out_specs=[pl.BlockSpec((tt, wv), lambda b, h, t: (b * nt + t, h)),
                   pl.BlockSpec((None, hb, HEAD_DIM, HEAD_DIM), lambda b, h, t: (b, h, 0, 0))],
        out_shape=[jax.ShapeDtypeStruct((batch * seq, v_heads * HEAD_DIM), BF16),
                   jax.ShapeDtypeStruct((batch, v_heads, HEAD_DIM, HEAD_DIM), state_dtype)],
        scratch_shapes=[pltpu.VMEM((hb // rep, HEAD_DIM, rep * HEAD_DIM), F32),
                        pltpu.VMEM((tt, wv), F32),
                        pltpu.VMEM((tt, wv), BF16),
                        pltpu.VMEM((tt, wv), BF16),
                        pltpu.VMEM((tt + SUBLANES, wq), F32),
                        pltpu.VMEM((tt + SUBLANES, wq), F32),
                        pltpu.VMEM((tt + SUBLANES, wv), F32),
                        pltpu.VMEM((tt, wq), F32),
                        pltpu.VMEM((tt, wq), F32),
                        pltpu.VMEM((tt, wv), F32)],
        compiler_params=_cparams("parallel", "parallel", "arbitrary"),
        name="gdn_prompt",
    )(proj, proj, proj, proj, proj, proj, conv_w, conv_w, conv_w, proj, bc, gc, gt, gr, norm_g)


def _gdn_sample_kernel(q_ref, k_ref, v_ref, z_ref, bc_ref, gc_ref, gt_ref, gr_ref, ng_ref, s0_ref,
                       o_ref, st_ref, *, hb, rep, seq):
    c = GDN_TILE
    nseq = c // seq
    per_group = SUBLANES // seq
    lsh = seq.bit_length() - 1
    rows_i, cols_i = _tile_iotas()
    same = (rows_i >> lsh) == (cols_i >> lsh)
    incl = (cols_i <= rows_i) & same
    strict = (cols_i < rows_i) & same
    row8 = lax.broadcasted_iota(jnp.int32, (SUBLANES, 1), 0)
    lane_seq = lax.broadcasted_iota(jnp.int32, (1, HEAD_DIM), 1) >> lsh
    ng = ng_ref[...]
    sel = _head_select(hb)
    bc = _lane_bcast_all(bc_ref[...], sel)
    gc = _lane_bcast_all(gc_ref[...], sel)
    gt = _lane_bcast_all(gt_ref[...], sel)
    gr = gr_ref[0]
    head = lambda ref, i: ref[:, i * HEAD_DIM:(i + 1) * HEAD_DIM]
    qs = [head(q_ref, i) for i in range(hb // rep)]
    ks = [head(k_ref, i) for i in range(hb // rep)]
    wy = _gdn_wy([(qs, ks, [head(v_ref, h) for h in range(hb)], bc, gc, gr)], rep,
                 rows_i, cols_i, incl, strict, seq)[0]
    k_t_seq = []
    for k in ks:
        k_t = _transpose_pad(k)
        k_t_seq.append(jnp.concatenate([jnp.where(lane_seq == b, k_t, 0.0)[:, :c] for b in range(nseq)],
                                       axis=0).astype(BF16))
    qe = [qs[h // rep] * jnp.exp(head(gc, h)) for h in range(hb)]
    ws_parts = [[] for _ in range(hb)]
    qs_parts = [[] for _ in range(hb)]
    for g in range(c // SUBLANES):
        gsl = slice(g * SUBLANES, (g + 1) * SUBLANES)
        for h in range(hb):
            acc = jnp.zeros((2 * SUBLANES, HEAD_DIM), F32)
            for p in range(per_group):
                valid = (row8 >= p * seq) & (row8 < (p + 1) * seq)
                lhs = jnp.concatenate([jnp.where(valid, wy[h][1][gsl], 0.0),
                                       jnp.where(valid, qe[h][gsl], 0.0)], axis=0)
                acc = acc + _bdot(lhs, s0_ref[g * per_group + p, h])
            ws_parts[h].append(acc[:SUBLANES])
            qs_parts[h].append(acc[SUBLANES:])
    v_new = [wy[h][0] - jnp.concatenate(ws_parts[h], axis=0) for h in range(hb)]
    outs = [jnp.concatenate(qs_parts[h], axis=0) + _bdot(wy[h][2][:, :c], v_new[h]) for h in range(hb)]
    upds = [jnp.dot(k_t_seq[h // rep], (v_new[h] * jnp.exp(head(gt, h) - head(gc, h))).astype(BF16),
                    preferred_element_type=F32) for h in range(hb)]
    for h in range(hb):
        o_ref[:, h * HEAD_DIM:(h + 1) * HEAD_DIM] = _head_out(outs[h], head(z_ref, h), ng).astype(o_ref.dtype)
        decay = jnp.exp(head(gt, h))
        for b in range(nseq):
            st_ref[b, h] = (decay[b * seq:b * seq + 1] * s0_ref[b, h].astype(F32)
                            + upds[h][b * HEAD_DIM:(b + 1) * HEAD_DIM]).astype(st_ref.dtype)


def _gdn_sample(qkv, proj, z_col0, gate_arrays, norm_g, s0, batch, seq, qk_heads, v_heads):
    rep = v_heads // qk_heads
    hb = _tile(v_heads, 8, rep)
    assert GDN_TILE % seq == 0 and SUBLANES % seq == 0 and seq & (seq - 1) == 0
    nb = GDN_TILE // seq
    assert batch % nb == 0
    r = GDN_TILE
    nhb = v_heads // hb
    wq = (hb // rep) * HEAD_DIM
    wv = hb * HEAD_DIM
    qk_w = qk_heads * HEAD_DIM
    bc, gc, gt, gr = _gdn_layouts(*gate_arrays, v_heads, hb)
    k0 = qk_w // wq
    v0 = 2 * qk_w // wv
    z0 = z_col0 // wv
    assert z_col0 % wv == 0
    st_spec = pl.BlockSpec((nb, hb, HEAD_DIM, HEAD_DIM), lambda b, h: (b, h, 0, 0))
    col_spec = pl.BlockSpec((None, r, hb), lambda b, h: (h, b, 0))
    return pl.pallas_call(
        functools.partial(_gdn_sample_kernel, hb=hb, rep=rep, seq=seq),
        grid=(batch // nb, nhb),
        in_specs=[pl.BlockSpec((r, wq), lambda b, h: (b, h)),
                  pl.BlockSpec((r, wq), lambda b, h: (b, k0 + h)),
                  pl.BlockSpec((r, wv), lambda b, h: (b, v0 + h)),
                  pl.BlockSpec((r, wv), lambda b, h: (b, z0 + h)),
                  col_spec, col_spec, col_spec,
                  pl.BlockSpec((None, 1, hb, HEAD_DIM), lambda b, h: (h, b, 0, 0)),
                  pl.BlockSpec((1, HEAD_DIM), lambda b, h: (0, 0)),
                  st_spec],
        out_specs=[pl.BlockSpec((r, wv), lambda b, h: (b, h)), st_spec],
        out_shape=[jax.ShapeDtypeStruct((batch * seq, v_heads * HEAD_DIM), BF16),
                   jax.ShapeDtypeStruct(s0.shape, s0.dtype)],
        compiler_params=_cparams("parallel", "parallel"),
        name="gdn_sample",
    )(qkv, qkv, qkv, proj, bc, gc, gt, gr, norm_g, s0)


def _merge_kernel(oh_ref, og_ref, wa_ref, wb_ref, ga_ref, gb_ref, o_ref):
    ya = _bdot(oh_ref[...], wa_ref[...])
    yb = _bdot(og_ref[...], wb_ref[...])
    o_ref[...] = (_sigmoid(ga_ref[...]) * ya + _sigmoid(gb_ref[...]) * yb).astype(o_ref.dtype)


def _merge(o_hg, o_gdn, w_a, w_b, gates, tm):
    m, ka = o_hg.shape
    kb = o_gdn.shape[1]
    d = w_a.shape[-1]
    tn = _tile(d, 256, 128)
    nj = d // tn
    return pl.pallas_call(
        _merge_kernel,
        grid=(m // tm, nj),
        in_specs=[pl.BlockSpec((tm, ka), lambda i, j: (i, 0)),
                  pl.BlockSpec((tm, kb), lambda i, j: (i, 0)),
                  pl.BlockSpec((None, ka, tn), lambda i, j: (0, 0, j)),
                  pl.BlockSpec((None, kb, tn), lambda i, j: (0, 0, j)),
                  pl.BlockSpec((tm, tn), lambda i, j: (i, j)),
                  pl.BlockSpec((tm, tn), lambda i, j: (i, nj + j))],
        out_specs=pl.BlockSpec((tm, tn), lambda i, j: (i, j)),
        out_shape=jax.ShapeDtypeStruct((m, d), BF16),
        compiler_params=_cparams("parallel", "arbitrary"),
        name="merge",
    )(o_hg, o_gdn, w_a, w_b, gates, gates)


def _outproj_kernel(m_ref, w_ref, x_ref, g1_ref, sc_ref, sh_ref, ng_ref, h_ref, a_ref):
    ng = ng_ref[...]
    mix = _bdot(m_ref[...], w_ref[...])
    n = mix.shape[0]
    h = x_ref[...].astype(F32) + _mod_rows(g1_ref[...], n) * (_rms(mix) * ng[0:1])
    h_ref[...] = h
    a_ref[...] = ((_rms(h) * ng[1:2]) * (1.0 + _mod_rows(sc_ref[...], n))
                  + _mod_rows(sh_ref[...], n)).astype(a_ref.dtype)


def _outproj(merged, w_o_bf16, x, mod, norm_g2, rows_per_seq, tm):
    m, d = x.shape
    kdim = merged.shape[1]
    row = lambda i: (i, 0)
    return pl.pallas_call(
        _outproj_kernel,
        grid=(m // tm,),
        in_specs=[pl.BlockSpec((tm, kdim), row),
                  pl.BlockSpec((kdim, d), lambda i: (0, 0), pipeline_mode=pl.Buffered(1)),
                  pl.BlockSpec((tm, d), row),
                  _mod_spec(mod, 2, tm, rows_per_seq),
                  _mod_spec(mod, 4, tm, rows_per_seq),
                  _mod_spec(mod, 3, tm, rows_per_seq),
                  pl.BlockSpec((2, d), lambda i: (0, 0))],
        out_specs=[pl.BlockSpec((tm, d), row), pl.BlockSpec((tm, d), row)],
        out_shape=[jax.ShapeDtypeStruct((m, d), F32), jax.ShapeDtypeStruct((m, d), BF16)],
        compiler_params=_cparams("parallel"),
        name="out_proj",
    )(merged, w_o_bf16, x, mod, mod, mod, norm_g2)


def _cast_kernel(x_ref, o_ref):
    o_ref[...] = x_ref[...].astype(o_ref.dtype)


def _to_bf16(w):
    kdim, n = w.shape
    tk = _tile(kdim, 512, 16)
    return pl.pallas_call(
        _cast_kernel,
        grid=(kdim // tk,),
        in_specs=[pl.BlockSpec((tk, n), lambda i: (i, 0))],
        out_specs=pl.BlockSpec((tk, n), lambda i: (i, 0)),
        out_shape=jax.ShapeDtypeStruct((kdim, n), BF16),
        compiler_params=_cparams("parallel"),
        name="weight_to_bf16",
    )(w)


def _ffn_up_kernel(a_ref, w_ref, o_ref):
    o_ref[...] = jnp.square(jnp.maximum(_bdot(a_ref[...], w_ref[...]), 0.0)).astype(o_ref.dtype)


def _ffn_up(a, w_up, tm):
    m, d = a.shape
    dff = w_up.shape[-1]
    tf = _tile(dff, 512, 128)
    return pl.pallas_call(
        _ffn_up_kernel,
        grid=(m // tm, dff // tf),
        in_specs=[pl.BlockSpec((tm, d), lambda i, f: (i, 0)),
                  pl.BlockSpec((None, d, tf), lambda i, f: (0, 0, f))],
        out_specs=pl.BlockSpec((tm, tf), lambda i, f: (i, f)),
        out_shape=jax.ShapeDtypeStruct((m, dff), BF16),
        compiler_params=_cparams("parallel", "arbitrary"),
        name="ffn_up",
    )(a, w_up)


EPILOGUE_ROWS = 128


def _ffn_down_kernel(hid_ref, w_ref, h_ref, g2_ref, ng_ref, y_ref):
    j = pl.program_id(1)
    tn = w_ref.shape[1]
    y_ref[:, pl.ds(pl.multiple_of(j * tn, tn), tn)] = _bdot(hid_ref[...], w_ref[...])

    @pl.when(j == pl.num_programs(1) - 1)
    def _():
        tm = y_ref.shape[0]
        step = min(EPILOGUE_ROWS, tm)
        g_rows = g2_ref.shape[0]
        g_step = max(g_rows * step // tm, 1)

        def body(r, carry):
            rows = pl.ds(pl.multiple_of(r * step, step), step)
            g2 = g2_ref[...] if g_rows == 1 else _mod_rows(
                g2_ref[pl.ds(pl.multiple_of(r * g_step, g_step), g_step), :], step)
            y_ref[rows, :] = h_ref[rows, :] + g2 * (_rms(y_ref[rows, :]) * ng_ref[...])
            return carry

        lax.fori_loop(0, y_ref.shape[0] // step, body, 0)


def _ffn_down(hid, w_down, h, mod, norm_g, rows_per_seq, tm, out_dtype):
    m, d = h.shape
    dff = hid.shape[1]
    tn = _tile(d, 512, 128)
    row = lambda i, j: (i, 0)
    assert out_dtype == F32, "the resident output block holds the f32 pre-norm result"
    return pl.pallas_call(
        _ffn_down_kernel,
        grid=(m // tm, d // tn),
        in_specs=[pl.BlockSpec((tm, dff), row),
                  pl.BlockSpec((dff, tn), lambda i, j: (0, j)),
                  pl.BlockSpec((tm, d), row),
                  _mod_spec(mod, 5, tm, rows_per_seq),
                  pl.BlockSpec((1, d), lambda i, j: (0, 0))],
        out_specs=pl.BlockSpec((tm, d), row),
        out_shape=jax.ShapeDtypeStruct((m, d), out_dtype),
        compiler_params=_cparams("parallel", "arbitrary", vmem_limit=VMEM_LIMIT_FFN_DOWN),
        name="ffn_down",
    )(hid, w_down, h, mod, norm_g)


def _run_group(x3, mod, rows_per_seq_for_mod, conv_buf, s_hg, s_gdn, lb, p, prompt):
    batch, seq, d = x3.shape
    m = batch * seq
    x = x3.reshape(m, d)
    hg_heads, gv_heads = p["hg_heads"], p["gv_heads"]
    gq_heads = p["gq_heads"]
    hg_w = hg_heads * HEAD_DIM
    gq_w = gq_heads * HEAD_DIM
    gv_w = gv_heads * HEAD_DIM
    ch = 2 * gq_w + gv_w
    main = 4 * hg_w + ch + gv_w
    conv0 = 4 * hg_w
    z0 = conv0 + ch
    rows_unit = seq if prompt else m
    tm = _tile(rows_unit, 512, SUBLANES)
    tm_proj = _tile(rows_unit, 2048, SUBLANES)
    tm_wide = _tile(rows_unit, 1024, SUBLANES)

    a1 = _prenorm(x, p["norm_g"][0:1], mod, 1, 0, rows_per_seq_for_mod, tm)
    tn_pref = 512 if tm_proj > 512 else 1024
    proj = _project(a1, p["w_in"], main, tm_proj, _tile(main, tn_pref, 128))
    tail = _project(a1, p["w_tail"], p["w_tail"].shape[0], tm_proj, p["w_tail"].shape[0])
    gates = _project(a1, p["w_gates"], 2 * d, tm_proj, _tile(2 * d, tn_pref, 128))

    if prompt:
        o_hg, s_hg_new = _hgrn_prompt(proj, lb, p["hg_norm_g"], batch, seq, hg_heads, p["hg_dtype"])
        gate_arrays = _gdn_gates(tail, p["a_log"], p["dt_bias"], GDN_CHUNK)
        o_gdn, s_gdn_new = _gdn_prompt(proj, conv0, z0, p["conv_w"], gate_arrays, p["gdn_norm_g"], batch, seq,
                                       gq_heads, gv_heads, p["gdn_dtype"])
        assert seq % SUBLANES == 0
        conv_new = proj.reshape(batch, seq, main)[:, seq - (CONV_W - 1):, conv0:conv0 + ch]
    else:
        o_hg, s_hg_new = _hgrn_sample(proj, lb, p["hg_norm_g"], s_hg, batch, seq, hg_heads)
        gate_arrays = _gdn_gates(tail, p["a_log"], p["dt_bias"], seq)
        qkv, conv_new = _conv_sample(proj, conv0, conv_buf.astype(F32), p["conv_w"], seq, gq_w, ch)
        o_gdn, s_gdn_new = _gdn_sample(qkv, proj, z0, gate_arrays, p["gdn_norm_g"], s_gdn, batch, seq,
                                       gq_heads, gv_heads)

    merged = _merge(o_hg, o_gdn, p["w_out_hg"], p["w_out_gdn"], gates, tm_wide)
    h1, a2 = _outproj(merged, p["w_o_bf16"], x, mod, p["norm_g"][1:3], rows_per_seq_for_mod, tm)
    hid = _ffn_up(a2, p["w_up"], tm_proj)
    y = _ffn_down(hid, p["w_down_bf16"], h1, mod, p["norm_g"][3:4], rows_per_seq_for_mod, tm, x3.dtype)
    return y.reshape(batch, seq, d), conv_new, s_hg_new, s_gdn_new


def kernel(x_prompt, x_sample, state_hgrn, state_gdn, cache_conv, c_prompt, c_sample, lb_logits, w_ada,
           b_ada, norm_g, w_in, conv_w, A_log, dt_bias, hg_norm_g, gdn_norm_g, w_out_hg, w_out_gdn, w_o,
           w_up, w_down):
    depth = w_in.shape[0]
    assert depth == 1, "single-layer trunk"
    bp, tp, d = x_prompt.shape
    bs, ts, _ = x_sample.shape
    hg_heads = state_hgrn.shape[2]
    gv_heads = state_gdn.shape[2]
    ch = cache_conv.shape[-1]
    gq_heads = (ch // HEAD_DIM - gv_heads) // 2
    assert tp >= CONV_W - 1 and ts >= CONV_W - 1
    assert tp % GDN_CHUNK == 0 and tp % HG_CHUNK == 0

    main = 4 * hg_heads * HEAD_DIM + ch + gv_heads * HEAD_DIM
    w_in_t = jnp.swapaxes(w_in[0], 0, 1)
    p = dict(
        hg_heads=hg_heads, gv_heads=gv_heads, gq_heads=gq_heads,
        hg_dtype=state_hgrn.dtype, gdn_dtype=state_gdn.dtype,
        norm_g=norm_g[0].astype(F32), w_in=w_in_t,
        w_tail=w_in_t[main:main + 2 * gv_heads],
        w_gates=w_in_t[main + 2 * gv_heads:],
        conv_w=conv_w.astype(F32), a_log=A_log.astype(F32), dt_bias=dt_bias.astype(F32),
        hg_norm_g=hg_norm_g.astype(F32), gdn_norm_g=gdn_norm_g.astype(F32),
        w_out_hg=w_out_hg, w_out_gdn=w_out_gdn, w_o_bf16=_to_bf16(w_o[0]), w_up=w_up,
        w_down_bf16=_to_bf16(w_down[0]),
    )

    lbs = jnp.cumsum(jax.nn.softmax(lb_logits.astype(F32), axis=0), axis=0)
    lb = lbs[0:1]

    n_c = bp + bs
    pad = (-n_c) % SUBLANES
    c_all = jnp.concatenate([c_prompt.astype(F32), c_sample.astype(F32), jnp.zeros((pad, d), F32)], axis=0)
    mod_all = _ada(c_all, w_ada, b_ada.astype(F32))
    mod_p = mod_all[:bp].reshape(bp * 6, 1, d)
    mod_s = mod_all[bp:bp + bs].reshape(bs, 6, d).transpose(1, 0, 2)

    y_p, conv_p, hg_p, gdn_p = _run_group(x_prompt, mod_p, tp, None, None, None, lb, p, prompt=True)
    y_s, conv_s, hg_s, gdn_s = _run_group(x_sample, mod_s, ts, cache_conv[0], state_hgrn[0], state_gdn[0],
                                          lb, p, prompt=False)
    return (y_p, y_s,
            hg_p[None], gdn_p[None], conv_p.astype(cache_conv.dtype)[None],
            hg_s[None], gdn_s[None], conv_s.astype(cache_conv.dtype)[None])
```

```python
import functools

import jax
import jax.numpy as jnp
from jax import lax
from jax.experimental import pallas as pl
from jax.experimental.pallas import tpu as pltpu

F32 = jnp.float32
BF16 = jnp.bfloat16

HEAD_DIM = 128
HG_CHUNK = 32
GDN_CHUNK = 64
CONV_W = 4
NORM_EPS = 1e-6
SUBLANES = 8
V7X_VMEM_BYTES = 64 * 1024 * 1024
VMEM_LIMIT = V7X_VMEM_BYTES * 3 // 4
VMEM_LIMIT_FFN_DOWN = V7X_VMEM_BYTES * 27 // 32


def _cparams(*sem, vmem_limit=VMEM_LIMIT):
    return pltpu.CompilerParams(dimension_semantics=sem, vmem_limit_bytes=vmem_limit)


def _tile(n, pref, quantum):
    if n <= pref:
        return n
    t = (pref // quantum) * quantum
    while t > quantum and n % t:
        t -= quantum
    assert n % t == 0, (n, pref, quantum)
    return t


def _bdot(a, b):
    return jnp.dot(a.astype(BF16), b.astype(BF16), preferred_element_type=F32)


def _bdot_nt(a, b):
    return lax.dot_general(a.astype(BF16), b.astype(BF16), (((1,), (1,)), ((), ())),
                           preferred_element_type=F32)


def _split3(x):
    hi = x.astype(BF16)
    r = x - hi.astype(F32)
    mid = r.astype(BF16)
    lo = (r - mid.astype(F32)).astype(BF16)
    return hi, mid, lo


def _mask_dot(m_bf, x):
    hi, mid, lo = _split3(x)
    return (jnp.dot(m_bf, hi, preferred_element_type=F32)
            + jnp.dot(m_bf, mid, preferred_element_type=F32)
            + jnp.dot(m_bf, lo, preferred_element_type=F32))


def _sigmoid(x):
    return 1.0 / (1.0 + jnp.exp(-x))


def _silu(x):
    return x * _sigmoid(x)


def _rms(x):
    return x * lax.rsqrt(jnp.mean(x * x, axis=-1, keepdims=True) + NORM_EPS)


def _transpose_pad(x):
    r = x.shape[0]
    if r < HEAD_DIM:
        x = jnp.concatenate([x, jnp.zeros((HEAD_DIM - r, HEAD_DIM), x.dtype)], axis=0)
    return x.T


def _tri_masks(c):
    row = lax.broadcasted_iota(jnp.int32, (c, c), 0)
    col = lax.broadcasted_iota(jnp.int32, (c, c), 1)
    return col <= row, col < row


def _ada_kernel(c_ref, w_ref, b_ref, o_ref):
    cs = _silu(c_ref[...])
    w = w_ref[...]
    c_hi = cs.astype(BF16)
    c_lo = (cs - c_hi.astype(F32)).astype(BF16)
    w_hi = w.astype(BF16)
    w_lo = (w - w_hi.astype(F32)).astype(BF16)
    acc = jnp.dot(c_hi, w_hi, preferred_element_type=F32)
    acc += jnp.dot(c_lo, w_hi, preferred_element_type=F32)
    acc += jnp.dot(c_hi, w_lo, preferred_element_type=F32)
    o_ref[...] = acc + b_ref[...]


def _ada(c_all, w_ada, b_ada):
    m, d = c_all.shape
    n = w_ada.shape[-1]
    tn = _tile(n, 512, 128)
    return pl.pallas_call(
        _ada_kernel,
        grid=(n // tn,),
        in_specs=[pl.BlockSpec((m, d), lambda j: (0, 0)),
                  pl.BlockSpec((None, d, tn), lambda j: (0, 0, j)),
                  pl.BlockSpec((1, tn), lambda j: (0, j))],
        out_specs=pl.BlockSpec((m, tn), lambda j: (0, j)),
        out_shape=jax.ShapeDtypeStruct((m, n), F32),
        compiler_params=_cparams("arbitrary"),
        name="ada_mod",
    )(c_all, w_ada, b_ada)


def _mod_spec(mod, k, tm, rows_per_seq):
    d = mod.shape[-1]
    if mod.shape[1] == 1:
        tiles_per_seq = rows_per_seq // tm
        return pl.BlockSpec((None, 1, d), lambda i, *_: ((i // tiles_per_seq) * 6 + k, 0, 0))
    return pl.BlockSpec((None, tm // rows_per_seq, d), lambda i, *_: (k, i, 0))


def _mod_rows(v, n):
    b = v.shape[0]
    if b == 1 or b == n:
        return v
    rep = n // b
    row = lax.broadcasted_iota(jnp.int32, (n, b), 0)
    col = lax.broadcasted_iota(jnp.int32, (n, b), 1)
    owner = (row >= col * rep) & (row < (col + 1) * rep)
    return _mask_dot(jnp.where(owner, 1.0, 0.0).astype(BF16), v)


def _prenorm_kernel(x_ref, g_ref, sc_ref, sh_ref, o_ref):
    xn = _rms(x_ref[...].astype(F32)) * g_ref[...]
    n = xn.shape[0]
    o_ref[...] = (xn * (1.0 + _mod_rows(sc_ref[...], n)) + _mod_rows(sh_ref[...], n)).astype(o_ref.dtype)


def _prenorm(x, g, mod, k_scale, k_shift, rows_per_seq, tm):
    m, d = x.shape
    return pl.pallas_call(
        _prenorm_kernel,
        grid=(m // tm,),
        in_specs=[pl.BlockSpec((tm, d), lambda i: (i, 0)),
                  pl.BlockSpec((1, d), lambda i: (0, 0)),
                  _mod_spec(mod, k_scale, tm, rows_per_seq),
                  _mod_spec(mod, k_shift, tm, rows_per_seq)],
        out_specs=pl.BlockSpec((tm, d), lambda i: (i, 0)),
        out_shape=jax.ShapeDtypeStruct((m, d), BF16),
        compiler_params=_cparams("parallel"),
        name="prenorm",
    )(x, g, mod, mod)


def _mm_nt_kernel(a_ref, w_ref, o_ref):
    o_ref[...] = _bdot_nt(a_ref[...], w_ref[...])


def _project(a, w_t, n_cols, tm, tn):
    m, kdim = a.shape
    return pl.pallas_call(
        _mm_nt_kernel,
        grid=(m // tm, n_cols // tn),
        in_specs=[pl.BlockSpec((tm, kdim), lambda i, j: (i, 0)),
                  pl.BlockSpec((tn, kdim), lambda i, j: (j, 0))],
        out_specs=pl.BlockSpec((tm, tn), lambda i, j: (i, j)),
        out_shape=jax.ShapeDtypeStruct((m, n_cols), F32),
        compiler_params=_cparams("parallel", "arbitrary"),
        name="in_proj",
    )(a, w_t)


def _hgrn_elementwise(hq, hf, lb):
    q = _silu(hq)
    f = lb + (1.0 - lb) * _sigmoid(hf)
    return q, jnp.log(f), 1.0 - f


def _hgrn_chunk_free(q, g, k, inp, n, tril, tril_bf):
    c = q.shape[0]
    heads = [slice(h * HEAD_DIM, (h + 1) * HEAD_DIM) for h in range(n)]
    big_g = _mask_dot(tril_bf, g)
    g_ref = big_g[c // 2:c // 2 + 1]
    g_last = big_g[c - 1:c]
    qe = q * jnp.exp(big_g - g_ref)
    ke = k * jnp.exp(g_ref - big_g)
    qs = q * jnp.exp(big_g)
    kl = k * jnp.exp(g_last - big_g)
    d_hi, d_mid, d_lo = [p.astype(F32) for p in _split3(jnp.exp(g_last))]
    extra = jnp.concatenate([d_hi, d_mid, d_lo, jnp.zeros((SUBLANES - 3, q.shape[1]), F32)], axis=0)
    a = [jnp.where(tril, _bdot_nt(qe[:, sl], ke[:, sl]), 0.0) for sl in heads]
    o_intra = [_bdot(a[h], inp[:, sl]) for h, sl in enumerate(heads)]
    t = [_transpose_pad(jnp.concatenate([kl[:, sl], extra[:, sl]], axis=0))[:, :c + SUBLANES] for sl in heads]
    piece_rows = lax.broadcasted_iota(jnp.int32, (SUBLANES, HEAD_DIM), 0) < 3
    ones_rows = jnp.where(piece_rows, 1.0, 0.0)
    zeros_c = jnp.zeros((c, HEAD_DIM), F32)
    uds = []
    for h, sl in enumerate(heads):
        rhs = jnp.concatenate([jnp.concatenate([inp[:, sl], zeros_c], axis=1),
                               jnp.concatenate([jnp.zeros((SUBLANES, HEAD_DIM), F32), ones_rows], axis=1)], axis=0)
        uds.append(_bdot(t[h], rhs))
    return o_intra, qs, uds


def _hgrn_chunk_heads(q, g, k, inp, states, tril, tril_bf):
    o_intra, qs, uds = _hgrn_chunk_free(q, g, k, inp, len(states), tril, tril_bf)
    out = []
    for h, s in enumerate(states):
        sl = slice(h * HEAD_DIM, (h + 1) * HEAD_DIM)
        out.append((o_intra[h] + _bdot(qs[:, sl], s), uds[h][:, HEAD_DIM:] * s + uds[h][:, :HEAD_DIM]))
    return out


def _head_out(o, gate, norm_g):
    mean_sq = _lane_sum_bcast(o * o) * (1.0 / HEAD_DIM)
    return o * lax.rsqrt(mean_sq + NORM_EPS) * norm_g * _silu(gate)


HGRN_CHUNKS_PER_PASS = 8


def _hgrn_prompt_kernel(q_ref, f_ref, i_ref, g_ref, lb_ref, ng_ref, o_ref, st_ref,
                        s_scr, oi_scr, qs_scr, ud_scr, *, hb):
    t = pl.program_id(2)
    c = HG_CHUNK
    n_chunks = q_ref.shape[0] // c
    per = HGRN_CHUNKS_PER_PASS if n_chunks % HGRN_CHUNKS_PER_PASS == 0 else 1
    heads = [slice(h * HEAD_DIM, (h + 1) * HEAD_DIM) for h in range(hb)]

    @pl.when(t == 0)
    def _():
        s_scr[...] = jnp.zeros_like(s_scr)

    tril, _ = _tri_masks(c)
    tril_bf = jnp.where(tril, 1.0, 0.0).astype(BF16)
    lb = lb_ref[...]
    ng = ng_ref[...]

    def free_part(it, carry):
        rows = [pl.ds(pl.multiple_of((it * per + j) * c, c), c) for j in range(per)]
        parts = [_hgrn_elementwise(q_ref[r, :], f_ref[r, :], lb) for r in rows]
        cat = lambda xs: jnp.concatenate(xs, axis=1)
        o_intra, qs, uds = _hgrn_chunk_free(cat([p[0] for p in parts]), cat([p[1] for p in parts]),
                                            cat([p[2] for p in parts]), cat([i_ref[r, :] for r in rows]),
                                            per * hb, tril, tril_bf)
        for j, r in enumerate(rows):
            qs_scr[r, :] = qs[:, j * hb * HEAD_DIM:(j + 1) * hb * HEAD_DIM].astype(qs_scr.dtype)
            for h, sl in enumerate(heads):
                oi_scr[r, sl] = o_intra[j * hb + h]
                ud_scr[(it * per + j) * hb + h] = uds[j * hb + h]
        return carry

    lax.fori_loop(0, n_chunks // per, free_part, 0)

    states = [s_scr[h] for h in range(hb)]
    outs = []
    for ci in range(n_chunks):
        rows = pl.ds(ci * c, c)
        outs.append([oi_scr[rows, sl] + _bdot(qs_scr[rows, sl], states[h]) for h, sl in enumerate(heads)])
        for h in range(hb):
            ud = ud_scr[ci * hb + h]
            states[h] = ud[:, HEAD_DIM:] * states[h] + ud[:, :HEAD_DIM]
    for h in range(hb):
        s_scr[h] = states[h]
    for ci in range(n_chunks):
        rows = pl.ds(ci * c, c)
        for h, sl in enumerate(heads):
            o_ref[rows, sl] = _head_out(outs[ci][h], g_ref[rows, sl], ng).astype(o_ref.dtype)

    @pl.when(t == pl.num_programs(2) - 1)
    def _():
        st_ref[...] = s_scr[...].astype(st_ref.dtype)


def _hgrn_prompt(proj, lb, norm_g, batch, seq, heads, state_dtype):
    hb = _tile(heads, 8, 1)
    tt = _tile(seq, 256, HG_CHUNK)
    w = hb * HEAD_DIM
    nhb = heads // hb
    nt = seq // tt

    def col(seg):
        return pl.BlockSpec((tt, w), lambda b, h, t: (b * nt + t, seg * nhb + h))

    return pl.pallas_call(
        functools.partial(_hgrn_prompt_kernel, hb=hb),
        grid=(batch, nhb, nt),
        in_specs=[col(0), col(1), col(2), col(3),
                  pl.BlockSpec((1, w), lambda b, h, t: (0, h)),
                  pl.BlockSpec((1, HEAD_DIM), lambda b, h, t: (0, 0))],
        out_specs=[pl.BlockSpec((tt, w), lambda b, h, t: (b * nt + t, h)),
                   pl.BlockSpec((None, hb, HEAD_DIM, HEAD_DIM), lambda b, h, t: (b, h, 0, 0))],
        out_shape=[jax.ShapeDtypeStruct((batch * seq, heads * HEAD_DIM), BF16),
                   jax.ShapeDtypeStruct((batch, heads, HEAD_DIM, HEAD_DIM), state_dtype)],
        scratch_shapes=[pltpu.VMEM((hb, HEAD_DIM, HEAD_DIM), F32),
                        pltpu.VMEM((tt, w), F32),
                        pltpu.VMEM((tt, w), BF16),
                        pltpu.VMEM(((tt // HG_CHUNK) * hb, HEAD_DIM, 2 * HEAD_DIM), F32)],
        compiler_params=_cparams("parallel", "parallel", "arbitrary"),
        name="hgrn_prompt",
    )(proj, proj, proj, proj, lb, norm_g)


def _hgrn_sample_kernel(q_ref, f_ref, i_ref, g_ref, lb_ref, ng_ref, s0_ref, o_ref, st_ref, *, hb, seq):
    c = SUBLANES
    per_group = c // seq
    tril, _ = _tri_masks(c)
    tril_bf = jnp.where(tril, 1.0, 0.0).astype(BF16)
    lb = lb_ref[...]
    ng = ng_ref[...]
    row = lax.broadcasted_iota(jnp.int32, (c, 1), 0)

    n_groups = q_ref.shape[0] // c
    valid = [(row >= p * seq) & (row < (p + 1) * seq) for p in range(per_group)]
    masked = lambda x: [jnp.where(v, x, 0.0) for v in valid]
    qs, gs, ks, inps, states = [], [], [], [], []
    for gi in range(n_groups):
        rows = pl.ds(gi * c, c)
        q, g, k = _hgrn_elementwise(q_ref[rows, :], f_ref[rows, :], lb)
        qs += masked(q)
        gs += masked(g)
        ks += masked(k)
        inps += [i_ref[rows, :]] * per_group
        states += [s0_ref[gi * per_group + p, h].astype(F32) for p in range(per_group) for h in range(hb)]
    cat = lambda xs: jnp.concatenate(xs, axis=1)
    res = _hgrn_chunk_heads(cat(qs), cat(gs), cat(ks), cat(inps), states, tril, tril_bf)
    for gi in range(n_groups):
        rows = pl.ds(gi * c, c)
        for h in range(hb):
            sl = slice(h * HEAD_DIM, (h + 1) * HEAD_DIM)
            o_all = jnp.zeros((c, HEAD_DIM), F32)
            for p in range(per_group):
                o, s_new = res[(gi * per_group + p) * hb + h]
                st_ref[gi * per_group + p, h] = s_new.astype(st_ref.dtype)
                o_all = jnp.where(valid[p], o, o_all)
            o_ref[rows, sl] = _head_out(o_all, g_ref[rows, sl], ng).astype(o_ref.dtype)


def _hgrn_sample(proj, lb, norm_g, s0, batch, seq, heads):
    assert SUBLANES % seq == 0
    hb = _tile(heads, 4, 1)
    nb = _tile(batch, 16, SUBLANES // seq)
    w = hb * HEAD_DIM
    nhb = heads // hb
    r = nb * seq

    def col(seg):
        return pl.BlockSpec((r, w), lambda b, h: (b, seg * nhb + h))

    st_spec = pl.BlockSpec((nb, hb, HEAD_DIM, HEAD_DIM), lambda b, h: (b, h, 0, 0))
    return pl.pallas_call(
        functools.partial(_hgrn_sample_kernel, hb=hb, seq=seq),
        grid=(batch // nb, nhb),
        in_specs=[col(0), col(1), col(2), col(3),
                  pl.BlockSpec((1, w), lambda b, h: (0, h)),
                  pl.BlockSpec((1, HEAD_DIM), lambda b, h: (0, 0)),
                  st_spec],
        out_specs=[pl.BlockSpec((r, w), lambda b, h: (b, h)), st_spec],
        out_shape=[jax.ShapeDtypeStruct((batch * seq, heads * HEAD_DIM), BF16),
                   jax.ShapeDtypeStruct(s0.shape, s0.dtype)],
        compiler_params=_cparams("parallel", "parallel"),
        name="hgrn_sample",
    )(proj, proj, proj, proj, lb, norm_g, s0)


def _gdn_gates_kernel(x_ref, alog_ref, dtb_ref, beta_ref, gam_ref, gtot_ref, *, nv, chunk):
    x = x_ref[...]
    r = x.shape[0]
    beta_ref[...] = _sigmoid(x[:, :nv])
    z = x[:, nv:] + dtb_ref[...]
    softplus = jnp.maximum(z, 0.0) + jnp.log(1.0 + jnp.exp(-jnp.abs(z)))
    g = -jnp.exp(alog_ref[...]) * softplus
    row = lax.broadcasted_iota(jnp.int32, (r, r), 0)
    col = lax.broadcasted_iota(jnp.int32, (r, r), 1)
    same = (row // chunk) == (col // chunk)
    gam_ref[...] = _mask_dot(jnp.where(same & (col <= row), 1.0, 0.0).astype(BF16), g)
    gtot_ref[...] = _mask_dot(jnp.where(same, 1.0, 0.0).astype(BF16), g)


def _gdn_gates(x, a_log, dt_bias, chunk):
    m, two_nv = x.shape
    nv = two_nv // 2
    tr = _tile(m, 256, max(chunk, SUBLANES))
    out = jax.ShapeDtypeStruct((m, nv), F32)
    return pl.pallas_call(
        functools.partial(_gdn_gates_kernel, nv=nv, chunk=chunk),
        grid=(m // tr,),
        in_specs=[pl.BlockSpec((tr, two_nv), lambda i: (i, 0)),
                  pl.BlockSpec((1, nv), lambda i: (0, 0)),
                  pl.BlockSpec((1, nv), lambda i: (0, 0))],
        out_specs=[pl.BlockSpec((tr, nv), lambda i: (i, 0))] * 3,
        out_shape=[out, out, out],
        compiler_params=_cparams("parallel"),
        name="gdn_gates",
    )(x, a_log, dt_bias)


def _lane_sum_bcast(x):
    hi = x.astype(BF16)
    lo = (x - hi.astype(F32)).astype(BF16)
    return jnp.dot(jnp.concatenate([hi, lo], axis=1), jnp.ones((2 * HEAD_DIM, HEAD_DIM), BF16),
                   preferred_element_type=F32)


def _l2_heads(y, mult):
    parts = []
    for h in range(y.shape[1] // HEAD_DIM):
        yh = y[:, h * HEAD_DIM:(h + 1) * HEAD_DIM]
        parts.append(yh * (lax.rsqrt(_lane_sum_bcast(yh * yh) + NORM_EPS) * mult))
    return jnp.concatenate(parts, axis=1)


def _conv_finish(y, kind_q, kind_k, o_ref, scale):
    y = _silu(y)

    @pl.when(kind_q)
    def _():
        o_ref[...] = _l2_heads(y, scale)

    @pl.when(kind_k)
    def _():
        o_ref[...] = _l2_heads(y, 1.0)

    @pl.when(jnp.logical_not(kind_q | kind_k))
    def _():
        o_ref[...] = y


def _conv_sample_kernel(u_ref, buf_ref, cw_ref, o_ref, cache_ref, *, seq, nq_tiles, scale):
    j = pl.program_id(1)
    u = u_ref[...]
    r = u.shape[0]
    nb = r // seq
    tok = lax.broadcasted_iota(jnp.int32, (r, 1), 0) % seq
    hist = [_mod_rows(buf_ref[slot], r) for slot in range(CONV_W - 1)]
    cw = cw_ref[...]
    y = cw[CONV_W - 1:CONV_W] * u
    for s in range(1, CONV_W):
        tap = pltpu.roll(u, s, axis=0)
        for t in range(min(s, seq)):
            tap = jnp.where(tok == t, hist[t + CONV_W - 1 - s], tap)
        y = y + cw[CONV_W - 1 - s:CONV_W - s] * tap
    _conv_finish(y, j < nq_tiles, (j >= nq_tiles) & (j < 2 * nq_tiles), o_ref, scale)
    seq_i = lax.broadcasted_iota(jnp.int32, (nb, r), 0)
    row_i = lax.broadcasted_iota(jnp.int32, (nb, r), 1)
    for slot in range(CONV_W - 1):
        pick = row_i == seq_i * seq + (seq - (CONV_W - 1) + slot)
        cache_ref[slot] = _mask_dot(jnp.where(pick, 1.0, 0.0).astype(BF16), u).astype(cache_ref.dtype)


def _conv_sample(proj, col0, conv_buf, conv_w, seq, qk_w, ch):
    m = proj.shape[0]
    wc = _tile(qk_w, 512, HEAD_DIM)
    tr = _tile(m, 256, SUBLANES * seq)
    c0 = col0 // wc
    assert col0 % wc == 0 and tr % seq == 0 and seq >= CONV_W - 1
    nseq = tr // seq
    slots = pl.BlockSpec((CONV_W - 1, nseq, wc), lambda i, j: (0, i, j))
    buf = jnp.transpose(conv_buf, (1, 0, 2))
    qkv, cache = pl.pallas_call(
        functools.partial(_conv_sample_kernel, seq=seq, nq_tiles=qk_w // wc,
                          scale=float(HEAD_DIM) ** -0.5),
        grid=(m // tr, ch // wc),
        in_specs=[pl.BlockSpec((tr, wc), lambda i, j: (i, c0 + j)),
                  slots,
                  pl.BlockSpec((None, CONV_W, wc), lambda i, j: (0, 0, j))],
        out_specs=[pl.BlockSpec((tr, wc), lambda i, j: (i, j)), slots],
        out_shape=[jax.ShapeDtypeStruct((m, ch), F32), jax.ShapeDtypeStruct(buf.shape, conv_buf.dtype)],
        compiler_params=_cparams("parallel", "parallel"),
        name="gdn_conv_sample",
    )(proj, buf, conv_w)
    return qkv, jnp.transpose(cache, (1, 0, 2))


GDN_TILE = 64


def _mm_tile(a, b):
    return _bdot(a[:, :GDN_TILE], b)


def _lane_bcast_all(x, sel):
    return _mask_dot_right(x, sel)


def _mask_dot_right(x, m_bf):
    hi, mid, lo = _split3(x)
    return (jnp.dot(hi, m_bf, preferred_element_type=F32)
            + jnp.dot(mid, m_bf, preferred_element_type=F32)
            + jnp.dot(lo, m_bf, preferred_element_type=F32))


def _head_select(hb):
    head = lax.broadcasted_iota(jnp.int32, (hb, hb * HEAD_DIM), 0)
    lane_head = lax.broadcasted_iota(jnp.int32, (hb, hb * HEAD_DIM), 1) // HEAD_DIM
    return jnp.where(head == lane_head, 1.0, 0.0).astype(BF16)


def _unit_lower_inverses(a_list, rows, cols, block):
    eye = jnp.where(rows == cols, 1.0, 0.0)
    pairs = (rows >> 1) == (cols >> 1)
    d = [eye - jnp.where(pairs, a, 0.0) for a in a_list]
    b = 2
    while b < block:
        sh = b.bit_length()
        coupling = ((rows >> sh) == (cols >> sh)) & ((rows >> (sh - 1)) != (cols >> (sh - 1)))
        m1 = [_mm_tile(jnp.where(coupling, a, 0.0), di) for a, di in zip(a_list, d)]
        m2 = [_mm_tile(di, mi) for di, mi in zip(d, m1)]
        d = [di - mi for di, mi in zip(d, m2)]
        b *= 2
    return d


def _gdn_pair_products(q, k):
    k_pad = jnp.concatenate([k, jnp.zeros_like(k)], axis=0)
    kq = _bdot_nt(jnp.concatenate([k, q], axis=0), k_pad)
    return kq[:GDN_TILE], kq[GDN_TILE:]


def _gdn_wy(tiles, rep, rows, cols, incl, strict, block):
    heads = [slice(h * HEAD_DIM, (h + 1) * HEAD_DIM) for h in range(len(tiles[0][2]))]
    prods = [[_gdn_pair_products(q, k) for q, k in zip(qs, ks)] for qs, ks, *_ in tiles]
    decays, a_list = [], []
    for (qs, ks, vs, bc, gc, gr), prod in zip(tiles, prods):
        for h, sl in enumerate(heads):
            decay = jnp.where(incl, jnp.exp(gc[:, sl] - gr[h:h + 1, :]), 0.0)
            decays.append(decay)
            a_list.append(jnp.where(strict, bc[:, sl] * prod[h // rep][0] * decay, 0.0))
    t_inv = _unit_lower_inverses(a_list, rows, cols, block)
    xs = []
    for t, (qs, ks, vs, bc, gc, gr) in enumerate(tiles):
        for h, sl in enumerate(heads):
            beta = bc[:, sl]
            rhs = jnp.concatenate([vs[h] * beta, ks[h // rep] * (beta * jnp.exp(gc[:, sl]))], axis=1)
            xs.append(_mm_tile(t_inv[t * len(heads) + h], rhs))
    out = []
    for t in range(len(tiles)):
        out.append([(xs[t * len(heads) + h][:, :HEAD_DIM], xs[t * len(heads) + h][:, HEAD_DIM:],
                     prods[t][h // rep][1] * decays[t * len(heads) + h]) for h in range(len(heads))])
    return out


def _tile_iotas():
    shape = (GDN_TILE, HEAD_DIM)
    return lax.broadcasted_iota(jnp.int32, shape, 0), lax.broadcasted_iota(jnp.int32, shape, 1)


GDN_CHUNKS_PER_PASS = 4


def _causal_conv(x_scr, cw, r0, n, cols):
    y = None
    for s in range(CONV_W):
        lo = SUBLANES + r0 - s
        term = cw[CONV_W - 1 - s:CONV_W - s, cols] * x_scr[lo:lo + n, cols]
        y = term if y is None else y + term
    return y


def _gdn_prompt_kernel(uq_ref, uk_ref, uv_ref, pq_ref, pk_ref, pv_ref, cwq_ref, cwk_ref, cwv_ref,
                       z_ref, bc_ref, gc_ref, gt_ref, gr_ref, ng_ref,
                       o_ref, st_ref, s_scr, u_scr, w_scr, a_scr, xq_scr, xk_scr, xv_scr, q_scr, k_scr, gcb_scr,
                       *, hb, rep, scale):
    t = pl.program_id(2)
    c = GDN_CHUNK
    tt = uq_ref.shape[0]
    n_chunks = tt // c

    @pl.when(t == 0)
    def _():
        s_scr[...] = jnp.zeros_like(s_scr)

    keep = jnp.where(t == 0, 0.0, 1.0)
    for x_scr, prev_ref, cur_ref in ((xq_scr, pq_ref, uq_ref), (xk_scr, pk_ref, uk_ref), (xv_scr, pv_ref, uv_ref)):
        x_scr[0:SUBLANES, :] = prev_ref[...] * keep
        x_scr[SUBLANES:SUBLANES + tt, :] = cur_ref[...]

    rows_i, cols_i = _tile_iotas()
    incl = cols_i <= rows_i
    strict = cols_i < rows_i
    ng = ng_ref[...]
    sel = _head_select(hb)
    cwq, cwk, cwv = cwq_ref[...], cwk_ref[...], cwv_ref[...]
    heads = [slice(h * HEAD_DIM, (h + 1) * HEAD_DIM) for h in range(hb)]

    def chunk_operands(ci):
        rows = pl.ds(ci * c, c)
        qs = [_l2_heads(_silu(_causal_conv(xq_scr, cwq, ci * c, c, heads[i])), scale) for i in range(hb // rep)]
        ks = [_l2_heads(_silu(_causal_conv(xk_scr, cwk, ci * c, c, heads[i])), 1.0) for i in range(hb // rep)]
        vs = [_silu(_causal_conv(xv_scr, cwv, ci * c, c, heads[h])) for h in range(hb)]
        for i in range(hb // rep):
            q_scr[rows, heads[i]] = qs[i]
            k_scr[rows, heads[i]] = ks[i]
        gc = _lane_bcast_all(gc_ref[rows, :], sel)
        gcb_scr[rows, :] = gc
        return (qs, ks, vs, _lane_bcast_all(bc_ref[rows, :], sel), gc, gr_ref[ci])

    per = GDN_CHUNKS_PER_PASS if n_chunks % GDN_CHUNKS_PER_PASS == 0 else 1
    for c0 in range(0, n_chunks, per):
        wy = _gdn_wy([chunk_operands(ci) for ci in range(c0, c0 + per)], rep, rows_i, cols_i, incl, strict, c)
        for j in range(per):
            rows = pl.ds((c0 + j) * c, c)
            for h, (u, w, attn) in enumerate(wy[j]):
                sl = slice(h * HEAD_DIM, (h + 1) * HEAD_DIM)
                u_scr[rows, sl] = u
                w_scr[rows, sl] = w.astype(w_scr.dtype)
                a_scr[rows, sl] = attn.astype(a_scr.dtype)

    pairs = range(hb // rep)
    s_pairs = [s_scr[pr] for pr in pairs]
    outs = []
    for ci in range(n_chunks):
        rows = pl.ds(ci * c, c)
        gc = gcb_scr[rows, :]
        gt = _lane_bcast_all(gt_ref[rows, :], sel)
        qs = [_bdot(q_scr[rows, heads[pr]], s_pairs[pr]) for pr in pairs]
        v_new = [u_scr[rows, sl] - _bdot(w_scr[rows, sl], s_pairs[h // rep][:, heads[h % rep]])
                 for h, sl in enumerate(heads)]
        outs.append([qs[h // rep][:, heads[h % rep]] * jnp.exp(gc[:, sl])
                     + _bdot(a_scr[rows, sl][:, :c], v_new[h]) for h, sl in enumerate(heads)])
        for pr in pairs:
            k_t = _transpose_pad(k_scr[rows, heads[pr]])[:, :c]
            mine = range(pr * rep, (pr + 1) * rep)
            v_scaled = jnp.concatenate([v_new[h] * jnp.exp(gt[:, heads[h]] - gc[:, heads[h]]) for h in mine], axis=1)
            decays = jnp.concatenate([jnp.exp(gt[0:1, heads[h]]) for h in mine], axis=1)
            s_pairs[pr] = decays * s_pairs[pr] + _bdot(k_t, v_scaled)
    for pr in pairs:
        s_scr[pr] = s_pairs[pr]
    for ci in range(n_chunks):
        rows = pl.ds(ci * c, c)
        for h, sl in enumerate(heads):
            o_ref[rows, sl] = _head_out(outs[ci][h], z_ref[rows, sl], ng).astype(o_ref.dtype)

    @pl.when(t == pl.num_programs(2) - 1)
    def _():
        for h in range(hb):
            st_ref[h] = s_scr[h // rep][:, (h % rep) * HEAD_DIM:(h % rep + 1) * HEAD_DIM].astype(st_ref.dtype)


def _gdn_layouts(beta, gam, gtot, nv, hb):
    m = beta.shape[0]
    nhb = nv // hb
    as_cols = lambda a: a.reshape(m, nhb, hb).transpose(1, 0, 2)
    g_rows = gam.reshape(m // GDN_TILE, GDN_TILE, nhb, hb).transpose(2, 0, 3, 1)
    g_rows = jnp.pad(g_rows, ((0, 0), (0, 0), (0, 0), (0, HEAD_DIM - GDN_TILE)))
    return as_cols(beta), as_cols(gam), as_cols(gtot), g_rows


def _gdn_prompt(proj, conv_col0, z_col0, conv_w, gate_arrays, norm_g, batch, seq, qk_heads, v_heads,
                state_dtype):
    rep = v_heads // qk_heads
    hb = _tile(v_heads, 8, rep)
    tt = _tile(seq, 256, GDN_CHUNK)
    nt = seq // tt
    nhb = v_heads // hb
    wq = (hb // rep) * HEAD_DIM
    wv = hb * HEAD_DIM
    qk_w = qk_heads * HEAD_DIM
    bc, gc, gt, gr = _gdn_layouts(*gate_arrays, v_heads, hb)
    assert conv_col0 % wv == 0 and z_col0 % wv == 0 and GDN_CHUNK == GDN_TILE and tt % SUBLANES == 0
    q0 = conv_col0 // wq
    k0 = (conv_col0 + qk_w) // wq
    v0 = (conv_col0 + 2 * qk_w) // wv
    z0 = z_col0 // wv
    kc0 = qk_w // wq
    vc0 = 2 * qk_w // wv
    col_spec = pl.BlockSpec((None, tt, hb), lambda b, h, t: (h, b * nt + t, 0))
    sub = tt // SUBLANES

    def cur(width, c0):
        return pl.BlockSpec((tt, width), lambda b, h, t: (b * nt + t, c0 + h))

    def prev(width, c0):
        return pl.BlockSpec((SUBLANES, width), lambda b, h, t: (jnp.maximum((b * nt + t) * sub - 1, 0), c0 + h))

    def taps(width, c0):
        return pl.BlockSpec((None, CONV_W, width), lambda b, h, t: (0, 0, c0 + h))

    return pl.pallas_call(
        functools.partial(_gdn_prompt_kernel, hb=hb, rep=rep, scale=float(HEAD_DIM) ** -0.5),
        grid=(batch, nhb, nt),
        in_specs=[cur(wq, q0), cur(wq, k0), cur(wv, v0),
                  prev(wq, q0), prev(wq, k0), prev(wv, v0),
                  taps(wq, 0), taps(wq, kc0), taps(wv, vc0),
                  cur(wv, z0),
                  col_spec, col_spec, col_spec,
                  pl.BlockSpec((None, tt // GDN_TILE, hb, HEAD_DIM), lambda b, h, t: (h, b * nt + t, 0, 0)),
                  pl.BlockSpec((1, HEAD_DIM), lambda b, h, t: (0, 0))],
        out_specs=[pl.BlockSpec((tt, wv), lambda b, h, t: (b * nt + t, h)),
                   pl.BlockSpec((None, hb, HEAD_DIM, HEAD_DIM), lambda b, h, t: (b, h, 0, 0))],
        out_shape=[jax.ShapeDtypeStruct((batch * seq, v_heads * HEAD_DIM), BF16),
                   jax.ShapeDtypeStruct((batch, v_heads, HEAD_DIM, HEAD_DIM), state_dtype)],
        scratch_shapes=[pltpu.VMEM((hb // rep, HEAD_DIM, rep * HEAD_DIM), F32),
                        pltpu.VMEM((tt, wv), F32),
                        pltpu.VMEM((tt, wv), BF16),
                        pltpu.VMEM((tt, wv), BF16),
                        pltpu.VMEM((tt + SUBLANES, wq), F32),
                        pltpu.VMEM((tt + SUBLANES, wq), F32),
                        pltpu.VMEM((tt + SUBLANES, wv), F32),
                        pltpu.VMEM((tt, wq), F32),
                        pltpu.VMEM((tt, wq), F32),
                        pltpu.VMEM((tt, wv), F32)],
        compiler_params=_cparams("parallel", "parallel", "arbitrary"),
        name="gdn_prompt",
    )(proj, proj, proj, proj, proj, proj, conv_w, conv_w, conv_w, proj, bc, gc, gt, gr, norm_g)


def _gdn_sample_kernel(q_ref, k_ref, v_ref, z_ref, bc_ref, gc_ref, gt_ref, gr_ref, ng_ref, s0_ref,
                       o_ref, st_ref, *, hb, rep, seq):
    c = GDN_TILE
    nseq = c // seq
    per_group = SUBLANES // seq
    lsh = seq.bit_length() - 1
    rows_i, cols_i = _tile_iotas()
    same = (rows_i >> lsh) == (cols_i >> lsh)
    incl = (cols_i <= rows_i) & same
    strict = (cols_i < rows_i) & same
    row8 = lax.broadcasted_iota(jnp.int32, (SUBLANES, 1), 0)
    lane_seq = lax.broadcasted_iota(jnp.int32, (1, HEAD_DIM), 1) >> lsh
    ng = ng_ref[...]
    sel = _head_select(hb)
    bc = _lane_bcast_all(bc_ref[...], sel)
    gc = _lane_bcast_all(gc_ref[...], sel)
    gt = _lane_bcast_all(gt_ref[...], sel)
    gr = gr_ref[0]
    head = lambda ref, i: ref[:, i * HEAD_DIM:(i + 1) * HEAD_DIM]
    qs = [head(q_ref, i) for i in range(hb // rep)]
    ks = [head(k_ref, i) for i in range(hb // rep)]
    wy = _gdn_wy([(qs, ks, [head(v_ref, h) for h in range(hb)], bc, gc, gr)], rep,
                 rows_i, cols_i, incl, strict, seq)[0]
    k_t_seq = []
    for k in ks:
        k_t = _transpose_pad(k)
        k_t_seq.append(jnp.concatenate([jnp.where(lane_seq == b, k_t, 0.0)[:, :c] for b in range(nseq)],
                                       axis=0).astype(BF16))
    qe = [qs[h // rep] * jnp.exp(head(gc, h)) for h in range(hb)]
    ws_parts = [[] for _ in range(hb)]
    qs_parts = [[] for _ in range(hb)]
    for g in range(c // SUBLANES):
        gsl = slice(g * SUBLANES, (g + 1) * SUBLANES)
        for h in range(hb):
            acc = jnp.zeros((2 * SUBLANES, HEAD_DIM), F32)
            for p in range(per_group):
                valid = (row8 >= p * seq) & (row8 < (p + 1) * seq)
                lhs = jnp.concatenate([jnp.where(valid, wy[h][1][gsl], 0.0),
                                       jnp.where(valid, qe[h][gsl], 0.0)], axis=0)
                acc = acc + _bdot(lhs, s0_ref[g * per_group + p, h])
            ws_parts[h].append(acc[:SUBLANES])
            qs_parts[h].append(acc[SUBLANES:])
    v_new = [wy[h][0] - jnp.concatenate(ws_parts[h], axis=0) for h in range(hb)]
    outs = [jnp.concatenate(qs_parts[h], axis=0) + _bdot(wy[h][2][:, :c], v_new[h]) for h in range(hb)]
    upds = [jnp.dot(k_t_seq[h // rep], (v_new[h] * jnp.exp(head(gt, h) - head(gc, h))).astype(BF16),
                    preferred_element_type=F32) for h in range(hb)]
    for h in range(hb):
        o_ref[:, h * HEAD_DIM:(h + 1) * HEAD_DIM] = _head_out(outs[h], head(z_ref, h), ng).astype(o_ref.dtype)
        decay = jnp.exp(head(gt, h))
        for b in range(nseq):
            st_ref[b, h] = (decay[b * seq:b * seq + 1] * s0_ref[b, h].astype(F32)
                            + upds[h][b * HEAD_DIM:(b + 1) * HEAD_DIM]).astype(st_ref.dtype)


def _gdn_sample(qkv, proj, z_col0, gate_arrays, norm_g, s0, batch, seq, qk_heads, v_heads):
    rep = v_heads // qk_heads
    hb = _tile(v_heads, 8, rep)
    assert GDN_TILE % seq == 0 and SUBLANES % seq == 0 and seq & (seq - 1) == 0
    nb = GDN_TILE // seq
    assert batch % nb == 0
    r = GDN_TILE
    nhb = v_heads // hb
    wq = (hb // rep) * HEAD_DIM
    wv = hb * HEAD_DIM
    qk_w = qk_heads * HEAD_DIM
    bc, gc, gt, gr = _gdn_layouts(*gate_arrays, v_heads, hb)
    k0 = qk_w // wq
    v0 = 2 * qk_w // wv
    z0 = z_col0 // wv
    assert z_col0 % wv == 0
    st_spec = pl.BlockSpec((nb, hb, HEAD_DIM, HEAD_DIM), lambda b, h: (b, h, 0, 0))
    col_spec = pl.BlockSpec((None, r, hb), lambda b, h: (h, b, 0))
    return pl.pallas_call(
        functools.partial(_gdn_sample_kernel, hb=hb, rep=rep, seq=seq),
        grid=(batch // nb, nhb),
        in_specs=[pl.BlockSpec((r, wq), lambda b, h: (b, h)),
                  pl.BlockSpec((r, wq), lambda b, h: (b, k0 + h)),
                  pl.BlockSpec((r, wv), lambda b, h: (b, v0 + h)),
                  pl.BlockSpec((r, wv), lambda b, h: (b, z0 + h)),
                  col_spec, col_spec, col_spec,
                  pl.BlockSpec((None, 1, hb, HEAD_DIM), lambda b, h: (h, b, 0, 0)),
                  pl.BlockSpec((1, HEAD_DIM), lambda b, h: (0, 0)),
                  st_spec],
        out_specs=[pl.BlockSpec((r, wv), lambda b, h: (b, h)), st_spec],
        out_shape=[jax.ShapeDtypeStruct((batch * seq, v_heads * HEAD_DIM), BF16),
                   jax.ShapeDtypeStruct(s0.shape, s0.dtype)],
        compiler_params=_cparams("parallel", "parallel"),
        name="gdn_sample",
    )(qkv, qkv, qkv, proj, bc, gc, gt, gr, norm_g, s0)


def _merge_kernel(oh_ref, og_ref, wa_ref, wb_ref, ga_ref, gb_ref, o_ref):
    ya = _bdot(oh_ref[...], wa_ref[...])
    yb = _bdot(og_ref[...], wb_ref[...])
    o_ref[...] = (_sigmoid(ga_ref[...]) * ya + _sigmoid(gb_ref[...]) * yb).astype(o_ref.dtype)


def _merge(o_hg, o_gdn, w_a, w_b, gates, tm):
    m, ka = o_hg.shape
    kb = o_gdn.shape[1]
    d = w_a.shape[-1]
    tn = _tile(d, 256, 128)
    nj = d // tn
    return pl.pallas_call(
        _merge_kernel,
        grid=(m // tm, nj),
        in_specs=[pl.BlockSpec((tm, ka), lambda i, j: (i, 0)),
                  pl.BlockSpec((tm, kb), lambda i, j: (i, 0)),
                  pl.BlockSpec((ka, tn), lambda i, j: (0, j)),
                  pl.BlockSpec((kb, tn), lambda i, j: (0, j)),
                  pl.BlockSpec((tm, tn), lambda i, j: (i, j)),
                  pl.BlockSpec((tm, tn), lambda i, j: (i, nj + j))],
        out_specs=pl.BlockSpec((tm, tn), lambda i, j: (i, j)),
        out_shape=jax.ShapeDtypeStruct((m, d), BF16),
        compiler_params=_cparams("parallel", "arbitrary"),
        name="merge",
    )(o_hg, o_gdn, w_a, w_b, gates, gates)


def _outproj_kernel(m_ref, w_ref, x_ref, g1_ref, sc_ref, sh_ref, ng_ref, h_ref, a_ref):
    ng = ng_ref[...]
    mix = _bdot(m_ref[...], w_ref[...])
    n = mix.shape[0]
    h = x_ref[...].astype(F32) + _mod_rows(g1_ref[...], n) * (_rms(mix) * ng[0:1])
    h_ref[...] = h
    a_ref[...] = ((_rms(h) * ng[1:2]) * (1.0 + _mod_rows(sc_ref[...], n))
                  + _mod_rows(sh_ref[...], n)).astype(a_ref.dtype)


def _outproj(merged, w_o_bf16, x, mod, norm_g2, rows_per_seq, tm):
    m, d = x.shape
    kdim = merged.shape[1]
    row = lambda i: (i, 0)
    return pl.pallas_call(
        _outproj_kernel,
        grid=(m // tm,),
        in_specs=[pl.BlockSpec((tm, kdim), row),
                  pl.BlockSpec((kdim, d), lambda i: (0, 0), pipeline_mode=pl.Buffered(1)),
                  pl.BlockSpec((tm, d), row),
                  _mod_spec(mod, 2, tm, rows_per_seq),
                  _mod_spec(mod, 4, tm, rows_per_seq),
                  _mod_spec(mod, 3, tm, rows_per_seq),
                  pl.BlockSpec((2, d), lambda i: (0, 0))],
        out_specs=[pl.BlockSpec((tm, d), row), pl.BlockSpec((tm, d), row)],
        out_shape=[jax.ShapeDtypeStruct((m, d), F32), jax.ShapeDtypeStruct((m, d), BF16)],
        compiler_params=_cparams("parallel"),
        name="out_proj",
    )(merged, w_o_bf16, x, mod, mod, mod, norm_g2)


def _cast_kernel(x_ref, o_ref):
    o_ref[...] = x_ref[...].astype(o_ref.dtype)


def _to_bf16(w):
    kdim, n = w.shape
    tk = _tile(kdim, 512, 16)
    return pl.pallas_call(
        _cast_kernel,
        grid=(kdim // tk,),
        in_specs=[pl.BlockSpec((tk, n), lambda i: (i, 0))],
        out_specs=pl.BlockSpec((tk, n), lambda i: (i, 0)),
        out_shape=jax.ShapeDtypeStruct((kdim, n), BF16),
        compiler_params=_cparams("parallel"),
        name="weight_to_bf16",
    )(w)


def _ffn_up_kernel(a_ref, w_ref, o_ref):
    o_ref[...] = jnp.square(jnp.maximum(_bdot(a_ref[...], w_ref[...]), 0.0)).astype(o_ref.dtype)


def _ffn_up(a, w_up, tm):
    m, d = a.shape
    dff = w_up.shape[-1]
    tf = _tile(dff, 512, 128)
    return pl.pallas_call(
        _ffn_up_kernel,
        grid=(m // tm, dff // tf),
        in_specs=[pl.BlockSpec((tm, d), lambda i, f: (i, 0)),
                  pl.BlockSpec((None, d, tf), lambda i, f: (0, 0, f))],
        out_specs=pl.BlockSpec((tm, tf), lambda i, f: (i, f)),
        out_shape=jax.ShapeDtypeStruct((m, dff), BF16),
        compiler_params=_cparams("parallel", "arbitrary"),
        name="ffn_up",
    )(a, w_up)


EPILOGUE_ROWS = 128


def _ffn_down_kernel(hid_ref, w_ref, h_ref, g2_ref, ng_ref, y_ref):
    j = pl.program_id(1)
    tn = w_ref.shape[1]
    y_ref[:, pl.ds(pl.multiple_of(j * tn, tn), tn)] = _bdot(hid_ref[...], w_ref[...])

    @pl.when(j == pl.num_programs(1) - 1)
    def _():
        tm = y_ref.shape[0]
        step = min(EPILOGUE_ROWS, tm)
        g_rows = g2_ref.shape[0]
        g_step = max(g_rows * step // tm, 1)

        def body(r, carry):
            rows = pl.ds(pl.multiple_of(r * step, step), step)
            g2 = g2_ref[...] if g_rows == 1 else _mod_rows(
                g2_ref[pl.ds(pl.multiple_of(r * g_step, g_step), g_step), :], step)
            y_ref[rows, :] = h_ref[rows, :] + g2 * (_rms(y_ref[rows, :]) * ng_ref[...])
            return carry

        lax.fori_loop(0, y_ref.shape[0] // step, body, 0)


def _ffn_down(hid, w_down, h, mod, norm_g, rows_per_seq, tm, out_dtype):
    m, d = h.shape
    dff = hid.shape[1]
    tn = _tile(d, 512, 128)
    row = lambda i, j: (i, 0)
    assert out_dtype == F32, "the resident output block holds the f32 pre-norm result"
    return pl.pallas_call(
        _ffn_down_kernel,
        grid=(m // tm, d // tn),
        in_specs=[pl.BlockSpec((tm, dff), row),
                  pl.BlockSpec((dff, tn), lambda i, j: (0, j)),
                  pl.BlockSpec((tm, d), row),
                  _mod_spec(mod, 5, tm, rows_per_seq),
                  pl.BlockSpec((1, d), lambda i, j: (0, 0))],
        out_specs=pl.BlockSpec((tm, d), row),
        out_shape=jax.ShapeDtypeStruct((m, d), out_dtype),
        compiler_params=_cparams("parallel", "arbitrary", vmem_limit=VMEM_LIMIT_FFN_DOWN),
        name="ffn_down",
    )(hid, w_down, h, mod, norm_g)


def _run_group(x3, mod, rows_per_seq_for_mod, conv_buf, s_hg, s_gdn, lb, p, prompt):
    batch, seq, d = x3.shape
    m = batch * seq
    x = x3.reshape(m, d)
    hg_heads, gv_heads = p["hg_heads"], p["gv_heads"]
    gq_heads = p["gq_heads"]
    hg_w = hg_heads * HEAD_DIM
    gq_w = gq_heads * HEAD_DIM
    gv_w = gv_heads * HEAD_DIM
    ch = 2 * gq_w + gv_w
    main = 4 * hg_w + ch + gv_w
    conv0 = 4 * hg_w
    z0 = conv0 + ch
    rows_unit = seq if prompt else m
    tm = _tile(rows_unit, 512, SUBLANES)
    tm_proj = _tile(rows_unit, 2048, SUBLANES)
    tm_wide = _tile(rows_unit, 1024, SUBLANES)

    a1 = _prenorm(x, p["norm_g"][0:1], mod, 1, 0, rows_per_seq_for_mod, tm)
    tn_pref = 512 if tm_proj > 512 else 1024
    proj = _project(a1, p["w_in"], main, tm_proj, _tile(main, tn_pref, 128))
    tail = _project(a1, p["w_tail"], p["w_tail"].shape[0], tm_proj, p["w_tail"].shape[0])
    gates = _project(a1, p["w_gates"], 2 * d, tm_proj, _tile(2 * d, tn_pref, 128))

    if prompt:
        o_hg, s_hg_new = _hgrn_prompt(proj, lb, p["hg_norm_g"], batch, seq, hg_heads, p["hg_dtype"])
        gate_arrays = _gdn_gates(tail, p["a_log"], p["dt_bias"], GDN_CHUNK)
        o_gdn, s_gdn_new = _gdn_prompt(proj, conv0, z0, p["conv_w"], gate_arrays, p["gdn_norm_g"], batch, seq,
                                       gq_heads, gv_heads, p["gdn_dtype"])
        assert seq % SUBLANES == 0
        conv_new = proj.reshape(batch, seq, main)[:, seq - (CONV_W - 1):, conv0:conv0 + ch]
    else:
        o_hg, s_hg_new = _hgrn_sample(proj, lb, p["hg_norm_g"], s_hg, batch, seq, hg_heads)
        gate_arrays = _gdn_gates(tail, p["a_log"], p["dt_bias"], seq)
        qkv, conv_new = _conv_sample(proj, conv0, conv_buf.astype(F32), p["conv_w"], seq, gq_w, ch)
        o_gdn, s_gdn_new = _gdn_sample(qkv, proj, z0, gate_arrays, p["gdn_norm_g"], s_gdn, batch, seq,
                                       gq_heads, gv_heads)

    merged = _merge(o_hg, o_gdn, p["w_out_hg"], p["w_out_gdn"], gates, tm_wide)
    h1, a2 = _outproj(merged, p["w_o_bf16"], x, mod, p["norm_g"][1:3], rows_per_seq_for_mod, tm)
    hid = _ffn_up(a2, p["w_up"], tm_proj)
    y = _ffn_down(hid, p["w_down_bf16"], h1, mod, p["norm_g"][3:4], rows_per_seq_for_mod, tm, x3.dtype)
    return y.reshape(batch, seq, d), conv_new, s_hg_new, s_gdn_new


def kernel(x_prompt, x_sample, state_hgrn, state_gdn, cache_conv, c_prompt, c_sample, lb_logits, w_ada,
           b_ada, norm_g, w_in, conv_w, A_log, dt_bias, hg_norm_g, gdn_norm_g, w_out_hg, w_out_gdn, w_o,
           w_up, w_down):
    depth = w_in.shape[0]
    assert depth == 1, "single-layer trunk"
    bp, tp, d = x_prompt.shape
    bs, ts, _ = x_sample.shape
    hg_heads = state_hgrn.shape[2]
    gv_heads = state_gdn.shape[2]
    ch = cache_conv.shape[-1]
    gq_heads = (ch // HEAD_DIM - gv_heads) // 2
    assert tp >= CONV_W - 1 and ts >= CONV_W - 1
    assert tp % GDN_CHUNK == 0 and tp % HG_CHUNK == 0

    main = 4 * hg_heads * HEAD_DIM + ch + gv_heads * HEAD_DIM
    w_in_t = jnp.swapaxes(w_in[0], 0, 1)
    p = dict(
        hg_heads=hg_heads, gv_heads=gv_heads, gq_heads=gq_heads,
        hg_dtype=state_hgrn.dtype, gdn_dtype=state_gdn.dtype,
        norm_g=norm_g[0].astype(F32), w_in=w_in_t,
        w_tail=w_in_t[main:main + 2 * gv_heads],
        w_gates=w_in_t[main + 2 * gv_heads:],
        conv_w=conv_w.astype(F32), a_log=A_log.astype(F32), dt_bias=dt_bias.astype(F32),
        hg_norm_g=hg_norm_g.astype(F32), gdn_norm_g=gdn_norm_g.astype(F32),
        w_out_hg=_to_bf16(w_out_hg[0]), w_out_gdn=_to_bf16(w_out_gdn[0]), w_o_bf16=_to_bf16(w_o[0]), w_up=w_up,
        w_down_bf16=_to_bf16(w_down[0]),
    )

    lbs = jnp.cumsum(jax.nn.softmax(lb_logits.astype(F32), axis=0), axis=0)
    lb = lbs[0:1]

    n_c = bp + bs
    pad = (-n_c) % SUBLANES
    c_all = jnp.concatenate([c_prompt.astype(F32), c_sample.astype(F32), jnp.zeros((pad, d), F32)], axis=0)
    mod_all = _ada(c_all, w_ada, b_ada.astype(F32))
    mod_p = mod_all[:bp].reshape(bp * 6, 1, d)
    mod_s = mod_all[bp:bp + bs].reshape(bs, 6, d).transpose(1, 0, 2)

    y_p, conv_p, hg_p, gdn_p = _run_group(x_prompt, mod_p, tp, None, None, None, lb, p, prompt=True)
    y_s, conv_s, hg_s, gdn_s = _run_group(x_sample, mod_s, ts, cache_conv[0], state_hgrn[0], state_gdn[0],
                                          lb, p, prompt=False)
    return (y_p, y_s,
            hg_p[None], gdn_p[None], conv_p.astype(cache_conv.dtype)[None],
            hg_s[None], gdn_s[None], conv_s.astype(cache_conv.dtype)[None])
```
